```python
import math
import jax
import jax.numpy as jnp
from jax import lax
import numpy as np

D_MODEL = 2048
BATCH = 1
SEQ = 16384
DEPTH = 2

N_EVEN = (DEPTH + 1) // 2
N_ODD = DEPTH // 2
RMS_EPS = 1e-6
LN_EPS = 1e-5
NEG_INF = -1e30

CONV_DIM = D_MODEL // 2
CONV_WIDTH = 31
DIL_HEAD_DIM = 64
DIL_HEADS = (D_MODEL // 2) // DIL_HEAD_DIM
DIL_DIM = DIL_HEADS * DIL_HEAD_DIM
DIL_CONFIGS = ((128, 1), (512, 4), (2048, 16))
DIL_HALF = DIL_CONFIGS[0][0] // (2 * DIL_CONFIGS[0][1])
N_BUCKETS = 32
T5_MAX_DIST = DIL_CONFIGS[-1][0] // 2
MLA_HEADS = 12
Q_LORA = 512
KV_LORA = 128
NOPE_DIM = 128
ROPE_DIM = 64
V_DIM = 128
ROPE_BASE = 10000.0
Q_BLOCK = 128
S5_DIM = D_MODEL // 4
S5_GROUP = 16
S5_GROUPS = S5_DIM // S5_GROUP
S5_STATE = 64
EVEN_IN = 2 * CONV_DIM + 3 * DIL_DIM
EVEN_MIX = CONV_DIM + DIL_DIM
ODD_IN = Q_LORA + KV_LORA + ROPE_DIM + S5_DIM
ODD_MIX = MLA_HEADS * V_DIM + S5_DIM
F_DENSE = 5632
N_EXPERTS = 8
TOP_K = 2
F_EXPERT = 7168

kernel_name = 'hybrid_conv_dilattn_mla_s5_moe_encoder'


def rms_norm(x, g):
    xf = x.astype(jnp.float32)
    y = xf * lax.rsqrt(jnp.mean(xf * xf, axis=-1, keepdims=True) + RMS_EPS)
    return (y * g.astype(jnp.float32)).astype(x.dtype)


def layer_norm(x, g, b):
    xf = x.astype(jnp.float32)
    mu = jnp.mean(xf, axis=-1, keepdims=True)
    var = jnp.mean(jnp.square(xf - mu), axis=-1, keepdims=True)
    y = (xf - mu) * lax.rsqrt(var + LN_EPS)
    return (y * g.astype(jnp.float32) + b.astype(jnp.float32)).astype(x.dtype)


def modulate(h, shift, scale):
    return h * (1 + scale) + shift


def swiglu(h, w1, w3, w2):
    return (jax.nn.silu(h @ w1) * (h @ w3)) @ w2


def t5_bucket(rel):
    half = N_BUCKETS // 2
    exact = half // 2
    n = jnp.abs(rel)
    large = exact + (jnp.log(jnp.maximum(n, 1).astype(jnp.float32) / exact)
                     / math.log(T5_MAX_DIST / exact) * (half - exact)).astype(jnp.int32)
    large = jnp.minimum(large, half - 1)
    return jnp.where(rel > 0, half, 0) + jnp.where(n < exact, n, large)


def band_attention(q, k, v, bias):
    n, length, nh, hd = q.shape
    blk = DIL_HALF
    nb = -(-length // blk)
    pad = nb * blk - length
    q = jnp.pad(q, ((0, 0), (0, pad), (0, 0), (0, 0)))
    k = jnp.pad(k, ((0, 0), (blk, pad + blk), (0, 0), (0, 0)))
    v = jnp.pad(v, ((0, 0), (blk, pad + blk), (0, 0), (0, 0)))
    qb = q.reshape(n, nb, blk, nh, hd)
    kb = k.reshape(n, nb + 2, blk, nh, hd)
    vb = v.reshape(n, nb + 2, blk, nh, hd)
    kw = jnp.concatenate([kb[:, :-2], kb[:, 1:-1], kb[:, 2:]], axis=2)
    vw = jnp.concatenate([vb[:, :-2], vb[:, 1:-1], vb[:, 2:]], axis=2)
    s = jnp.einsum('nbqhd,nbkhd->nbhqk', qb, kw) * (hd ** -0.5) + bias
    qpos = (jnp.arange(nb) * blk)[:, None] + jnp.arange(blk)
    kpos = (jnp.arange(nb) * blk - blk)[:, None] + jnp.arange(3 * blk)
    dist = kpos[:, None, :] - qpos[:, :, None]
    valid = (jnp.abs(dist) <= blk) & (kpos[:, None, :] >= 0) & (kpos[:, None, :] < length)
    s = jnp.where(valid[None, :, None], s, NEG_INF)
    m = jnp.max(s, axis=-1, keepdims=True)
    p = jnp.exp(s - m)
    l = jnp.sum(p, axis=-1)
    l = jnp.swapaxes(l, -1, -2)
    o = jnp.einsum('nbhqk,nbkhd->nbqhd', p, vw) / l[..., None]
    m = jnp.swapaxes(m[..., 0], -1, -2)
    return (o.reshape(n, nb * blk, nh, hd)[:, :length],
            m.reshape(n, nb * blk, nh)[:, :length],
            l.reshape(n, nb * blk, nh)[:, :length])


def dilated_attention(q, k, v, t5_table):
    bsz, seq, nh, hd = q.shape
    q, k, v = (t.astype(jnp.float32) for t in (q, k, v))
    slot = jnp.arange(3 * DIL_HALF)[None, :] - DIL_HALF - jnp.arange(DIL_HALF)[:, None]
    outs, maxes, dens = [], [], []
    for _, dil in DIL_CONFIGS:
        sub = seq // dil

        def to_sub(t):
            return t.reshape(bsz, sub, dil, *t.shape[2:]).swapaxes(1, 2).reshape(bsz * dil, sub, *t.shape[2:])

        def from_sub(t):
            return t.reshape(bsz, dil, sub, *t.shape[2:]).swapaxes(1, 2).reshape(bsz, seq, *t.shape[2:])

        bias = jnp.transpose(t5_table[t5_bucket(slot * dil)], (2, 0, 1)).astype(jnp.float32)
        o, m, l = band_attention(to_sub(q), to_sub(k), to_sub(v), bias)
        outs.append(from_sub(o))
        maxes.append(from_sub(m))
        dens.append(from_sub(l))
    o, m, l = jnp.stack(outs), jnp.stack(maxes), jnp.stack(dens)
    w = l * jnp.exp(m - jnp.max(m, axis=0, keepdims=True))
    return jnp.sum(w[..., None] * o, axis=0) / jnp.sum(w, axis=0)[..., None]


def even_mixer(h, t5_table, w_in, conv_w, conv_b, cln_g, cln_b, w_out):
    bsz, seq, _ = h.shape
    z = h @ w_in
    a_val, a_gate, q, k, v = jnp.split(
        z, [CONV_DIM, 2 * CONV_DIM, 2 * CONV_DIM + DIL_DIM, 2 * CONV_DIM + 2 * DIL_DIM], axis=-1)
    a = a_val * jax.nn.sigmoid(a_gate)
    a = lax.conv_general_dilated(
        a, conv_w[:, None, :].astype(a.dtype), window_strides=(1,),
        padding=((CONV_WIDTH // 2, CONV_WIDTH // 2),),
        dimension_numbers=('NWC', 'WIO', 'NWC'), feature_group_count=CONV_DIM) + conv_b
    a = jax.nn.silu(layer_norm(a, cln_g, cln_b))
    o = dilated_attention(q.reshape(bsz, seq, DIL_HEADS, DIL_HEAD_DIM),
                          k.reshape(bsz, seq, DIL_HEADS, DIL_HEAD_DIM),
                          v.reshape(bsz, seq, DIL_HEADS, DIL_HEAD_DIM), t5_table)
    o = o.reshape(bsz, seq, DIL_DIM).astype(h.dtype)
    return jnp.concatenate([a.astype(h.dtype), o], axis=-1) @ w_out


def rope_tables(seq):
    pos = jnp.arange(seq, dtype=jnp.float32)
    inv_freq = ROPE_BASE ** (-jnp.arange(0, ROPE_DIM, 2, dtype=jnp.float32) / ROPE_DIM)
    ang = pos[:, None] * inv_freq[None, :]
    return jnp.cos(ang), jnp.sin(ang)


def apply_rope(x, cos, sin):
    xf = x.astype(jnp.float32)
    x1, x2 = xf[..., :ROPE_DIM // 2], xf[..., ROPE_DIM // 2:]
    return jnp.concatenate([x1 * cos - x2 * sin, x2 * cos + x1 * sin], axis=-1).astype(x.dtype)


def mla_attention(q_nope, q_pe, k_nope, k_pe, v):
    bsz, seq, nh, _ = q_nope.shape
    nq = seq // Q_BLOCK
    scale = (NOPE_DIM + ROPE_DIM) ** -0.5
    k_nope, k_pe, v = (t.astype(jnp.float32) for t in (k_nope, k_pe, v))

    def blocks(t):
        return t.astype(jnp.float32).reshape(bsz, nq, Q_BLOCK, *t.shape[2:]).swapaxes(0, 1)

    def attend(qs):
        qn, qp = qs
        s = (jnp.einsum('bqhd,bkhd->bhqk', qn, k_nope)
             + jnp.einsum('bqhr,bkr->bhqk', qp, k_pe)) * scale
        p = jax.nn.softmax(s, axis=-1)
        return jnp.einsum('bhqk,bkhd->bqhd', p, v)

    o = lax.map(attend, (blocks(q_nope), blocks(q_pe)))
    return o.swapaxes(0, 1).reshape(bsz, seq, nh, V_DIM)


def s5_scan(u, a_re, a_im, log_dt, b_re, b_im, c_re, c_im, reverse):
    A = lax.complex(a_re.astype(jnp.float32), a_im.astype(jnp.float32))
    dt = jnp.exp(log_dt.astype(jnp.float32))[:, None]
    a_bar = jnp.exp(A * dt)
    b_c = lax.complex(b_re.astype(jnp.float32), b_im.astype(jnp.float32))
    b_bar = ((a_bar - 1.0) / A)[..., None] * b_c
    bu = jnp.einsum('bsgc,gpc->bsgp', u.astype(jnp.complex64), b_bar)

    def combine(e1, e2):
        a1, b1 = e1
        a2, b2 = e2
        return a1 * a2, a2 * b1 + b2

    _, xs = lax.associative_scan(combine, (jnp.broadcast_to(a_bar, bu.shape), bu),
                                 reverse=reverse, axis=1)
    c_c = lax.complex(c_re.astype(jnp.float32), c_im.astype(jnp.float32))
    return jnp.real(jnp.einsum('bsgp,gcp->bsgc', xs, c_c))


def odd_mixer(h, w_in, q_norm_g, w_uq, kv_norm_g, w_ukv, a_re, a_im, log_dt,
              b_re, b_im, c_re, c_im, d_skip, w_glu, w_out):
    bsz, seq, _ = h.shape
    z = h @ w_in
    c_q, c_kv, k_pe, u = jnp.split(z, [Q_LORA, Q_LORA + KV_LORA, Q_LORA + KV_LORA + ROPE_DIM], axis=-1)
    q = (rms_norm(c_q, q_norm_g) @ w_uq).reshape(bsz, seq, MLA_HEADS, NOPE_DIM + ROPE_DIM)
    kv = (rms_norm(c_kv, kv_norm_g) @ w_ukv).reshape(bsz, seq, MLA_HEADS, NOPE_DIM + V_DIM)
    cos, sin = rope_tables(seq)
    q_pe = apply_rope(q[..., NOPE_DIM:], cos[:, None, :], sin[:, None, :])
    k_pe = apply_rope(k_pe, cos, sin)
    o_mla = mla_attention(q[..., :NOPE_DIM], q_pe, kv[..., :NOPE_DIM], k_pe, kv[..., NOPE_DIM:])
    o_mla = o_mla.reshape(bsz, seq, MLA_HEADS * V_DIM).astype(h.dtype)
    uf = u.astype(jnp.float32)
    ug = uf.reshape(bsz, seq, S5_GROUPS, S5_GROUP)
    y = (s5_scan(ug, a_re[0], a_im[0], log_dt[0], b_re[0], b_im[0], c_re[0], c_im[0], False)
         + s5_scan(ug, a_re[1], a_im[1], log_dt[1], b_re[1], b_im[1], c_re[1], c_im[1], True))
    y = y.reshape(bsz, seq, S5_DIM) + d_skip.astype(jnp.float32) * uf
    y = jax.nn.gelu(y)
    y = y * jax.nn.sigmoid(y @ w_glu.astype(jnp.float32))
    return jnp.concatenate([o_mla, y.astype(h.dtype)], axis=-1) @ w_out


def moe_swiglu(h, router_w, w1, w3, w2):
    logits = (h @ router_w).astype(jnp.float32)
    top_v, top_i = lax.top_k(logits, TOP_K)
    gates = jax.nn.softmax(top_v, axis=-1)
    combine = jnp.sum(jax.nn.one_hot(top_i, N_EXPERTS, dtype=jnp.float32) * gates[..., None], axis=-2)
    combine = combine.astype(h.dtype)
    out = jnp.zeros_like(h)
    for e in range(N_EXPERTS):
        out = out + combine[..., e:e + 1] * swiglu(h, w1[e], w3[e], w2[e])
    return out


def setup_inputs(seed: int = 0) -> dict:
    key = jax.random.key(seed)
    ks = iter(jax.random.split(key, 48))

    def nrm(shape, std):
        return std * jax.random.normal(next(ks), shape, jnp.float32)

    def gain(shape):
        return 1.0 + nrm(shape, 0.05)

    D = D_MODEL
    G, P, Gc = S5_GROUPS, S5_STATE, S5_GROUP
    a_im0 = jnp.pi * jnp.arange(P, dtype=jnp.float32)
    return {
        'x': nrm((BATCH, SEQ, D), 1.0),
        'c': nrm((BATCH, D), 1.0),
        't5_table': nrm((N_BUCKETS, DIL_HEADS), 0.5),
        'norm_g': gain((DEPTH, 4, D)),
        'ada_w': nrm((DEPTH, D, 6 * D), 0.5 * D ** -0.5),
        'ada_b': nrm((DEPTH, 6 * D), 0.02),
        'e_w_in': nrm((N_EVEN, D, EVEN_IN), D ** -0.5),
        'e_conv_w': nrm((N_EVEN, CONV_WIDTH, CONV_DIM), CONV_WIDTH ** -0.5),
        'e_conv_b': nrm((N_EVEN, CONV_DIM), 0.02),
        'e_cln_g': gain((N_EVEN, CONV_DIM)),
        'e_cln_b': nrm((N_EVEN, CONV_DIM), 0.02),
        'e_w_out': nrm((N_EVEN, EVEN_MIX, D), EVEN_MIX ** -0.5),
        'e_ffn_w1': nrm((N_EVEN, D, F_DENSE), D ** -0.5),
        'e_ffn_w3': nrm((N_EVEN, D, F_DENSE), D ** -0.5),
        'e_ffn_w2': nrm((N_EVEN, F_DENSE, D), F_DENSE ** -0.5),
        'o_w_in': nrm((N_ODD, D, ODD_IN), D ** -0.5),
        'o_q_norm_g': gain((N_ODD, Q_LORA)),
        'o_w_uq': nrm((N_ODD, Q_LORA, MLA_HEADS * (NOPE_DIM + ROPE_DIM)), Q_LORA ** -0.5),
        'o_kv_norm_g': gain((N_ODD, KV_LORA)),
        'o_w_ukv': nrm((N_ODD, KV_LORA, MLA_HEADS * (NOPE_DIM + V_DIM)), KV_LORA ** -0.5),
        's5_a_re': -0.5 + nrm((N_ODD, 2, G, P), 0.01),
        's5_a_im': a_im0 + nrm((N_ODD, 2, G, P), 0.01),
        's5_log_dt': jax.random.uniform(next(ks), (N_ODD, 2, G), jnp.float32,
                                        math.log(1e-3), math.log(1e-1)),
        's5_b_re': nrm((N_ODD, 2, G, P, Gc), (2 * Gc) ** -0.5),
        's5_b_im': nrm((N_ODD, 2, G, P, Gc), (2 * Gc) ** -0.5),
        's5_c_re': nrm((N_ODD, 2, G, Gc, P), (2 * P) ** -0.5),
        's5_c_im': nrm((N_ODD, 2, G, Gc, P), (2 * P) ** -0.5),
        's5_d': nrm((N_ODD, S5_DIM), 1.0),
        's5_w_glu': nrm((N_ODD, S5_DIM, S5_DIM), S5_DIM ** -0.5),
        'o_w_out': nrm((N_ODD, ODD_MIX, D), ODD_MIX ** -0.5),
        'router_w': nrm((N_ODD, D, N_EXPERTS), D ** -0.5),
        'moe_w1': nrm((N_ODD, N_EXPERTS, D, F_EXPERT), D ** -0.5),
        'moe_w3': nrm((N_ODD, N_EXPERTS, D, F_EXPERT), D ** -0.5),
        'moe_w2': nrm((N_ODD, N_EXPERTS, F_EXPERT, D), F_EXPERT ** -0.5),
    }


def reference(x, c, t5_table, norm_g, ada_w, ada_b, e_w_in, e_conv_w, e_conv_b, e_cln_g,
              e_cln_b, e_w_out, e_ffn_w1, e_ffn_w3, e_ffn_w2, o_w_in, o_q_norm_g, o_w_uq,
              o_kv_norm_g, o_w_ukv, s5_a_re, s5_a_im, s5_log_dt, s5_b_re, s5_b_im, s5_c_re,
              s5_c_im, s5_d, s5_w_glu, o_w_out, router_w, moe_w1, moe_w3, moe_w2):
    cond = jax.nn.silu(c)
    for layer in range(DEPTH):
        i = layer // 2
        mod = cond @ ada_w[layer] + ada_b[layer]
        sh_m, sc_m, gt_m, sh_f, sc_f, gt_f = jnp.split(mod[:, None, :], 6, axis=-1)
        g = norm_g[layer]
        h = modulate(rms_norm(x, g[0]), sh_m, sc_m)
        if layer % 2 == 0:
            y = even_mixer(h, t5_table, e_w_in[i], e_conv_w[i], e_conv_b[i], e_cln_g[i],
                           e_cln_b[i], e_w_out[i])
        else:
            y = odd_mixer(h, o_w_in[i], o_q_norm_g[i], o_w_uq[i], o_kv_norm_g[i], o_w_ukv[i],
                          s5_a_re[i], s5_a_im[i], s5_log_dt[i], s5_b_re[i], s5_b_im[i],
                          s5_c_re[i], s5_c_im[i], s5_d[i], s5_w_glu[i], o_w_out[i])
        x = x + gt_m * rms_norm(y, g[1])
        h = modulate(rms_norm(x, g[2]), sh_f, sc_f)
        if layer % 2 == 0:
            y = swiglu(h, e_ffn_w1[i], e_ffn_w3[i], e_ffn_w2[i])
        else:
            y = moe_swiglu(h, router_w[i], moe_w1[i], moe_w3[i], moe_w2[i])
        x = x + gt_f * rms_norm(y, g[3])
    return x
```

```python
import functools
import math

import jax
import jax.numpy as jnp
from jax import lax
from jax.experimental import pallas as pl
from jax.experimental.pallas import tpu as pltpu

F32 = jnp.float32
BF16 = jnp.bfloat16

RMS_EPS = 1e-6
LN_EPS = 1e-5
NEG_INF = -1e30

CONV_WIDTH = 31
CONV_HALO = 16
DIL_HEAD_DIM = 64
DIL_CONFIGS = ((128, 1), (512, 4), (2048, 16))
DIL_HALF = 64
N_BUCKETS = 32
T5_MAX_DIST = DIL_CONFIGS[-1][0] // 2
MLA_HEADS = 12
NOPE_DIM = 128
ROPE_DIM = 64
V_DIM = 128
ROPE_BASE = 10000.0
MLA_QK_PAD = 256
S5_GROUP = 16
S5_STATE = 64
S5_CHUNK = 32
N_EXPERTS = 8
LANE = 128
VMEM_LIMIT_MB = 56


def _cp(sem, vmem_mb=VMEM_LIMIT_MB):
    return pltpu.CompilerParams(dimension_semantics=sem, vmem_limit_bytes=vmem_mb * 1024 * 1024)


def _rms(x):
    return x * lax.rsqrt(jnp.mean(x * x, axis=-1, keepdims=True) + RMS_EPS)


def _modnorm(x, g, sh, sc):
    return (_rms(x) * g) * (1.0 + sc) + sh


def _sigmoid(x):
    return 1.0 / (1.0 + jnp.exp(-x))


def _silu(x):
    return x * _sigmoid(x)


def _adaln_kernel(c_ref, w_ref, b_ref, o_ref, cb_ref):
    @pl.when((pl.program_id(0) == 0) & (pl.program_id(1) == 0))
    def _():
        cc = c_ref[...]
        cb_ref[...] = jnp.broadcast_to(_silu(cc), cb_ref.shape)

    cb = cb_ref[...]
    for j in range(o_ref.shape[-1] // LANE):
        sl = slice(j * LANE, (j + 1) * LANE)
        o_ref[:, sl] = jnp.sum(w_ref[:, sl] * cb, axis=0, keepdims=True) + b_ref[:, sl]


def adaln(c, ada_w, ada_b, tn=1024):
    nl, d, n = ada_w.shape
    out = pl.pallas_call(
        _adaln_kernel,
        out_shape=jax.ShapeDtypeStruct((nl, 1, n), F32),
        grid=(nl, n // tn),
        in_specs=[pl.BlockSpec((d, 1), lambda l, j: (0, 0)),
                  pl.BlockSpec((None, d, tn), lambda l, j: (l, 0, j)),
                  pl.BlockSpec((None, 1, tn), lambda l, j: (l, 0, j))],
        out_specs=pl.BlockSpec((None, 1, tn), lambda l, j: (l, 0, j)),
        scratch_shapes=[pltpu.VMEM((d, LANE), F32)],
        compiler_params=_cp(("arbitrary", "arbitrary")),
        name="adaln",
    )(c.reshape(d, 1), ada_w, ada_b.reshape(nl, 1, n))
    return out


def _nmm_kernel(x_ref, g_ref, sh_ref, sc_ref, w_ref, o_ref, h_ref):
    @pl.when(pl.program_id(1) == 0)
    def _():
        h_ref[...] = _modnorm(x_ref[...], g_ref[...], sh_ref[...], sc_ref[...]).astype(h_ref.dtype)

    o_ref[...] = jnp.dot(h_ref[...], w_ref[...], preferred_element_type=F32).astype(o_ref.dtype)


def norm_mod_matmul(x, g, sh, sc, w, tm=512, tn=1024):
    s, d = x.shape
    n = w.shape[1]
    tm, tn = min(tm, s), min(tn, n)
    vec = pl.BlockSpec((1, d), lambda i, j: (0, 0))
    return pl.pallas_call(
        _nmm_kernel,
        out_shape=jax.ShapeDtypeStruct((s, n), BF16),
        grid=(s // tm, n // tn),
        in_specs=[pl.BlockSpec((tm, d), lambda i, j: (i, 0)), vec, vec, vec,
                  pl.BlockSpec((d, tn), lambda i, j: (0, j))],
        out_specs=pl.BlockSpec((tm, tn), lambda i, j: (i, j)),
        scratch_shapes=[pltpu.VMEM((tm, d), BF16)],
        compiler_params=_cp(("parallel", "arbitrary")),
        name="norm_mod_matmul",
    )(x, g, sh, sc, w)


def _conv_kernel(av_ref, ag_ref, avp_ref, agp_ref, avn_ref, agn_ref, w_ref, b_ref, lg_ref, lb_ref,
                 o_ref, buf_ref, *, rows):
    i = pl.program_id(0)
    n = pl.num_programs(0)
    ts = av_ref.shape[0]

    def glu(v_ref, g_ref):
        return v_ref[...].astype(F32) * _sigmoid(g_ref[...].astype(F32))

    buf_ref[0:CONV_HALO, :] = jnp.where(i > 0, glu(avp_ref, agp_ref), 0.0)
    buf_ref[CONV_HALO:CONV_HALO + ts, :] = glu(av_ref, ag_ref)
    buf_ref[CONV_HALO + ts:2 * CONV_HALO + ts, :] = jnp.where(i < n - 1, glu(avn_ref, agn_ref), 0.0)

    off = CONV_HALO - CONV_WIDTH // 2
    for r in range(ts // rows):
        acc = jnp.broadcast_to(b_ref[...], (rows, b_ref.shape[-1]))
        for k in range(CONV_WIDTH):
            lo = r * rows + k + off
            acc = acc + w_ref[k:k + 1, :] * buf_ref[lo:lo + rows, :]
        mu = jnp.mean(acc, axis=-1, keepdims=True)
        dlt = acc - mu
        var = jnp.mean(dlt * dlt, axis=-1, keepdims=True)
        y = dlt * lax.rsqrt(var + LN_EPS) * lg_ref[...] + lb_ref[...]
        o_ref[r * rows:(r + 1) * rows, :] = _silu(y).astype(o_ref.dtype)


def conformer_conv(z, conv_w, conv_b, cln_g, cln_b, ts=256, rows=32):
    s = z.shape[0]
    c = conv_w.shape[1]
    ts = min(ts, s)
    hb = ts // CONV_HALO
    nh = s // CONV_HALO
    main = lambda col: pl.BlockSpec((ts, c), lambda i: (i, col))
    prev = lambda col: pl.BlockSpec((CONV_HALO, c), lambda i: (jnp.maximum(i * hb - 1, 0), col))
    nxt = lambda col: pl.BlockSpec((CONV_HALO, c), lambda i: (jnp.minimum((i + 1) * hb, nh - 1), col))
    vec = lambda r: pl.BlockSpec((r, c), lambda i: (0, 0))
    return pl.pallas_call(
        functools.partial(_conv_kernel, rows=min(rows, ts)),
        out_shape=jax.ShapeDtypeStruct((s, c), BF16),
        grid=(s // ts,),
        in_specs=[main(0), main(1), prev(0), prev(1), nxt(0), nxt(1),
                  vec(CONV_WIDTH), vec(1), vec(1), vec(1)],
        out_specs=pl.BlockSpec((ts, c), lambda i: (i, 0)),
        scratch_shapes=[pltpu.VMEM((ts + 2 * CONV_HALO, c), F32)],
        compiler_params=_cp(("parallel",)),
        name="conformer_conv",
    )(z, z, z, z, z, z, conv_w, conv_b.reshape(1, c), cln_g.reshape(1, c), cln_b.reshape(1, c))


def _t5_bucket(rel):
    half = N_BUCKETS // 2
    exact = half // 2
    n = jnp.abs(rel)
    large = exact + (jnp.log(jnp.maximum(n, 1).astype(F32) / exact)
                     / math.log(T5_MAX_DIST / exact) * (half - exact)).astype(jnp.int32)
    large = jnp.minimum(large, half - 1)
    return jnp.where(rel > 0, half, 0) + jnp.where(n < exact, n, large)


def _band_bias(t5_table, dil, qb):
    dist = jnp.arange(qb + 2 * DIL_HALF)[None, :] - DIL_HALF - jnp.arange(qb)[:, None]
    bias = jnp.transpose(t5_table[_t5_bucket(dist * dil)], (2, 0, 1)).astype(F32)
    return jnp.where((jnp.abs(dist) <= DIL_HALF)[None], bias, NEG_INF)


def _band_kernel(*refs, qb, first, last):
    q_ref, kp_ref, kc_ref, kn_ref, vp_ref, vc_ref, vn_ref, bias_ref = refs[:8]
    pos = 8
    if not first:
        acc_in, m_in, l_in = refs[pos:pos + 3]
        pos += 3
    if last:
        o_ref = refs[pos]
        pos += 1
    else:
        acc_out, m_out, l_out = refs[pos:pos + 3]
        pos += 3
    kcat_ref, vcat_ref = refs[pos:pos + 2]

    jb = pl.program_id(2)
    njb = pl.num_programs(2)
    tj = kc_ref.shape[0]
    hd = DIL_HEAD_DIM
    nh = q_ref.shape[1] // hd
    kw_len = qb + 2 * DIL_HALF

    kcat_ref[0:DIL_HALF, :] = kp_ref[...]
    kcat_ref[DIL_HALF:DIL_HALF + tj, :] = kc_ref[...]
    kcat_ref[DIL_HALF + tj:, :] = kn_ref[...]
    vcat_ref[0:DIL_HALF, :] = vp_ref[...]
    vcat_ref[DIL_HALF:DIL_HALF + tj, :] = vc_ref[...]
    vcat_ref[DIL_HALF + tj:, :] = vn_ref[...]

    col = lax.broadcasted_iota(jnp.int32, (1, kw_len), 1)
    lo_pen = jnp.where((col < DIL_HALF) & (jb == 0), NEG_INF, 0.0)
    hi_pen = jnp.where((col >= qb + DIL_HALF) & (jb == njb - 1), NEG_INF, 0.0)

    nsb = tj // qb
    for sb in range(nsb):
        rows = slice(sb * qb, (sb + 1) * qb)
        outs, ms, ls = [], [], []
        for h in range(nh):
            lanes = slice(h * hd, (h + 1) * hd)
            q = q_ref[rows, lanes]
            kw = kcat_ref[sb * qb:sb * qb + kw_len, lanes]
            vw = vcat_ref[sb * qb:sb * qb + kw_len, lanes]
            s = lax.dot_general(q, kw, (((1,), (1,)), ((), ())), preferred_element_type=F32)
            s = s * (hd ** -0.5) + bias_ref[h]
            if sb == 0:
                s = s + lo_pen
            if sb == nsb - 1:
                s = s + hi_pen
            m = jnp.max(s, axis=-1, keepdims=True)
            p = jnp.exp(s - m)
            l = jnp.sum(p, axis=-1, keepdims=True)
            acc = jnp.dot(p.astype(BF16), vw, preferred_element_type=F32)
            m = jnp.broadcast_to(m, (qb, hd))
            l = jnp.broadcast_to(l, (qb, hd))
            if not first:
                m_prev = m_in[rows, lanes]
                m_new = jnp.maximum(m_prev, m)
                alpha = jnp.exp(m_prev - m_new)
                beta = jnp.exp(m - m_new)
                acc = acc_in[rows, lanes] * alpha + acc * beta
                l = l_in[rows, lanes] * alpha + l * beta
                m = m_new
            if last:
                outs.append(acc / l)
            else:
                outs.append(acc)
                ms.append(m)
                ls.append(l)
        if last:
            o_ref[rows, :] = jnp.concatenate(outs, axis=-1).astype(o_ref.dtype)
        else:
            acc_out[rows, :] = jnp.concatenate(outs, axis=-1)
            m_out[rows, :] = jnp.concatenate(ms, axis=-1)
            l_out[rows, :] = jnp.concatenate(ls, axis=-1)


def dilated_attention(z, t5_table, col0, width, tj=512, qb=128, hg_lanes=256):
    s, zw = z.shape
    nhg = width // hg_lanes
    heads_per = hg_lanes // DIL_HEAD_DIM
    cb = lambda base: base // hg_lanes
    state = None
    for ci, (_, dil) in enumerate(DIL_CONFIGS):
        first, last = ci == 0, ci == len(DIL_CONFIGS) - 1
        sub = s // dil
        tjc = min(tj, sub)
        qbc = min(qb, tjc)
        hb = tjc // DIL_HALF
        nhalo = sub // DIL_HALF
        zv = z.reshape(sub, dil * zw)
        bias = _band_bias(t5_table, dil, qbc)

        def colblk(r, hg, base):
            return r * cb(zw) + cb(base) + hg

        def cur(base):
            return pl.BlockSpec((tjc, hg_lanes), lambda hg, r, jb, base=base: (jb, colblk(r, hg, base)))

        def prev(base):
            return pl.BlockSpec((DIL_HALF, hg_lanes),
                                lambda hg, r, jb, base=base: (jnp.maximum(jb * hb - 1, 0), colblk(r, hg, base)))

        def nxt(base):
            return pl.BlockSpec((DIL_HALF, hg_lanes),
                                lambda hg, r, jb, base=base: (jnp.minimum((jb + 1) * hb, nhalo - 1),
                                                              colblk(r, hg, base)))

        qc, kc, vc = col0, col0 + width, col0 + 2 * width
        st_spec = pl.BlockSpec((tjc, hg_lanes), lambda hg, r, jb: (jb, r * nhg + hg))
        in_specs = [cur(qc), prev(kc), cur(kc), nxt(kc), prev(vc), cur(vc), nxt(vc),
                    pl.BlockSpec((heads_per, qbc, qbc + 2 * DIL_HALF), lambda hg, r, jb: (hg, 0, 0))]
        args = [zv] * 7 + [bias]
        if not first:
            in_specs += [st_spec] * 3
            args += [t.reshape(sub, dil * width) for t in state]
        if last:
            out_shape = jax.ShapeDtypeStruct((sub, dil * width), BF16)
            out_specs = st_spec
        else:
            out_shape = [jax.ShapeDtypeStruct((sub, dil * width), F32)] * 3
            out_specs = [st_spec] * 3
        res = pl.pallas_call(
            functools.partial(_band_kernel, qb=qbc, first=first, last=last),
            out_shape=out_shape,
            grid=(nhg, dil, sub // tjc),
            in_specs=in_specs,
            out_specs=out_specs,
            scratch_shapes=[pltpu.VMEM((tjc + 2 * DIL_HALF, hg_lanes), BF16)] * 2,
            compiler_params=_cp(("parallel", "parallel", "parallel")),
            name=f"band_attention_dil{dil}",
        )(*args)
        if last:
            return res.reshape(s, width)
        state = [t.reshape(s, width) for t in res]


def _oproj_kernel(a1_ref, a2_ref, w_ref, x_ref, g_ref, gt_ref, o_ref):
    k1 = a1_ref.shape[1]
    y = jnp.dot(a1_ref[...], w_ref[0:k1, :], preferred_element_type=F32)
    y = y + jnp.dot(a2_ref[...], w_ref[k1:, :], preferred_element_type=F32)
    o_ref[...] = x_ref[...] + gt_ref[...] * (_rms(y) * g_ref[...])


def out_proj_residual(a1, a2, w, x, g, gt, tm=512):
    s, d = x.shape
    k1, k2 = a1.shape[1], a2.shape[1]
    tm = min(tm, s)
    vec = pl.BlockSpec((1, d), lambda i: (0, 0))
    return pl.pallas_call(
        _oproj_kernel,
        out_shape=jax.ShapeDtypeStruct((s, d), F32),
        grid=(s // tm,),
        in_specs=[pl.BlockSpec((tm, k1), lambda i: (i, 0)), pl.BlockSpec((tm, k2), lambda i: (i, 0)),
                  pl.BlockSpec((k1 + k2, d), lambda i: (0, 0)),
                  pl.BlockSpec((tm, d), lambda i: (i, 0)), vec, vec],
        out_specs=pl.BlockSpec((tm, d), lambda i: (i, 0)),
        compiler_params=_cp(("parallel",)),
        name="out_proj_residual",
    )(a1, a2, w, x, g, gt)


def _ffn_kernel(x_ref, g_ref, sh_ref, sc_ref, g2_ref, gt_ref, w1_ref, w3_ref, w2_ref, o_ref, h_ref, acc_ref):
    f = pl.program_id(1)

    @pl.when(f == 0)
    def _():
        h_ref[...] = _modnorm(x_ref[...], g_ref[...], sh_ref[...], sc_ref[...]).astype(h_ref.dtype)
        acc_ref[...] = jnp.zeros_like(acc_ref)

    h = h_ref[...]
    u = jnp.dot(h, w1_ref[...], preferred_element_type=F32)
    v = jnp.dot(h, w3_ref[...], preferred_element_type=F32)
    acc_ref[...] += jnp.dot((_silu(u) * v).astype(BF16), w2_ref[...], preferred_element_type=F32)

    @pl.when(f == pl.num_programs(1) - 1)
    def _():
        o_ref[...] = x_ref[...] + gt_ref[...] * (_rms(acc_ref[...]) * g2_ref[...])


def ffn_sublayer(x, g, sh, sc, g2, gt, w1, w3, w2, tm=512, tf=512):
    s, d = x.shape
    f = w1.shape[1]
    tm, tf = min(tm, s), min(tf, f)
    vec = pl.BlockSpec((1, d), lambda i, j: (0, 0))
    return pl.pallas_call(
        _ffn_kernel,
        out_shape=jax.ShapeDtypeStruct((s, d), F32),
        grid=(s // tm, f // tf),
        in_specs=[pl.BlockSpec((tm, d), lambda i, j: (i, 0)), vec, vec, vec, vec, vec,
                  pl.BlockSpec((d, tf), lambda i, j: (0, j)), pl.BlockSpec((d, tf), lambda i, j: (0, j)),
                  pl.BlockSpec((tf, d), lambda i, j: (j, 0))],
        out_specs=pl.BlockSpec((tm, d), lambda i, j: (i, 0)),
        scratch_shapes=[pltpu.VMEM((tm, d), BF16), pltpu.VMEM((tm, d), F32)],
        compiler_params=_cp(("parallel", "arbitrary")),
        name="ffn_sublayer",
    )(x, g, sh, sc, g2, gt, w1, w3, w2)


def _rope_slab(t, c_ref, s1_ref, s2_ref):
    return (t * c_ref[...] + pltpu.roll(t, LANE - ROPE_DIM // 2, 1) * s1_ref[...]
            + pltpu.roll(t, ROPE_DIM // 2, 1) * s2_ref[...])


def _oddin_kernel(x_ref, g_ref, sh_ref, sc_ref, win_ref, qg_ref, wuq_ref, kvg_ref, wuk_ref, wuv_ref,
                  c_ref, s1_ref, s2_ref, q_ref, k_ref, v_ref, u_ref, *, q_lora, kv_lora, scale):
    h = _modnorm(x_ref[...], g_ref[...], sh_ref[...], sc_ref[...]).astype(BF16)
    z = jnp.dot(h, win_ref[...], preferred_element_type=F32)
    o_kv, o_pe, o_u = q_lora, q_lora + kv_lora, q_lora + kv_lora + LANE
    u_ref[...] = z[:, o_u:]

    qn = (_rms(z[:, :q_lora]) * qg_ref[...]).astype(BF16)
    q = jnp.dot(qn, wuq_ref[...], preferred_element_type=F32)
    kvn = (_rms(z[:, o_kv:o_pe]) * kvg_ref[...]).astype(BF16)
    kn = jnp.dot(kvn, wuk_ref[...], preferred_element_type=F32)
    v_ref[...] = jnp.dot(kvn, wuv_ref[...], preferred_element_type=F32).astype(v_ref.dtype)
    kpe = _rope_slab(z[:, o_pe:o_u], c_ref, s1_ref, s2_ref).astype(k_ref.dtype)

    for hh in range(MLA_HEADS):
        b = hh * MLA_QK_PAD
        q_ref[:, b:b + NOPE_DIM] = (q[:, b:b + NOPE_DIM] * scale).astype(q_ref.dtype)
        qpe = _rope_slab(q[:, b + NOPE_DIM:b + MLA_QK_PAD], c_ref, s1_ref, s2_ref)
        q_ref[:, b + NOPE_DIM:b + MLA_QK_PAD] = (qpe * scale).astype(q_ref.dtype)
        k_ref[:, b:b + NOPE_DIM] = kn[:, hh * NOPE_DIM:(hh + 1) * NOPE_DIM].astype(k_ref.dtype)
        k_ref[:, b + NOPE_DIM:b + MLA_QK_PAD] = kpe


def _rope_tables(seq):
    pos = jnp.arange(seq, dtype=F32)
    inv_freq = ROPE_BASE ** (-jnp.arange(0, ROPE_DIM, 2, dtype=F32) / ROPE_DIM)
    ang = pos[:, None] * inv_freq[None, :]
    cos, sin = jnp.cos(ang), jnp.sin(ang)
    zero = jnp.zeros_like(cos)
    c = jnp.concatenate([cos, cos, zero, zero], axis=-1)
    s1 = jnp.concatenate([-sin, zero, zero, zero], axis=-1)
    s2 = jnp.concatenate([zero, sin, zero, zero], axis=-1)
    return c, s1, s2


def odd_in_proj(x, g, sh, sc, w_in, q_norm_g, w_uq, kv_norm_g, w_ukv, tm=256):
    s, d = x.shape
    q_lora, kv_lora = q_norm_g.shape[0], kv_norm_g.shape[0]
    s5_dim = w_in.shape[1] - q_lora - kv_lora - ROPE_DIM
    tm = min(tm, s)
    w_in_p = jnp.concatenate(
        [w_in[:, :q_lora + kv_lora + ROPE_DIM], jnp.zeros((d, LANE - ROPE_DIM), w_in.dtype),
         w_in[:, q_lora + kv_lora + ROPE_DIM:]], axis=1).astype(BF16)
    wq = w_uq.reshape(q_lora, MLA_HEADS, NOPE_DIM + ROPE_DIM)
    wq = jnp.pad(wq, ((0, 0), (0, 0), (0, MLA_QK_PAD - NOPE_DIM - ROPE_DIM)))
    wq = wq.reshape(q_lora, MLA_HEADS * MLA_QK_PAD).astype(BF16)
    wkv = w_ukv.reshape(kv_lora, MLA_HEADS, NOPE_DIM + V_DIM)
    wuk = wkv[:, :, :NOPE_DIM].reshape(kv_lora, MLA_HEADS * NOPE_DIM).astype(BF16)
    wuv = wkv[:, :, NOPE_DIM:].reshape(kv_lora, MLA_HEADS * V_DIM).astype(BF16)
    c, s1, s2 = _rope_tables(s)
    scale = (NOPE_DIM + ROPE_DIM) ** -0.5

    full = lambda a: pl.BlockSpec(a.shape, lambda i: (0, 0))
    row = lambda w: pl.BlockSpec((tm, w), lambda i: (i, 0))
    vec = pl.BlockSpec((1, d), lambda i: (0, 0))
    qg = q_norm_g.reshape(1, q_lora)
    kvg = kv_norm_g.reshape(1, kv_lora)
    return pl.pallas_call(
        functools.partial(_oddin_kernel, q_lora=q_lora, kv_lora=kv_lora, scale=scale),
        out_shape=[jax.ShapeDtypeStruct((s, MLA_HEADS * MLA_QK_PAD), BF16),
                   jax.ShapeDtypeStruct((s, MLA_HEADS * MLA_QK_PAD), BF16),
                   jax.ShapeDtypeStruct((s, MLA_HEADS * V_DIM), BF16),
                   jax.ShapeDtypeStruct((s, s5_dim), F32)],
        grid=(s // tm,),
        in_specs=[row(d), vec, vec, vec, full(w_in_p), full(qg), full(wq), full(kvg), full(wuk), full(wuv),
                  row(LANE), row(LANE), row(LANE)],
        out_specs=[row(MLA_HEADS * MLA_QK_PAD), row(MLA_HEADS * MLA_QK_PAD), row(MLA_HEADS * V_DIM),
                   row(s5_dim)],
        compiler_params=_cp(("parallel",)),
        name="odd_in_proj",
    )(x, g, sh, sc, w_in_p, qg, wq, kvg, wuk, wuv, c, s1, s2)


def _flash_kernel(q_ref, k_ref, v_ref, o_ref, m_ref, l_ref, acc_ref):
    j = pl.program_id(2)

    @pl.when(j == 0)
    def _():
        m_ref[...] = jnp.full_like(m_ref, -jnp.inf)
        l_ref[...] = jnp.zeros_like(l_ref)
        acc_ref[...] = jnp.zeros_like(acc_ref)

    s = lax.dot_general(q_ref[...], k_ref[...], (((1,), (1,)), ((), ())), preferred_element_type=F32)
    m_prev = m_ref[...]
    m_new = jnp.maximum(m_prev, jnp.max(s, axis=-1, keepdims=True))
    p = jnp.exp(s - m_new[:, :1])
    alpha = jnp.exp(m_prev - m_new)
    l_ref[...] = alpha * l_ref[...] + jnp.sum(p, axis=-1, keepdims=True)
    acc_ref[...] = alpha * acc_ref[...] + jnp.dot(p.astype(BF16), v_ref[...], preferred_element_type=F32)
    m_ref[...] = m_new

    @pl.when(j == pl.num_programs(2) - 1)
    def _():
        o_ref[...] = (acc_ref[...] / l_ref[...]).astype(o_ref.dtype)


def mla_attention(q, k, v, tq=1024, tk=512):
    s = q.shape[0]
    tq, tk = min(tq, s), min(tk, s)
    return pl.pallas_call(
        _flash_kernel,
        out_shape=jax.ShapeDtypeStruct((s, MLA_HEADS * V_DIM), BF16),
        grid=(MLA_HEADS, s // tq, s // tk),
        in_specs=[pl.BlockSpec((tq, MLA_QK_PAD), lambda h, i, j: (i, h)),
                  pl.BlockSpec((tk, MLA_QK_PAD), lambda h, i, j: (j, h)),
                  pl.BlockSpec((tk, V_DIM), lambda h, i, j: (j, h))],
        out_specs=pl.BlockSpec((tq, V_DIM), lambda h, i, j: (i, h)),
        scratch_shapes=[pltpu.VMEM((tq, V_DIM), F32)] * 3,
        compiler_params=_cp(("parallel", "parallel", "arbitrary")),
        name="mla_flash_attention",
    )(q, k, v)


def _s5_matrices(a_re, a_im, log_dt, b_re, b_im, c_re, c_im, chunk):
    L = chunk
    A = lax.complex(a_re.astype(F32), a_im.astype(F32))
    dt = jnp.exp(log_dt.astype(F32))[..., None]
    adt = A * dt
    a_bar = jnp.exp(adt)
    b_bar = ((a_bar - 1.0) / A)[..., None] * lax.complex(b_re.astype(F32), b_im.astype(F32))
    c_c = lax.complex(c_re.astype(F32), c_im.astype(F32))
    kk = jnp.arange(L + 1, dtype=F32)
    apow = jnp.exp(adt[:, :, None, :] * kk[None, None, :, None].astype(jnp.complex64))
    g, p, gc = b_bar.shape[1], b_bar.shape[2], b_bar.shape[3]

    ker = jnp.real(jnp.einsum('dgcp,dgkp,dgpi->dgkci', c_c, apow[:, :, :L], b_bar))
    lag = jnp.arange(L)[None, :] - jnp.arange(L)[:, None]
    tf = jnp.where((lag >= 0)[None, :, :, None, None], ker[0][:, jnp.clip(lag, 0, L - 1)], 0.0)
    tb = jnp.where((lag <= 0)[None, :, :, None, None], ker[1][:, jnp.clip(-lag, 0, L - 1)], 0.0)
    t_mat = jnp.transpose(tf + tb, (0, 1, 4, 2, 3)).reshape(g, L * gc, L * gc)

    pf = apow[0][:, ::-1][:, 1:][:, :, :, None] * b_bar[0][:, None]
    pb = apow[1][:, :L][:, :, :, None] * b_bar[1][:, None]
    def p_lay(m):
        return jnp.transpose(m, (0, 1, 3, 2)).reshape(g, L * gc, p)
    p_mat = jnp.concatenate([p_lay(jnp.real(pf)), p_lay(jnp.imag(pf)),
                             p_lay(jnp.real(pb)), p_lay(jnp.imag(pb))], axis=-1)

    wf = c_c[0][:, None] * apow[0][:, 1:][:, :, None, :]
    wb = c_c[1][:, None] * apow[1][:, ::-1][:, :L][:, :, None, :]
    def q_lay(m):
        return jnp.transpose(m, (0, 3, 1, 2)).reshape(g, p, L * gc)
    q_mat = jnp.concatenate([q_lay(jnp.real(wf)), q_lay(-jnp.imag(wf)),
                             q_lay(jnp.real(wb)), q_lay(-jnp.imag(wb))], axis=1)

    al = apow[:, :, L]
    dec = jnp.stack([jnp.real(al[0]), jnp.imag(al[0]), jnp.real(al[1]), jnp.imag(al[1])]).reshape(4, g * p)
    return t_mat, p_mat, q_mat, dec


def _s5_state_kernel(u_ref, p_ref, fre_ref, fim_ref, bre_ref, bim_ref):
    outs = [[], [], [], []]
    for gi in range(u_ref.shape[0]):
        xe = jnp.dot(u_ref[gi].astype(BF16), p_ref[gi], preferred_element_type=F32)
        for part in range(4):
            outs[part].append(xe[:, part * S5_STATE:(part + 1) * S5_STATE])
    for part, ref in enumerate((fre_ref, fim_ref, bre_ref, bim_ref)):
        ref[...] = jnp.concatenate(outs[part], axis=-1)


def _s5_scan_kernel(fre_ref, fim_ref, bre_ref, bim_ref, dec_ref, ofre_ref, ofim_ref, obre_ref, obim_ref):
    nc = fre_ref.shape[0]
    w = fre_ref.shape[1]
    far, fai, bar, bai = (dec_ref[i:i + 1, :] for i in range(4))

    def fwd(c, st):
        re, im = st
        ofre_ref[pl.ds(c, 1), :] = re
        ofim_ref[pl.ds(c, 1), :] = im
        return (far * re - fai * im + fre_ref[pl.ds(c, 1), :],
                far * im + fai * re + fim_ref[pl.ds(c, 1), :])

    def bwd(i, st):
        c = nc - 1 - i
        re, im = st
        obre_ref[pl.ds(c, 1), :] = re
        obim_ref[pl.ds(c, 1), :] = im
        return (bar * re - bai * im + bre_ref[pl.ds(c, 1), :],
                bar * im + bai * re + bim_ref[pl.ds(c, 1), :])

    zero = (jnp.zeros((1, w), F32), jnp.zeros((1, w), F32))
    lax.fori_loop(0, nc, fwd, zero)
    lax.fori_loop(0, nc, bwd, zero)


def _s5_out_kernel(u_ref, t_ref, q_ref, fre_ref, fim_ref, bre_ref, bim_ref, y_ref):
    for gi in range(u_ref.shape[0]):
        lanes = slice(gi * S5_STATE, (gi + 1) * S5_STATE)
        y = jnp.dot(u_ref[gi].astype(BF16), t_ref[gi], preferred_element_type=F32)
        xin = jnp.concatenate([r[:, lanes] for r in (fre_ref, fim_ref, bre_ref, bim_ref)], axis=-1)
        y_ref[gi] = y + jnp.dot(xin.astype(BF16), q_ref[gi], preferred_element_type=F32)


def s5_bidirectional(u, a_re, a_im, log_dt, b_re, b_im, c_re, c_im, chunk=S5_CHUNK):
    s, dim = u.shape
    gc, p = S5_GROUP, S5_STATE
    g = dim // gc
    chunk = min(chunk, s)
    nc = s // chunk
    gp = LANE // p
    t_mat, p_mat, q_mat, dec = _s5_matrices(a_re, a_im, log_dt, b_re, b_im, c_re, c_im, chunk)
    t_mat, p_mat, q_mat = t_mat.astype(BF16), p_mat.astype(BF16), q_mat.astype(BF16)
    u_t = jnp.transpose(u.reshape(nc, chunk, g, gc), (2, 0, 1, 3)).reshape(g, nc, chunk * gc)

    grp = lambda a, b: pl.BlockSpec((gp, a, b), lambda i: (i, 0, 0))
    st = pl.BlockSpec((nc, LANE), lambda i: (0, i))
    st_shape = jax.ShapeDtypeStruct((nc, g * p), F32)
    xe = pl.pallas_call(
        _s5_state_kernel,
        out_shape=[st_shape] * 4,
        grid=(g // gp,),
        in_specs=[grp(nc, chunk * gc), grp(chunk * gc, 4 * p)],
        out_specs=[st] * 4,
        compiler_params=_cp(("parallel",)),
        name="s5_chunk_states",
    )(u_t, p_mat)

    whole = lambda shape: pl.BlockSpec(shape, lambda: (0,) * len(shape))
    xin = pl.pallas_call(
        _s5_scan_kernel,
        out_shape=[st_shape] * 4,
        in_specs=[whole((nc, g * p))] * 4 + [whole((4, g * p))],
        out_specs=[whole((nc, g * p))] * 4,
        compiler_params=pltpu.CompilerParams(vmem_limit_bytes=VMEM_LIMIT_MB * 1024 * 1024),
        name="s5_boundary_scan",
    )(*xe, dec)

    y_t = pl.pallas_call(
        _s5_out_kernel,
        out_shape=jax.ShapeDtypeStruct((g, nc, chunk * gc), F32),
        grid=(g // gp,),
        in_specs=[grp(nc, chunk * gc), grp(chunk * gc, chunk * gc), grp(4 * p, chunk * gc)] + [st] * 4,
        out_specs=grp(nc, chunk * gc),
        compiler_params=_cp(("parallel",)),
        name="s5_outputs",
    )(u_t, t_mat, q_mat, *xin)
    return jnp.transpose(y_t.reshape(g, nc, chunk, gc), (1, 2, 0, 3)).reshape(s, dim)


def _s5_gate_kernel(y_ref, u_ref, d_ref, w_ref, o_ref):
    y = y_ref[...] + d_ref[...] * u_ref[...]
    y = 0.5 * y * (1.0 + jnp.tanh(math.sqrt(2.0 / math.pi) * (y + 0.044715 * (y * y * y))))
    z = jnp.dot(y.astype(BF16), w_ref[...], preferred_element_type=F32)
    o_ref[...] = (y * _sigmoid(z)).astype(o_ref.dtype)


def s5_gate(y, u, d_skip, w_glu, tm=1024):
    s, dim = y.shape
    tm = min(tm, s)
    row = pl.BlockSpec((tm, dim), lambda i: (i, 0))
    return pl.pallas_call(
        _s5_gate_kernel,
        out_shape=jax.ShapeDtypeStruct((s, dim), BF16),
        grid=(s // tm,),
        in_specs=[row, row, pl.BlockSpec((1, dim), lambda i: (0, 0)), pl.BlockSpec((dim, dim), lambda i: (0, 0))],
        out_specs=row,
        compiler_params=_cp(("parallel",)),
        name="s5_gate",
    )(y, u, d_skip.reshape(1, dim), w_glu.astype(BF16))


def _router_kernel(x_ref, g_ref, sh_ref, sc_ref, rw_ref, h_ref, info_ref):
    h = _modnorm(x_ref[...], g_ref[...], sh_ref[...], sc_ref[...])
    h_ref[...] = h.astype(h_ref.dtype)
    logits = jnp.dot(h, rw_ref[...], preferred_element_type=F32, precision=lax.Precision.HIGHEST)
    lane = lax.broadcasted_iota(jnp.int32, logits.shape, 1)
    logits = jnp.where(lane < N_EXPERTS, logits, -jnp.inf)
    m1 = jnp.max(logits, axis=-1, keepdims=True)
    i1 = jnp.min(jnp.where(logits == m1, lane, LANE), axis=-1, keepdims=True)
    rest = jnp.where(lane == i1, -jnp.inf, logits)
    m2 = jnp.max(rest, axis=-1, keepdims=True)
    i2 = jnp.min(jnp.where(rest == m2, lane, LANE), axis=-1, keepdims=True)
    e = jnp.exp(m2 - m1)
    g1 = 1.0 / (1.0 + e)
    g2 = e / (1.0 + e)
    info = jnp.where(lane == 0, i1.astype(F32),
                     jnp.where(lane == 1, i2.astype(F32),
                               jnp.where(lane == 2, g1, jnp.where(lane == 3, g2, 0.0))))
    info_ref[...] = info


def moe_router(x, g, sh, sc, router_w, tm=512):
    s, d = x.shape
    tm = min(tm, s)
    rw = jnp.pad(router_w, ((0, 0), (0, LANE - router_w.shape[1])))
    vec = pl.BlockSpec((1, d), lambda i: (0, 0))
    return pl.pallas_call(
        _router_kernel,
        out_shape=[jax.ShapeDtypeStruct((s, d), BF16), jax.ShapeDtypeStruct((s, LANE), F32)],
        grid=(s // tm,),
        in_specs=[pl.BlockSpec((tm, d), lambda i: (i, 0)), vec, vec, vec,
                  pl.BlockSpec((d, LANE), lambda i: (0, 0))],
        out_specs=[pl.BlockSpec((tm, d), lambda i: (i, 0)), pl.BlockSpec((tm, LANE), lambda i: (i, 0))],
        compiler_params=_cp(("parallel",)),
        name="moe_router",
    )(x, g, sh, sc, rw)


def _expert_kernel(te_ref, tv_ref, h_ref, w1_ref, w3_ref, w2_ref, o_ref, acc_ref):
    t = pl.program_id(0)
    f = pl.program_id(1)
    valid = tv_ref[t] > 0

    @pl.when(f == 0)
    def _():
        acc_ref[...] = jnp.zeros_like(acc_ref)

    @pl.when(valid)
    def _():
        h = h_ref[...]
        u = jnp.dot(h, w1_ref[...].astype(BF16), preferred_element_type=F32)
        v = jnp.dot(h, w3_ref[...].astype(BF16), preferred_element_type=F32)
        acc_ref[...] += jnp.dot((_silu(u) * v).astype(BF16), w2_ref[...].astype(BF16),
                                preferred_element_type=F32)

    @pl.when(f == pl.num_programs(1) - 1)
    def _():
        o_ref[...] = acc_ref[...].astype(o_ref.dtype)


def expert_ffn(h_sorted, tile_expert, tile_valid, w1, w3, w2, tm, tf=256):
    n, d = h_sorted.shape
    f = w1.shape[2]
    tf = min(tf, f)
    nf = f // tf

    def fblk(j, t, tv):
        return jnp.where(tv[t] > 0, j, nf - 1)

    return pl.pallas_call(
        _expert_kernel,
        out_shape=jax.ShapeDtypeStruct((n, d), BF16),
        grid_spec=pltpu.PrefetchScalarGridSpec(
            num_scalar_prefetch=2,
            grid=(n // tm, nf),
            in_specs=[pl.BlockSpec((tm, d), lambda t, j, te, tv: (t, 0)),
                      pl.BlockSpec((None, d, tf), lambda t, j, te, tv: (te[t], 0, fblk(j, t, tv))),
                      pl.BlockSpec((None, d, tf), lambda t, j, te, tv: (te[t], 0, fblk(j, t, tv))),
                      pl.BlockSpec((None, tf, d), lambda t, j, te, tv: (te[t], fblk(j, t, tv), 0))],
            out_specs=pl.BlockSpec((tm, d), lambda t, j, te, tv: (t, 0)),
            scratch_shapes=[pltpu.VMEM((tm, d), F32)]),
        compiler_params=_cp(("arbitrary", "arbitrary")),
        name="expert_ffn",
    )(tile_expert, tile_valid, h_sorted, w1, w3, w2)


def _dispatch(idx, tm):
    s = idx.shape[0]
    e_flat = idx.reshape(-1)
    onehot = (e_flat[:, None] == jnp.arange(N_EXPERTS)[None, :]).astype(jnp.int32)
    csum = jnp.cumsum(onehot, axis=0)
    rank = jnp.sum((csum - onehot) * onehot, axis=1)
    counts = csum[-1]
    padded = ((counts + tm - 1) // tm) * tm
    gend = jnp.cumsum(padded)
    slot = (gend - padded)[e_flat] + rank
    nt = (2 * s) // tm + N_EXPERTS
    src = jnp.zeros((nt * tm,), jnp.int32).at[slot].set(jnp.arange(2 * s, dtype=jnp.int32) // 2)
    tstart = jnp.arange(nt, dtype=jnp.int32) * tm
    valid = tstart < gend[-1]
    te = jnp.minimum(jnp.sum((tstart[:, None] >= gend[None, :]).astype(jnp.int32), axis=1), N_EXPERTS - 1)
    nvalid = gend[-1] // tm
    te = jnp.where(valid, te, te[jnp.maximum(nvalid - 1, 0)])
    return slot.reshape(s, 2), src, te.astype(jnp.int32), valid.astype(jnp.int32)


def _combine_kernel(ya_ref, yb_ref, info_ref, x_ref, g_ref, gt_ref, o_ref):
    info = info_ref[...]
    y = info[:, 2:3] * ya_ref[...].astype(F32) + info[:, 3:4] * yb_ref[...].astype(F32)
    o_ref[...] = x_ref[...] + gt_ref[...] * (_rms(y) * g_ref[...])


def moe_combine(ya, yb, info, x, g, gt, tm=512):
    s, d = x.shape
    tm = min(tm, s)
    row = lambda w: pl.BlockSpec((tm, w), lambda i: (i, 0))
    vec = pl.BlockSpec((1, d), lambda i: (0, 0))
    return pl.pallas_call(
        _combine_kernel,
        out_shape=jax.ShapeDtypeStruct((s, d), F32),
        grid=(s // tm,),
        in_specs=[row(d), row(d), row(LANE), row(d), vec, vec],
        out_specs=row(d),
        compiler_params=_cp(("parallel",)),
        name="moe_combine",
    )(ya, yb, info, x, g, gt)


def moe_sublayer(x, g, sh, sc, g2, gt, router_w, w1, w3, w2, tm_e=1024):
    s = x.shape[0]
    tm_e = min(tm_e, s)
    h, info = moe_router(x, g, sh, sc, router_w)
    idx = info[:, :2].astype(jnp.int32)
    slot, src, te, tv = _dispatch(idx, tm_e)
    y_sorted = expert_ffn(jnp.take(h, src, axis=0), te, tv, w1, w3, w2, tm_e)
    ya = jnp.take(y_sorted, slot[:, 0], axis=0)
    yb = jnp.take(y_sorted, slot[:, 1], axis=0)
    return moe_combine(ya, yb, info, x, g2, gt)


def kernel(x, c, t5_table, norm_g, ada_w, ada_b, e_w_in, e_conv_w, e_conv_b, e_cln_g, e_cln_b, e_w_out, e_ffn_w1, e_ffn_w3, e_ffn_w2, o_w_in, o_q_norm_g, o_w_uq, o_kv_norm_g, o_w_ukv, s5_a_re, s5_a_im, s5_log_dt, s5_b_re, s5_b_im, s5_c_re, s5_c_im, s5_d, s5_w_glu, o_w_out, router_w, moe_w1, moe_w3, moe_w2):
    bsz, seq, d = x.shape
    assert bsz == 1
    depth = norm_g.shape[0]
    conv_dim = e_conv_w.shape[2]
    xs = x.reshape(seq, d)
    mod = adaln(c, ada_w, ada_b)
    for layer in range(depth):
        i = layer // 2
        sh_m, sc_m, gt_m, sh_f, sc_f, gt_f = (mod[layer, :, k * d:(k + 1) * d] for k in range(6))
        g = [norm_g[layer, k].reshape(1, d) for k in range(4)]
        if layer % 2 == 0:
            z = norm_mod_matmul(xs, g[0], sh_m, sc_m, e_w_in[i].astype(BF16))
            a = conformer_conv(z, e_conv_w[i], e_conv_b[i], e_cln_g[i], e_cln_b[i])
            o = dilated_attention(z, t5_table, 2 * conv_dim, (z.shape[1] - 2 * conv_dim) // 3)
            xs = out_proj_residual(a, o, e_w_out[i].astype(BF16), xs, g[1], gt_m)
            xs = ffn_sublayer(xs, g[2], sh_f, sc_f, g[3], gt_f, e_ffn_w1[i].astype(BF16),
                              e_ffn_w3[i].astype(BF16), e_ffn_w2[i].astype(BF16))
        else:
            q, k, v, u = odd_in_proj(xs, g[0], sh_m, sc_m, o_w_in[i], o_q_norm_g[i], o_w_uq[i],
                                     o_kv_norm_g[i], o_w_ukv[i])
            o_mla = mla_attention(q, k, v)
            y = s5_bidirectional(u, s5_a_re[i], s5_a_im[i], s5_log_dt[i], s5_b_re[i], s5_b_im[i],
                                 s5_c_re[i], s5_c_im[i])
            y = s5_gate(y, u, s5_d[i], s5_w_glu[i])
            xs = out_proj_residual(o_mla, y, o_w_out[i].astype(BF16), xs, g[1], gt_m)
            xs = moe_sublayer(xs, g[2], sh_f, sc_f, g[3], gt_f, router_w[i], moe_w1[i], moe_w3[i], moe_w2[i])
    return xs.reshape(bsz, seq, d)
```

```python
import functools
import math

import jax
import jax.numpy as jnp
from jax import lax
from jax.experimental import pallas as pl
from jax.experimental.pallas import tpu as pltpu

F32 = jnp.float32
BF16 = jnp.bfloat16

RMS_EPS = 1e-6
LN_EPS = 1e-5
NEG_INF = -1e30

CONV_WIDTH = 31
CONV_HALO = 16
DIL_HEAD_DIM = 64
DIL_CONFIGS = ((128, 1), (512, 4), (2048, 16))
DIL_HALF = 64
DIL_TILE = 1024
DIL_QB = 128
N_BUCKETS = 32
T5_MAX_DIST = DIL_CONFIGS[-1][0] // 2
MLA_HEADS = 12
NOPE_DIM = 128
ROPE_DIM = 64
V_DIM = 128
ROPE_BASE = 10000.0
MLA_QK_PAD = 256
S5_GROUP = 16
S5_STATE = 64
S5_CHUNK = 32
N_EXPERTS = 8
LANE = 128
VMEM_LIMIT_MB = 56


def _cp(sem, vmem_mb=VMEM_LIMIT_MB):
    return pltpu.CompilerParams(dimension_semantics=sem, vmem_limit_bytes=vmem_mb * 1024 * 1024)


def _rms(x):
    return x * lax.rsqrt(jnp.mean(x * x, axis=-1, keepdims=True) + RMS_EPS)


def _modnorm(x, g, sh, sc):
    return (_rms(x) * g) * (1.0 + sc) + sh


def _sigmoid(x):
    return 1.0 / (1.0 + jnp.exp(-x))


def _silu(x):
    return x * _sigmoid(x)


def _adaln_kernel(c_ref, w_ref, b_ref, o_ref, cb_ref):
    @pl.when((pl.program_id(0) == 0) & (pl.program_id(1) == 0))
    def _():
        cc = c_ref[...]
        cb_ref[...] = jnp.broadcast_to(_silu(cc), cb_ref.shape)

    cb = cb_ref[...]
    for j in range(o_ref.shape[-1] // LANE):
        sl = slice(j * LANE, (j + 1) * LANE)
        o_ref[:, sl] = jnp.sum(w_ref[:, sl] * cb, axis=0, keepdims=True) + b_ref[:, sl]


def adaln(c, ada_w, ada_b, tn=1024):
    nl, d, n = ada_w.shape
    out = pl.pallas_call(
        _adaln_kernel,
        out_shape=jax.ShapeDtypeStruct((nl, 1, n), F32),
        grid=(nl, n // tn),
        in_specs=[pl.BlockSpec((d, 1), lambda l, j: (0, 0)),
                  pl.BlockSpec((None, d, tn), lambda l, j: (l, 0, j)),
                  pl.BlockSpec((None, 1, tn), lambda l, j: (l, 0, j))],
        out_specs=pl.BlockSpec((None, 1, tn), lambda l, j: (l, 0, j)),
        scratch_shapes=[pltpu.VMEM((d, LANE), F32)],
        compiler_params=_cp(("arbitrary", "arbitrary")),
        name="adaln",
    )(c.reshape(d, 1), ada_w, ada_b.reshape(nl, 1, n))
    return out


def _nmm_kernel(x_ref, g_ref, sh_ref, sc_ref, w_ref, o_ref, h_ref):
    @pl.when(pl.program_id(1) == 0)
    def _():
        h_ref[...] = _modnorm(x_ref[...], g_ref[...], sh_ref[...], sc_ref[...]).astype(h_ref.dtype)

    o_ref[...] = jnp.dot(h_ref[...], w_ref[...], preferred_element_type=F32).astype(o_ref.dtype)


def norm_mod_matmul(x, g, sh, sc, w, tm=512, tn=1024):
    s, d = x.shape
    n = w.shape[1]
    tm, tn = min(tm, s), min(tn, n)
    vec = pl.BlockSpec((1, d), lambda i, j: (0, 0))
    return pl.pallas_call(
        _nmm_kernel,
        out_shape=jax.ShapeDtypeStruct((s, n), BF16),
        grid=(s // tm, n // tn),
        in_specs=[pl.BlockSpec((tm, d), lambda i, j: (i, 0)), vec, vec, vec,
                  pl.BlockSpec((d, tn), lambda i, j: (0, j))],
        out_specs=pl.BlockSpec((tm, tn), lambda i, j: (i, j)),
        scratch_shapes=[pltpu.VMEM((tm, d), BF16)],
        compiler_params=_cp(("parallel", "arbitrary")),
        name="norm_mod_matmul",
    )(x, g, sh, sc, w)


def _conv_kernel(av_ref, ag_ref, avp_ref, agp_ref, avn_ref, agn_ref, w_ref, b_ref, lg_ref, lb_ref,
                 o_ref, buf_ref, *, rows):
    i = pl.program_id(0)
    n = pl.num_programs(0)
    ts = av_ref.shape[0]

    def glu(v_ref, g_ref):
        return v_ref[...].astype(F32) * _sigmoid(g_ref[...].astype(F32))

    buf_ref[0:CONV_HALO, :] = jnp.where(i > 0, glu(avp_ref, agp_ref), 0.0)
    buf_ref[CONV_HALO:CONV_HALO + ts, :] = glu(av_ref, ag_ref)
    buf_ref[CONV_HALO + ts:2 * CONV_HALO + ts, :] = jnp.where(i < n - 1, glu(avn_ref, agn_ref), 0.0)

    off = CONV_HALO - CONV_WIDTH // 2
    for r in range(ts // rows):
        acc = jnp.broadcast_to(b_ref[...], (rows, b_ref.shape[-1]))
        for k in range(CONV_WIDTH):
            lo = r * rows + k + off
            acc = acc + w_ref[k:k + 1, :] * buf_ref[lo:lo + rows, :]
        mu = jnp.mean(acc, axis=-1, keepdims=True)
        dlt = acc - mu
        var = jnp.mean(dlt * dlt, axis=-1, keepdims=True)
        y = dlt * lax.rsqrt(var + LN_EPS) * lg_ref[...] + lb_ref[...]
        o_ref[r * rows:(r + 1) * rows, :] = _silu(y).astype(o_ref.dtype)


def conformer_conv(z, conv_w, conv_b, cln_g, cln_b, ts=256, rows=32):
    s = z.shape[0]
    c = conv_w.shape[1]
    ts = min(ts, s)
    hb = ts // CONV_HALO
    nh = s // CONV_HALO
    main = lambda col: pl.BlockSpec((ts, c), lambda i: (i, col))
    prev = lambda col: pl.BlockSpec((CONV_HALO, c), lambda i: (jnp.maximum(i * hb - 1, 0), col))
    nxt = lambda col: pl.BlockSpec((CONV_HALO, c), lambda i: (jnp.minimum((i + 1) * hb, nh - 1), col))
    vec = lambda r: pl.BlockSpec((r, c), lambda i: (0, 0))
    return pl.pallas_call(
        functools.partial(_conv_kernel, rows=min(rows, ts)),
        out_shape=jax.ShapeDtypeStruct((s, c), BF16),
        grid=(s // ts,),
        in_specs=[main(0), main(1), prev(0), prev(1), nxt(0), nxt(1),
                  vec(CONV_WIDTH), vec(1), vec(1), vec(1)],
        out_specs=pl.BlockSpec((ts, c), lambda i: (i, 0)),
        scratch_shapes=[pltpu.VMEM((ts + 2 * CONV_HALO, c), F32)],
        compiler_params=_cp(("parallel",)),
        name="conformer_conv",
    )(z, z, z, z, z, z, conv_w, conv_b.reshape(1, c), cln_g.reshape(1, c), cln_b.reshape(1, c))


def _t5_bucket(rel):
    half = N_BUCKETS // 2
    exact = half // 2
    n = jnp.abs(rel)
    large = exact + (jnp.log(jnp.maximum(n, 1).astype(F32) / exact)
                     / math.log(T5_MAX_DIST / exact) * (half - exact)).astype(jnp.int32)
    large = jnp.minimum(large, half - 1)
    return jnp.where(rel > 0, half, 0) + jnp.where(n < exact, n, large)


def _band_bias(t5_table, dil, qb):
    dist = jnp.arange(qb + 2 * DIL_HALF)[None, :] - DIL_HALF - jnp.arange(qb)[:, None]
    bias = jnp.transpose(t5_table[_t5_bucket(dist * dil)], (2, 0, 1)).astype(F32)
    return jnp.where((jnp.abs(dist) <= DIL_HALF)[None], bias, NEG_INF)


def _band_block(q, kw, vw, bias_ref, pen):
    hd = DIL_HEAD_DIM
    qb = q.shape[0]
    accs, ms, ls = [], [], []
    for h in range(q.shape[1] // hd):
        lanes = slice(h * hd, (h + 1) * hd)
        s = lax.dot_general(q[:, lanes].astype(BF16), kw[:, lanes].astype(BF16),
                            (((1,), (1,)), ((), ())), preferred_element_type=F32)
        s = s * (hd ** -0.5) + bias_ref[h]
        if pen is not None:
            s = s + pen
        m = jnp.max(s, axis=-1, keepdims=True)
        p = jnp.exp(s - m)
        l = jnp.sum(p, axis=-1, keepdims=True)
        accs.append(jnp.dot(p.astype(BF16), vw[:, lanes].astype(BF16), preferred_element_type=F32))
        ms.append(jnp.broadcast_to(m, (qb, hd)))
        ls.append(jnp.broadcast_to(l, (qb, hd)))
    return jnp.concatenate(accs, axis=-1), jnp.concatenate(ms, axis=-1), jnp.concatenate(ls, axis=-1)


def _dil_kernel(q_ref, kp_ref, kc_ref, kn_ref, vp_ref, vc_ref, vn_ref, b1_ref, b4_ref, b16_ref, o_ref,
                qf, kf, vf, a1, m1, l1, a4, m4, l4, a16, m16, l16):
    i = pl.program_id(1)
    first = i == 0
    last = i == pl.num_programs(1) - 1
    t = DIL_TILE
    half = DIL_HALF

    qf[...] = q_ref[...].astype(F32)
    for dst, (p_ref, c_ref, n_ref) in ((kf, (kp_ref, kc_ref, kn_ref)), (vf, (vp_ref, vc_ref, vn_ref))):
        dst[0:t, :] = p_ref[...].astype(F32)
        dst[t:2 * t, :] = c_ref[...].astype(F32)
        dst[2 * t:3 * t, :] = n_ref[...].astype(F32)

    def pens(qb):
        col = lax.broadcasted_iota(jnp.int32, (1, qb + 2 * half), 1)
        lo = jnp.where((col < half) & first, NEG_INF, 0.0)
        hi = jnp.where((col >= qb + half) & last, NEG_INF, 0.0)
        return lo, hi

    qb = DIL_QB
    lo, hi = pens(qb)
    nsb = t // qb
    for sb in range(nsb):
        pen = lo if sb == 0 else (hi if sb == nsb - 1 else None)
        k0 = t + sb * qb - half
        acc, m, l = _band_block(qf[sb * qb:(sb + 1) * qb, :], kf[k0:k0 + qb + 2 * half, :],
                                vf[k0:k0 + qb + 2 * half, :], b1_ref, pen)
        a1[sb * qb:(sb + 1) * qb, :] = acc
        m1[sb * qb:(sb + 1) * qb, :] = m
        l1[sb * qb:(sb + 1) * qb, :] = l

    dil = 4
    nsb = t // dil // qb

    def dil4(r, carry):
        for sb in range(nsb):
            pen = lo if sb == 0 else (hi if sb == nsb - 1 else None)
            qrows = pl.ds(sb * qb * dil + r, qb, stride=dil)
            krows = pl.ds(t + (sb * qb - half) * dil + r, qb + 2 * half, stride=dil)
            acc, m, l = _band_block(qf[qrows, :], kf[krows, :], vf[krows, :], b4_ref, pen)
            a4[qrows, :] = acc
            m4[qrows, :] = m
            l4[qrows, :] = l
        return carry

    lax.fori_loop(0, dil, dil4, 0, unroll=True)

    dil = 16
    qb16 = t // dil
    lo16, hi16 = pens(qb16)
    pen16 = lo16 + hi16

    def dil16(r, carry):
        qrows = pl.ds(r, qb16, stride=dil)
        krows = pl.ds(t - half * dil + r, qb16 + 2 * half, stride=dil)
        acc, m, l = _band_block(qf[qrows, :], kf[krows, :], vf[krows, :], b16_ref, pen16)
        a16[qrows, :] = acc
        m16[qrows, :] = m
        l16[qrows, :] = l
        return carry

    lax.fori_loop(0, dil, dil16, 0, unroll=4)

    mm = jnp.maximum(jnp.maximum(m1[...], m4[...]), m16[...])
    w1, w4, w16 = jnp.exp(m1[...] - mm), jnp.exp(m4[...] - mm), jnp.exp(m16[...] - mm)
    num = w1 * a1[...] + w4 * a4[...] + w16 * a16[...]
    den = w1 * l1[...] + w4 * l4[...] + w16 * l16[...]
    o_ref[...] = (num / den).astype(o_ref.dtype)


def dilated_attention(z, t5_table, col0, width):
    s, zw = z.shape
    t = DIL_TILE
    assert s % t == 0 and col0 % LANE == 0 and width % LANE == 0
    nt = s // t
    heads_per = LANE // DIL_HEAD_DIM
    cq, ck, cv = ((col0 + k * width) // LANE for k in range(3))
    biases = [_band_bias(t5_table, dil, qb) for dil, qb in ((1, DIL_QB), (4, DIL_QB), (16, t // 16))]

    def blk(col, shift):
        return pl.BlockSpec((t, LANE), lambda hg, i: (jnp.clip(i + shift, 0, nt - 1), col + hg))

    bias_spec = lambda b: pl.BlockSpec((heads_per,) + b.shape[1:], lambda hg, i: (hg, 0, 0))
    return pl.pallas_call(
        _dil_kernel,
        out_shape=jax.ShapeDtypeStruct((s, width), BF16),
        grid=(width // LANE, nt),
        in_specs=[blk(cq, 0), blk(ck, -1), blk(ck, 0), blk(ck, 1), blk(cv, -1), blk(cv, 0), blk(cv, 1)]
                 + [bias_spec(b) for b in biases],
        out_specs=pl.BlockSpec((t, LANE), lambda hg, i: (i, hg)),
        scratch_shapes=[pltpu.VMEM((t, LANE), F32), pltpu.VMEM((3 * t, LANE), F32),
                        pltpu.VMEM((3 * t, LANE), F32)] + [pltpu.VMEM((t, LANE), F32)] * 9,
        compiler_params=_cp(("parallel", "parallel")),
        name="dilated_attention",
    )(*([z] * 7 + biases))


def _oproj_kernel(a1_ref, a2_ref, w_ref, x_ref, g_ref, gt_ref, o_ref):
    k1 = a1_ref.shape[1]
    y = jnp.dot(a1_ref[...], w_ref[0:k1, :], preferred_element_type=F32)
    y = y + jnp.dot(a2_ref[...], w_ref[k1:, :], preferred_element_type=F32)
    o_ref[...] = x_ref[...] + gt_ref[...] * (_rms(y) * g_ref[...])


def out_proj_residual(a1, a2, w, x, g, gt, tm=512):
    s, d = x.shape
    k1, k2 = a1.shape[1], a2.shape[1]
    tm = min(tm, s)
    vec = pl.BlockSpec((1, d), lambda i: (0, 0))
    return pl.pallas_call(
        _oproj_kernel,
        out_shape=jax.ShapeDtypeStruct((s, d), F32),
        grid=(s // tm,),
        in_specs=[pl.BlockSpec((tm, k1), lambda i: (i, 0)), pl.BlockSpec((tm, k2), lambda i: (i, 0)),
                  pl.BlockSpec((k1 + k2, d), lambda i: (0, 0)),
                  pl.BlockSpec((tm, d), lambda i: (i, 0)), vec, vec],
        out_specs=pl.BlockSpec((tm, d), lambda i: (i, 0)),
        compiler_params=_cp(("parallel",)),
        name="out_proj_residual",
    )(a1, a2, w, x, g, gt)


def _ffn_kernel(x_ref, g_ref, sh_ref, sc_ref, g2_ref, gt_ref, w1_ref, w3_ref, w2_ref, o_ref, h_ref, acc_ref):
    f = pl.program_id(1)

    @pl.when(f == 0)
    def _():
        h_ref[...] = _modnorm(x_ref[...], g_ref[...], sh_ref[...], sc_ref[...]).astype(h_ref.dtype)
        acc_ref[...] = jnp.zeros_like(acc_ref)

    h = h_ref[...]
    u = jnp.dot(h, w1_ref[...], preferred_element_type=F32)
    v = jnp.dot(h, w3_ref[...], preferred_element_type=F32)
    acc_ref[...] += jnp.dot((_silu(u) * v).astype(BF16), w2_ref[...], preferred_element_type=F32)

    @pl.when(f == pl.num_programs(1) - 1)
    def _():
        o_ref[...] = x_ref[...] + gt_ref[...] * (_rms(acc_ref[...]) * g2_ref[...])


def ffn_sublayer(x, g, sh, sc, g2, gt, w1, w3, w2, tm=512, tf=512):
    s, d = x.shape
    f = w1.shape[1]
    tm, tf = min(tm, s), min(tf, f)
    vec = pl.BlockSpec((1, d), lambda i, j: (0, 0))
    return pl.pallas_call(
        _ffn_kernel,
        out_shape=jax.ShapeDtypeStruct((s, d), F32),
        grid=(s // tm, f // tf),
        in_specs=[pl.BlockSpec((tm, d), lambda i, j: (i, 0)), vec, vec, vec, vec, vec,
                  pl.BlockSpec((d, tf), lambda i, j: (0, j)), pl.BlockSpec((d, tf), lambda i, j: (0, j)),
                  pl.BlockSpec((tf, d), lambda i, j: (j, 0))],
        out_specs=pl.BlockSpec((tm, d), lambda i, j: (i, 0)),
        scratch_shapes=[pltpu.VMEM((tm, d), BF16), pltpu.VMEM((tm, d), F32)],
        compiler_params=_cp(("parallel", "arbitrary")),
        name="ffn_sublayer",
    )(x, g, sh, sc, g2, gt, w1, w3, w2)


def _rope_slab(t, c_ref, s1_ref, s2_ref):
    return (t * c_ref[...] + pltpu.roll(t, LANE - ROPE_DIM // 2, 1) * s1_ref[...]
            + pltpu.roll(t, ROPE_DIM // 2, 1) * s2_ref[...])


def _oddin_kernel(x_ref, g_ref, sh_ref, sc_ref, win_ref, qg_ref, wuq_ref, kvg_ref, wuk_ref, wuv_ref,
                  c_ref, s1_ref, s2_ref, q_ref, k_ref, v_ref, u_ref, *, q_lora, kv_lora, scale):
    h = _modnorm(x_ref[...], g_ref[...], sh_ref[...], sc_ref[...]).astype(BF16)
    z = jnp.dot(h, win_ref[...], preferred_element_type=F32)
    o_kv, o_pe, o_u = q_lora, q_lora + kv_lora, q_lora + kv_lora + LANE
    u_ref[...] = z[:, o_u:]

    qn = (_rms(z[:, :q_lora]) * qg_ref[...]).astype(BF16)
    q = jnp.dot(qn, wuq_ref[...], preferred_element_type=F32)
    kvn = (_rms(z[:, o_kv:o_pe]) * kvg_ref[...]).astype(BF16)
    kn = jnp.dot(kvn, wuk_ref[...], preferred_element_type=F32)
    v_ref[...] = jnp.dot(kvn, wuv_ref[...], preferred_element_type=F32).astype(v_ref.dtype)
    kpe = _rope_slab(z[:, o_pe:o_u], c_ref, s1_ref, s2_ref).astype(k_ref.dtype)

    for hh in range(MLA_HEADS):
        b = hh * MLA_QK_PAD
        q_ref[:, b:b + NOPE_DIM] = (q[:, b:b + NOPE_DIM] * scale).astype(q_ref.dtype)
        qpe = _rope_slab(q[:, b + NOPE_DIM:b + MLA_QK_PAD], c_ref, s1_ref, s2_ref)
        q_ref[:, b + NOPE_DIM:b + MLA_QK_PAD] = (qpe * scale).astype(q_ref.dtype)
        k_ref[:, b:b + NOPE_DIM] = kn[:, hh * NOPE_DIM:(hh + 1) * NOPE_DIM].astype(k_ref.dtype)
        k_ref[:, b + NOPE_DIM:b + MLA_QK_PAD] = kpe


def _rope_tables(seq):
    pos = jnp.arange(seq, dtype=F32)
    inv_freq = ROPE_BASE ** (-jnp.arange(0, ROPE_DIM, 2, dtype=F32) / ROPE_DIM)
    ang = pos[:, None] * inv_freq[None, :]
    cos, sin = jnp.cos(ang), jnp.sin(ang)
    zero = jnp.zeros_like(cos)
    c = jnp.concatenate([cos, cos, zero, zero], axis=-1)
    s1 = jnp.concatenate([-sin, zero, zero, zero], axis=-1)
    s2 = jnp.concatenate([zero, sin, zero, zero], axis=-1)
    return c, s1, s2


def odd_in_proj(x, g, sh, sc, w_in, q_norm_g, w_uq, kv_norm_g, w_ukv, tm=256):
    s, d = x.shape
    q_lora, kv_lora = q_norm_g.shape[0], kv_norm_g.shape[0]
    s5_dim = w_in.shape[1] - q_lora - kv_lora - ROPE_DIM
    tm = min(tm, s)
    w_in_p = jnp.concatenate(
        [w_in[:, :q_lora + kv_lora + ROPE_DIM], jnp.zeros((d, LANE - ROPE_DIM), w_in.dtype),
         w_in[:, q_lora + kv_lora + ROPE_DIM:]], axis=1).astype(BF16)
    wq = w_uq.reshape(q_lora, MLA_HEADS, NOPE_DIM + ROPE_DIM)
    wq = jnp.pad(wq, ((0, 0), (0, 0), (0, MLA_QK_PAD - NOPE_DIM - ROPE_DIM)))
    wq = wq.reshape(q_lora, MLA_HEADS * MLA_QK_PAD).astype(BF16)
    wkv = w_ukv.reshape(kv_lora, MLA_HEADS, NOPE_DIM + V_DIM)
    wuk = wkv[:, :, :NOPE_DIM].reshape(kv_lora, MLA_HEADS * NOPE_DIM).astype(BF16)
    wuv = wkv[:, :, NOPE_DIM:].reshape(kv_lora, MLA_HEADS * V_DIM).astype(BF16)
    c, s1, s2 = _rope_tables(s)
    scale = (NOPE_DIM + ROPE_DIM) ** -0.5 * math.log2(math.e)

    full = lambda a: pl.BlockSpec(a.shape, lambda i: (0, 0))
    row = lambda w: pl.BlockSpec((tm, w), lambda i: (i, 0))
    vec = pl.BlockSpec((1, d), lambda i: (0, 0))
    qg = q_norm_g.reshape(1, q_lora)
    kvg = kv_norm_g.reshape(1, kv_lora)
    return pl.pallas_call(
        functools.partial(_oddin_kernel, q_lora=q_lora, kv_lora=kv_lora, scale=scale),
        out_shape=[jax.ShapeDtypeStruct((s, MLA_HEADS * MLA_QK_PAD), BF16),
                   jax.ShapeDtypeStruct((s, MLA_HEADS * MLA_QK_PAD), BF16),
                   jax.ShapeDtypeStruct((s, MLA_HEADS * V_DIM), BF16),
                   jax.ShapeDtypeStruct((s, s5_dim), F32)],
        grid=(s // tm,),
        in_specs=[row(d), vec, vec, vec, full(w_in_p), full(qg), full(wq), full(kvg), full(wuk), full(wuv),
                  row(LANE), row(LANE), row(LANE)],
        out_specs=[row(MLA_HEADS * MLA_QK_PAD), row(MLA_HEADS * MLA_QK_PAD), row(MLA_HEADS * V_DIM),
                   row(s5_dim)],
        compiler_params=_cp(("parallel",)),
        name="odd_in_proj",
    )(x, g, sh, sc, w_in_p, qg, wq, kvg, wuk, wuv, c, s1, s2)


def _flash_kernel(q_ref, k_ref, v_ref, o_ref, m_ref, acc_ref, s0, s1, p0, p1, a0, a1, *, rows):
    t = pl.program_id(2)
    nk = pl.num_programs(2) - 2
    tq = q_ref.shape[0]
    vd = v_ref.shape[1]

    @pl.when(t == 0)
    def _():
        m_ref[...] = jnp.full_like(m_ref, -jnp.inf)
        acc_ref[...] = jnp.zeros_like(acc_ref)
        for r in (s0, s1):
            r[...] = jnp.full_like(r, -jnp.inf)
        for r in (p0, p1):
            r[...] = jnp.zeros_like(r)
        for r in (a0, a1):
            r[...] = jnp.ones_like(r)

    def stages(s_rd, s_wr, p_rd, p_wr, a_rd, a_wr):
        v = v_ref[...]
        pv = jnp.dot(p_rd[...], jnp.concatenate([v, jnp.ones_like(v)], axis=1), preferred_element_type=F32)
        alpha_old = a_rd[...]
        acc_ref[:, :vd] = alpha_old * acc_ref[:, :vd] + pv[:, :vd]
        acc_ref[:, vd:] = alpha_old * acc_ref[:, vd:] + pv[:, vd:]
        live = t >= 1
        for c in range(tq // rows):
            rs = slice(c * rows, (c + 1) * rows)
            s = s_rd[rs, :]
            m_prev = m_ref[rs, :]
            m_cand = jnp.maximum(m_prev, jnp.max(s, axis=-1, keepdims=True))
            m_new = jnp.where(live, m_cand, m_prev)
            m_sub = jnp.where(live, m_cand, 0.0)
            p_wr[rs, :] = jnp.exp2((s - m_sub[:, :1]).astype(BF16))
            a_wr[rs, :] = jnp.where(live, jnp.exp2(m_prev - m_new), 1.0)
            m_ref[rs, :] = m_new
        s_wr[...] = lax.dot_general(q_ref[...], k_ref[...], (((1,), (1,)), ((), ())),
                                    preferred_element_type=F32)

    parity = lax.rem(t, 2)

    @pl.when(parity == 0)
    def _():
        stages(s1, s0, p1, p0, a1, a0)

    @pl.when(parity == 1)
    def _():
        stages(s0, s1, p0, p1, a0, a1)

    @pl.when(t == nk + 1)
    def _():
        o_ref[...] = (acc_ref[:, :vd] / acc_ref[:, vd:]).astype(o_ref.dtype)


def mla_attention(q, k, v, tq=1024, tk=1024, rows=32):
    s = q.shape[0]
    tq, tk = min(tq, s), min(tk, s)
    nk = s // tk
    return pl.pallas_call(
        functools.partial(_flash_kernel, rows=rows),
        out_shape=jax.ShapeDtypeStruct((s, MLA_HEADS * V_DIM), BF16),
        grid=(MLA_HEADS, s // tq, nk + 2),
        in_specs=[pl.BlockSpec((tq, MLA_QK_PAD), lambda h, i, t: (i, h)),
                  pl.BlockSpec((tk, MLA_QK_PAD), lambda h, i, t: (jnp.minimum(t, nk - 1), h)),
                  pl.BlockSpec((tk, V_DIM), lambda h, i, t: (jnp.clip(t - 2, 0, nk - 1), h))],
        out_specs=pl.BlockSpec((tq, V_DIM), lambda h, i, t: (i, h)),
        scratch_shapes=[pltpu.VMEM((tq, V_DIM), F32), pltpu.VMEM((tq, 2 * V_DIM), F32),
                        pltpu.VMEM((tq, tk), F32), pltpu.VMEM((tq, tk), F32),
                        pltpu.VMEM((tq, tk), BF16), pltpu.VMEM((tq, tk), BF16),
                        pltpu.VMEM((tq, V_DIM), F32), pltpu.VMEM((tq, V_DIM), F32)],
        compiler_params=_cp(("parallel", "parallel", "arbitrary")),
        name="mla_flash_attention",
    )(q, k, v)


def _s5_matrices(a_re, a_im, log_dt, b_re, b_im, c_re, c_im, chunk):
    L = chunk
    A = lax.complex(a_re.astype(F32), a_im.astype(F32))
    dt = jnp.exp(log_dt.astype(F32))[..., None]
    adt = A * dt
    a_bar = jnp.exp(adt)
    b_bar = ((a_bar - 1.0) / A)[..., None] * lax.complex(b_re.astype(F32), b_im.astype(F32))
    c_c = lax.complex(c_re.astype(F32), c_im.astype(F32))
    kk = jnp.arange(L + 1, dtype=F32)
    apow = jnp.exp(adt[:, :, None, :] * kk[None, None, :, None].astype(jnp.complex64))
    g, p, gc = b_bar.shape[1], b_bar.shape[2], b_bar.shape[3]

    ker = jnp.real(jnp.einsum('dgcp,dgkp,dgpi->dgkci', c_c, apow[:, :, :L], b_bar))
    lag = jnp.arange(L)[None, :] - jnp.arange(L)[:, None]
    tf = jnp.where((lag >= 0)[None, :, :, None, None], ker[0][:, jnp.clip(lag, 0, L - 1)], 0.0)
    tb = jnp.where((lag <= 0)[None, :, :, None, None], ker[1][:, jnp.clip(-lag, 0, L - 1)], 0.0)
    t_mat = jnp.transpose(tf + tb, (0, 1, 4, 2, 3)).reshape(g, L * gc, L * gc)

    pf = apow[0][:, ::-1][:, 1:][:, :, :, None] * b_bar[0][:, None]
    pb = apow[1][:, :L][:, :, :, None] * b_bar[1][:, None]
    def p_lay(m):
        return jnp.transpose(m, (0, 1, 3, 2)).reshape(g, L * gc, p)
    p_mat = jnp.concatenate([p_lay(jnp.real(pf)), p_lay(jnp.imag(pf)),
                             p_lay(jnp.real(pb)), p_lay(jnp.imag(pb))], axis=-1)

    wf = c_c[0][:, None] * apow[0][:, 1:][:, :, None, :]
    wb = c_c[1][:, None] * apow[1][:, ::-1][:, :L][:, :, None, :]
    def q_lay(m):
        return jnp.transpose(m, (0, 3, 1, 2)).reshape(g, p, L * gc)
    q_mat = jnp.concatenate([q_lay(jnp.real(wf)), q_lay(-jnp.imag(wf)),
                             q_lay(jnp.real(wb)), q_lay(-jnp.imag(wb))], axis=1)

    al = apow[:, :, L]
    dec = jnp.stack([jnp.real(al[0]), jnp.imag(al[0]), jnp.real(al[1]), jnp.imag(al[1])]).reshape(4, g * p)
    return t_mat, p_mat, q_mat, dec


def _s5_state_kernel(u_ref, p_ref, fre_ref, fim_ref, bre_ref, bim_ref):
    outs = [[], [], [], []]
    for gi in range(u_ref.shape[0]):
        xe = jnp.dot(u_ref[gi].astype(BF16), p_ref[gi], preferred_element_type=F32)
        for part in range(4):
            outs[part].append(xe[:, part * S5_STATE:(part + 1) * S5_STATE])
    for part, ref in enumerate((fre_ref, fim_ref, bre_ref, bim_ref)):
        ref[...] = jnp.concatenate(outs[part], axis=-1)


def _s5_scan_kernel(fre_ref, fim_ref, bre_ref, bim_ref, dec_ref, ofre_ref, ofim_ref, obre_ref, obim_ref):
    nc = fre_ref.shape[0]
    w = fre_ref.shape[1]
    far, fai, bar, bai = (dec_ref[i:i + 1, :] for i in range(4))

    def fwd(c, st):
        re, im = st
        ofre_ref[pl.ds(c, 1), :] = re
        ofim_ref[pl.ds(c, 1), :] = im
        return (far * re - fai * im + fre_ref[pl.ds(c, 1), :],
                far * im + fai * re + fim_ref[pl.ds(c, 1), :])

    def bwd(i, st):
        c = nc - 1 - i
        re, im = st
        obre_ref[pl.ds(c, 1), :] = re
        obim_ref[pl.ds(c, 1), :] = im
        return (bar * re - bai * im + bre_ref[pl.ds(c, 1), :],
                bar * im + bai * re + bim_ref[pl.ds(c, 1), :])

    zero = (jnp.zeros((1, w), F32), jnp.zeros((1, w), F32))
    lax.fori_loop(0, nc, fwd, zero)
    lax.fori_loop(0, nc, bwd, zero)


def _s5_out_kernel(u_ref, t_ref, q_ref, fre_ref, fim_ref, bre_ref, bim_ref, y_ref):
    for gi in range(u_ref.shape[0]):
        lanes = slice(gi * S5_STATE, (gi + 1) * S5_STATE)
        y = jnp.dot(u_ref[gi].astype(BF16), t_ref[gi], preferred_element_type=F32)
        xin = jnp.concatenate([r[:, lanes] for r in (fre_ref, fim_ref, bre_ref, bim_ref)], axis=-1)
        y_ref[gi] = y + jnp.dot(xin.astype(BF16), q_ref[gi], preferred_element_type=F32)


def s5_bidirectional(u, a_re, a_im, log_dt, b_re, b_im, c_re, c_im, chunk=S5_CHUNK):
    s, dim = u.shape
    gc, p = S5_GROUP, S5_STATE
    g = dim // gc
    chunk = min(chunk, s)
    nc = s // chunk
    gp = LANE // p
    t_mat, p_mat, q_mat, dec = _s5_matrices(a_re, a_im, log_dt, b_re, b_im, c_re, c_im, chunk)
    t_mat, p_mat, q_mat = t_mat.astype(BF16), p_mat.astype(BF16), q_mat.astype(BF16)
    u_t = jnp.transpose(u.reshape(nc, chunk, g, gc), (2, 0, 1, 3)).reshape(g, nc, chunk * gc)

    grp = lambda a, b: pl.BlockSpec((gp, a, b), lambda i: (i, 0, 0))
    st = pl.BlockSpec((nc, LANE), lambda i: (0, i))
    st_shape = jax.ShapeDtypeStruct((nc, g * p), F32)
    xe = pl.pallas_call(
        _s5_state_kernel,
        out_shape=[st_shape] * 4,
        grid=(g // gp,),
        in_specs=[grp(nc, chunk * gc), grp(chunk * gc, 4 * p)],
        out_specs=[st] * 4,
        compiler_params=_cp(("parallel",)),
        name="s5_chunk_states",
    )(u_t, p_mat)

    whole = lambda shape: pl.BlockSpec(shape, lambda: (0,) * len(shape))
    xin = pl.pallas_call(
        _s5_scan_kernel,
        out_shape=[st_shape] * 4,
        in_specs=[whole((nc, g * p))] * 4 + [whole((4, g * p))],
        out_specs=[whole((nc, g * p))] * 4,
        compiler_params=pltpu.CompilerParams(vmem_limit_bytes=VMEM_LIMIT_MB * 1024 * 1024),
        name="s5_boundary_scan",
    )(*xe, dec)

    y_t = pl.pallas_call(
        _s5_out_kernel,
        out_shape=jax.ShapeDtypeStruct((g, nc, chunk * gc), F32),
        grid=(g // gp,),
        in_specs=[grp(nc, chunk * gc), grp(chunk * gc, chunk * gc), grp(4 * p, chunk * gc)] + [st] * 4,
        out_specs=grp(nc, chunk * gc),
        compiler_params=_cp(("parallel",)),
        name="s5_outputs",
    )(u_t, t_mat, q_mat, *xin)
    return jnp.transpose(y_t.reshape(g, nc, chunk, gc), (1, 2, 0, 3)).reshape(s, dim)


def _s5_gate_kernel(y_ref, u_ref, d_ref, w_ref, o_ref):
    y = y_ref[...] + d_ref[...] * u_ref[...]
    y = 0.5 * y * (1.0 + jnp.tanh(math.sqrt(2.0 / math.pi) * (y + 0.044715 * (y * y * y))))
    z = jnp.dot(y.astype(BF16), w_ref[...], preferred_element_type=F32)
    o_ref[...] = (y * _sigmoid(z)).astype(o_ref.dtype)


def s5_gate(y, u, d_skip, w_glu, tm=1024):
    s, dim = y.shape
    tm = min(tm, s)
    row = pl.BlockSpec((tm, dim), lambda i: (i, 0))
    return pl.pallas_call(
        _s5_gate_kernel,
        out_shape=jax.ShapeDtypeStruct((s, dim), BF16),
        grid=(s // tm,),
        in_specs=[row, row, pl.BlockSpec((1, dim), lambda i: (0, 0)), pl.BlockSpec((dim, dim), lambda i: (0, 0))],
        out_specs=row,
        compiler_params=_cp(("parallel",)),
        name="s5_gate",
    )(y, u, d_skip.reshape(1, dim), w_glu.astype(BF16))


def _router_kernel(x_ref, g_ref, sh_ref, sc_ref, rw_ref, h_ref, info_ref):
    h = _modnorm(x_ref[...], g_ref[...], sh_ref[...], sc_ref[...])
    h_ref[...] = h.astype(h_ref.dtype)
    logits = jnp.dot(h, rw_ref[...], preferred_element_type=F32, precision=lax.Precision.HIGHEST)
    lane = lax.broadcasted_iota(jnp.int32, logits.shape, 1)
    logits = jnp.where(lane < N_EXPERTS, logits, -jnp.inf)
    m1 = jnp.max(logits, axis=-1, keepdims=True)
    i1 = jnp.min(jnp.where(logits == m1, lane, LANE), axis=-1, keepdims=True)
    rest = jnp.where(lane == i1, -jnp.inf, logits)
    m2 = jnp.max(rest, axis=-1, keepdims=True)
    i2 = jnp.min(jnp.where(rest == m2, lane, LANE), axis=-1, keepdims=True)
    e = jnp.exp(m2 - m1)
    g1 = 1.0 / (1.0 + e)
    g2 = e / (1.0 + e)
    info = jnp.where(lane == 0, i1.astype(F32),
                     jnp.where(lane == 1, i2.astype(F32),
                               jnp.where(lane == 2, g1, jnp.where(lane == 3, g2, 0.0))))
    info_ref[...] = info


def moe_router(x, g, sh, sc, router_w, tm=512):
    s, d = x.shape
    tm = min(tm, s)
    rw = jnp.pad(router_w, ((0, 0), (0, LANE - router_w.shape[1])))
    vec = pl.BlockSpec((1, d), lambda i: (0, 0))
    return pl.pallas_call(
        _router_kernel,
        out_shape=[jax.ShapeDtypeStruct((s, d), BF16), jax.ShapeDtypeStruct((s, LANE), F32)],
        grid=(s // tm,),
        in_specs=[pl.BlockSpec((tm, d), lambda i: (i, 0)), vec, vec, vec,
                  pl.BlockSpec((d, LANE), lambda i: (0, 0))],
        out_specs=[pl.BlockSpec((tm, d), lambda i: (i, 0)), pl.BlockSpec((tm, LANE), lambda i: (i, 0))],
        compiler_params=_cp(("parallel",)),
        name="moe_router",
    )(x, g, sh, sc, rw)


def _expert_kernel(te_ref, tv_ref, h_ref, w1_ref, w3_ref, w2_ref, o_ref, acc_ref):
    t = pl.program_id(0)
    f = pl.program_id(1)
    valid = tv_ref[t] > 0

    @pl.when(f == 0)
    def _():
        acc_ref[...] = jnp.zeros_like(acc_ref)

    @pl.when(valid)
    def _():
        h = h_ref[...]
        u = jnp.dot(h, w1_ref[...].astype(BF16), preferred_element_type=F32)
        v = jnp.dot(h, w3_ref[...].astype(BF16), preferred_element_type=F32)
        acc_ref[...] += jnp.dot((_silu(u) * v).astype(BF16), w2_ref[...].astype(BF16),
                                preferred_element_type=F32)

    @pl.when(f == pl.num_programs(1) - 1)
    def _():
        o_ref[...] = acc_ref[...].astype(o_ref.dtype)


def expert_ffn(h_sorted, tile_expert, tile_valid, w1, w3, w2, tm, tf=256):
    n, d = h_sorted.shape
    f = w1.shape[2]
    tf = min(tf, f)
    nf = f // tf

    def fblk(j, t, tv):
        return jnp.where(tv[t] > 0, j, nf - 1)

    return pl.pallas_call(
        _expert_kernel,
        out_shape=jax.ShapeDtypeStruct((n, d), BF16),
        grid_spec=pltpu.PrefetchScalarGridSpec(
            num_scalar_prefetch=2,
            grid=(n // tm, nf),
            in_specs=[pl.BlockSpec((tm, d), lambda t, j, te, tv: (t, 0)),
                      pl.BlockSpec((None, d, tf), lambda t, j, te, tv: (te[t], 0, fblk(j, t, tv))),
                      pl.BlockSpec((None, d, tf), lambda t, j, te, tv: (te[t], 0, fblk(j, t, tv))),
                      pl.BlockSpec((None, tf, d), lambda t, j, te, tv: (te[t], fblk(j, t, tv), 0))],
            out_specs=pl.BlockSpec((tm, d), lambda t, j, te, tv: (t, 0)),
            scratch_shapes=[pltpu.VMEM((tm, d), F32)]),
        compiler_params=_cp(("arbitrary", "arbitrary")),
        name="expert_ffn",
    )(tile_expert, tile_valid, h_sorted, w1, w3, w2)


def _dispatch(idx, tm):
    s = idx.shape[0]
    e_flat = idx.reshape(-1)
    onehot = (e_flat[:, None] == jnp.arange(N_EXPERTS)[None, :]).astype(jnp.int32)
    csum = jnp.cumsum(onehot, axis=0)
    rank = jnp.sum((csum - onehot) * onehot, axis=1)
    counts = csum[-1]
    padded = ((counts + tm - 1) // tm) * tm
    gend = jnp.cumsum(padded)
    slot = (gend - padded)[e_flat] + rank
    nt = (2 * s) // tm + N_EXPERTS
    src = jnp.zeros((nt * tm,), jnp.int32).at[slot].set(jnp.arange(2 * s, dtype=jnp.int32) // 2)
    tstart = jnp.arange(nt, dtype=jnp.int32) * tm
    valid = tstart < gend[-1]
    te = jnp.minimum(jnp.sum((tstart[:, None] >= gend[None, :]).astype(jnp.int32), axis=1), N_EXPERTS - 1)
    nvalid = gend[-1] // tm
    te = jnp.where(valid, te, te[jnp.maximum(nvalid - 1, 0)])
    return slot.reshape(s, 2), src, te.astype(jnp.int32), valid.astype(jnp.int32)


def _combine_kernel(ya_ref, yb_ref, info_ref, x_ref, g_ref, gt_ref, o_ref):
    info = info_ref[...]
    y = info[:, 2:3] * ya_ref[...].astype(F32) + info[:, 3:4] * yb_ref[...].astype(F32)
    o_ref[...] = x_ref[...] + gt_ref[...] * (_rms(y) * g_ref[...])


def moe_combine(ya, yb, info, x, g, gt, tm=512):
    s, d = x.shape
    tm = min(tm, s)
    row = lambda w: pl.BlockSpec((tm, w), lambda i: (i, 0))
    vec = pl.BlockSpec((1, d), lambda i: (0, 0))
    return pl.pallas_call(
        _combine_kernel,
        out_shape=jax.ShapeDtypeStruct((s, d), F32),
        grid=(s // tm,),
        in_specs=[row(d), row(d), row(LANE), row(d), vec, vec],
        out_specs=row(d),
        compiler_params=_cp(("parallel",)),
        name="moe_combine",
    )(ya, yb, info, x, g, gt)


def moe_sublayer(x, g, sh, sc, g2, gt, router_w, w1, w3, w2, tm_e=1024):
    s = x.shape[0]
    tm_e = min(tm_e, s)
    h, info = moe_router(x, g, sh, sc, router_w)
    idx = info[:, :2].astype(jnp.int32)
    slot, src, te, tv = _dispatch(idx, tm_e)
    y_sorted = expert_ffn(jnp.take(h, src, axis=0), te, tv, w1, w3, w2, tm_e)
    ya = jnp.take(y_sorted, slot[:, 0], axis=0)
    yb = jnp.take(y_sorted, slot[:, 1], axis=0)
    return moe_combine(ya, yb, info, x, g2, gt)


def kernel(x, c, t5_table, norm_g, ada_w, ada_b, e_w_in, e_conv_w, e_conv_b, e_cln_g, e_cln_b, e_w_out, e_ffn_w1, e_ffn_w3, e_ffn_w2, o_w_in, o_q_norm_g, o_w_uq, o_kv_norm_g, o_w_ukv, s5_a_re, s5_a_im, s5_log_dt, s5_b_re, s5_b_im, s5_c_re, s5_c_im, s5_d, s5_w_glu, o_w_out, router_w, moe_w1, moe_w3, moe_w2):
    bsz, seq, d = x.shape
    assert bsz == 1
    depth = norm_g.shape[0]
    conv_dim = e_conv_w.shape[2]
    xs = x.reshape(seq, d)
    mod = adaln(c, ada_w, ada_b)
    for layer in range(depth):
        i = layer // 2
        sh_m, sc_m, gt_m, sh_f, sc_f, gt_f = (mod[layer, :, k * d:(k + 1) * d] for k in range(6))
        g = [norm_g[layer, k].reshape(1, d) for k in range(4)]
        if layer % 2 == 0:
            z = norm_mod_matmul(xs, g[0], sh_m, sc_m, e_w_in[i].astype(BF16))
            a = conformer_conv(z, e_conv_w[i], e_conv_b[i], e_cln_g[i], e_cln_b[i])
            o = dilated_attention(z, t5_table, 2 * conv_dim, (z.shape[1] - 2 * conv_dim) // 3)
            xs = out_proj_residual(a, o, e_w_out[i].astype(BF16), xs, g[1], gt_m)
            xs = ffn_sublayer(xs, g[2], sh_f, sc_f, g[3], gt_f, e_ffn_w1[i].astype(BF16),
                              e_ffn_w3[i].astype(BF16), e_ffn_w2[i].astype(BF16))
        else:
            q, k, v, u = odd_in_proj(xs, g[0], sh_m, sc_m, o_w_in[i], o_q_norm_g[i], o_w_uq[i],
                                     o_kv_norm_g[i], o_w_ukv[i])
            o_mla = mla_attention(q, k, v)
            y = s5_bidirectional(u, s5_a_re[i], s5_a_im[i], s5_log_dt[i], s5_b_re[i], s5_b_im[i],
                                 s5_c_re[i], s5_c_im[i])
            y = s5_gate(y, u, s5_d[i], s5_w_glu[i])
            xs = out_proj_residual(o_mla, y, o_w_out[i].astype(BF16), xs, g[1], gt_m)
            xs = moe_sublayer(xs, g[2], sh_f, sc_f, g[3], gt_f, router_w[i], moe_w1[i], moe_w3[i], moe_w2[i])
    return xs.reshape(bsz, seq, d)
```

```python
import functools
import math

import jax
import jax.numpy as jnp
from jax import lax
from jax.experimental import pallas as pl
from jax.experimental.pallas import tpu as pltpu

F32 = jnp.float32
BF16 = jnp.bfloat16

RMS_EPS = 1e-6
LN_EPS = 1e-5
NEG_INF = -1e30

CONV_WIDTH = 31
CONV_HALO = 16
DIL_HEAD_DIM = 64
DIL_CONFIGS = ((128, 1), (512, 4), (2048, 16))
DIL_HALF = 64
DIL_TILE = 1024
DIL_QB = 256
DIL_MERGE = 4
N_BUCKETS = 32
T5_MAX_DIST = DIL_CONFIGS[-1][0] // 2
MLA_HEADS = 12
NOPE_DIM = 128
ROPE_DIM = 64
V_DIM = 128
ROPE_BASE = 10000.0
MLA_QK_PAD = 256
S5_GROUP = 16
S5_STATE = 64
S5_CHUNK = 32
N_EXPERTS = 8
LANE = 128
SUBLANES = 8
VMEM_LIMIT_MB = 56


def _cp(sem, vmem_mb=VMEM_LIMIT_MB):
    return pltpu.CompilerParams(dimension_semantics=sem, vmem_limit_bytes=vmem_mb * 1024 * 1024)


def _rms(x):
    return x * lax.rsqrt(jnp.mean(x * x, axis=-1, keepdims=True) + RMS_EPS)


def _modnorm(x, g, sh, sc):
    return (_rms(x) * g) * (1.0 + sc) + sh


def _sigmoid(x):
    return 1.0 / (1.0 + jnp.exp(-x))


def _silu(x):
    return x * _sigmoid(x)


def _adaln_kernel(c_ref, w_ref, b_ref, o_ref, cb_ref):
    @pl.when((pl.program_id(0) == 0) & (pl.program_id(1) == 0))
    def _():
        cc = c_ref[...]
        cb_ref[...] = jnp.broadcast_to(_silu(cc), cb_ref.shape)

    cb = cb_ref[...]
    for j in range(o_ref.shape[-1] // LANE):
        sl = slice(j * LANE, (j + 1) * LANE)
        o_ref[:, sl] = jnp.sum(w_ref[:, sl] * cb, axis=0, keepdims=True) + b_ref[:, sl]


def adaln(c, ada_w, ada_b, tn=1024):
    nl, d, n = ada_w.shape
    out = pl.pallas_call(
        _adaln_kernel,
        out_shape=jax.ShapeDtypeStruct((nl, 1, n), F32),
        grid=(nl, n // tn),
        in_specs=[pl.BlockSpec((d, 1), lambda l, j: (0, 0)),
                  pl.BlockSpec((None, d, tn), lambda l, j: (l, 0, j)),
                  pl.BlockSpec((None, 1, tn), lambda l, j: (l, 0, j))],
        out_specs=pl.BlockSpec((None, 1, tn), lambda l, j: (l, 0, j)),
        scratch_shapes=[pltpu.VMEM((d, LANE), F32)],
        compiler_params=_cp(("arbitrary", "arbitrary")),
        name="adaln",
    )(c.reshape(d, 1), ada_w, ada_b.reshape(nl, 1, n))
    return out


def _nmm_kernel(x_ref, g_ref, sh_ref, sc_ref, w_ref, o_ref, h_ref):
    @pl.when(pl.program_id(1) == 0)
    def _():
        h_ref[...] = _modnorm(x_ref[...], g_ref[...], sh_ref[...], sc_ref[...]).astype(h_ref.dtype)

    o_ref[...] = jnp.dot(h_ref[...], w_ref[...], preferred_element_type=F32).astype(o_ref.dtype)


def norm_mod_matmul(x, g, sh, sc, w, tm=512, tn=1024):
    s, d = x.shape
    n = w.shape[1]
    tm, tn = min(tm, s), min(tn, n)
    vec = pl.BlockSpec((1, d), lambda i, j: (0, 0))
    return pl.pallas_call(
        _nmm_kernel,
        out_shape=jax.ShapeDtypeStruct((s, n), BF16),
        grid=(s // tm, n // tn),
        in_specs=[pl.BlockSpec((tm, d), lambda i, j: (i, 0)), vec, vec, vec,
                  pl.BlockSpec((d, tn), lambda i, j: (0, j))],
        out_specs=pl.BlockSpec((tm, tn), lambda i, j: (i, j)),
        scratch_shapes=[pltpu.VMEM((tm, d), BF16)],
        compiler_params=_cp(("parallel", "arbitrary")),
        name="norm_mod_matmul",
    )(x, g, sh, sc, w)


def _conv_kernel(av_ref, ag_ref, avp_ref, agp_ref, avn_ref, agn_ref, w_ref, b_ref, lg_ref, lb_ref,
                 o_ref, buf_ref, sh_ref, *, rows):
    i = pl.program_id(0)
    n = pl.num_programs(0)
    ts = av_ref.shape[0]

    def glu(v_ref, g_ref):
        return v_ref[...].astype(F32) * _sigmoid(g_ref[...].astype(F32))

    buf_ref[0:CONV_HALO, :] = jnp.where(i > 0, glu(avp_ref, agp_ref), 0.0)
    buf_ref[CONV_HALO:CONV_HALO + ts, :] = glu(av_ref, ag_ref)
    buf_ref[CONV_HALO + ts:2 * CONV_HALO + ts, :] = jnp.where(i < n - 1, glu(avn_ref, agn_ref), 0.0)
    buf_ref[2 * CONV_HALO + ts:, :] = jnp.zeros((SUBLANES, buf_ref.shape[1]), F32)

    span = ts + 2 * CONV_HALO
    for o in range(SUBLANES):
        sh_ref[o] = buf_ref[o:o + span, :]

    off = CONV_HALO - CONV_WIDTH // 2
    for r in range(ts // rows):
        acc = jnp.broadcast_to(b_ref[...], (rows, b_ref.shape[-1]))
        for k in range(CONV_WIDTH):
            lo = r * rows + k + off
            base = lo - lo % SUBLANES
            acc = acc + w_ref[k:k + 1, :] * sh_ref[lo % SUBLANES, base:base + rows, :]
        mu = jnp.mean(acc, axis=-1, keepdims=True)
        dlt = acc - mu
        var = jnp.mean(dlt * dlt, axis=-1, keepdims=True)
        y = dlt * lax.rsqrt(var + LN_EPS) * lg_ref[...] + lb_ref[...]
        o_ref[r * rows:(r + 1) * rows, :] = _silu(y).astype(o_ref.dtype)


def conformer_conv(z, conv_w, conv_b, cln_g, cln_b, ts=256, rows=32):
    s = z.shape[0]
    c = conv_w.shape[1]
    ts = min(ts, s)
    hb = ts // CONV_HALO
    nh = s // CONV_HALO
    main = lambda col: pl.BlockSpec((ts, c), lambda i: (i, col))
    prev = lambda col: pl.BlockSpec((CONV_HALO, c), lambda i: (jnp.maximum(i * hb - 1, 0), col))
    nxt = lambda col: pl.BlockSpec((CONV_HALO, c), lambda i: (jnp.minimum((i + 1) * hb, nh - 1), col))
    vec = lambda r: pl.BlockSpec((r, c), lambda i: (0, 0))
    return pl.pallas_call(
        functools.partial(_conv_kernel, rows=min(rows, ts)),
        out_shape=jax.ShapeDtypeStruct((s, c), BF16),
        grid=(s // ts,),
        in_specs=[main(0), main(1), prev(0), prev(1), nxt(0), nxt(1),
                  vec(CONV_WIDTH), vec(1), vec(1), vec(1)],
        out_specs=pl.BlockSpec((ts, c), lambda i: (i, 0)),
        scratch_shapes=[pltpu.VMEM((ts + 2 * CONV_HALO + SUBLANES, c), F32),
                        pltpu.VMEM((SUBLANES, ts + 2 * CONV_HALO, c), F32)],
        compiler_params=_cp(("parallel",)),
        name="conformer_conv",
    )(z, z, z, z, z, z, conv_w, conv_b.reshape(1, c), cln_g.reshape(1, c), cln_b.reshape(1, c))


def _t5_bucket(rel):
    half = N_BUCKETS // 2
    exact = half // 2
    n = jnp.abs(rel)
    large = exact + (jnp.log(jnp.maximum(n, 1).astype(F32) / exact)
                     / math.log(T5_MAX_DIST / exact) * (half - exact)).astype(jnp.int32)
    large = jnp.minimum(large, half - 1)
    return jnp.where(rel > 0, half, 0) + jnp.where(n < exact, n, large)


def _band_bias(t5_table, dil, qb):
    dist = jnp.arange(qb + 2 * DIL_HALF)[None, :] - DIL_HALF - jnp.arange(qb)[:, None]
    bias = jnp.transpose(t5_table[_t5_bucket(dist * dil)], (2, 0, 1)).astype(F32)
    return jnp.where((jnp.abs(dist) <= DIL_HALF)[None], bias, NEG_INF)


def _band_group(q, kw, vw, bias_ref, pen):
    r = q.shape[0]
    is_a = lax.broadcasted_iota(jnp.int32, (1, LANE), 1) < DIL_HEAD_DIM
    q2 = jnp.concatenate([jnp.where(is_a, q, 0.0), jnp.where(is_a, 0.0, q)], axis=0).astype(BF16)
    s = lax.dot_general(q2, kw.astype(BF16), (((1,), (1,)), ((), ())), preferred_element_type=F32)
    s = s * (DIL_HEAD_DIM ** -0.5) + bias_ref[...]
    if pen is not None:
        s = s + pen
    m = jnp.max(s, axis=-1, keepdims=True)
    p = jnp.exp(s - m)
    l = jnp.sum(p, axis=-1, keepdims=True)
    pv = jnp.dot(p.astype(BF16), vw.astype(BF16), preferred_element_type=F32)
    pick = lambda x: jnp.where(is_a, x[:r], x[r:])
    return pick(pv), pick(m), pick(l)


def _dil_kernel(q_ref, kp_ref, kc_ref, kn_ref, vp_ref, vc_ref, vn_ref, b1_ref, b4_ref, b16_ref, o_ref,
                qf, kf, vf, a1, m1, l1, a4, m4, l4, a16, m16, l16):
    i = pl.program_id(1)
    first = i == 0
    last = i == pl.num_programs(1) - 1
    t = DIL_TILE
    half = DIL_HALF

    qf[...] = q_ref[...].astype(F32)
    for dst, (p_ref, c_ref, n_ref) in ((kf, (kp_ref, kc_ref, kn_ref)), (vf, (vp_ref, vc_ref, vn_ref))):
        dst[0:t, :] = p_ref[...].astype(F32)
        dst[t:2 * t, :] = c_ref[...].astype(F32)
        dst[2 * t:3 * t, :] = n_ref[...].astype(F32)

    def store(refs, rows, vals):
        for ref, val in zip(refs, vals):
            ref[rows, :] = val

    qb = DIL_QB
    kl = qb + 2 * half
    col = lax.broadcasted_iota(jnp.int32, (1, kl), 1)
    lo = jnp.where((col < half) & first, NEG_INF, 0.0)
    hi = jnp.where((col >= qb + half) & last, NEG_INF, 0.0)

    nb = t // qb
    for b in range(nb):
        pen = lo + hi if nb == 1 else (lo if b == 0 else (hi if b == nb - 1 else None))
        k0 = t + b * qb - half
        rows = slice(b * qb, (b + 1) * qb)
        store((a1, m1, l1), rows, _band_group(qf[rows, :], kf[k0:k0 + kl, :], vf[k0:k0 + kl, :], b1_ref, pen))

    dil = 4
    for r in range(dil):
        qrows = pl.ds(r, qb, stride=dil)
        krows = pl.ds(t - half * dil + r, kl, stride=dil)
        store((a4, m4, l4), qrows, _band_group(qf[qrows, :], kf[krows, :], vf[krows, :], b4_ref, lo + hi))

    dil = 16
    q16 = t // dil
    k16 = q16 + 2 * half
    col = lax.rem(lax.broadcasted_iota(jnp.int32, (1, DIL_MERGE * k16), 1), k16)
    pen16 = (jnp.where((col < half) & first, NEG_INF, 0.0)
             + jnp.where((col >= q16 + half) & last, NEG_INF, 0.0))
    for g in range(dil // DIL_MERGE):
        res = range(g * DIL_MERGE, (g + 1) * DIL_MERGE)
        qrows = [pl.ds(r, q16, stride=dil) for r in res]
        krows = [pl.ds(t - half * dil + r, k16, stride=dil) for r in res]
        out = _band_group(jnp.concatenate([qf[rr, :] for rr in qrows], axis=0),
                          jnp.concatenate([kf[rr, :] for rr in krows], axis=0),
                          jnp.concatenate([vf[rr, :] for rr in krows], axis=0), b16_ref, pen16)
        for u, rr in enumerate(qrows):
            store((a16, m16, l16), rr, [x[u * q16:(u + 1) * q16] for x in out])

    mm = jnp.maximum(jnp.maximum(m1[...], m4[...]), m16[...])
    w1, w4, w16 = jnp.exp(m1[...] - mm), jnp.exp(m4[...] - mm), jnp.exp(m16[...] - mm)
    num = w1 * a1[...] + w4 * a4[...] + w16 * a16[...]
    den = w1 * l1[...] + w4 * l4[...] + w16 * l16[...]
    o_ref[...] = (num / den).astype(o_ref.dtype)


def dilated_attention(z, t5_table, col0, width):
    s, zw = z.shape
    t = DIL_TILE
    assert s % t == 0 and col0 % LANE == 0 and width % LANE == 0 and t // 4 == DIL_QB
    nt = s // t
    nh = t5_table.shape[1]
    cq, ck, cv = ((col0 + k * width) // LANE for k in range(3))
    pair = lambda b: b.reshape(nh // 2, 2 * b.shape[1], b.shape[2])
    b16 = _band_bias(t5_table, 16, t // 16)
    same = jnp.eye(DIL_MERGE, dtype=bool)[None, :, None, :, None]
    b16 = jnp.where(same, b16[:, None, :, None, :], NEG_INF)
    b16 = b16.reshape(nh, DIL_MERGE * (t // 16), -1)
    biases = [pair(_band_bias(t5_table, 1, DIL_QB)), pair(_band_bias(t5_table, 4, DIL_QB)), pair(b16)]

    def blk(col, shift):
        return pl.BlockSpec((t, LANE), lambda hg, i: (jnp.clip(i + shift, 0, nt - 1), col + hg))

    bias_spec = lambda b: pl.BlockSpec((None,) + b.shape[1:], lambda hg, i: (hg, 0, 0))
    return pl.pallas_call(
        _dil_kernel,
        out_shape=jax.ShapeDtypeStruct((s, width), BF16),
        grid=(width // LANE, nt),
        in_specs=[blk(cq, 0), blk(ck, -1), blk(ck, 0), blk(ck, 1), blk(cv, -1), blk(cv, 0), blk(cv, 1)]
                 + [bias_spec(b) for b in biases],
        out_specs=pl.BlockSpec((t, LANE), lambda hg, i: (i, hg)),
        scratch_shapes=[pltpu.VMEM((t, LANE), F32), pltpu.VMEM((3 * t, LANE), F32),
                        pltpu.VMEM((3 * t, LANE), F32)] + [pltpu.VMEM((t, LANE), F32)] * 9,
        compiler_params=_cp(("parallel", "parallel")),
        name="dilated_attention",
    )(*([z] * 7 + biases))


def _oproj_kernel(a1_ref, a2_ref, w_ref, x_ref, g_ref, gt_ref, o_ref):
    k1 = a1_ref.shape[1]
    y = jnp.dot(a1_ref[...], w_ref[0:k1, :], preferred_element_type=F32)
    y = y + jnp.dot(a2_ref[...], w_ref[k1:, :], preferred_element_type=F32)
    o_ref[...] = x_ref[...] + gt_ref[...] * (_rms(y) * g_ref[...])


def out_proj_residual(a1, a2, w, x, g, gt, tm=512):
    s, d = x.shape
    k1, k2 = a1.shape[1], a2.shape[1]
    tm = min(tm, s)
    vec = pl.BlockSpec((1, d), lambda i: (0, 0))
    return pl.pallas_call(
        _oproj_kernel,
        out_shape=jax.ShapeDtypeStruct((s, d), F32),
        grid=(s // tm,),
        in_specs=[pl.BlockSpec((tm, k1), lambda i: (i, 0)), pl.BlockSpec((tm, k2), lambda i: (i, 0)),
                  pl.BlockSpec((k1 + k2, d), lambda i: (0, 0)),
                  pl.BlockSpec((tm, d), lambda i: (i, 0)), vec, vec],
        out_specs=pl.BlockSpec((tm, d), lambda i: (i, 0)),
        compiler_params=_cp(("parallel",)),
        name="out_proj_residual",
    )(a1, a2, w, x, g, gt)


def _ffn_kernel(x_ref, g_ref, sh_ref, sc_ref, g2_ref, gt_ref, w1_ref, w3_ref, w2_ref, o_ref, h_ref):
    f = pl.program_id(1)

    @pl.when(f == 0)
    def _():
        h_ref[...] = _modnorm(x_ref[...], g_ref[...], sh_ref[...], sc_ref[...]).astype(h_ref.dtype)
        o_ref[...] = jnp.zeros_like(o_ref)

    h = h_ref[...]
    u = jnp.dot(h, w1_ref[...], preferred_element_type=F32)
    v = jnp.dot(h, w3_ref[...], preferred_element_type=F32)
    o_ref[...] += jnp.dot((_silu(u) * v).astype(BF16), w2_ref[...], preferred_element_type=F32)

    @pl.when(f == pl.num_programs(1) - 1)
    def _():
        o_ref[...] = x_ref[...] + gt_ref[...] * (_rms(o_ref[...]) * g2_ref[...])


def ffn_sublayer(x, g, sh, sc, g2, gt, w1, w3, w2, tm=1024, tf=256):
    s, d = x.shape
    f = w1.shape[1]
    tm, tf = min(tm, s), min(tf, f)
    vec = pl.BlockSpec((1, d), lambda i, j: (0, 0))
    return pl.pallas_call(
        _ffn_kernel,
        out_shape=jax.ShapeDtypeStruct((s, d), F32),
        grid=(s // tm, f // tf),
        in_specs=[pl.BlockSpec((tm, d), lambda i, j: (i, 0)), vec, vec, vec, vec, vec,
                  pl.BlockSpec((d, tf), lambda i, j: (0, j)), pl.BlockSpec((d, tf), lambda i, j: (0, j)),
                  pl.BlockSpec((tf, d), lambda i, j: (j, 0))],
        out_specs=pl.BlockSpec((tm, d), lambda i, j: (i, 0)),
        scratch_shapes=[pltpu.VMEM((tm, d), BF16)],
        compiler_params=_cp(("parallel", "arbitrary")),
        name="ffn_sublayer",
    )(x, g, sh, sc, g2, gt, w1, w3, w2)


def _rope_slab(t, c_ref, s1_ref, s2_ref):
    return (t * c_ref[...] + pltpu.roll(t, LANE - ROPE_DIM // 2, 1) * s1_ref[...]
            + pltpu.roll(t, ROPE_DIM // 2, 1) * s2_ref[...])


def _oddin_kernel(x_ref, g_ref, sh_ref, sc_ref, win_ref, qg_ref, wuq_ref, kvg_ref, wuk_ref, wuv_ref,
                  c_ref, s1_ref, s2_ref, q_ref, k_ref, v_ref, u_ref, *, q_lora, kv_lora, scale):
    h = _modnorm(x_ref[...], g_ref[...], sh_ref[...], sc_ref[...]).astype(BF16)
    z = jnp.dot(h, win_ref[...], preferred_element_type=F32)
    o_kv, o_pe, o_u = q_lora, q_lora + kv_lora, q_lora + kv_lora + LANE
    u_ref[...] = z[:, o_u:]

    qn = (_rms(z[:, :q_lora]) * qg_ref[...]).astype(BF16)
    q = jnp.dot(qn, wuq_ref[...], preferred_element_type=F32)
    kvn = (_rms(z[:, o_kv:o_pe]) * kvg_ref[...]).astype(BF16)
    kn = jnp.dot(kvn, wuk_ref[...], preferred_element_type=F32)
    v_ref[...] = jnp.dot(kvn, wuv_ref[...], preferred_element_type=F32).astype(v_ref.dtype)
    kpe = _rope_slab(z[:, o_pe:o_u], c_ref, s1_ref, s2_ref).astype(k_ref.dtype)

    for hh in range(MLA_HEADS):
        b = hh * MLA_QK_PAD
        q_ref[:, b:b + NOPE_DIM] = (q[:, b:b + NOPE_DIM] * scale).astype(q_ref.dtype)
        qpe = _rope_slab(q[:, b + NOPE_DIM:b + MLA_QK_PAD], c_ref, s1_ref, s2_ref)
        q_ref[:, b + NOPE_DIM:b + MLA_QK_PAD] = (qpe * scale).astype(q_ref.dtype)
        k_ref[:, b:b + NOPE_DIM] = kn[:, hh * NOPE_DIM:(hh + 1) * NOPE_DIM].astype(k_ref.dtype)
        k_ref[:, b + NOPE_DIM:b + MLA_QK_PAD] = kpe


def _rope_tables(seq):
    pos = jnp.arange(seq, dtype=F32)
    inv_freq = ROPE_BASE ** (-jnp.arange(0, ROPE_DIM, 2, dtype=F32) / ROPE_DIM)
    ang = pos[:, None] * inv_freq[None, :]
    cos, sin = jnp.cos(ang), jnp.sin(ang)
    zero = jnp.zeros_like(cos)
    c = jnp.concatenate([cos, cos, zero, zero], axis=-1)
    s1 = jnp.concatenate([-sin, zero, zero, zero], axis=-1)
    s2 = jnp.concatenate([zero, sin, zero, zero], axis=-1)
    return c, s1, s2


def odd_in_proj(x, g, sh, sc, w_in, q_norm_g, w_uq, kv_norm_g, w_ukv, tm=256):
    s, d = x.shape
    q_lora, kv_lora = q_norm_g.shape[0], kv_norm_g.shape[0]
    s5_dim = w_in.shape[1] - q_lora - kv_lora - ROPE_DIM
    tm = min(tm, s)
    w_in_p = jnp.concatenate(
        [w_in[:, :q_lora + kv_lora + ROPE_DIM], jnp.zeros((d, LANE - ROPE_DIM), w_in.dtype),
         w_in[:, q_lora + kv_lora + ROPE_DIM:]], axis=1).astype(BF16)
    wq = w_uq.reshape(q_lora, MLA_HEADS, NOPE_DIM + ROPE_DIM)
    wq = jnp.pad(wq, ((0, 0), (0, 0), (0, MLA_QK_PAD - NOPE_DIM - ROPE_DIM)))
    wq = wq.reshape(q_lora, MLA_HEADS * MLA_QK_PAD).astype(BF16)
    wkv = w_ukv.reshape(kv_lora, MLA_HEADS, NOPE_DIM + V_DIM)
    wuk = wkv[:, :, :NOPE_DIM].reshape(kv_lora, MLA_HEADS * NOPE_DIM).astype(BF16)
    wuv = wkv[:, :, NOPE_DIM:].reshape(kv_lora, MLA_HEADS * V_DIM).astype(BF16)
    c, s1, s2 = _rope_tables(s)
    scale = (NOPE_DIM + ROPE_DIM) ** -0.5 * math.log2(math.e)

    full = lambda a: pl.BlockSpec(a.shape, lambda i: (0, 0))
    row = lambda w: pl.BlockSpec((tm, w), lambda i: (i, 0))
    vec = pl.BlockSpec((1, d), lambda i: (0, 0))
    qg = q_norm_g.reshape(1, q_lora)
    kvg = kv_norm_g.reshape(1, kv_lora)
    return pl.pallas_call(
        functools.partial(_oddin_kernel, q_lora=q_lora, kv_lora=kv_lora, scale=scale),
        out_shape=[jax.ShapeDtypeStruct((s, MLA_HEADS * MLA_QK_PAD), BF16),
                   jax.ShapeDtypeStruct((s, MLA_HEADS * MLA_QK_PAD), BF16),
                   jax.ShapeDtypeStruct((s, MLA_HEADS * V_DIM), BF16),
                   jax.ShapeDtypeStruct((s, s5_dim), F32)],
        grid=(s // tm,),
        in_specs=[row(d), vec, vec, vec, full(w_in_p), full(qg), full(wq), full(kvg), full(wuk), full(wuv),
                  row(LANE), row(LANE), row(LANE)],
        out_specs=[row(MLA_HEADS * MLA_QK_PAD), row(MLA_HEADS * MLA_QK_PAD), row(MLA_HEADS * V_DIM),
                   row(s5_dim)],
        compiler_params=_cp(("parallel",)),
        name="odd_in_proj",
    )(x, g, sh, sc, w_in_p, qg, wq, kvg, wuk, wuv, c, s1, s2)


def _flash_kernel(q_ref, k_ref, v_ref, o_ref, m_ref, acc_ref, s0, s1, p0, p1, a0, a1, *, rows, nk):
    t = pl.program_id(0)
    j1 = lax.rem(jnp.maximum(t - 1, 0), nk)
    j2 = lax.rem(jnp.maximum(t - 2, 0), nk)
    tq = q_ref.shape[0]
    vd = v_ref.shape[1]

    @pl.when(t == 0)
    def _():
        m_ref[...] = jnp.full_like(m_ref, -jnp.inf)
        acc_ref[...] = jnp.zeros_like(acc_ref)
        for r in (s0, s1):
            r[...] = jnp.full_like(r, -jnp.inf)
        for r in (p0, p1):
            r[...] = jnp.zeros_like(r)
        for r in (a0, a1):
            r[...] = jnp.ones_like(r)

    def stages(s_rd, s_wr, p_rd, p_wr, a_rd, a_wr):
        v = v_ref[...]
        pv = jnp.dot(p_rd[...], jnp.concatenate([v, jnp.ones_like(v)], axis=1), preferred_element_type=F32)
        alpha_old = a_rd[...]
        acc_ref[:, :vd] = alpha_old * acc_ref[:, :vd] + pv[:, :vd]
        acc_ref[:, vd:] = alpha_old * acc_ref[:, vd:] + pv[:, vd:]
        live = t >= 1
        fresh = j1 == 0
        for c in range(tq // rows):
            rs = slice(c * rows, (c + 1) * rows)
            s = s_rd[rs, :]
            m_prev = jnp.where(fresh, -jnp.inf, m_ref[rs, :])
            m_cand = jnp.maximum(m_prev, jnp.max(s, axis=-1, keepdims=True))
            m_new = jnp.where(live, m_cand, m_prev)
            m_sub = jnp.where(live, m_cand, 0.0)
            p_wr[rs, :] = jnp.exp2((s - m_sub[:, :1]).astype(BF16))
            a_wr[rs, :] = jnp.where(live, jnp.exp2(m_prev - m_new), 1.0)
            m_ref[rs, :] = m_new
        s_wr[...] = lax.dot_general(q_ref[...], k_ref[...], (((1,), (1,)), ((), ())),
                                    preferred_element_type=F32)

    parity = lax.rem(t, 2)

    @pl.when(parity == 0)
    def _():
        stages(s1, s0, p1, p0, a1, a0)

    @pl.when(parity == 1)
    def _():
        stages(s0, s1, p0, p1, a0, a1)

    @pl.when((t >= 2) & (j2 == nk - 1))
    def _():
        o_ref[...] = (acc_ref[:, :vd] / acc_ref[:, vd:]).astype(o_ref.dtype)


def mla_attention(q, k, v, tq=1024, tk=1024, rows=32):
    s = q.shape[0]
    tq, tk = min(tq, s), min(tk, s)
    ni, nk = s // tq, s // tk
    n = MLA_HEADS * ni * nk

    def item(t, lag):
        w = jnp.clip(t - lag, 0, n - 1)
        return w // (ni * nk), lax.rem(w, ni * nk) // nk, lax.rem(w, nk)

    def q_map(t):
        h, i, _ = item(t, 0)
        return i, h

    def k_map(t):
        h, _, j = item(t, 0)
        return j, h

    def v_map(t):
        h, _, j = item(t, 2)
        return j, h

    def o_map(t):
        h, i, _ = item(t, 2)
        return i, h

    return pl.pallas_call(
        functools.partial(_flash_kernel, rows=rows, nk=nk),
        out_shape=jax.ShapeDtypeStruct((s, MLA_HEADS * V_DIM), BF16),
        grid=(n + 2,),
        in_specs=[pl.BlockSpec((tq, MLA_QK_PAD), q_map),
                  pl.BlockSpec((tk, MLA_QK_PAD), k_map),
                  pl.BlockSpec((tk, V_DIM), v_map)],
        out_specs=pl.BlockSpec((tq, V_DIM), o_map),
        scratch_shapes=[pltpu.VMEM((tq, V_DIM), F32), pltpu.VMEM((tq, 2 * V_DIM), F32),
                        pltpu.VMEM((tq, tk), F32), pltpu.VMEM((tq, tk), F32),
                        pltpu.VMEM((tq, tk), BF16), pltpu.VMEM((tq, tk), BF16),
                        pltpu.VMEM((tq, V_DIM), F32), pltpu.VMEM((tq, V_DIM), F32)],
        compiler_params=_cp(("arbitrary",)),
        name="mla_flash_attention",
    )(q, k, v)


def _s5_matrices(a_re, a_im, log_dt, b_re, b_im, c_re, c_im, chunk):
    L = chunk
    A = lax.complex(a_re.astype(F32), a_im.astype(F32))
    dt = jnp.exp(log_dt.astype(F32))[..., None]
    adt = A * dt
    a_bar = jnp.exp(adt)
    b_bar = ((a_bar - 1.0) / A)[..., None] * lax.complex(b_re.astype(F32), b_im.astype(F32))
    c_c = lax.complex(c_re.astype(F32), c_im.astype(F32))
    kk = jnp.arange(L + 1, dtype=F32)
    apow = jnp.exp(adt[:, :, None, :] * kk[None, None, :, None].astype(jnp.complex64))
    g, p, gc = b_bar.shape[1], b_bar.shape[2], b_bar.shape[3]

    ker = jnp.real(jnp.einsum('dgcp,dgkp,dgpi->dgkci', c_c, apow[:, :, :L], b_bar))
    lag = jnp.arange(L)[None, :] - jnp.arange(L)[:, None]
    tf = jnp.where((lag >= 0)[None, :, :, None, None], ker[0][:, jnp.clip(lag, 0, L - 1)], 0.0)
    tb = jnp.where((lag <= 0)[None, :, :, None, None], ker[1][:, jnp.clip(-lag, 0, L - 1)], 0.0)
    t_mat = jnp.transpose(tf + tb, (0, 1, 4, 2, 3)).reshape(g, L * gc, L * gc)

    pf = apow[0][:, ::-1][:, 1:][:, :, :, None] * b_bar[0][:, None]
    pb = apow[1][:, :L][:, :, :, None] * b_bar[1][:, None]
    def p_lay(m):
        return jnp.transpose(m, (0, 1, 3, 2)).reshape(g, L * gc, p)
    p_mat = jnp.concatenate([p_lay(jnp.real(pf)), p_lay(jnp.imag(pf)),
                             p_lay(jnp.real(pb)), p_lay(jnp.imag(pb))], axis=-1)

    wf = c_c[0][:, None] * apow[0][:, 1:][:, :, None, :]
    wb = c_c[1][:, None] * apow[1][:, ::-1][:, :L][:, :, None, :]
    def q_lay(m):
        return jnp.transpose(m, (0, 3, 1, 2)).reshape(g, p, L * gc)
    q_mat = jnp.concatenate([q_lay(jnp.real(wf)), q_lay(-jnp.imag(wf)),
                             q_lay(jnp.real(wb)), q_lay(-jnp.imag(wb))], axis=1)

    al = apow[:, :, L]
    dec = jnp.stack([jnp.real(al[0]), jnp.imag(al[0]), jnp.real(al[1]), jnp.imag(al[1])]).reshape(4, g * p)
    return t_mat, p_mat, q_mat, dec


def _s5_state_kernel(u_ref, p_ref, fre_ref, fim_ref, bre_ref, bim_ref):
    outs = [[], [], [], []]
    for gi in range(u_ref.shape[0]):
        xe = jnp.dot(u_ref[gi].astype(BF16), p_ref[gi], preferred_element_type=F32)
        for part in range(4):
            outs[part].append(xe[:, part * S5_STATE:(part + 1) * S5_STATE])
    for part, ref in enumerate((fre_ref, fim_ref, bre_ref, bim_ref)):
        ref[...] = jnp.concatenate(outs[part], axis=-1)


def _s5_scan_kernel(fre_ref, fim_ref, bre_ref, bim_ref, dec_ref, ofre_ref, ofim_ref, obre_ref, obim_ref):
    nc = fre_ref.shape[0]
    w = fre_ref.shape[1]
    far, fai, bar, bai = (dec_ref[i:i + 1, :] for i in range(4))

    def fwd(c, st):
        re, im = st
        ofre_ref[pl.ds(c, 1), :] = re
        ofim_ref[pl.ds(c, 1), :] = im
        return (far * re - fai * im + fre_ref[pl.ds(c, 1), :],
                far * im + fai * re + fim_ref[pl.ds(c, 1), :])

    def bwd(i, st):
        c = nc - 1 - i
        re, im = st
        obre_ref[pl.ds(c, 1), :] = re
        obim_ref[pl.ds(c, 1), :] = im
        return (bar * re - bai * im + bre_ref[pl.ds(c, 1), :],
                bar * im + bai * re + bim_ref[pl.ds(c, 1), :])

    zero = (jnp.zeros((1, w), F32), jnp.zeros((1, w), F32))
    lax.fori_loop(0, nc, fwd, zero)
    lax.fori_loop(0, nc, bwd, zero)


def _s5_out_kernel(u_ref, t_ref, q_ref, fre_ref, fim_ref, bre_ref, bim_ref, y_ref):
    for gi in range(u_ref.shape[0]):
        lanes = slice(gi * S5_STATE, (gi + 1) * S5_STATE)
        y = jnp.dot(u_ref[gi].astype(BF16), t_ref[gi], preferred_element_type=F32)
        xin = jnp.concatenate([r[:, lanes] for r in (fre_ref, fim_ref, bre_ref, bim_ref)], axis=-1)
        y_ref[gi] = y + jnp.dot(xin.astype(BF16), q_ref[gi], preferred_element_type=F32)


def s5_bidirectional(u, a_re, a_im, log_dt, b_re, b_im, c_re, c_im, chunk=S5_CHUNK):
    s, dim = u.shape
    gc, p = S5_GROUP, S5_STATE
    g = dim // gc
    chunk = min(chunk, s)
    nc = s // chunk
    gp = LANE // p
    t_mat, p_mat, q_mat, dec = _s5_matrices(a_re, a_im, log_dt, b_re, b_im, c_re, c_im, chunk)
    t_mat, p_mat, q_mat = t_mat.astype(BF16), p_mat.astype(BF16), q_mat.astype(BF16)
    u_t = jnp.transpose(u.reshape(nc, chunk, g, gc), (2, 0, 1, 3)).reshape(g, nc, chunk * gc)

    grp = lambda a, b: pl.BlockSpec((gp, a, b), lambda i: (i, 0, 0))
    st = pl.BlockSpec((nc, LANE), lambda i: (0, i))
    st_shape = jax.ShapeDtypeStruct((nc, g * p), F32)
    xe = pl.pallas_call(
        _s5_state_kernel,
        out_shape=[st_shape] * 4,
        grid=(g // gp,),
        in_specs=[grp(nc, chunk * gc), grp(chunk * gc, 4 * p)],
        out_specs=[st] * 4,
        compiler_params=_cp(("parallel",)),
        name="s5_chunk_states",
    )(u_t, p_mat)

    whole = lambda shape: pl.BlockSpec(shape, lambda: (0,) * len(shape))
    xin = pl.pallas_call(
        _s5_scan_kernel,
        out_shape=[st_shape] * 4,
        in_specs=[whole((nc, g * p))] * 4 + [whole((4, g * p))],
        out_specs=[whole((nc, g * p))] * 4,
        compiler_params=pltpu.CompilerParams(vmem_limit_bytes=VMEM_LIMIT_MB * 1024 * 1024),
        name="s5_boundary_scan",
    )(*xe, dec)

    y_t = pl.pallas_call(
        _s5_out_kernel,
        out_shape=jax.ShapeDtypeStruct((g, nc, chunk * gc), F32),
        grid=(g // gp,),
        in_specs=[grp(nc, chunk * gc), grp(chunk * gc, chunk * gc), grp(4 * p, chunk * gc)] + [st] * 4,
        out_specs=grp(nc, chunk * gc),
        compiler_params=_cp(("parallel",)),
        name="s5_outputs",
    )(u_t, t_mat, q_mat, *xin)
    return jnp.transpose(y_t.reshape(g, nc, chunk, gc), (1, 2, 0, 3)).reshape(s, dim)


def _s5_gate_kernel(y_ref, u_ref, d_ref, w_ref, o_ref):
    y = y_ref[...] + d_ref[...] * u_ref[...]
    y = 0.5 * y * (1.0 + jnp.tanh(math.sqrt(2.0 / math.pi) * (y + 0.044715 * (y * y * y))))
    z = jnp.dot(y.astype(BF16), w_ref[...], preferred_element_type=F32)
    o_ref[...] = (y * _sigmoid(z)).astype(o_ref.dtype)


def s5_gate(y, u, d_skip, w_glu, tm=1024):
    s, dim = y.shape
    tm = min(tm, s)
    row = pl.BlockSpec((tm, dim), lambda i: (i, 0))
    return pl.pallas_call(
        _s5_gate_kernel,
        out_shape=jax.ShapeDtypeStruct((s, dim), BF16),
        grid=(s // tm,),
        in_specs=[row, row, pl.BlockSpec((1, dim), lambda i: (0, 0)), pl.BlockSpec((dim, dim), lambda i: (0, 0))],
        out_specs=row,
        compiler_params=_cp(("parallel",)),
        name="s5_gate",
    )(y, u, d_skip.reshape(1, dim), w_glu.astype(BF16))


def _router_kernel(x_ref, g_ref, sh_ref, sc_ref, rw_ref, h_ref, info_ref):
    h = _modnorm(x_ref[...], g_ref[...], sh_ref[...], sc_ref[...])
    h_ref[...] = h.astype(h_ref.dtype)
    logits = jnp.dot(h, rw_ref[...], preferred_element_type=F32, precision=lax.Precision.HIGHEST)
    lane = lax.broadcasted_iota(jnp.int32, logits.shape, 1)
    logits = jnp.where(lane < N_EXPERTS, logits, -jnp.inf)
    m1 = jnp.max(logits, axis=-1, keepdims=True)
    i1 = jnp.min(jnp.where(logits == m1, lane, LANE), axis=-1, keepdims=True)
    rest = jnp.where(lane == i1, -jnp.inf, logits)
    m2 = jnp.max(rest, axis=-1, keepdims=True)
    i2 = jnp.min(jnp.where(rest == m2, lane, LANE), axis=-1, keepdims=True)
    e = jnp.exp(m2 - m1)
    g1 = 1.0 / (1.0 + e)
    g2 = e / (1.0 + e)
    info = jnp.where(lane == 0, i1.astype(F32),
                     jnp.where(lane == 1, i2.astype(F32),
                               jnp.where(lane == 2, g1, jnp.where(lane == 3, g2, 0.0))))
    info_ref[...] = info


def moe_router(x, g, sh, sc, router_w, tm=512):
    s, d = x.shape
    tm = min(tm, s)
    rw = jnp.pad(router_w, ((0, 0), (0, LANE - router_w.shape[1])))
    vec = pl.BlockSpec((1, d), lambda i: (0, 0))
    return pl.pallas_call(
        _router_kernel,
        out_shape=[jax.ShapeDtypeStruct((s, d), BF16), jax.ShapeDtypeStruct((s, LANE), F32)],
        grid=(s // tm,),
        in_specs=[pl.BlockSpec((tm, d), lambda i: (i, 0)), vec, vec, vec,
                  pl.BlockSpec((d, LANE), lambda i: (0, 0))],
        out_specs=[pl.BlockSpec((tm, d), lambda i: (i, 0)), pl.BlockSpec((tm, LANE), lambda i: (i, 0))],
        compiler_params=_cp(("parallel",)),
        name="moe_router",
    )(x, g, sh, sc, rw)


def _expert_kernel(te_ref, tv_ref, h_ref, w1_ref, w3_ref, w2_ref, o_ref, acc_ref):
    t = pl.program_id(0)
    f = pl.program_id(1)
    valid = tv_ref[t] > 0

    @pl.when(f == 0)
    def _():
        acc_ref[...] = jnp.zeros_like(acc_ref)

    @pl.when(valid)
    def _():
        h = h_ref[...]
        u = jnp.dot(h, w1_ref[...].astype(BF16), preferred_element_type=F32)
        v = jnp.dot(h, w3_ref[...].astype(BF16), preferred_element_type=F32)
        acc_ref[...] += jnp.dot((_silu(u) * v).astype(BF16), w2_ref[...].astype(BF16),
                                preferred_element_type=F32)

    @pl.when(f == pl.num_programs(1) - 1)
    def _():
        o_ref[...] = acc_ref[...].astype(o_ref.dtype)


def expert_ffn(h_sorted, tile_expert, tile_valid, w1, w3, w2, tm, tf=256):
    n, d = h_sorted.shape
    f = w1.shape[2]
    tf = min(tf, f)
    nf = f // tf

    def fblk(j, t, tv):
        return jnp.where(tv[t] > 0, j, nf - 1)

    return pl.pallas_call(
        _expert_kernel,
        out_shape=jax.ShapeDtypeStruct((n, d), BF16),
        grid_spec=pltpu.PrefetchScalarGridSpec(
            num_scalar_prefetch=2,
            grid=(n // tm, nf),
            in_specs=[pl.BlockSpec((tm, d), lambda t, j, te, tv: (t, 0)),
                      pl.BlockSpec((None, d, tf), lambda t, j, te, tv: (te[t], 0, fblk(j, t, tv))),
                      pl.BlockSpec((None, d, tf), lambda t, j, te, tv: (te[t], 0, fblk(j, t, tv))),
                      pl.BlockSpec((None, tf, d), lambda t, j, te, tv: (te[t], fblk(j, t, tv), 0))],
            out_specs=pl.BlockSpec((tm, d), lambda t, j, te, tv: (t, 0)),
            scratch_shapes=[pltpu.VMEM((tm, d), F32)]),
        compiler_params=_cp(("arbitrary", "arbitrary")),
        name="expert_ffn",
    )(tile_expert, tile_valid, h_sorted, w1, w3, w2)


def _dispatch(idx, tm):
    s = idx.shape[0]
    e_flat = idx.reshape(-1)
    onehot = (e_flat[:, None] == jnp.arange(N_EXPERTS)[None, :]).astype(jnp.int32)
    csum = jnp.cumsum(onehot, axis=0)
    rank = jnp.sum((csum - onehot) * onehot, axis=1)
    counts = csum[-1]
    padded = ((counts + tm - 1) // tm) * tm
    gend = jnp.cumsum(padded)
    slot = (gend - padded)[e_flat] + rank
    nt = (2 * s) // tm + N_EXPERTS
    src = jnp.zeros((nt * tm,), jnp.int32).at[slot].set(jnp.arange(2 * s, dtype=jnp.int32) // 2)
    tstart = jnp.arange(nt, dtype=jnp.int32) * tm
    valid = tstart < gend[-1]
    te = jnp.minimum(jnp.sum((tstart[:, None] >= gend[None, :]).astype(jnp.int32), axis=1), N_EXPERTS - 1)
    nvalid = gend[-1] // tm
    te = jnp.where(valid, te, te[jnp.maximum(nvalid - 1, 0)])
    return slot.reshape(s, 2), src, te.astype(jnp.int32), valid.astype(jnp.int32)


def _combine_kernel(ya_ref, yb_ref, info_ref, x_ref, g_ref, gt_ref, o_ref):
    info = info_ref[...]
    y = info[:, 2:3] * ya_ref[...].astype(F32) + info[:, 3:4] * yb_ref[...].astype(F32)
    o_ref[...] = x_ref[...] + gt_ref[...] * (_rms(y) * g_ref[...])


def moe_combine(ya, yb, info, x, g, gt, tm=512):
    s, d = x.shape
    tm = min(tm, s)
    row = lambda w: pl.BlockSpec((tm, w), lambda i: (i, 0))
    vec = pl.BlockSpec((1, d), lambda i: (0, 0))
    return pl.pallas_call(
        _combine_kernel,
        out_shape=jax.ShapeDtypeStruct((s, d), F32),
        grid=(s // tm,),
        in_specs=[row(d), row(d), row(LANE), row(d), vec, vec],
        out_specs=row(d),
        compiler_params=_cp(("parallel",)),
        name="moe_combine",
    )(ya, yb, info, x, g, gt)


def moe_sublayer(x, g, sh, sc, g2, gt, router_w, w1, w3, w2, tm_e=1024):
    s = x.shape[0]
    tm_e = min(tm_e, s)
    h, info = moe_router(x, g, sh, sc, router_w)
    idx = info[:, :2].astype(jnp.int32)
    slot, src, te, tv = _dispatch(idx, tm_e)
    y_sorted = expert_ffn(jnp.take(h, src, axis=0), te, tv, w1, w3, w2, tm_e)
    ya = jnp.take(y_sorted, slot[:, 0], axis=0)
    yb = jnp.take(y_sorted, slot[:, 1], axis=0)
    return moe_combine(ya, yb, info, x, g2, gt)


def kernel(x, c, t5_table, norm_g, ada_w, ada_b, e_w_in, e_conv_w, e_conv_b, e_cln_g, e_cln_b, e_w_out, e_ffn_w1, e_ffn_w3, e_ffn_w2, o_w_in, o_q_norm_g, o_w_uq, o_kv_norm_g, o_w_ukv, s5_a_re, s5_a_im, s5_log_dt, s5_b_re, s5_b_im, s5_c_re, s5_c_im, s5_d, s5_w_glu, o_w_out, router_w, moe_w1, moe_w3, moe_w2):
    bsz, seq, d = x.shape
    assert bsz == 1
    depth = norm_g.shape[0]
    conv_dim = e_conv_w.shape[2]
    xs = x.reshape(seq, d)
    mod = adaln(c, ada_w, ada_b)
    for layer in range(depth):
        i = layer // 2
        sh_m, sc_m, gt_m, sh_f, sc_f, gt_f = (mod[layer, :, k * d:(k + 1) * d] for k in range(6))
        g = [norm_g[layer, k].reshape(1, d) for k in range(4)]
        if layer % 2 == 0:
            z = norm_mod_matmul(xs, g[0], sh_m, sc_m, e_w_in[i].astype(BF16))
            a = conformer_conv(z, e_conv_w[i], e_conv_b[i], e_cln_g[i], e_cln_b[i])
            o = dilated_attention(z, t5_table, 2 * conv_dim, (z.shape[1] - 2 * conv_dim) // 3)
            xs = out_proj_residual(a, o, e_w_out[i].astype(BF16), xs, g[1], gt_m)
            xs = ffn_sublayer(xs, g[2], sh_f, sc_f, g[3], gt_f, e_ffn_w1[i].astype(BF16),
                              e_ffn_w3[i].astype(BF16), e_ffn_w2[i].astype(BF16))
        else:
            q, k, v, u = odd_in_proj(xs, g[0], sh_m, sc_m, o_w_in[i], o_q_norm_g[i], o_w_uq[i],
                                     o_kv_norm_g[i], o_w_ukv[i])
            o_mla = mla_attention(q, k, v)
            y = s5_bidirectional(u, s5_a_re[i], s5_a_im[i], s5_log_dt[i], s5_b_re[i], s5_b_im[i],
                                 s5_c_re[i], s5_c_im[i])
            y = s5_gate(y, u, s5_d[i], s5_w_glu[i])
            xs = out_proj_residual(o_mla, y, o_w_out[i].astype(BF16), xs, g[1], gt_m)
            xs = moe_sublayer(xs, g[2], sh_f, sc_f, g[3], gt_f, router_w[i], moe_w1[i], moe_w3[i], moe_w2[i])
    return xs.reshape(bsz, seq, d)
```

```python
import functools
import math

import jax
import jax.numpy as jnp
from jax import lax
from jax.experimental import pallas as pl
from jax.experimental.pallas import tpu as pltpu

F32 = jnp.float32
BF16 = jnp.bfloat16

RMS_EPS = 1e-6
LN_EPS = 1e-5
NEG_INF = -1e30

CONV_WIDTH = 31
CONV_HALO = 16
DIL_HEAD_DIM = 64
DIL_CONFIGS = ((128, 1), (512, 4), (2048, 16))
DIL_HALF = 64
DIL_TILE = 1024
DIL_QB = 256
DIL_MERGE = 4
N_BUCKETS = 32
T5_MAX_DIST = DIL_CONFIGS[-1][0] // 2
MLA_HEADS = 12
NOPE_DIM = 128
ROPE_DIM = 64
V_DIM = 128
ROPE_BASE = 10000.0
MLA_QK_PAD = 256
S5_GROUP = 16
S5_STATE = 64
S5_CHUNK = 32
N_EXPERTS = 8
EXPERT_TILE = 896
LANE = 128
SUBLANES = 8
VMEM_LIMIT_MB = 56


def _cp(sem, vmem_mb=VMEM_LIMIT_MB):
    return pltpu.CompilerParams(dimension_semantics=sem, vmem_limit_bytes=vmem_mb * 1024 * 1024)


def _rms(x):
    return x * lax.rsqrt(jnp.mean(x * x, axis=-1, keepdims=True) + RMS_EPS)


def _modnorm(x, g, sh, sc):
    return (_rms(x) * g) * (1.0 + sc) + sh


def _sigmoid(x):
    return 1.0 / (1.0 + jnp.exp(-x))


def _silu(x):
    return x * _sigmoid(x)


def _adaln_kernel(c_ref, w_ref, b_ref, o_ref, cb_ref):
    @pl.when((pl.program_id(0) == 0) & (pl.program_id(1) == 0))
    def _():
        cc = c_ref[...]
        cb_ref[...] = jnp.broadcast_to(_silu(cc), cb_ref.shape)

    cb = cb_ref[...]
    for j in range(o_ref.shape[-1] // LANE):
        sl = slice(j * LANE, (j + 1) * LANE)
        o_ref[:, sl] = jnp.sum(w_ref[:, sl] * cb, axis=0, keepdims=True) + b_ref[:, sl]


def adaln(c, ada_w, ada_b, tn=1024):
    nl, d, n = ada_w.shape
    out = pl.pallas_call(
        _adaln_kernel,
        out_shape=jax.ShapeDtypeStruct((nl, 1, n), F32),
        grid=(nl, n // tn),
        in_specs=[pl.BlockSpec((d, 1), lambda l, j: (0, 0)),
                  pl.BlockSpec((None, d, tn), lambda l, j: (l, 0, j)),
                  pl.BlockSpec((None, 1, tn), lambda l, j: (l, 0, j))],
        out_specs=pl.BlockSpec((None, 1, tn), lambda l, j: (l, 0, j)),
        scratch_shapes=[pltpu.VMEM((d, LANE), F32)],
        compiler_params=_cp(("arbitrary", "arbitrary")),
        name="adaln",
    )(c.reshape(d, 1), ada_w, ada_b.reshape(nl, 1, n))
    return out


def _nmm_kernel(x_ref, g_ref, sh_ref, sc_ref, w_ref, o_ref, h_ref):
    @pl.when(pl.program_id(1) == 0)
    def _():
        h_ref[...] = _modnorm(x_ref[...], g_ref[...], sh_ref[...], sc_ref[...]).astype(h_ref.dtype)

    o_ref[...] = jnp.dot(h_ref[...], w_ref[...], preferred_element_type=F32).astype(o_ref.dtype)


def norm_mod_matmul(x, g, sh, sc, w, tm=512, tn=1024):
    s, d = x.shape
    n = w.shape[1]
    tm, tn = min(tm, s), min(tn, n)
    vec = pl.BlockSpec((1, d), lambda i, j: (0, 0))
    return pl.pallas_call(
        _nmm_kernel,
        out_shape=jax.ShapeDtypeStruct((s, n), BF16),
        grid=(s // tm, n // tn),
        in_specs=[pl.BlockSpec((tm, d), lambda i, j: (i, 0)), vec, vec, vec,
                  pl.BlockSpec((d, tn), lambda i, j: (0, j))],
        out_specs=pl.BlockSpec((tm, tn), lambda i, j: (i, j)),
        scratch_shapes=[pltpu.VMEM((tm, d), BF16)],
        compiler_params=_cp(("parallel", "arbitrary")),
        name="norm_mod_matmul",
    )(x, g, sh, sc, w)


def _conv_kernel(av_ref, ag_ref, avp_ref, agp_ref, avn_ref, agn_ref, w_ref, b_ref, lg_ref, lb_ref,
                 o_ref, buf_ref, sh_ref, *, rows):
    i = pl.program_id(0)
    n = pl.num_programs(0)
    ts = av_ref.shape[0]

    def glu(v_ref, g_ref):
        return v_ref[...].astype(F32) * _sigmoid(g_ref[...].astype(F32))

    buf_ref[0:CONV_HALO, :] = jnp.where(i > 0, glu(avp_ref, agp_ref), 0.0)
    buf_ref[CONV_HALO:CONV_HALO + ts, :] = glu(av_ref, ag_ref)
    buf_ref[CONV_HALO + ts:2 * CONV_HALO + ts, :] = jnp.where(i < n - 1, glu(avn_ref, agn_ref), 0.0)
    buf_ref[2 * CONV_HALO + ts:, :] = jnp.zeros((SUBLANES, buf_ref.shape[1]), F32)

    span = ts + 2 * CONV_HALO
    for o in range(SUBLANES):
        sh_ref[o] = buf_ref[o:o + span, :]

    off = CONV_HALO - CONV_WIDTH // 2
    for r in range(ts // rows):
        acc = jnp.broadcast_to(b_ref[...], (rows, b_ref.shape[-1]))
        for k in range(CONV_WIDTH):
            lo = r * rows + k + off
            base = lo - lo % SUBLANES
            acc = acc + w_ref[k:k + 1, :] * sh_ref[lo % SUBLANES, base:base + rows, :]
        mu = jnp.mean(acc, axis=-1, keepdims=True)
        dlt = acc - mu
        var = jnp.mean(dlt * dlt, axis=-1, keepdims=True)
        y = dlt * lax.rsqrt(var + LN_EPS) * lg_ref[...] + lb_ref[...]
        o_ref[r * rows:(r + 1) * rows, :] = _silu(y).astype(o_ref.dtype)


def conformer_conv(z, conv_w, conv_b, cln_g, cln_b, ts=256, rows=32):
    s = z.shape[0]
    c = conv_w.shape[1]
    ts = min(ts, s)
    hb = ts // CONV_HALO
    nh = s // CONV_HALO
    main = lambda col: pl.BlockSpec((ts, c), lambda i: (i, col))
    prev = lambda col: pl.BlockSpec((CONV_HALO, c), lambda i: (jnp.maximum(i * hb - 1, 0), col))
    nxt = lambda col: pl.BlockSpec((CONV_HALO, c), lambda i: (jnp.minimum((i + 1) * hb, nh - 1), col))
    vec = lambda r: pl.BlockSpec((r, c), lambda i: (0, 0))
    return pl.pallas_call(
        functools.partial(_conv_kernel, rows=min(rows, ts)),
        out_shape=jax.ShapeDtypeStruct((s, c), BF16),
        grid=(s // ts,),
        in_specs=[main(0), main(1), prev(0), prev(1), nxt(0), nxt(1),
                  vec(CONV_WIDTH), vec(1), vec(1), vec(1)],
        out_specs=pl.BlockSpec((ts, c), lambda i: (i, 0)),
        scratch_shapes=[pltpu.VMEM((ts + 2 * CONV_HALO + SUBLANES, c), F32),
                        pltpu.VMEM((SUBLANES, ts + 2 * CONV_HALO, c), F32)],
        compiler_params=_cp(("parallel",)),
        name="conformer_conv",
    )(z, z, z, z, z, z, conv_w, conv_b.reshape(1, c), cln_g.reshape(1, c), cln_b.reshape(1, c))


def _t5_bucket(rel):
    half = N_BUCKETS // 2
    exact = half // 2
    n = jnp.abs(rel)
    large = exact + (jnp.log(jnp.maximum(n, 1).astype(F32) / exact)
                     / math.log(T5_MAX_DIST / exact) * (half - exact)).astype(jnp.int32)
    large = jnp.minimum(large, half - 1)
    return jnp.where(rel > 0, half, 0) + jnp.where(n < exact, n, large)


def _toeplitz(w, rows, cols):
    n = w.shape[-1]
    wp = jnp.concatenate([w, jnp.zeros(w.shape[:-1] + (1,), w.dtype)], axis=-1)
    flat = jnp.tile(wp, (1,) * (w.ndim - 1) + (rows,))[..., :rows * n]
    return flat.reshape(w.shape[:-1] + (rows, n))[..., rows - 1:rows - 1 + cols]


def _band_bias(t5_table, dil, qb):
    kl = qb + 2 * DIL_HALF
    dist = jnp.arange(qb + kl - 1) - (qb - 1) - DIL_HALF
    diag = jnp.where((jnp.abs(dist) <= DIL_HALF)[None], t5_table[_t5_bucket(dist * dil)].T.astype(F32), NEG_INF)
    return _toeplitz(diag, qb, kl)


def _band_group(q, kw, vw, bias_ref, pen):
    r = q.shape[0]
    is_a = lax.broadcasted_iota(jnp.int32, (1, LANE), 1) < DIL_HEAD_DIM
    q2 = jnp.concatenate([jnp.where(is_a, q, 0.0), jnp.where(is_a, 0.0, q)], axis=0).astype(BF16)
    s = lax.dot_general(q2, kw.astype(BF16), (((1,), (1,)), ((), ())), preferred_element_type=F32)
    s = s * (DIL_HEAD_DIM ** -0.5) + bias_ref[...]
    if pen is not None:
        s = s + pen
    m = jnp.max(s, axis=-1, keepdims=True)
    p = jnp.exp(s - m)
    l = jnp.sum(p, axis=-1, keepdims=True)
    pv = jnp.dot(p.astype(BF16), vw.astype(BF16), preferred_element_type=F32)
    pick = lambda x: jnp.where(is_a, x[:r], x[r:])
    return pick(pv), pick(m), pick(l)


def _dil_kernel(q_ref, kp_ref, kc_ref, kn_ref, vp_ref, vc_ref, vn_ref, b1_ref, b4_ref, b16_ref, o_ref,
                qf, kf, vf, a1, m1, l1, a4, m4, l4, a16, m16, l16):
    i = pl.program_id(1)
    first = i == 0
    last = i == pl.num_programs(1) - 1
    t = DIL_TILE
    half = DIL_HALF

    qf[...] = q_ref[...].astype(F32)
    for dst, (p_ref, c_ref, n_ref) in ((kf, (kp_ref, kc_ref, kn_ref)), (vf, (vp_ref, vc_ref, vn_ref))):
        dst[0:t, :] = p_ref[...].astype(F32)
        dst[t:2 * t, :] = c_ref[...].astype(F32)
        dst[2 * t:3 * t, :] = n_ref[...].astype(F32)

    def store(refs, rows, vals):
        for ref, val in zip(refs, vals):
            ref[rows, :] = val

    qb = DIL_QB
    kl = qb + 2 * half
    col = lax.broadcasted_iota(jnp.int32, (1, kl), 1)
    lo = jnp.where((col < half) & first, NEG_INF, 0.0)
    hi = jnp.where((col >= qb + half) & last, NEG_INF, 0.0)

    nb = t // qb
    for b in range(nb):
        pen = lo + hi if nb == 1 else (lo if b == 0 else (hi if b == nb - 1 else None))
        k0 = t + b * qb - half
        rows = slice(b * qb, (b + 1) * qb)
        store((a1, m1, l1), rows, _band_group(qf[rows, :], kf[k0:k0 + kl, :], vf[k0:k0 + kl, :], b1_ref, pen))

    dil = 4
    for r in range(dil):
        qrows = pl.ds(r, qb, stride=dil)
        krows = pl.ds(t - half * dil + r, kl, stride=dil)
        store((a4, m4, l4), qrows, _band_group(qf[qrows, :], kf[krows, :], vf[krows, :], b4_ref, lo + hi))

    dil = 16
    q16 = t // dil
    k16 = q16 + 2 * half
    col = lax.rem(lax.broadcasted_iota(jnp.int32, (1, DIL_MERGE * k16), 1), k16)
    pen16 = (jnp.where((col < half) & first, NEG_INF, 0.0)
             + jnp.where((col >= q16 + half) & last, NEG_INF, 0.0))
    for g in range(dil // DIL_MERGE):
        res = range(g * DIL_MERGE, (g + 1) * DIL_MERGE)
        qrows = [pl.ds(r, q16, stride=dil) for r in res]
        krows = [pl.ds(t - half * dil + r, k16, stride=dil) for r in res]
        out = _band_group(jnp.concatenate([qf[rr, :] for rr in qrows], axis=0),
                          jnp.concatenate([kf[rr, :] for rr in krows], axis=0),
                          jnp.concatenate([vf[rr, :] for rr in krows], axis=0), b16_ref, pen16)
        for u, rr in enumerate(qrows):
            store((a16, m16, l16), rr, [x[u * q16:(u + 1) * q16] for x in out])

    mm = jnp.maximum(jnp.maximum(m1[...], m4[...]), m16[...])
    w1, w4, w16 = jnp.exp(m1[...] - mm), jnp.exp(m4[...] - mm), jnp.exp(m16[...] - mm)
    num = w1 * a1[...] + w4 * a4[...] + w16 * a16[...]
    den = w1 * l1[...] + w4 * l4[...] + w16 * l16[...]
    o_ref[...] = (num / den).astype(o_ref.dtype)


def dilated_attention(z, t5_table, col0, width):
    s, zw = z.shape
    t = DIL_TILE
    assert s % t == 0 and col0 % LANE == 0 and width % LANE == 0 and t // 4 == DIL_QB
    nt = s // t
    nh = t5_table.shape[1]
    cq, ck, cv = ((col0 + k * width) // LANE for k in range(3))
    pair = lambda b: b.reshape(nh // 2, 2 * b.shape[1], b.shape[2])
    b16 = _band_bias(t5_table, 16, t // 16)
    same = jnp.eye(DIL_MERGE, dtype=bool)[None, :, None, :, None]
    b16 = jnp.where(same, b16[:, None, :, None, :], NEG_INF)
    b16 = b16.reshape(nh, DIL_MERGE * (t // 16), -1)
    biases = [pair(_band_bias(t5_table, 1, DIL_QB)), pair(_band_bias(t5_table, 4, DIL_QB)), pair(b16)]

    def blk(col, shift):
        return pl.BlockSpec((t, LANE), lambda hg, i: (jnp.clip(i + shift, 0, nt - 1), col + hg))

    bias_spec = lambda b: pl.BlockSpec((None,) + b.shape[1:], lambda hg, i: (hg, 0, 0))
    return pl.pallas_call(
        _dil_kernel,
        out_shape=jax.ShapeDtypeStruct((s, width), BF16),
        grid=(width // LANE, nt),
        in_specs=[blk(cq, 0), blk(ck, -1), blk(ck, 0), blk(ck, 1), blk(cv, -1), blk(cv, 0), blk(cv, 1)]
                 + [bias_spec(b) for b in biases],
        out_specs=pl.BlockSpec((t, LANE), lambda hg, i: (i, hg)),
        scratch_shapes=[pltpu.VMEM((t, LANE), F32), pltpu.VMEM((3 * t, LANE), F32),
                        pltpu.VMEM((3 * t, LANE), F32)] + [pltpu.VMEM((t, LANE), F32)] * 9,
        compiler_params=_cp(("parallel", "parallel")),
        name="dilated_attention",
    )(*([z] * 7 + biases))


def _oproj_kernel(a1_ref, a2_ref, w_ref, x_ref, g_ref, gt_ref, o_ref):
    k1 = a1_ref.shape[1]
    y = jnp.dot(a1_ref[...], w_ref[0:k1, :], preferred_element_type=F32)
    y = y + jnp.dot(a2_ref[...], w_ref[k1:, :], preferred_element_type=F32)
    o_ref[...] = x_ref[...] + gt_ref[...] * (_rms(y) * g_ref[...])


def out_proj_residual(a1, a2, w, x, g, gt, tm=512):
    s, d = x.shape
    k1, k2 = a1.shape[1], a2.shape[1]
    tm = min(tm, s)
    vec = pl.BlockSpec((1, d), lambda i: (0, 0))
    return pl.pallas_call(
        _oproj_kernel,
        out_shape=jax.ShapeDtypeStruct((s, d), F32),
        grid=(s // tm,),
        in_specs=[pl.BlockSpec((tm, k1), lambda i: (i, 0)), pl.BlockSpec((tm, k2), lambda i: (i, 0)),
                  pl.BlockSpec((k1 + k2, d), lambda i: (0, 0)),
                  pl.BlockSpec((tm, d), lambda i: (i, 0)), vec, vec],
        out_specs=pl.BlockSpec((tm, d), lambda i: (i, 0)),
        compiler_params=_cp(("parallel",)),
        name="out_proj_residual",
    )(a1, a2, w, x, g, gt)


def _ffn_kernel(x_ref, g_ref, sh_ref, sc_ref, g2_ref, gt_ref, w1_ref, w3_ref, w2_ref, o_ref, h_ref):
    f = pl.program_id(1)

    @pl.when(f == 0)
    def _():
        h_ref[...] = _modnorm(x_ref[...], g_ref[...], sh_ref[...], sc_ref[...]).astype(h_ref.dtype)
        o_ref[...] = jnp.zeros_like(o_ref)

    h = h_ref[...]
    u = jnp.dot(h, w1_ref[...], preferred_element_type=F32)
    v = jnp.dot(h, w3_ref[...], preferred_element_type=F32)
    o_ref[...] += jnp.dot((_silu(u) * v).astype(BF16), w2_ref[...], preferred_element_type=F32)

    @pl.when(f == pl.num_programs(1) - 1)
    def _():
        o_ref[...] = x_ref[...] + gt_ref[...] * (_rms(o_ref[...]) * g2_ref[...])


def ffn_sublayer(x, g, sh, sc, g2, gt, w1, w3, w2, tm=1024, tf=256):
    s, d = x.shape
    f = w1.shape[1]
    tm, tf = min(tm, s), min(tf, f)
    vec = pl.BlockSpec((1, d), lambda i, j: (0, 0))
    return pl.pallas_call(
        _ffn_kernel,
        out_shape=jax.ShapeDtypeStruct((s, d), F32),
        grid=(s // tm, f // tf),
        in_specs=[pl.BlockSpec((tm, d), lambda i, j: (i, 0)), vec, vec, vec, vec, vec,
                  pl.BlockSpec((d, tf), lambda i, j: (0, j)), pl.BlockSpec((d, tf), lambda i, j: (0, j)),
                  pl.BlockSpec((tf, d), lambda i, j: (j, 0))],
        out_specs=pl.BlockSpec((tm, d), lambda i, j: (i, 0)),
        scratch_shapes=[pltpu.VMEM((tm, d), BF16)],
        compiler_params=_cp(("parallel", "arbitrary")),
        name="ffn_sublayer",
    )(x, g, sh, sc, g2, gt, w1, w3, w2)


def _rope_slab(t, c_ref, s1_ref, s2_ref):
    return (t * c_ref[...] + pltpu.roll(t, LANE - ROPE_DIM // 2, 1) * s1_ref[...]
            + pltpu.roll(t, ROPE_DIM // 2, 1) * s2_ref[...])


def _oddin_kernel(x_ref, g_ref, sh_ref, sc_ref, win_ref, qg_ref, wuq_ref, kvg_ref, wuk_ref, wuv_ref,
                  c_ref, s1_ref, s2_ref, q_ref, k_ref, v_ref, u_ref, *, q_lora, kv_lora, scale):
    h = _modnorm(x_ref[...], g_ref[...], sh_ref[...], sc_ref[...]).astype(BF16)
    z = jnp.dot(h, win_ref[...], preferred_element_type=F32)
    o_kv, o_pe, o_u = q_lora, q_lora + kv_lora, q_lora + kv_lora + LANE
    u_ref[...] = z[:, o_u:]

    qn = (_rms(z[:, :q_lora]) * qg_ref[...]).astype(BF16)
    q = jnp.dot(qn, wuq_ref[...], preferred_element_type=F32)
    kvn = (_rms(z[:, o_kv:o_pe]) * kvg_ref[...]).astype(BF16)
    kn = jnp.dot(kvn, wuk_ref[...], preferred_element_type=F32)
    v_ref[...] = jnp.dot(kvn, wuv_ref[...], preferred_element_type=F32).astype(v_ref.dtype)
    kpe = _rope_slab(z[:, o_pe:o_u], c_ref, s1_ref, s2_ref).astype(k_ref.dtype)

    for hh in range(MLA_HEADS):
        b = hh * MLA_QK_PAD
        q_ref[:, b:b + NOPE_DIM] = (q[:, b:b + NOPE_DIM] * scale).astype(q_ref.dtype)
        qpe = _rope_slab(q[:, b + NOPE_DIM:b + MLA_QK_PAD], c_ref, s1_ref, s2_ref)
        q_ref[:, b + NOPE_DIM:b + MLA_QK_PAD] = (qpe * scale).astype(q_ref.dtype)
        k_ref[:, b:b + NOPE_DIM] = kn[:, hh * NOPE_DIM:(hh + 1) * NOPE_DIM].astype(k_ref.dtype)
        k_ref[:, b + NOPE_DIM:b + MLA_QK_PAD] = kpe


def _rope_tables(seq):
    pos = jnp.arange(seq, dtype=F32)
    inv_freq = ROPE_BASE ** (-jnp.arange(0, ROPE_DIM, 2, dtype=F32) / ROPE_DIM)
    ang = pos[:, None] * inv_freq[None, :]
    cos, sin = jnp.cos(ang), jnp.sin(ang)
    zero = jnp.zeros_like(cos)
    c = jnp.concatenate([cos, cos, zero, zero], axis=-1)
    s1 = jnp.concatenate([-sin, zero, zero, zero], axis=-1)
    s2 = jnp.concatenate([zero, sin, zero, zero], axis=-1)
    return c, s1, s2


def odd_in_proj(x, g, sh, sc, w_in, q_norm_g, w_uq, kv_norm_g, w_ukv, tm=256):
    s, d = x.shape
    q_lora, kv_lora = q_norm_g.shape[0], kv_norm_g.shape[0]
    s5_dim = w_in.shape[1] - q_lora - kv_lora - ROPE_DIM
    tm = min(tm, s)
    w_in_p = jnp.concatenate(
        [w_in[:, :q_lora + kv_lora + ROPE_DIM], jnp.zeros((d, LANE - ROPE_DIM), w_in.dtype),
         w_in[:, q_lora + kv_lora + ROPE_DIM:]], axis=1).astype(BF16)
    wq = w_uq.reshape(q_lora, MLA_HEADS, NOPE_DIM + ROPE_DIM)
    wq = jnp.pad(wq, ((0, 0), (0, 0), (0, MLA_QK_PAD - NOPE_DIM - ROPE_DIM)))
    wq = wq.reshape(q_lora, MLA_HEADS * MLA_QK_PAD).astype(BF16)
    wkv = w_ukv.reshape(kv_lora, MLA_HEADS, NOPE_DIM + V_DIM)
    wuk = wkv[:, :, :NOPE_DIM].reshape(kv_lora, MLA_HEADS * NOPE_DIM).astype(BF16)
    wuv = wkv[:, :, NOPE_DIM:].reshape(kv_lora, MLA_HEADS * V_DIM).astype(BF16)
    c, s1, s2 = _rope_tables(s)
    scale = (NOPE_DIM + ROPE_DIM) ** -0.5 * math.log2(math.e)

    full = lambda a: pl.BlockSpec(a.shape, lambda i: (0, 0))
    row = lambda w: pl.BlockSpec((tm, w), lambda i: (i, 0))
    vec = pl.BlockSpec((1, d), lambda i: (0, 0))
    qg = q_norm_g.reshape(1, q_lora)
    kvg = kv_norm_g.reshape(1, kv_lora)
    return pl.pallas_call(
        functools.partial(_oddin_kernel, q_lora=q_lora, kv_lora=kv_lora, scale=scale),
        out_shape=[jax.ShapeDtypeStruct((s, MLA_HEADS * MLA_QK_PAD), BF16),
                   jax.ShapeDtypeStruct((s, MLA_HEADS * MLA_QK_PAD), BF16),
                   jax.ShapeDtypeStruct((s, MLA_HEADS * V_DIM), BF16),
                   jax.ShapeDtypeStruct((s, s5_dim), F32)],
        grid=(s // tm,),
        in_specs=[row(d), vec, vec, vec, full(w_in_p), full(qg), full(wq), full(kvg), full(wuk), full(wuv),
                  row(LANE), row(LANE), row(LANE)],
        out_specs=[row(MLA_HEADS * MLA_QK_PAD), row(MLA_HEADS * MLA_QK_PAD), row(MLA_HEADS * V_DIM),
                   row(s5_dim)],
        compiler_params=_cp(("parallel",)),
        name="odd_in_proj",
    )(x, g, sh, sc, w_in_p, qg, wq, kvg, wuk, wuv, c, s1, s2)


def _flash_kernel(q_ref, k_ref, v_ref, o_ref, m_ref, acc_ref, s0, s1, p0, p1, a0, a1, *, rows, nk):
    t = pl.program_id(0)
    j1 = lax.rem(jnp.maximum(t - 1, 0), nk)
    j2 = lax.rem(jnp.maximum(t - 2, 0), nk)
    tq = q_ref.shape[0]
    vd = v_ref.shape[1]

    @pl.when(t == 0)
    def _():
        m_ref[...] = jnp.full_like(m_ref, -jnp.inf)
        acc_ref[...] = jnp.zeros_like(acc_ref)
        for r in (s0, s1):
            r[...] = jnp.full_like(r, -jnp.inf)
        for r in (p0, p1):
            r[...] = jnp.zeros_like(r)
        for r in (a0, a1):
            r[...] = jnp.ones_like(r)

    def stages(s_rd, s_wr, p_rd, p_wr, a_rd, a_wr):
        v = v_ref[...]
        pv = jnp.dot(p_rd[...], jnp.concatenate([v, jnp.ones_like(v)], axis=1), preferred_element_type=F32)
        alpha_old = a_rd[...]
        acc_ref[:, :vd] = alpha_old * acc_ref[:, :vd] + pv[:, :vd]
        acc_ref[:, vd:] = alpha_old * acc_ref[:, vd:] + pv[:, vd:]
        live = t >= 1
        fresh = j1 == 0
        for c in range(tq // rows):
            rs = slice(c * rows, (c + 1) * rows)
            s = s_rd[rs, :]
            m_prev = jnp.where(fresh, -jnp.inf, m_ref[rs, :])
            m_cand = jnp.maximum(m_prev, jnp.max(s, axis=-1, keepdims=True))
            m_new = jnp.where(live, m_cand, m_prev)
            m_sub = jnp.where(live, m_cand, 0.0)
            p_wr[rs, :] = jnp.exp2((s - m_sub[:, :1]).astype(BF16))
            a_wr[rs, :] = jnp.where(live, jnp.exp2(m_prev - m_new), 1.0)
            m_ref[rs, :] = m_new
        s_wr[...] = lax.dot_general(q_ref[...], k_ref[...], (((1,), (1,)), ((), ())),
                                    preferred_element_type=F32)

    parity = lax.rem(t, 2)

    @pl.when(parity == 0)
    def _():
        stages(s1, s0, p1, p0, a1, a0)

    @pl.when(parity == 1)
    def _():
        stages(s0, s1, p0, p1, a0, a1)

    @pl.when((t >= 2) & (j2 == nk - 1))
    def _():
        o_ref[...] = (acc_ref[:, :vd] / acc_ref[:, vd:]).astype(o_ref.dtype)


def mla_attention(q, k, v, tq=1024, tk=1024, rows=32):
    s = q.shape[0]
    tq, tk = min(tq, s), min(tk, s)
    ni, nk = s // tq, s // tk
    n = MLA_HEADS * ni * nk

    def item(t, lag):
        w = jnp.clip(t - lag, 0, n - 1)
        return w // (ni * nk), lax.rem(w, ni * nk) // nk, lax.rem(w, nk)

    def q_map(t):
        h, i, _ = item(t, 0)
        return i, h

    def k_map(t):
        h, _, j = item(t, 0)
        return j, h

    def v_map(t):
        h, _, j = item(t, 2)
        return j, h

    def o_map(t):
        h, i, _ = item(t, 2)
        return i, h

    return pl.pallas_call(
        functools.partial(_flash_kernel, rows=rows, nk=nk),
        out_shape=jax.ShapeDtypeStruct((s, MLA_HEADS * V_DIM), BF16),
        grid=(n + 2,),
        in_specs=[pl.BlockSpec((tq, MLA_QK_PAD), q_map),
                  pl.BlockSpec((tk, MLA_QK_PAD), k_map),
                  pl.BlockSpec((tk, V_DIM), v_map)],
        out_specs=pl.BlockSpec((tq, V_DIM), o_map),
        scratch_shapes=[pltpu.VMEM((tq, V_DIM), F32), pltpu.VMEM((tq, 2 * V_DIM), F32),
                        pltpu.VMEM((tq, tk), F32), pltpu.VMEM((tq, tk), F32),
                        pltpu.VMEM((tq, tk), BF16), pltpu.VMEM((tq, tk), BF16),
                        pltpu.VMEM((tq, V_DIM), F32), pltpu.VMEM((tq, V_DIM), F32)],
        compiler_params=_cp(("arbitrary",)),
        name="mla_flash_attention",
    )(q, k, v)


def _s5_matrices(a_re, a_im, log_dt, b_re, b_im, c_re, c_im, chunk):
    L = chunk
    A = lax.complex(a_re.astype(F32), a_im.astype(F32))
    dt = jnp.exp(log_dt.astype(F32))[..., None]
    adt = A * dt
    a_bar = jnp.exp(adt)
    b_bar = ((a_bar - 1.0) / A)[..., None] * lax.complex(b_re.astype(F32), b_im.astype(F32))
    c_c = lax.complex(c_re.astype(F32), c_im.astype(F32))
    kk = jnp.arange(L + 1, dtype=F32)
    apow = jnp.exp(adt[:, :, None, :] * kk[None, None, :, None].astype(jnp.complex64))
    g, p, gc = b_bar.shape[1], b_bar.shape[2], b_bar.shape[3]

    ker = jnp.real(jnp.einsum('dgcp,dgkp,dgpi->dgkci', c_c, apow[:, :, :L], b_bar))
    kb = ker[1][:, ::-1]
    diag = jnp.concatenate([kb[:, :L - 1], ker[0][:, :1] + kb[:, L - 1:], ker[0][:, 1:]], axis=1)
    t_mat = _toeplitz(jnp.transpose(diag, (0, 2, 3, 1)), L, L)
    t_mat = jnp.transpose(t_mat, (0, 3, 2, 4, 1)).reshape(g, L * gc, L * gc)

    pf = apow[0][:, ::-1][:, 1:][:, :, :, None] * b_bar[0][:, None]
    pb = apow[1][:, :L][:, :, :, None] * b_bar[1][:, None]
    def p_lay(m):
        return jnp.transpose(m, (0, 1, 3, 2)).reshape(g, L * gc, p)
    p_mat = jnp.concatenate([p_lay(jnp.real(pf)), p_lay(jnp.imag(pf)),
                             p_lay(jnp.real(pb)), p_lay(jnp.imag(pb))], axis=-1)

    wf = c_c[0][:, None] * apow[0][:, 1:][:, :, None, :]
    wb = c_c[1][:, None] * apow[1][:, ::-1][:, :L][:, :, None, :]
    def q_lay(m):
        return jnp.transpose(m, (0, 3, 1, 2)).reshape(g, p, L * gc)
    q_mat = jnp.concatenate([q_lay(jnp.real(wf)), q_lay(-jnp.imag(wf)),
                             q_lay(jnp.real(wb)), q_lay(-jnp.imag(wb))], axis=1)

    al = apow[:, :, L]
    dec = jnp.stack([jnp.real(al[0]), jnp.imag(al[0]), jnp.real(al[1]), jnp.imag(al[1])]).reshape(4, g * p)
    return t_mat, p_mat, q_mat, dec


def _s5_state_kernel(u_ref, p_ref, fre_ref, fim_ref, bre_ref, bim_ref):
    outs = [[], [], [], []]
    for gi in range(u_ref.shape[0]):
        xe = jnp.dot(u_ref[gi].astype(BF16), p_ref[gi], preferred_element_type=F32)
        for part in range(4):
            outs[part].append(xe[:, part * S5_STATE:(part + 1) * S5_STATE])
    for part, ref in enumerate((fre_ref, fim_ref, bre_ref, bim_ref)):
        ref[...] = jnp.concatenate(outs[part], axis=-1)


def _s5_scan_kernel(fre_ref, fim_ref, bre_ref, bim_ref, dec_ref, ofre_ref, ofim_ref, obre_ref, obim_ref):
    nc = fre_ref.shape[0]
    w = fre_ref.shape[1]
    far, fai, bar, bai = (dec_ref[i:i + 1, :] for i in range(4))

    def fwd(c, st):
        re, im = st
        ofre_ref[pl.ds(c, 1), :] = re
        ofim_ref[pl.ds(c, 1), :] = im
        return (far * re - fai * im + fre_ref[pl.ds(c, 1), :],
                far * im + fai * re + fim_ref[pl.ds(c, 1), :])

    def bwd(i, st):
        c = nc - 1 - i
        re, im = st
        obre_ref[pl.ds(c, 1), :] = re
        obim_ref[pl.ds(c, 1), :] = im
        return (bar * re - bai * im + bre_ref[pl.ds(c, 1), :],
                bar * im + bai * re + bim_ref[pl.ds(c, 1), :])

    zero = (jnp.zeros((1, w), F32), jnp.zeros((1, w), F32))
    lax.fori_loop(0, nc, fwd, zero)
    lax.fori_loop(0, nc, bwd, zero)


def _s5_out_kernel(u_ref, t_ref, q_ref, fre_ref, fim_ref, bre_ref, bim_ref, y_ref):
    for gi in range(u_ref.shape[0]):
        lanes = slice(gi * S5_STATE, (gi + 1) * S5_STATE)
        y = jnp.dot(u_ref[gi].astype(BF16), t_ref[gi], preferred_element_type=F32)
        xin = jnp.concatenate([r[:, lanes] for r in (fre_ref, fim_ref, bre_ref, bim_ref)], axis=-1)
        y_ref[gi] = y + jnp.dot(xin.astype(BF16), q_ref[gi], preferred_element_type=F32)


def s5_bidirectional(u, a_re, a_im, log_dt, b_re, b_im, c_re, c_im, chunk=S5_CHUNK):
    s, dim = u.shape
    gc, p = S5_GROUP, S5_STATE
    g = dim // gc
    chunk = min(chunk, s)
    nc = s // chunk
    gp = LANE // p
    t_mat, p_mat, q_mat, dec = _s5_matrices(a_re, a_im, log_dt, b_re, b_im, c_re, c_im, chunk)
    t_mat, p_mat, q_mat = t_mat.astype(BF16), p_mat.astype(BF16), q_mat.astype(BF16)
    u_t = jnp.transpose(u.reshape(nc, chunk, g, gc), (2, 0, 1, 3)).reshape(g, nc, chunk * gc)

    grp = lambda a, b: pl.BlockSpec((gp, a, b), lambda i: (i, 0, 0))
    st = pl.BlockSpec((nc, LANE), lambda i: (0, i))
    st_shape = jax.ShapeDtypeStruct((nc, g * p), F32)
    xe = pl.pallas_call(
        _s5_state_kernel,
        out_shape=[st_shape] * 4,
        grid=(g // gp,),
        in_specs=[grp(nc, chunk * gc), grp(chunk * gc, 4 * p)],
        out_specs=[st] * 4,
        compiler_params=_cp(("parallel",)),
        name="s5_chunk_states",
    )(u_t, p_mat)

    whole = lambda shape: pl.BlockSpec(shape, lambda: (0,) * len(shape))
    xin = pl.pallas_call(
        _s5_scan_kernel,
        out_shape=[st_shape] * 4,
        in_specs=[whole((nc, g * p))] * 4 + [whole((4, g * p))],
        out_specs=[whole((nc, g * p))] * 4,
        compiler_params=pltpu.CompilerParams(vmem_limit_bytes=VMEM_LIMIT_MB * 1024 * 1024),
        name="s5_boundary_scan",
    )(*xe, dec)

    y_t = pl.pallas_call(
        _s5_out_kernel,
        out_shape=jax.ShapeDtypeStruct((g, nc, chunk * gc), F32),
        grid=(g // gp,),
        in_specs=[grp(nc, chunk * gc), grp(chunk * gc, chunk * gc), grp(4 * p, chunk * gc)] + [st] * 4,
        out_specs=grp(nc, chunk * gc),
        compiler_params=_cp(("parallel",)),
        name="s5_outputs",
    )(u_t, t_mat, q_mat, *xin)
    return jnp.transpose(y_t.reshape(g, nc, chunk, gc), (1, 2, 0, 3)).reshape(s, dim)


def _s5_gate_kernel(y_ref, u_ref, d_ref, w_ref, o_ref):
    y = y_ref[...] + d_ref[...] * u_ref[...]
    y = 0.5 * y * (1.0 + jnp.tanh(math.sqrt(2.0 / math.pi) * (y + 0.044715 * (y * y * y))))
    z = jnp.dot(y.astype(BF16), w_ref[...], preferred_element_type=F32)
    o_ref[...] = (y * _sigmoid(z)).astype(o_ref.dtype)


def s5_gate(y, u, d_skip, w_glu, tm=1024):
    s, dim = y.shape
    tm = min(tm, s)
    row = pl.BlockSpec((tm, dim), lambda i: (i, 0))
    return pl.pallas_call(
        _s5_gate_kernel,
        out_shape=jax.ShapeDtypeStruct((s, dim), BF16),
        grid=(s // tm,),
        in_specs=[row, row, pl.BlockSpec((1, dim), lambda i: (0, 0)), pl.BlockSpec((dim, dim), lambda i: (0, 0))],
        out_specs=row,
        compiler_params=_cp(("parallel",)),
        name="s5_gate",
    )(y, u, d_skip.reshape(1, dim), w_glu.astype(BF16))


def _router_kernel(x_ref, g_ref, sh_ref, sc_ref, rw_ref, h_ref, info_ref):
    h = _modnorm(x_ref[...], g_ref[...], sh_ref[...], sc_ref[...])
    half = h.shape[1] // 2
    bits = lambda v: lax.bitcast_convert_type(v.astype(BF16).astype(F32), jnp.uint32)
    h_ref[...] = (bits(h[:, :half]) >> 16) | (bits(h[:, half:]) & jnp.uint32(0xFFFF0000))
    logits = jnp.dot(h, rw_ref[...], preferred_element_type=F32, precision=lax.Precision.HIGHEST)
    lane = lax.broadcasted_iota(jnp.int32, logits.shape, 1)
    logits = jnp.where(lane < N_EXPERTS, logits, -jnp.inf)
    m1 = jnp.max(logits, axis=-1, keepdims=True)
    i1 = jnp.min(jnp.where(logits == m1, lane, LANE), axis=-1, keepdims=True)
    rest = jnp.where(lane == i1, -jnp.inf, logits)
    m2 = jnp.max(rest, axis=-1, keepdims=True)
    i2 = jnp.min(jnp.where(rest == m2, lane, LANE), axis=-1, keepdims=True)
    e = jnp.exp(m2 - m1)
    g1 = 1.0 / (1.0 + e)
    g2 = e / (1.0 + e)
    info = jnp.where(lane == 0, i1.astype(F32),
                     jnp.where(lane == 1, i2.astype(F32),
                               jnp.where(lane == 2, g1, jnp.where(lane == 3, g2, 0.0))))
    info_ref[...] = info


def moe_router(x, g, sh, sc, router_w, tm=512):
    s, d = x.shape
    tm = min(tm, s)
    rw = jnp.pad(router_w, ((0, 0), (0, LANE - router_w.shape[1])))
    vec = pl.BlockSpec((1, d), lambda i: (0, 0))
    return pl.pallas_call(
        _router_kernel,
        out_shape=[jax.ShapeDtypeStruct((s, d // 2), jnp.uint32), jax.ShapeDtypeStruct((s, LANE), F32)],
        grid=(s // tm,),
        in_specs=[pl.BlockSpec((tm, d), lambda i: (i, 0)), vec, vec, vec,
                  pl.BlockSpec((d, LANE), lambda i: (0, 0))],
        out_specs=[pl.BlockSpec((tm, d // 2), lambda i: (i, 0)), pl.BlockSpec((tm, LANE), lambda i: (i, 0))],
        compiler_params=_cp(("parallel",)),
        name="moe_router",
    )(x, g, sh, sc, rw)


def _expert_kernel(te_ref, tv_ref, src_ref, hp_hbm, w1_ref, w3_ref, w2_ref, o_ref, acc_ref, gbuf, h_ref, sem):
    t = pl.program_id(0)
    f = pl.program_id(1)
    nt = pl.num_programs(0)
    nf = pl.num_programs(1)
    tm = acc_ref.shape[0]
    per_step = tm // nf
    valid = tv_ref[t] > 0
    slot = lax.rem(t, 2)

    def row_copy(tile, buf, r):
        return pltpu.make_async_copy(hp_hbm.at[pl.ds(src_ref[tile * tm + r], 1)],
                                     gbuf.at[buf, pl.ds(r, 1)], sem.at[buf])

    def wait_half(buf):
        pltpu.make_async_copy(hp_hbm.at[pl.ds(0, tm)], gbuf.at[buf], sem.at[buf]).wait()

    @pl.when((t == 0) & (f == 0))
    def _():
        def issue(r, carry):
            row_copy(0, 0, r).start()
            return carry
        lax.fori_loop(0, tm, issue, 0)

    @pl.when((f == 0) & ((t == 0) | (tv_ref[jnp.maximum(t - 1, 0)] > 0)))
    def _():
        wait_half(slot)

    @pl.when((f == 0) & valid)
    def _():
        w = gbuf[slot]
        half = w.shape[1]
        h_ref[:, :half] = lax.bitcast_convert_type(w << 16, F32).astype(BF16)
        h_ref[:, half:] = lax.bitcast_convert_type(w & jnp.uint32(0xFFFF0000), F32).astype(BF16)

    @pl.when(f == 0)
    def _():
        acc_ref[...] = jnp.zeros_like(acc_ref)

    @pl.when(valid)
    def _():
        for u in range(per_step):
            row_copy(t + 1, 1 - slot, f * per_step + u).start()
        h = h_ref[...]
        u = jnp.dot(h, w1_ref[...].astype(BF16), preferred_element_type=F32)
        v = jnp.dot(h, w3_ref[...].astype(BF16), preferred_element_type=F32)
        acc_ref[...] += jnp.dot((_silu(u) * v).astype(BF16), w2_ref[...].astype(BF16),
                                preferred_element_type=F32)

    @pl.when(f == nf - 1)
    def _():
        o_ref[...] = acc_ref[...].astype(o_ref.dtype)

    @pl.when((t == nt - 1) & (f == nf - 1) & valid)
    def _():
        wait_half(1 - slot)


def expert_ffn(h_packed, src, tile_expert, tile_valid, w1, w3, w2, tm, tf=256):
    dh = h_packed.shape[1]
    d = 2 * dh
    n = src.shape[0] - tm
    f = w1.shape[2]
    tf = min(tf, f)
    nf = f // tf
    assert tm % nf == 0 and n % tm == 0

    def fblk(j, t, tv):
        return jnp.where(tv[t] > 0, j, nf - 1)

    return pl.pallas_call(
        _expert_kernel,
        out_shape=jax.ShapeDtypeStruct((n, d), BF16),
        grid_spec=pltpu.PrefetchScalarGridSpec(
            num_scalar_prefetch=3,
            grid=(n // tm, nf),
            in_specs=[pl.BlockSpec(memory_space=pl.ANY),
                      pl.BlockSpec((None, d, tf), lambda t, j, te, tv, sr: (te[t], 0, fblk(j, t, tv))),
                      pl.BlockSpec((None, d, tf), lambda t, j, te, tv, sr: (te[t], 0, fblk(j, t, tv))),
                      pl.BlockSpec((None, tf, d), lambda t, j, te, tv, sr: (te[t], fblk(j, t, tv), 0))],
            out_specs=pl.BlockSpec((tm, d), lambda t, j, te, tv, sr: (t, 0)),
            scratch_shapes=[pltpu.VMEM((tm, d), F32), pltpu.VMEM((2, tm, dh), jnp.uint32),
                            pltpu.VMEM((tm, d), BF16), pltpu.SemaphoreType.DMA((2,))]),
        compiler_params=_cp(("arbitrary", "arbitrary")),
        name="expert_ffn",
    )(tile_expert, tile_valid, src, h_packed, w1, w3, w2)


def _dispatch(idx, tm):
    s = idx.shape[0]
    e_flat = idx.reshape(-1)
    onehot = (e_flat[:, None] == jnp.arange(N_EXPERTS)[None, :]).astype(jnp.int32)
    csum = jnp.cumsum(onehot, axis=0)
    rank = jnp.sum((csum - onehot) * onehot, axis=1)
    counts = csum[-1]
    padded = ((counts + tm - 1) // tm) * tm
    gend = jnp.cumsum(padded)
    slot = (gend - padded)[e_flat] + rank
    nt = -(-(2 * s) // tm) + N_EXPERTS
    src = jnp.zeros(((nt + 1) * tm,), jnp.int32).at[slot].set(jnp.arange(2 * s, dtype=jnp.int32) // 2)
    tstart = jnp.arange(nt, dtype=jnp.int32) * tm
    valid = tstart < gend[-1]
    te = jnp.minimum(jnp.sum((tstart[:, None] >= gend[None, :]).astype(jnp.int32), axis=1), N_EXPERTS - 1)
    nvalid = gend[-1] // tm
    te = jnp.where(valid, te, te[jnp.maximum(nvalid - 1, 0)])
    return slot.reshape(s, 2), src, te.astype(jnp.int32), valid.astype(jnp.int32)


def _combine_kernel(ya_ref, yb_ref, info_ref, x_ref, g_ref, gt_ref, o_ref):
    info = info_ref[...]
    y = info[:, 2:3] * ya_ref[...].astype(F32) + info[:, 3:4] * yb_ref[...].astype(F32)
    o_ref[...] = x_ref[...] + gt_ref[...] * (_rms(y) * g_ref[...])


def moe_combine(ya, yb, info, x, g, gt, tm=512):
    s, d = x.shape
    tm = min(tm, s)
    row = lambda w: pl.BlockSpec((tm, w), lambda i: (i, 0))
    vec = pl.BlockSpec((1, d), lambda i: (0, 0))
    return pl.pallas_call(
        _combine_kernel,
        out_shape=jax.ShapeDtypeStruct((s, d), F32),
        grid=(s // tm,),
        in_specs=[row(d), row(d), row(LANE), row(d), vec, vec],
        out_specs=row(d),
        compiler_params=_cp(("parallel",)),
        name="moe_combine",
    )(ya, yb, info, x, g, gt)


def moe_sublayer(x, g, sh, sc, g2, gt, router_w, w1, w3, w2, tm_e=EXPERT_TILE):
    s = x.shape[0]
    tm_e = min(tm_e, s)
    h_packed, info = moe_router(x, g, sh, sc, router_w)
    idx = info[:, :2].astype(jnp.int32)
    slot, src, te, tv = _dispatch(idx, tm_e)
    y_sorted = expert_ffn(h_packed, src, te, tv, w1, w3, w2, tm_e)
    ya = jnp.take(y_sorted, slot[:, 0], axis=0)
    yb = jnp.take(y_sorted, slot[:, 1], axis=0)
    return moe_combine(ya, yb, info, x, g2, gt)


def kernel(x, c, t5_table, norm_g, ada_w, ada_b, e_w_in, e_conv_w, e_conv_b, e_cln_g, e_cln_b, e_w_out, e_ffn_w1, e_ffn_w3, e_ffn_w2, o_w_in, o_q_norm_g, o_w_uq, o_kv_norm_g, o_w_ukv, s5_a_re, s5_a_im, s5_log_dt, s5_b_re, s5_b_im, s5_c_re, s5_c_im, s5_d, s5_w_glu, o_w_out, router_w, moe_w1, moe_w3, moe_w2):
    bsz, seq, d = x.shape
    assert bsz == 1
    depth = norm_g.shape[0]
    conv_dim = e_conv_w.shape[2]
    xs = x.reshape(seq, d)
    mod = adaln(c, ada_w, ada_b)
    for layer in range(depth):
        i = layer // 2
        sh_m, sc_m, gt_m, sh_f, sc_f, gt_f = (mod[layer, :, k * d:(k + 1) * d] for k in range(6))
        g = [norm_g[layer, k].reshape(1, d) for k in range(4)]
        if layer % 2 == 0:
            z = norm_mod_matmul(xs, g[0], sh_m, sc_m, e_w_in[i].astype(BF16))
            a = conformer_conv(z, e_conv_w[i], e_conv_b[i], e_cln_g[i], e_cln_b[i])
            o = dilated_attention(z, t5_table, 2 * conv_dim, (z.shape[1] - 2 * conv_dim) // 3)
            xs = out_proj_residual(a, o, e_w_out[i].astype(BF16), xs, g[1], gt_m)
            xs = ffn_sublayer(xs, g[2], sh_f, sc_f, g[3], gt_f, e_ffn_w1[i].astype(BF16),
                              e_ffn_w3[i].astype(BF16), e_ffn_w2[i].astype(BF16))
        else:
            q, k, v, u = odd_in_proj(xs, g[0], sh_m, sc_m, o_w_in[i], o_q_norm_g[i], o_w_uq[i],
                                     o_kv_norm_g[i], o_w_ukv[i])
            o_mla = mla_attention(q, k, v)
            y = s5_bidirectional(u, s5_a_re[i], s5_a_im[i], s5_log_dt[i], s5_b_re[i], s5_b_im[i],
                                 s5_c_re[i], s5_c_im[i])
            y = s5_gate(y, u, s5_d[i], s5_w_glu[i])
            xs = out_proj_residual(o_mla, y, o_w_out[i].astype(BF16), xs, g[1], gt_m)
            xs = moe_sublayer(xs, g[2], sh_f, sc_f, g[3], gt_f, router_w[i], moe_w1[i], moe_w3[i], moe_w2[i])
    return xs.reshape(bsz, seq, d)
```

```python
import functools
import math

import jax
import jax.numpy as jnp
from jax import lax
from jax.experimental import pallas as pl
from jax.experimental.pallas import tpu as pltpu

F32 = jnp.float32
BF16 = jnp.bfloat16

RMS_EPS = 1e-6
LN_EPS = 1e-5
NEG_INF = -1e30

CONV_WIDTH = 31
CONV_HALO = 16
DIL_HEAD_DIM = 64
DIL_CONFIGS = ((128, 1), (512, 4), (2048, 16))
DIL_HALF = 64
DIL_TILE = 1024
DIL_QB = 256
DIL_MERGE = 4
N_BUCKETS = 32
T5_MAX_DIST = DIL_CONFIGS[-1][0] // 2
MLA_HEADS = 12
NOPE_DIM = 128
ROPE_DIM = 64
V_DIM = 128
ROPE_BASE = 10000.0
MLA_QK_PAD = 256
S5_GROUP = 16
S5_STATE = 64
S5_CHUNK = 32
N_EXPERTS = 8
EXPERT_TILE = 896
LANE = 128
SUBLANES = 8
VMEM_LIMIT_MB = 56


def _cp(sem, vmem_mb=VMEM_LIMIT_MB):
    return pltpu.CompilerParams(dimension_semantics=sem, vmem_limit_bytes=vmem_mb * 1024 * 1024)


def _rms(x):
    return x * lax.rsqrt(jnp.mean(x * x, axis=-1, keepdims=True) + RMS_EPS)


def _modnorm(x, g, sh, sc):
    return (_rms(x) * g) * (1.0 + sc) + sh


def _sigmoid(x):
    return 1.0 / (1.0 + jnp.exp(-x))


def _silu(x):
    return x * _sigmoid(x)


def _adaln_kernel(c_ref, w_ref, b_ref, o_ref, cb_ref):
    @pl.when((pl.program_id(0) == 0) & (pl.program_id(1) == 0))
    def _():
        cc = c_ref[...]
        cb_ref[...] = jnp.broadcast_to(_silu(cc), cb_ref.shape)

    cb = cb_ref[...]
    for j in range(o_ref.shape[-1] // LANE):
        sl = slice(j * LANE, (j + 1) * LANE)
        o_ref[:, sl] = jnp.sum(w_ref[:, sl] * cb, axis=0, keepdims=True) + b_ref[:, sl]


def adaln(c, ada_w, ada_b, tn=1024):
    nl, d, n = ada_w.shape
    out = pl.pallas_call(
        _adaln_kernel,
        out_shape=jax.ShapeDtypeStruct((nl, 1, n), F32),
        grid=(nl, n // tn),
        in_specs=[pl.BlockSpec((d, 1), lambda l, j: (0, 0)),
                  pl.BlockSpec((None, d, tn), lambda l, j: (l, 0, j)),
                  pl.BlockSpec((None, 1, tn), lambda l, j: (l, 0, j))],
        out_specs=pl.BlockSpec((None, 1, tn), lambda l, j: (l, 0, j)),
        scratch_shapes=[pltpu.VMEM((d, LANE), F32)],
        compiler_params=_cp(("arbitrary", "arbitrary")),
        name="adaln",
    )(c.reshape(d, 1), ada_w, ada_b.reshape(nl, 1, n))
    return out


def _nmm_kernel(x_ref, g_ref, sh_ref, sc_ref, w_ref, o_ref, h_ref):
    @pl.when(pl.program_id(1) == 0)
    def _():
        h_ref[...] = _modnorm(x_ref[...], g_ref[...], sh_ref[...], sc_ref[...]).astype(h_ref.dtype)

    o_ref[...] = jnp.dot(h_ref[...], w_ref[...], preferred_element_type=F32).astype(o_ref.dtype)


def norm_mod_matmul(x, g, sh, sc, w, tm=1024, tn=512):
    s, d = x.shape
    n = w.shape[1]
    tm, tn = min(tm, s), min(tn, n)
    vec = pl.BlockSpec((1, d), lambda i, j: (0, 0))
    return pl.pallas_call(
        _nmm_kernel,
        out_shape=jax.ShapeDtypeStruct((s, n), BF16),
        grid=(s // tm, n // tn),
        in_specs=[pl.BlockSpec((tm, d), lambda i, j: (i, 0)), vec, vec, vec,
                  pl.BlockSpec((d, tn), lambda i, j: (0, j))],
        out_specs=pl.BlockSpec((tm, tn), lambda i, j: (i, j)),
        scratch_shapes=[pltpu.VMEM((tm, d), BF16)],
        compiler_params=_cp(("parallel", "arbitrary")),
        name="norm_mod_matmul",
    )(x, g, sh, sc, w)


def _conv_kernel(av_ref, ag_ref, avp_ref, agp_ref, avn_ref, agn_ref, w_ref, b_ref, lg_ref, lb_ref,
                 o_ref, buf_ref, sh_ref, *, rows):
    i = pl.program_id(0)
    n = pl.num_programs(0)
    ts = av_ref.shape[0]

    def glu(v_ref, g_ref):
        return v_ref[...].astype(F32) * _sigmoid(g_ref[...].astype(F32))

    buf_ref[0:CONV_HALO, :] = jnp.where(i > 0, glu(avp_ref, agp_ref), 0.0)
    buf_ref[CONV_HALO:CONV_HALO + ts, :] = glu(av_ref, ag_ref)
    buf_ref[CONV_HALO + ts:2 * CONV_HALO + ts, :] = jnp.where(i < n - 1, glu(avn_ref, agn_ref), 0.0)
    buf_ref[2 * CONV_HALO + ts:, :] = jnp.zeros((SUBLANES, buf_ref.shape[1]), F32)

    span = ts + 2 * CONV_HALO
    for o in range(SUBLANES):
        sh_ref[o] = buf_ref[o:o + span, :]

    off = CONV_HALO - CONV_WIDTH // 2
    for r in range(ts // rows):
        acc = jnp.broadcast_to(b_ref[...], (rows, b_ref.shape[-1]))
        for k in range(CONV_WIDTH):
            lo = r * rows + k + off
            base = lo - lo % SUBLANES
            acc = acc + w_ref[k:k + 1, :] * sh_ref[lo % SUBLANES, base:base + rows, :]
        mu = jnp.mean(acc, axis=-1, keepdims=True)
        dlt = acc - mu
        var = jnp.mean(dlt * dlt, axis=-1, keepdims=True)
        y = dlt * lax.rsqrt(var + LN_EPS) * lg_ref[...] + lb_ref[...]
        o_ref[r * rows:(r + 1) * rows, :] = _silu(y).astype(o_ref.dtype)


def conformer_conv(z, conv_w, conv_b, cln_g, cln_b, ts=256, rows=32):
    s = z.shape[0]
    c = conv_w.shape[1]
    ts = min(ts, s)
    hb = ts // CONV_HALO
    nh = s // CONV_HALO
    main = lambda col: pl.BlockSpec((ts, c), lambda i: (i, col))
    prev = lambda col: pl.BlockSpec((CONV_HALO, c), lambda i: (jnp.maximum(i * hb - 1, 0), col))
    nxt = lambda col: pl.BlockSpec((CONV_HALO, c), lambda i: (jnp.minimum((i + 1) * hb, nh - 1), col))
    vec = lambda r: pl.BlockSpec((r, c), lambda i: (0, 0))
    return pl.pallas_call(
        functools.partial(_conv_kernel, rows=min(rows, ts)),
        out_shape=jax.ShapeDtypeStruct((s, c), BF16),
        grid=(s // ts,),
        in_specs=[main(0), main(1), prev(0), prev(1), nxt(0), nxt(1),
                  vec(CONV_WIDTH), vec(1), vec(1), vec(1)],
        out_specs=pl.BlockSpec((ts, c), lambda i: (i, 0)),
        scratch_shapes=[pltpu.VMEM((ts + 2 * CONV_HALO + SUBLANES, c), F32),
                        pltpu.VMEM((SUBLANES, ts + 2 * CONV_HALO, c), F32)],
        compiler_params=_cp(("parallel",)),
        name="conformer_conv",
    )(z, z, z, z, z, z, conv_w, conv_b.reshape(1, c), cln_g.reshape(1, c), cln_b.reshape(1, c))


def _t5_bucket(rel):
    half = N_BUCKETS // 2
    exact = half // 2
    n = jnp.abs(rel)
    large = exact + (jnp.log(jnp.maximum(n, 1).astype(F32) / exact)
                     / math.log(T5_MAX_DIST / exact) * (half - exact)).astype(jnp.int32)
    large = jnp.minimum(large, half - 1)
    return jnp.where(rel > 0, half, 0) + jnp.where(n < exact, n, large)


def _toeplitz(w, rows, cols):
    n = w.shape[-1]
    wp = jnp.concatenate([w, jnp.zeros(w.shape[:-1] + (1,), w.dtype)], axis=-1)
    flat = jnp.tile(wp, (1,) * (w.ndim - 1) + (rows,))[..., :rows * n]
    return flat.reshape(w.shape[:-1] + (rows, n))[..., rows - 1:rows - 1 + cols]


def _band_bias(t5_table, dil, qb):
    kl = qb + 2 * DIL_HALF
    dist = jnp.arange(qb + kl - 1) - (qb - 1) - DIL_HALF
    diag = jnp.where((jnp.abs(dist) <= DIL_HALF)[None], t5_table[_t5_bucket(dist * dil)].T.astype(F32), NEG_INF)
    return _toeplitz(diag, qb, kl)


def _band_group(q, kw, vw, bias_ref, pen):
    r = q.shape[0]
    is_a = lax.broadcasted_iota(jnp.int32, (1, LANE), 1) < DIL_HEAD_DIM
    q2 = jnp.concatenate([jnp.where(is_a, q, 0.0), jnp.where(is_a, 0.0, q)], axis=0).astype(BF16)
    s = lax.dot_general(q2, kw.astype(BF16), (((1,), (1,)), ((), ())), preferred_element_type=F32)
    s = s + bias_ref[...]
    if pen is not None:
        s = s + pen
    m = jnp.max(s, axis=-1, keepdims=True)
    p = jnp.exp(s - m)
    l = jnp.sum(p, axis=-1, keepdims=True)
    pv = jnp.dot(p.astype(BF16), vw.astype(BF16), preferred_element_type=F32)
    pick = lambda x: jnp.where(is_a, x[:r], x[r:])
    return pick(pv), pick(m), pick(l)


def _dil_kernel(q_ref, kp_ref, kc_ref, kn_ref, vp_ref, vc_ref, vn_ref, b1_ref, b4_ref, b16_ref, o_ref,
                qf, kf, vf, a1, m1, l1, a4, m4, l4, a16, m16, l16):
    i = pl.program_id(1)
    first = i == 0
    last = i == pl.num_programs(1) - 1
    t = DIL_TILE
    half = DIL_HALF

    qf[...] = q_ref[...].astype(F32) * (DIL_HEAD_DIM ** -0.5)
    for dst, (p_ref, c_ref, n_ref) in ((kf, (kp_ref, kc_ref, kn_ref)), (vf, (vp_ref, vc_ref, vn_ref))):
        dst[0:t, :] = p_ref[...].astype(F32)
        dst[t:2 * t, :] = c_ref[...].astype(F32)
        dst[2 * t:3 * t, :] = n_ref[...].astype(F32)

    def store(refs, rows, vals):
        for ref, val in zip(refs, vals):
            ref[rows, :] = val

    qb = DIL_QB
    kl = qb + 2 * half
    col = lax.broadcasted_iota(jnp.int32, (1, kl), 1)
    lo = jnp.where((col < half) & first, NEG_INF, 0.0)
    hi = jnp.where((col >= qb + half) & last, NEG_INF, 0.0)

    nb = t // qb
    for b in range(nb):
        pen = lo + hi if nb == 1 else (lo if b == 0 else (hi if b == nb - 1 else None))
        k0 = t + b * qb - half
        rows = slice(b * qb, (b + 1) * qb)
        store((a1, m1, l1), rows, _band_group(qf[rows, :], kf[k0:k0 + kl, :], vf[k0:k0 + kl, :], b1_ref, pen))

    dil = 4
    for r in range(dil):
        qrows = pl.ds(r, qb, stride=dil)
        krows = pl.ds(t - half * dil + r, kl, stride=dil)
        store((a4, m4, l4), qrows, _band_group(qf[qrows, :], kf[krows, :], vf[krows, :], b4_ref, lo + hi))

    dil = 16
    q16 = t // dil
    k16 = q16 + 2 * half
    col = lax.rem(lax.broadcasted_iota(jnp.int32, (1, DIL_MERGE * k16), 1), k16)
    pen16 = (jnp.where((col < half) & first, NEG_INF, 0.0)
             + jnp.where((col >= q16 + half) & last, NEG_INF, 0.0))
    for g in range(dil // DIL_MERGE):
        res = range(g * DIL_MERGE, (g + 1) * DIL_MERGE)
        qrows = [pl.ds(r, q16, stride=dil) for r in res]
        krows = [pl.ds(t - half * dil + r, k16, stride=dil) for r in res]
        out = _band_group(jnp.concatenate([qf[rr, :] for rr in qrows], axis=0),
                          jnp.concatenate([kf[rr, :] for rr in krows], axis=0),
                          jnp.concatenate([vf[rr, :] for rr in krows], axis=0), b16_ref, pen16)
        for u, rr in enumerate(qrows):
            store((a16, m16, l16), rr, [x[u * q16:(u + 1) * q16] for x in out])

    mm = jnp.maximum(jnp.maximum(m1[...], m4[...]), m16[...])
    w1, w4, w16 = jnp.exp(m1[...] - mm), jnp.exp(m4[...] - mm), jnp.exp(m16[...] - mm)
    num = w1 * a1[...] + w4 * a4[...] + w16 * a16[...]
    den = w1 * l1[...] + w4 * l4[...] + w16 * l16[...]
    o_ref[...] = (num / den).astype(o_ref.dtype)


def dilated_attention(z, t5_table, col0, width):
    s, zw = z.shape
    t = DIL_TILE
    assert s % t == 0 and col0 % LANE == 0 and width % LANE == 0 and t // 4 == DIL_QB
    nt = s // t
    nh = t5_table.shape[1]
    cq, ck, cv = ((col0 + k * width) // LANE for k in range(3))
    pair = lambda b: b.reshape(nh // 2, 2 * b.shape[1], b.shape[2])
    b16 = _band_bias(t5_table, 16, t // 16)
    same = jnp.eye(DIL_MERGE, dtype=bool)[None, :, None, :, None]
    b16 = jnp.where(same, b16[:, None, :, None, :], NEG_INF)
    b16 = b16.reshape(nh, DIL_MERGE * (t // 16), -1)
    biases = [pair(_band_bias(t5_table, 1, DIL_QB)), pair(_band_bias(t5_table, 4, DIL_QB)), pair(b16)]

    def blk(col, shift):
        return pl.BlockSpec((t, LANE), lambda hg, i: (jnp.clip(i + shift, 0, nt - 1), col + hg))

    bias_spec = lambda b: pl.BlockSpec((None,) + b.shape[1:], lambda hg, i: (hg, 0, 0))
    return pl.pallas_call(
        _dil_kernel,
        out_shape=jax.ShapeDtypeStruct((s, width), BF16),
        grid=(width // LANE, nt),
        in_specs=[blk(cq, 0), blk(ck, -1), blk(ck, 0), blk(ck, 1), blk(cv, -1), blk(cv, 0), blk(cv, 1)]
                 + [bias_spec(b) for b in biases],
        out_specs=pl.BlockSpec((t, LANE), lambda hg, i: (i, hg)),
        scratch_shapes=[pltpu.VMEM((t, LANE), F32), pltpu.VMEM((3 * t, LANE), F32),
                        pltpu.VMEM((3 * t, LANE), F32)] + [pltpu.VMEM((t, LANE), F32)] * 9,
        compiler_params=_cp(("parallel", "parallel")),
        name="dilated_attention",
    )(*([z] * 7 + biases))


def _oproj_kernel(a1_ref, a2_ref, w_ref, x_ref, g_ref, gt_ref, o_ref):
    k1 = a1_ref.shape[1]
    y = jnp.dot(a1_ref[...], w_ref[0:k1, :], preferred_element_type=F32)
    y = y + jnp.dot(a2_ref[...], w_ref[k1:, :], preferred_element_type=F32)
    o_ref[...] = x_ref[...] + gt_ref[...] * (_rms(y) * g_ref[...])


def out_proj_residual(a1, a2, w, x, g, gt, tm=512):
    s, d = x.shape
    k1, k2 = a1.shape[1], a2.shape[1]
    tm = min(tm, s)
    vec = pl.BlockSpec((1, d), lambda i: (0, 0))
    return pl.pallas_call(
        _oproj_kernel,
        out_shape=jax.ShapeDtypeStruct((s, d), F32),
        grid=(s // tm,),
        in_specs=[pl.BlockSpec((tm, k1), lambda i: (i, 0)), pl.BlockSpec((tm, k2), lambda i: (i, 0)),
                  pl.BlockSpec((k1 + k2, d), lambda i: (0, 0)),
                  pl.BlockSpec((tm, d), lambda i: (i, 0)), vec, vec],
        out_specs=pl.BlockSpec((tm, d), lambda i: (i, 0)),
        compiler_params=_cp(("parallel",)),
        name="out_proj_residual",
    )(a1, a2, w, x, g, gt)


def _ffn_kernel(x_ref, g_ref, sh_ref, sc_ref, g2_ref, gt_ref, w1_ref, w3_ref, w2_ref, o_ref, h_ref):
    f = pl.program_id(1)

    @pl.when(f == 0)
    def _():
        h_ref[...] = _modnorm(x_ref[...], g_ref[...], sh_ref[...], sc_ref[...]).astype(h_ref.dtype)
        o_ref[...] = jnp.zeros_like(o_ref)

    h = h_ref[...]
    u = jnp.dot(h, w1_ref[...], preferred_element_type=F32)
    v = jnp.dot(h, w3_ref[...], preferred_element_type=F32)
    o_ref[...] += jnp.dot((_silu(u) * v).astype(BF16), w2_ref[...], preferred_element_type=F32)

    @pl.when(f == pl.num_programs(1) - 1)
    def _():
        o_ref[...] = x_ref[...] + gt_ref[...] * (_rms(o_ref[...]) * g2_ref[...])


def ffn_sublayer(x, g, sh, sc, g2, gt, w1, w3, w2, tm=1024, tf=256):
    s, d = x.shape
    f = w1.shape[1]
    tm, tf = min(tm, s), min(tf, f)
    vec = pl.BlockSpec((1, d), lambda i, j: (0, 0))
    return pl.pallas_call(
        _ffn_kernel,
        out_shape=jax.ShapeDtypeStruct((s, d), F32),
        grid=(s // tm, f // tf),
        in_specs=[pl.BlockSpec((tm, d), lambda i, j: (i, 0)), vec, vec, vec, vec, vec,
                  pl.BlockSpec((d, tf), lambda i, j: (0, j)), pl.BlockSpec((d, tf), lambda i, j: (0, j)),
                  pl.BlockSpec((tf, d), lambda i, j: (j, 0))],
        out_specs=pl.BlockSpec((tm, d), lambda i, j: (i, 0)),
        scratch_shapes=[pltpu.VMEM((tm, d), BF16)],
        compiler_params=_cp(("parallel", "arbitrary")),
        name="ffn_sublayer",
    )(x, g, sh, sc, g2, gt, w1, w3, w2)


def _rope_slab(t, c_ref, s1_ref, s2_ref):
    return (t * c_ref[...] + pltpu.roll(t, LANE - ROPE_DIM // 2, 1) * s1_ref[...]
            + pltpu.roll(t, ROPE_DIM // 2, 1) * s2_ref[...])


def _oddin_kernel(x_ref, g_ref, sh_ref, sc_ref, win_ref, qg_ref, wuq_ref, kvg_ref, wuk_ref, wuv_ref,
                  c_ref, s1_ref, s2_ref, q_ref, k_ref, v_ref, u_ref, *, q_lora, kv_lora, scale):
    h = _modnorm(x_ref[...], g_ref[...], sh_ref[...], sc_ref[...]).astype(BF16)
    z = jnp.dot(h, win_ref[...], preferred_element_type=F32)
    o_kv, o_pe, o_u = q_lora, q_lora + kv_lora, q_lora + kv_lora + LANE
    u_ref[...] = z[:, o_u:]

    qn = (_rms(z[:, :q_lora]) * qg_ref[...]).astype(BF16)
    q = jnp.dot(qn, wuq_ref[...], preferred_element_type=F32)
    kvn = (_rms(z[:, o_kv:o_pe]) * kvg_ref[...]).astype(BF16)
    kn = jnp.dot(kvn, wuk_ref[...], preferred_element_type=F32)
    v_ref[...] = jnp.dot(kvn, wuv_ref[...], preferred_element_type=F32).astype(v_ref.dtype)
    kpe = _rope_slab(z[:, o_pe:o_u], c_ref, s1_ref, s2_ref).astype(k_ref.dtype)

    for hh in range(MLA_HEADS):
        b = hh * MLA_QK_PAD
        q_ref[:, b:b + NOPE_DIM] = (q[:, b:b + NOPE_DIM] * scale).astype(q_ref.dtype)
        qpe = _rope_slab(q[:, b + NOPE_DIM:b + MLA_QK_PAD], c_ref, s1_ref, s2_ref)
        q_ref[:, b + NOPE_DIM:b + MLA_QK_PAD] = (qpe * scale).astype(q_ref.dtype)
        k_ref[:, b:b + NOPE_DIM] = kn[:, hh * NOPE_DIM:(hh + 1) * NOPE_DIM].astype(k_ref.dtype)
        k_ref[:, b + NOPE_DIM:b + MLA_QK_PAD] = kpe


def _rope_tables(seq):
    pos = jnp.arange(seq, dtype=F32)
    inv_freq = ROPE_BASE ** (-jnp.arange(0, ROPE_DIM, 2, dtype=F32) / ROPE_DIM)
    ang = pos[:, None] * inv_freq[None, :]
    cos, sin = jnp.cos(ang), jnp.sin(ang)
    zero = jnp.zeros_like(cos)
    c = jnp.concatenate([cos, cos, zero, zero], axis=-1)
    s1 = jnp.concatenate([-sin, zero, zero, zero], axis=-1)
    s2 = jnp.concatenate([zero, sin, zero, zero], axis=-1)
    return c, s1, s2


def odd_in_proj(x, g, sh, sc, w_in, q_norm_g, w_uq, kv_norm_g, w_ukv, tm=256):
    s, d = x.shape
    q_lora, kv_lora = q_norm_g.shape[0], kv_norm_g.shape[0]
    s5_dim = w_in.shape[1] - q_lora - kv_lora - ROPE_DIM
    tm = min(tm, s)
    w_in_p = jnp.concatenate(
        [w_in[:, :q_lora + kv_lora + ROPE_DIM], jnp.zeros((d, LANE - ROPE_DIM), w_in.dtype),
         w_in[:, q_lora + kv_lora + ROPE_DIM:]], axis=1).astype(BF16)
    wq = w_uq.reshape(q_lora, MLA_HEADS, NOPE_DIM + ROPE_DIM)
    wq = jnp.pad(wq, ((0, 0), (0, 0), (0, MLA_QK_PAD - NOPE_DIM - ROPE_DIM)))
    wq = wq.reshape(q_lora, MLA_HEADS * MLA_QK_PAD).astype(BF16)
    wkv = w_ukv.reshape(kv_lora, MLA_HEADS, NOPE_DIM + V_DIM)
    wuk = wkv[:, :, :NOPE_DIM].reshape(kv_lora, MLA_HEADS * NOPE_DIM).astype(BF16)
    wuv = wkv[:, :, NOPE_DIM:].reshape(kv_lora, MLA_HEADS * V_DIM).astype(BF16)
    c, s1, s2 = _rope_tables(s)
    scale = (NOPE_DIM + ROPE_DIM) ** -0.5 * math.log2(math.e)

    full = lambda a: pl.BlockSpec(a.shape, lambda i: (0, 0))
    row = lambda w: pl.BlockSpec((tm, w), lambda i: (i, 0))
    vec = pl.BlockSpec((1, d), lambda i: (0, 0))
    qg = q_norm_g.reshape(1, q_lora)
    kvg = kv_norm_g.reshape(1, kv_lora)
    return pl.pallas_call(
        functools.partial(_oddin_kernel, q_lora=q_lora, kv_lora=kv_lora, scale=scale),
        out_shape=[jax.ShapeDtypeStruct((s, MLA_HEADS * MLA_QK_PAD), BF16),
                   jax.ShapeDtypeStruct((s, MLA_HEADS * MLA_QK_PAD), BF16),
                   jax.ShapeDtypeStruct((s, MLA_HEADS * V_DIM), BF16),
                   jax.ShapeDtypeStruct((s, s5_dim), F32)],
        grid=(s // tm,),
        in_specs=[row(d), vec, vec, vec, full(w_in_p), full(qg), full(wq), full(kvg), full(wuk), full(wuv),
                  row(LANE), row(LANE), row(LANE)],
        out_specs=[row(MLA_HEADS * MLA_QK_PAD), row(MLA_HEADS * MLA_QK_PAD), row(MLA_HEADS * V_DIM),
                   row(s5_dim)],
        compiler_params=_cp(("parallel",)),
        name="odd_in_proj",
    )(x, g, sh, sc, w_in_p, qg, wq, kvg, wuk, wuv, c, s1, s2)


def _flash_kernel(q_ref, k_ref, v_ref, o_ref, m_ref, acc_ref, s0, s1, *, rows, nk):
    t = pl.program_id(0)
    j1 = lax.rem(jnp.maximum(t - 1, 0), nk)
    tq = q_ref.shape[0]
    vd = v_ref.shape[1]

    @pl.when(t == 0)
    def _():
        m_ref[...] = jnp.full_like(m_ref, -jnp.inf)
        acc_ref[...] = jnp.zeros_like(acc_ref)
        s1[...] = jnp.full_like(s1, -jnp.inf)

    def stages(s_rd, s_wr):
        v = v_ref[...]
        v1 = jnp.concatenate([v, jnp.ones_like(v)], axis=1)
        k = k_ref[...]
        live = t >= 1
        fresh = j1 == 0
        for c in range(tq // rows):
            rs = slice(c * rows, (c + 1) * rows)
            s = s_rd[rs, :]
            m_prev = jnp.where(fresh, -jnp.inf, m_ref[rs, :])
            m_cand = jnp.maximum(m_prev, jnp.max(s, axis=-1, keepdims=True))
            m_new = jnp.where(live, m_cand, m_prev)
            m_sub = jnp.where(live, m_cand, 0.0)
            p = jnp.exp2((s - m_sub[:, :1]).astype(BF16))
            alpha = jnp.where(live, jnp.exp2(m_prev - m_new), 1.0)
            m_ref[rs, :] = m_new
            pv = jnp.dot(p, v1, preferred_element_type=F32)
            acc_ref[rs, :vd] = alpha * acc_ref[rs, :vd] + pv[:, :vd]
            acc_ref[rs, vd:] = alpha * acc_ref[rs, vd:] + pv[:, vd:]
            s_wr[rs, :] = lax.dot_general(q_ref[rs, :], k, (((1,), (1,)), ((), ())),
                                          preferred_element_type=F32)

    parity = lax.rem(t, 2)

    @pl.when(parity == 0)
    def _():
        stages(s1, s0)

    @pl.when(parity == 1)
    def _():
        stages(s0, s1)

    @pl.when((t >= 1) & (j1 == nk - 1))
    def _():
        o_ref[...] = (acc_ref[:, :vd] / acc_ref[:, vd:]).astype(o_ref.dtype)


def mla_attention(q, k, v, tq=1024, tk=1024, rows=512):
    s = q.shape[0]
    tq, tk = min(tq, s), min(tk, s)
    rows = min(rows, tq)
    ni, nk = s // tq, s // tk
    n = MLA_HEADS * ni * nk

    def item(t, lag):
        w = jnp.clip(t - lag, 0, n - 1)
        return w // (ni * nk), lax.rem(w, ni * nk) // nk, lax.rem(w, nk)

    def q_map(t):
        h, i, _ = item(t, 0)
        return i, h

    def k_map(t):
        h, _, j = item(t, 0)
        return j, h

    def v_map(t):
        h, _, j = item(t, 1)
        return j, h

    def o_map(t):
        h, i, _ = item(t, 1)
        return i, h

    return pl.pallas_call(
        functools.partial(_flash_kernel, rows=rows, nk=nk),
        out_shape=jax.ShapeDtypeStruct((s, MLA_HEADS * V_DIM), BF16),
        grid=(n + 1,),
        in_specs=[pl.BlockSpec((tq, MLA_QK_PAD), q_map),
                  pl.BlockSpec((tk, MLA_QK_PAD), k_map),
                  pl.BlockSpec((tk, V_DIM), v_map)],
        out_specs=pl.BlockSpec((tq, V_DIM), o_map),
        scratch_shapes=[pltpu.VMEM((tq, V_DIM), F32), pltpu.VMEM((tq, 2 * V_DIM), F32),
                        pltpu.VMEM((tq, tk), F32), pltpu.VMEM((tq, tk), F32)],
        compiler_params=_cp(("arbitrary",)),
        name="mla_flash_attention",
    )(q, k, v)


def _s5_matrices(a_re, a_im, log_dt, b_re, b_im, c_re, c_im, chunk):
    L = chunk
    A = lax.complex(a_re.astype(F32), a_im.astype(F32))
    dt = jnp.exp(log_dt.astype(F32))[..., None]
    adt = A * dt
    a_bar = jnp.exp(adt)
    b_bar = ((a_bar - 1.0) / A)[..., None] * lax.complex(b_re.astype(F32), b_im.astype(F32))
    c_c = lax.complex(c_re.astype(F32), c_im.astype(F32))
    kk = jnp.arange(L + 1, dtype=F32)
    apow = jnp.exp(adt[:, :, None, :] * kk[None, None, :, None].astype(jnp.complex64))
    g, p, gc = b_bar.shape[1], b_bar.shape[2], b_bar.shape[3]

    ker = jnp.real(jnp.einsum('dgcp,dgkp,dgpi->dgkci', c_c, apow[:, :, :L], b_bar))
    kb = ker[1][:, ::-1]
    diag = jnp.concatenate([kb[:, :L - 1], ker[0][:, :1] + kb[:, L - 1:], ker[0][:, 1:]], axis=1)
    t_mat = _toeplitz(jnp.transpose(diag, (0, 2, 3, 1)), L, L)
    t_mat = jnp.transpose(t_mat, (0, 3, 2, 4, 1)).reshape(g, L * gc, L * gc)

    pf = apow[0][:, ::-1][:, 1:][:, :, :, None] * b_bar[0][:, None]
    pb = apow[1][:, :L][:, :, :, None] * b_bar[1][:, None]
    def p_lay(m):
        return jnp.transpose(m, (0, 1, 3, 2)).reshape(g, L * gc, p)
    p_mat = jnp.concatenate([p_lay(jnp.real(pf)), p_lay(jnp.imag(pf)),
                             p_lay(jnp.real(pb)), p_lay(jnp.imag(pb))], axis=-1)

    wf = c_c[0][:, None] * apow[0][:, 1:][:, :, None, :]
    wb = c_c[1][:, None] * apow[1][:, ::-1][:, :L][:, :, None, :]
    def q_lay(m):
        return jnp.transpose(m, (0, 3, 1, 2)).reshape(g, p, L * gc)
    q_mat = jnp.concatenate([q_lay(jnp.real(wf)), q_lay(-jnp.imag(wf)),
                             q_lay(jnp.real(wb)), q_lay(-jnp.imag(wb))], axis=1)

    al = apow[:, :, L]
    dec = jnp.stack([jnp.real(al[0]), jnp.imag(al[0]), jnp.real(al[1]), jnp.imag(al[1])]).reshape(4, g * p)
    return t_mat, p_mat, q_mat, dec


def _s5_state_kernel(u_ref, p_ref, fre_ref, fim_ref, bre_ref, bim_ref):
    outs = [[], [], [], []]
    for gi in range(u_ref.shape[0]):
        xe = jnp.dot(u_ref[gi].astype(BF16), p_ref[gi], preferred_element_type=F32)
        for part in range(4):
            outs[part].append(xe[:, part * S5_STATE:(part + 1) * S5_STATE])
    for part, ref in enumerate((fre_ref, fim_ref, bre_ref, bim_ref)):
        ref[...] = jnp.concatenate(outs[part], axis=-1)


def _s5_scan_kernel(fre_ref, fim_ref, bre_ref, bim_ref, dec_ref, ofre_ref, ofim_ref, obre_ref, obim_ref):
    nc = fre_ref.shape[0]
    w = fre_ref.shape[1]
    far, fai, bar, bai = (dec_ref[i:i + 1, :] for i in range(4))

    def fwd(c, st):
        re, im = st
        ofre_ref[pl.ds(c, 1), :] = re
        ofim_ref[pl.ds(c, 1), :] = im
        return (far * re - fai * im + fre_ref[pl.ds(c, 1), :],
                far * im + fai * re + fim_ref[pl.ds(c, 1), :])

    def bwd(i, st):
        c = nc - 1 - i
        re, im = st
        obre_ref[pl.ds(c, 1), :] = re
        obim_ref[pl.ds(c, 1), :] = im
        return (bar * re - bai * im + bre_ref[pl.ds(c, 1), :],
                bar * im + bai * re + bim_ref[pl.ds(c, 1), :])

    zero = (jnp.zeros((1, w), F32), jnp.zeros((1, w), F32))
    lax.fori_loop(0, nc, fwd, zero)
    lax.fori_loop(0, nc, bwd, zero)


def _s5_out_kernel(u_ref, t_ref, q_ref, fre_ref, fim_ref, bre_ref, bim_ref, y_ref):
    for gi in range(u_ref.shape[0]):
        lanes = slice(gi * S5_STATE, (gi + 1) * S5_STATE)
        y = jnp.dot(u_ref[gi].astype(BF16), t_ref[gi], preferred_element_type=F32)
        xin = jnp.concatenate([r[:, lanes] for r in (fre_ref, fim_ref, bre_ref, bim_ref)], axis=-1)
        y_ref[gi] = y + jnp.dot(xin.astype(BF16), q_ref[gi], preferred_element_type=F32)


def s5_bidirectional(u, a_re, a_im, log_dt, b_re, b_im, c_re, c_im, chunk=S5_CHUNK):
    s, dim = u.shape
    gc, p = S5_GROUP, S5_STATE
    g = dim // gc
    chunk = min(chunk, s)
    nc = s // chunk
    gp = LANE // p
    t_mat, p_mat, q_mat, dec = _s5_matrices(a_re, a_im, log_dt, b_re, b_im, c_re, c_im, chunk)
    t_mat, p_mat, q_mat = t_mat.astype(BF16), p_mat.astype(BF16), q_mat.astype(BF16)
    u_t = jnp.transpose(u.reshape(nc, chunk, g, gc), (2, 0, 1, 3)).reshape(g, nc, chunk * gc)

    grp = lambda a, b: pl.BlockSpec((gp, a, b), lambda i: (i, 0, 0))
    st = pl.BlockSpec((nc, LANE), lambda i: (0, i))
    st_shape = jax.ShapeDtypeStruct((nc, g * p), F32)
    xe = pl.pallas_call(
        _s5_state_kernel,
        out_shape=[st_shape] * 4,
        grid=(g // gp,),
        in_specs=[grp(nc, chunk * gc), grp(chunk * gc, 4 * p)],
        out_specs=[st] * 4,
        compiler_params=_cp(("parallel",)),
        name="s5_chunk_states",
    )(u_t, p_mat)

    whole = lambda shape: pl.BlockSpec(shape, lambda: (0,) * len(shape))
    xin = pl.pallas_call(
        _s5_scan_kernel,
        out_shape=[st_shape] * 4,
        in_specs=[whole((nc, g * p))] * 4 + [whole((4, g * p))],
        out_specs=[whole((nc, g * p))] * 4,
        compiler_params=pltpu.CompilerParams(vmem_limit_bytes=VMEM_LIMIT_MB * 1024 * 1024),
        name="s5_boundary_scan",
    )(*xe, dec)

    y_t = pl.pallas_call(
        _s5_out_kernel,
        out_shape=jax.ShapeDtypeStruct((g, nc, chunk * gc), F32),
        grid=(g // gp,),
        in_specs=[grp(nc, chunk * gc), grp(chunk * gc, chunk * gc), grp(4 * p, chunk * gc)] + [st] * 4,
        out_specs=grp(nc, chunk * gc),
        compiler_params=_cp(("parallel",)),
        name="s5_outputs",
    )(u_t, t_mat, q_mat, *xin)
    return jnp.transpose(y_t.reshape(g, nc, chunk, gc), (1, 2, 0, 3)).reshape(s, dim)


def _s5_gate_kernel(y_ref, u_ref, d_ref, w_ref, o_ref):
    y = y_ref[...] + d_ref[...] * u_ref[...]
    y = 0.5 * y * (1.0 + jnp.tanh(math.sqrt(2.0 / math.pi) * (y + 0.044715 * (y * y * y))))
    z = jnp.dot(y.astype(BF16), w_ref[...], preferred_element_type=F32)
    o_ref[...] = (y * _sigmoid(z)).astype(o_ref.dtype)


def s5_gate(y, u, d_skip, w_glu, tm=1024):
    s, dim = y.shape
    tm = min(tm, s)
    row = pl.BlockSpec((tm, dim), lambda i: (i, 0))
    return pl.pallas_call(
        _s5_gate_kernel,
        out_shape=jax.ShapeDtypeStruct((s, dim), BF16),
        grid=(s // tm,),
        in_specs=[row, row, pl.BlockSpec((1, dim), lambda i: (0, 0)), pl.BlockSpec((dim, dim), lambda i: (0, 0))],
        out_specs=row,
        compiler_params=_cp(("parallel",)),
        name="s5_gate",
    )(y, u, d_skip.reshape(1, dim), w_glu.astype(BF16))


def _router_kernel(x_ref, g_ref, sh_ref, sc_ref, rw_ref, h_ref, info_ref):
    h = _modnorm(x_ref[...], g_ref[...], sh_ref[...], sc_ref[...])
    half = h.shape[1] // 2
    bits = lambda v: lax.bitcast_convert_type(v.astype(BF16).astype(F32), jnp.uint32)
    h_ref[...] = (bits(h[:, :half]) >> 16) | (bits(h[:, half:]) & jnp.uint32(0xFFFF0000))
    logits = jnp.dot(h, rw_ref[...], preferred_element_type=F32, precision=lax.Precision.HIGHEST)
    lane = lax.broadcasted_iota(jnp.int32, logits.shape, 1)
    logits = jnp.where(lane < N_EXPERTS, logits, -jnp.inf)
    m1 = jnp.max(logits, axis=-1, keepdims=True)
    i1 = jnp.min(jnp.where(logits == m1, lane, LANE), axis=-1, keepdims=True)
    rest = jnp.where(lane == i1, -jnp.inf, logits)
    m2 = jnp.max(rest, axis=-1, keepdims=True)
    i2 = jnp.min(jnp.where(rest == m2, lane, LANE), axis=-1, keepdims=True)
    e = jnp.exp(m2 - m1)
    g1 = 1.0 / (1.0 + e)
    g2 = e / (1.0 + e)
    info = jnp.where(lane == 0, i1.astype(F32),
                     jnp.where(lane == 1, i2.astype(F32),
                               jnp.where(lane == 2, g1, jnp.where(lane == 3, g2, 0.0))))
    info_ref[...] = info


def moe_router(x, g, sh, sc, router_w, tm=512):
    s, d = x.shape
    tm = min(tm, s)
    rw = jnp.pad(router_w, ((0, 0), (0, LANE - router_w.shape[1])))
    vec = pl.BlockSpec((1, d), lambda i: (0, 0))
    return pl.pallas_call(
        _router_kernel,
        out_shape=[jax.ShapeDtypeStruct((s, d // 2), jnp.uint32), jax.ShapeDtypeStruct((s, LANE), F32)],
        grid=(s // tm,),
        in_specs=[pl.BlockSpec((tm, d), lambda i: (i, 0)), vec, vec, vec,
                  pl.BlockSpec((d, LANE), lambda i: (0, 0))],
        out_specs=[pl.BlockSpec((tm, d // 2), lambda i: (i, 0)), pl.BlockSpec((tm, LANE), lambda i: (i, 0))],
        compiler_params=_cp(("parallel",)),
        name="moe_router",
    )(x, g, sh, sc, rw)


def _expert_kernel(te_ref, tv_ref, src_ref, hp_hbm, w1_ref, w3_ref, w2_ref, o_ref, acc_ref, gbuf, h_ref, sem):
    t = pl.program_id(0)
    f = pl.program_id(1)
    nt = pl.num_programs(0)
    nf = pl.num_programs(1)
    tm = acc_ref.shape[0]
    per_step = tm // nf
    valid = tv_ref[t] > 0
    slot = lax.rem(t, 2)

    def row_copy(tile, buf, r):
        return pltpu.make_async_copy(hp_hbm.at[pl.ds(src_ref[tile * tm + r], 1)],
                                     gbuf.at[buf, pl.ds(r, 1)], sem.at[buf])

    def wait_half(buf):
        pltpu.make_async_copy(hp_hbm.at[pl.ds(0, tm)], gbuf.at[buf], sem.at[buf]).wait()

    @pl.when((t == 0) & (f == 0))
    def _():
        def issue(r, carry):
            row_copy(0, 0, r).start()
            return carry
        lax.fori_loop(0, tm, issue, 0)

    @pl.when((f == 0) & ((t == 0) | (tv_ref[jnp.maximum(t - 1, 0)] > 0)))
    def _():
        wait_half(slot)

    @pl.when((f == 0) & valid)
    def _():
        w = gbuf[slot]
        half = w.shape[1]
        h_ref[:, :half] = lax.bitcast_convert_type(w << 16, F32).astype(BF16)
        h_ref[:, half:] = lax.bitcast_convert_type(w & jnp.uint32(0xFFFF0000), F32).astype(BF16)

    @pl.when(f == 0)
    def _():
        acc_ref[...] = jnp.zeros_like(acc_ref)

    def compute(rows):
        for u in range(per_step):
            row_copy(t + 1, 1 - slot, f * per_step + u).start()
        h = h_ref[:rows, :]
        u = jnp.dot(h, w1_ref[...].astype(BF16), preferred_element_type=F32)
        v = jnp.dot(h, w3_ref[...].astype(BF16), preferred_element_type=F32)
        acc_ref[:rows, :] += jnp.dot((_silu(u) * v).astype(BF16), w2_ref[...].astype(BF16),
                                     preferred_element_type=F32)

    @pl.when(tv_ref[t] > tm // 2)
    def _():
        compute(tm)

    @pl.when(valid & (tv_ref[t] <= tm // 2))
    def _():
        compute(tm // 2)

    @pl.when(f == nf - 1)
    def _():
        o_ref[...] = acc_ref[...].astype(o_ref.dtype)

    @pl.when((t == nt - 1) & (f == nf - 1) & valid)
    def _():
        wait_half(1 - slot)


def expert_ffn(h_packed, src, tile_expert, tile_valid, w1, w3, w2, tm, tf=256):
    dh = h_packed.shape[1]
    d = 2 * dh
    n = src.shape[0] - tm
    f = w1.shape[2]
    tf = min(tf, f)
    nf = f // tf
    assert tm % nf == 0 and n % tm == 0

    def fblk(j, t, tv):
        return jnp.where(tv[t] > 0, j, nf - 1)

    return pl.pallas_call(
        _expert_kernel,
        out_shape=jax.ShapeDtypeStruct((n, d), BF16),
        grid_spec=pltpu.PrefetchScalarGridSpec(
            num_scalar_prefetch=3,
            grid=(n // tm, nf),
            in_specs=[pl.BlockSpec(memory_space=pl.ANY),
                      pl.BlockSpec((None, d, tf), lambda t, j, te, tv, sr: (te[t], 0, fblk(j, t, tv))),
                      pl.BlockSpec((None, d, tf), lambda t, j, te, tv, sr: (te[t], 0, fblk(j, t, tv))),
                      pl.BlockSpec((None, tf, d), lambda t, j, te, tv, sr: (te[t], fblk(j, t, tv), 0))],
            out_specs=pl.BlockSpec((tm, d), lambda t, j, te, tv, sr: (t, 0)),
            scratch_shapes=[pltpu.VMEM((tm, d), F32), pltpu.VMEM((2, tm, dh), jnp.uint32),
                            pltpu.VMEM((tm, d), BF16), pltpu.SemaphoreType.DMA((2,))]),
        compiler_params=_cp(("arbitrary", "arbitrary")),
        name="expert_ffn",
    )(tile_expert, tile_valid, src, h_packed, w1, w3, w2)


def _dispatch(idx, tm):
    s = idx.shape[0]
    e_flat = idx.reshape(-1)
    onehot = (e_flat[:, None] == jnp.arange(N_EXPERTS)[None, :]).astype(jnp.int32)
    csum = jnp.cumsum(onehot, axis=0)
    rank = jnp.sum((csum - onehot) * onehot, axis=1)
    counts = csum[-1]
    padded = ((counts + tm - 1) // tm) * tm
    gend = jnp.cumsum(padded)
    slot = (gend - padded)[e_flat] + rank
    nt = -(-(2 * s) // tm) + N_EXPERTS
    src = jnp.zeros(((nt + 1) * tm,), jnp.int32).at[slot].set(jnp.arange(2 * s, dtype=jnp.int32) // 2)
    tstart = jnp.arange(nt, dtype=jnp.int32) * tm
    valid = tstart < gend[-1]
    te = jnp.minimum(jnp.sum((tstart[:, None] >= gend[None, :]).astype(jnp.int32), axis=1), N_EXPERTS - 1)
    nvalid = gend[-1] // tm
    fill = jnp.clip((gend - padded + counts)[te] - tstart, 0, tm)
    fill = jnp.where(valid, fill, 0)
    te = jnp.where(valid, te, te[jnp.maximum(nvalid - 1, 0)])
    return slot.reshape(s, 2), src, te.astype(jnp.int32), fill.astype(jnp.int32)


def _combine_kernel(ya_ref, yb_ref, info_ref, x_ref, g_ref, gt_ref, o_ref):
    info = info_ref[...]
    y = info[:, 2:3] * ya_ref[...].astype(F32) + info[:, 3:4] * yb_ref[...].astype(F32)
    o_ref[...] = x_ref[...] + gt_ref[...] * (_rms(y) * g_ref[...])


def moe_combine(ya, yb, info, x, g, gt, tm=512):
    s, d = x.shape
    tm = min(tm, s)
    row = lambda w: pl.BlockSpec((tm, w), lambda i: (i, 0))
    vec = pl.BlockSpec((1, d), lambda i: (0, 0))
    return pl.pallas_call(
        _combine_kernel,
        out_shape=jax.ShapeDtypeStruct((s, d), F32),
        grid=(s // tm,),
        in_specs=[row(d), row(d), row(LANE), row(d), vec, vec],
        out_specs=row(d),
        compiler_params=_cp(("parallel",)),
        name="moe_combine",
    )(ya, yb, info, x, g, gt)


def moe_sublayer(x, g, sh, sc, g2, gt, router_w, w1, w3, w2, tm_e=EXPERT_TILE):
    s = x.shape[0]
    tm_e = min(tm_e, s)
    h_packed, info = moe_router(x, g, sh, sc, router_w)
    idx = info[:, :2].astype(jnp.int32)
    slot, src, te, tv = _dispatch(idx, tm_e)
    y_sorted = expert_ffn(h_packed, src, te, tv, w1, w3, w2, tm_e)
    ya = jnp.take(y_sorted, slot[:, 0], axis=0)
    yb = jnp.take(y_sorted, slot[:, 1], axis=0)
    return moe_combine(ya, yb, info, x, g2, gt)


def kernel(x, c, t5_table, norm_g, ada_w, ada_b, e_w_in, e_conv_w, e_conv_b, e_cln_g, e_cln_b, e_w_out, e_ffn_w1, e_ffn_w3, e_ffn_w2, o_w_in, o_q_norm_g, o_w_uq, o_kv_norm_g, o_w_ukv, s5_a_re, s5_a_im, s5_log_dt, s5_b_re, s5_b_im, s5_c_re, s5_c_im, s5_d, s5_w_glu, o_w_out, router_w, moe_w1, moe_w3, moe_w2):
    bsz, seq, d = x.shape
    assert bsz == 1
    depth = norm_g.shape[0]
    conv_dim = e_conv_w.shape[2]
    xs = x.reshape(seq, d)
    mod = adaln(c, ada_w, ada_b)
    for layer in range(depth):
        i = layer // 2
        sh_m, sc_m, gt_m, sh_f, sc_f, gt_f = (mod[layer, :, k * d:(k + 1) * d] for k in range(6))
        g = [norm_g[layer, k].reshape(1, d) for k in range(4)]
        if layer % 2 == 0:
            z = norm_mod_matmul(xs, g[0], sh_m, sc_m, e_w_in[i].astype(BF16))
            a = conformer_conv(z, e_conv_w[i], e_conv_b[i], e_cln_g[i], e_cln_b[i])
            o = dilated_attention(z, t5_table, 2 * conv_dim, (z.shape[1] - 2 * conv_dim) // 3)
            xs = out_proj_residual(a, o, e_w_out[i].astype(BF16), xs, g[1], gt_m)
            xs = ffn_sublayer(xs, g[2], sh_f, sc_f, g[3], gt_f, e_ffn_w1[i].astype(BF16),
                              e_ffn_w3[i].astype(BF16), e_ffn_w2[i].astype(BF16))
        else:
            q, k, v, u = odd_in_proj(xs, g[0], sh_m, sc_m, o_w_in[i], o_q_norm_g[i], o_w_uq[i],
                                     o_kv_norm_g[i], o_w_ukv[i])
            o_mla = mla_attention(q, k, v)
            y = s5_bidirectional(u, s5_a_re[i], s5_a_im[i], s5_log_dt[i], s5_b_re[i], s5_b_im[i],
                                 s5_c_re[i], s5_c_im[i])
            y = s5_gate(y, u, s5_d[i], s5_w_glu[i])
            xs = out_proj_residual(o_mla, y, o_w_out[i].astype(BF16), xs, g[1], gt_m)
            xs = moe_sublayer(xs, g[2], sh_f, sc_f, g[3], gt_f, router_w[i], moe_w1[i], moe_w3[i], moe_w2[i])
    return xs.reshape(bsz, seq, d)
```

```python
import functools
import math

import jax
import jax.numpy as jnp
from jax import lax
from jax.experimental import pallas as pl
from jax.experimental.pallas import tpu as pltpu

F32 = jnp.float32
BF16 = jnp.bfloat16

RMS_EPS = 1e-6
LN_EPS = 1e-5
NEG_INF = -1e30

CONV_WIDTH = 31
CONV_HALO = 16
DIL_HEAD_DIM = 64
DIL_CONFIGS = ((128, 1), (512, 4), (2048, 16))
DIL_HALF = 64
DIL_TILE = 1024
DIL_QB = 256
DIL_MERGE = 4
N_BUCKETS = 32
T5_MAX_DIST = DIL_CONFIGS[-1][0] // 2
MLA_HEADS = 12
NOPE_DIM = 128
ROPE_DIM = 64
V_DIM = 128
ROPE_BASE = 10000.0
MLA_QK_PAD = 256
S5_GROUP = 16
S5_STATE = 64
S5_CHUNK = 32
N_EXPERTS = 8
EXPERT_TILE = 896
LANE = 128
SUBLANES = 8
VMEM_LIMIT_MB = 56


def _cp(sem, vmem_mb=VMEM_LIMIT_MB):
    return pltpu.CompilerParams(dimension_semantics=sem, vmem_limit_bytes=vmem_mb * 1024 * 1024)


def _rms(x):
    return x * lax.rsqrt(jnp.mean(x * x, axis=-1, keepdims=True) + RMS_EPS)


def _modnorm(x, g, sh, sc):
    return (_rms(x) * g) * (1.0 + sc) + sh


def _sigmoid(x):
    return 1.0 / (1.0 + jnp.exp(-x))


def _silu(x):
    return x * _sigmoid(x)


def _adaln_kernel(c_ref, w_ref, b_ref, o_ref, cb_ref):
    @pl.when((pl.program_id(0) == 0) & (pl.program_id(1) == 0))
    def _():
        cc = c_ref[...]
        cb_ref[...] = jnp.broadcast_to(_silu(cc), cb_ref.shape)

    cb = cb_ref[...]
    for j in range(o_ref.shape[-1] // LANE):
        sl = slice(j * LANE, (j + 1) * LANE)
        o_ref[:, sl] = jnp.sum(w_ref[:, sl] * cb, axis=0, keepdims=True) + b_ref[:, sl]


def adaln(c, ada_w, ada_b, tn=1024):
    nl, d, n = ada_w.shape
    out = pl.pallas_call(
        _adaln_kernel,
        out_shape=jax.ShapeDtypeStruct((nl, 1, n), F32),
        grid=(nl, n // tn),
        in_specs=[pl.BlockSpec((d, 1), lambda l, j: (0, 0)),
                  pl.BlockSpec((None, d, tn), lambda l, j: (l, 0, j)),
                  pl.BlockSpec((None, 1, tn), lambda l, j: (l, 0, j))],
        out_specs=pl.BlockSpec((None, 1, tn), lambda l, j: (l, 0, j)),
        scratch_shapes=[pltpu.VMEM((d, LANE), F32)],
        compiler_params=_cp(("arbitrary", "arbitrary")),
        name="adaln",
    )(c.reshape(d, 1), ada_w, ada_b.reshape(nl, 1, n))
    return out


def _nmm_kernel(x_ref, g_ref, sh_ref, sc_ref, w_ref, o_ref, h_ref):
    @pl.when(pl.program_id(1) == 0)
    def _():
        h_ref[...] = _modnorm(x_ref[...], g_ref[...], sh_ref[...], sc_ref[...]).astype(h_ref.dtype)

    o_ref[...] = jnp.dot(h_ref[...], w_ref[...], preferred_element_type=F32).astype(o_ref.dtype)


def norm_mod_matmul(x, g, sh, sc, w, tm=1024, tn=512):
    s, d = x.shape
    n = w.shape[1]
    tm, tn = min(tm, s), min(tn, n)
    vec = pl.BlockSpec((1, d), lambda i, j: (0, 0))
    return pl.pallas_call(
        _nmm_kernel,
        out_shape=jax.ShapeDtypeStruct((s, n), BF16),
        grid=(s // tm, n // tn),
        in_specs=[pl.BlockSpec((tm, d), lambda i, j: (i, 0)), vec, vec, vec,
                  pl.BlockSpec((d, tn), lambda i, j: (0, j))],
        out_specs=pl.BlockSpec((tm, tn), lambda i, j: (i, j)),
        scratch_shapes=[pltpu.VMEM((tm, d), BF16)],
        compiler_params=_cp(("parallel", "arbitrary")),
        name="norm_mod_matmul",
    )(x, g, sh, sc, w)


def _conv_kernel(av_ref, ag_ref, avp_ref, agp_ref, avn_ref, agn_ref, w_ref, b_ref, lg_ref, lb_ref,
                 o_ref, buf_ref, sh_ref, *, rows):
    i = pl.program_id(0)
    n = pl.num_programs(0)
    ts = av_ref.shape[0]

    def glu(v_ref, g_ref):
        return v_ref[...].astype(F32) * _sigmoid(g_ref[...].astype(F32))

    buf_ref[0:CONV_HALO, :] = jnp.where(i > 0, glu(avp_ref, agp_ref), 0.0)
    buf_ref[CONV_HALO:CONV_HALO + ts, :] = glu(av_ref, ag_ref)
    buf_ref[CONV_HALO + ts:2 * CONV_HALO + ts, :] = jnp.where(i < n - 1, glu(avn_ref, agn_ref), 0.0)
    buf_ref[2 * CONV_HALO + ts:, :] = jnp.zeros((SUBLANES, buf_ref.shape[1]), F32)

    span = ts + 2 * CONV_HALO
    for o in range(SUBLANES):
        sh_ref[o] = buf_ref[o:o + span, :]

    off = CONV_HALO - CONV_WIDTH // 2
    for r in range(ts // rows):
        acc = jnp.broadcast_to(b_ref[...], (rows, b_ref.shape[-1]))
        for k in range(CONV_WIDTH):
            lo = r * rows + k + off
            base = lo - lo % SUBLANES
            acc = acc + w_ref[k:k + 1, :] * sh_ref[lo % SUBLANES, base:base + rows, :]
        mu = jnp.mean(acc, axis=-1, keepdims=True)
        dlt = acc - mu
        var = jnp.mean(dlt * dlt, axis=-1, keepdims=True)
        y = dlt * lax.rsqrt(var + LN_EPS) * lg_ref[...] + lb_ref[...]
        o_ref[r * rows:(r + 1) * rows, :] = _silu(y).astype(o_ref.dtype)


def conformer_conv(z, conv_w, conv_b, cln_g, cln_b, ts=256, rows=32):
    s = z.shape[0]
    c = conv_w.shape[1]
    ts = min(ts, s)
    hb = ts // CONV_HALO
    nh = s // CONV_HALO
    main = lambda col: pl.BlockSpec((ts, c), lambda i: (i, col))
    prev = lambda col: pl.BlockSpec((CONV_HALO, c), lambda i: (jnp.maximum(i * hb - 1, 0), col))
    nxt = lambda col: pl.BlockSpec((CONV_HALO, c), lambda i: (jnp.minimum((i + 1) * hb, nh - 1), col))
    vec = lambda r: pl.BlockSpec((r, c), lambda i: (0, 0))
    return pl.pallas_call(
        functools.partial(_conv_kernel, rows=min(rows, ts)),
        out_shape=jax.ShapeDtypeStruct((s, c), BF16),
        grid=(s // ts,),
        in_specs=[main(0), main(1), prev(0), prev(1), nxt(0), nxt(1),
                  vec(CONV_WIDTH), vec(1), vec(1), vec(1)],
        out_specs=pl.BlockSpec((ts, c), lambda i: (i, 0)),
        scratch_shapes=[pltpu.VMEM((ts + 2 * CONV_HALO + SUBLANES, c), F32),
                        pltpu.VMEM((SUBLANES, ts + 2 * CONV_HALO, c), F32)],
        compiler_params=_cp(("parallel",)),
        name="conformer_conv",
    )(z, z, z, z, z, z, conv_w, conv_b.reshape(1, c), cln_g.reshape(1, c), cln_b.reshape(1, c))


def _t5_bucket(rel):
    half = N_BUCKETS // 2
    exact = half // 2
    n = jnp.abs(rel)
    large = exact + (jnp.log(jnp.maximum(n, 1).astype(F32) / exact)
                     / math.log(T5_MAX_DIST / exact) * (half - exact)).astype(jnp.int32)
    large = jnp.minimum(large, half - 1)
    return jnp.where(rel > 0, half, 0) + jnp.where(n < exact, n, large)


def _toeplitz(w, rows, cols):
    n = w.shape[-1]
    wp = jnp.concatenate([w, jnp.zeros(w.shape[:-1] + (1,), w.dtype)], axis=-1)
    flat = jnp.tile(wp, (1,) * (w.ndim - 1) + (rows,))[..., :rows * n]
    return flat.reshape(w.shape[:-1] + (rows, n))[..., rows - 1:rows - 1 + cols]


def _band_diag(t5_table, dil, qb):
    kl = qb + 2 * DIL_HALF
    n = -(-(qb + kl - 1) // LANE) * LANE
    dist = jnp.arange(n) - (qb - 1) - DIL_HALF
    return jnp.where((jnp.abs(dist) <= DIL_HALF)[None], t5_table[_t5_bucket(dist * dil)].T.astype(F32), NEG_INF)


def _band_bias_rows(diag_row, qb):
    n = diag_row.shape[1]
    rolled = pltpu.roll(jnp.broadcast_to(diag_row, (qb, n)), n - (qb - 1), 1, stride=1, stride_axis=0)
    return rolled[:, :qb + 2 * DIL_HALF]


def _band_group(q, kw, vw, bias_ref, pen):
    r = q.shape[0]
    is_a = lax.broadcasted_iota(jnp.int32, (1, LANE), 1) < DIL_HEAD_DIM
    q2 = jnp.concatenate([jnp.where(is_a, q, 0.0), jnp.where(is_a, 0.0, q)], axis=0).astype(BF16)
    s = lax.dot_general(q2, kw.astype(BF16), (((1,), (1,)), ((), ())), preferred_element_type=F32)
    s = s + bias_ref[...]
    if pen is not None:
        s = s + pen
    m = jnp.max(s, axis=-1, keepdims=True)
    p = jnp.exp(s - m)
    l = jnp.sum(p, axis=-1, keepdims=True)
    pv = jnp.dot(p.astype(BF16), vw.astype(BF16), preferred_element_type=F32)
    pick = lambda x: jnp.where(is_a, x[:r], x[r:])
    return pick(pv), pick(m), pick(l)


def _dil_kernel(q_ref, kp_ref, kc_ref, kn_ref, vp_ref, vc_ref, vn_ref, d1_ref, d4_ref, d16_ref, o_ref,
                qf, kf, vf, a1, m1, l1, a4, m4, l4, a16, m16, l16, b1_ref, b4_ref, b16_ref):
    i = pl.program_id(1)
    first = i == 0
    last = i == pl.num_programs(1) - 1
    t = DIL_TILE
    half = DIL_HALF

    @pl.when(first)
    def _():
        q16 = t // 16
        b16_ref[...] = jnp.full(b16_ref.shape, NEG_INF, F32)
        for h in range(2):
            b1_ref[h * DIL_QB:(h + 1) * DIL_QB, :] = _band_bias_rows(d1_ref[h:h + 1, :], DIL_QB)
            b4_ref[h * DIL_QB:(h + 1) * DIL_QB, :] = _band_bias_rows(d4_ref[h:h + 1, :], DIL_QB)
            blk = _band_bias_rows(d16_ref[h:h + 1, :], q16)
            for u in range(DIL_MERGE):
                r0 = (h * DIL_MERGE + u) * q16
                b16_ref[r0:r0 + q16, u * blk.shape[1]:(u + 1) * blk.shape[1]] = blk

    qf[...] = q_ref[...].astype(F32) * (DIL_HEAD_DIM ** -0.5)
    for dst, (p_ref, c_ref, n_ref) in ((kf, (kp_ref, kc_ref, kn_ref)), (vf, (vp_ref, vc_ref, vn_ref))):
        dst[0:t, :] = p_ref[...].astype(F32)
        dst[t:2 * t, :] = c_ref[...].astype(F32)
        dst[2 * t:3 * t, :] = n_ref[...].astype(F32)

    def store(refs, rows, vals):
        for ref, val in zip(refs, vals):
            ref[rows, :] = val

    qb = DIL_QB
    kl = qb + 2 * half
    col = lax.broadcasted_iota(jnp.int32, (1, kl), 1)
    lo = jnp.where((col < half) & first, NEG_INF, 0.0)
    hi = jnp.where((col >= qb + half) & last, NEG_INF, 0.0)

    nb = t // qb
    for b in range(nb):
        pen = lo + hi if nb == 1 else (lo if b == 0 else (hi if b == nb - 1 else None))
        k0 = t + b * qb - half
        rows = slice(b * qb, (b + 1) * qb)
        store((a1, m1, l1), rows, _band_group(qf[rows, :], kf[k0:k0 + kl, :], vf[k0:k0 + kl, :], b1_ref, pen))

    dil = 4
    for r in range(dil):
        qrows = pl.ds(r, qb, stride=dil)
        krows = pl.ds(t - half * dil + r, kl, stride=dil)
        store((a4, m4, l4), qrows, _band_group(qf[qrows, :], kf[krows, :], vf[krows, :], b4_ref, lo + hi))

    dil = 16
    q16 = t // dil
    k16 = q16 + 2 * half
    col = lax.rem(lax.broadcasted_iota(jnp.int32, (1, DIL_MERGE * k16), 1), k16)
    pen16 = (jnp.where((col < half) & first, NEG_INF, 0.0)
             + jnp.where((col >= q16 + half) & last, NEG_INF, 0.0))
    for g in range(dil // DIL_MERGE):
        res = range(g * DIL_MERGE, (g + 1) * DIL_MERGE)
        qrows = [pl.ds(r, q16, stride=dil) for r in res]
        krows = [pl.ds(t - half * dil + r, k16, stride=dil) for r in res]
        out = _band_group(jnp.concatenate([qf[rr, :] for rr in qrows], axis=0),
                          jnp.concatenate([kf[rr, :] for rr in krows], axis=0),
                          jnp.concatenate([vf[rr, :] for rr in krows], axis=0), b16_ref, pen16)
        for u, rr in enumerate(qrows):
            store((a16, m16, l16), rr, [x[u * q16:(u + 1) * q16] for x in out])

    mm = jnp.maximum(jnp.maximum(m1[...], m4[...]), m16[...])
    w1, w4, w16 = jnp.exp(m1[...] - mm), jnp.exp(m4[...] - mm), jnp.exp(m16[...] - mm)
    num = w1 * a1[...] + w4 * a4[...] + w16 * a16[...]
    den = w1 * l1[...] + w4 * l4[...] + w16 * l16[...]
    o_ref[...] = (num / den).astype(o_ref.dtype)


def dilated_attention(z, t5_table, col0, width):
    s, zw = z.shape
    t = DIL_TILE
    assert s % t == 0 and col0 % LANE == 0 and width % LANE == 0 and t // 4 == DIL_QB
    nt = s // t
    nh = t5_table.shape[1]
    cq, ck, cv = ((col0 + k * width) // LANE for k in range(3))
    q16 = t // 16
    diags = [_band_diag(t5_table, dil, qb).reshape(nh // 2, 2, -1)
             for dil, qb in ((1, DIL_QB), (4, DIL_QB), (16, q16))]
    kl, k16 = DIL_QB + 2 * DIL_HALF, q16 + 2 * DIL_HALF

    def blk(col, shift):
        return pl.BlockSpec((t, LANE), lambda hg, i: (jnp.clip(i + shift, 0, nt - 1), col + hg))

    diag_spec = lambda d: pl.BlockSpec((None,) + d.shape[1:], lambda hg, i: (hg, 0, 0))
    return pl.pallas_call(
        _dil_kernel,
        out_shape=jax.ShapeDtypeStruct((s, width), BF16),
        grid=(width // LANE, nt),
        in_specs=[blk(cq, 0), blk(ck, -1), blk(ck, 0), blk(ck, 1), blk(cv, -1), blk(cv, 0), blk(cv, 1)]
                 + [diag_spec(d) for d in diags],
        out_specs=pl.BlockSpec((t, LANE), lambda hg, i: (i, hg)),
        scratch_shapes=[pltpu.VMEM((t, LANE), F32), pltpu.VMEM((3 * t, LANE), F32),
                        pltpu.VMEM((3 * t, LANE), F32)] + [pltpu.VMEM((t, LANE), F32)] * 9
                       + [pltpu.VMEM((2 * DIL_QB, kl), F32), pltpu.VMEM((2 * DIL_QB, kl), F32),
                          pltpu.VMEM((2 * DIL_MERGE * q16, DIL_MERGE * k16), F32)],
        compiler_params=_cp(("parallel", "arbitrary")),
        name="dilated_attention",
    )(*([z] * 7 + diags))


def _oproj_kernel(a1_ref, a2_ref, w_ref, x_ref, g_ref, gt_ref, o_ref):
    k1 = a1_ref.shape[1]
    y = jnp.dot(a1_ref[...], w_ref[0:k1, :], preferred_element_type=F32)
    y = y + jnp.dot(a2_ref[...], w_ref[k1:, :], preferred_element_type=F32)
    o_ref[...] = x_ref[...] + gt_ref[...] * (_rms(y) * g_ref[...])


def out_proj_residual(a1, a2, w, x, g, gt, tm=512):
    s, d = x.shape
    k1, k2 = a1.shape[1], a2.shape[1]
    tm = min(tm, s)
    vec = pl.BlockSpec((1, d), lambda i: (0, 0))
    return pl.pallas_call(
        _oproj_kernel,
        out_shape=jax.ShapeDtypeStruct((s, d), F32),
        grid=(s // tm,),
        in_specs=[pl.BlockSpec((tm, k1), lambda i: (i, 0)), pl.BlockSpec((tm, k2), lambda i: (i, 0)),
                  pl.BlockSpec((k1 + k2, d), lambda i: (0, 0)),
                  pl.BlockSpec((tm, d), lambda i: (i, 0)), vec, vec],
        out_specs=pl.BlockSpec((tm, d), lambda i: (i, 0)),
        compiler_params=_cp(("parallel",)),
        name="out_proj_residual",
    )(a1, a2, w, x, g, gt)


def _ffn_kernel(x_ref, g_ref, sh_ref, sc_ref, g2_ref, gt_ref, w1_ref, w3_ref, w2_ref, o_ref, h_ref):
    f = pl.program_id(1)

    @pl.when(f == 0)
    def _():
        h_ref[...] = _modnorm(x_ref[...], g_ref[...], sh_ref[...], sc_ref[...]).astype(h_ref.dtype)
        o_ref[...] = jnp.zeros_like(o_ref)

    h = h_ref[...]
    u = jnp.dot(h, w1_ref[...], preferred_element_type=F32)
    v = jnp.dot(h, w3_ref[...], preferred_element_type=F32)
    o_ref[...] += jnp.dot((_silu(u) * v).astype(BF16), w2_ref[...], preferred_element_type=F32)

    @pl.when(f == pl.num_programs(1) - 1)
    def _():
        o_ref[...] = x_ref[...] + gt_ref[...] * (_rms(o_ref[...]) * g2_ref[...])


def ffn_sublayer(x, g, sh, sc, g2, gt, w1, w3, w2, tm=1024, tf=256):
    s, d = x.shape
    f = w1.shape[1]
    tm, tf = min(tm, s), min(tf, f)
    vec = pl.BlockSpec((1, d), lambda i, j: (0, 0))
    return pl.pallas_call(
        _ffn_kernel,
        out_shape=jax.ShapeDtypeStruct((s, d), F32),
        grid=(s // tm, f // tf),
        in_specs=[pl.BlockSpec((tm, d), lambda i, j: (i, 0)), vec, vec, vec, vec, vec,
                  pl.BlockSpec((d, tf), lambda i, j: (0, j)), pl.BlockSpec((d, tf), lambda i, j: (0, j)),
                  pl.BlockSpec((tf, d), lambda i, j: (j, 0))],
        out_specs=pl.BlockSpec((tm, d), lambda i, j: (i, 0)),
        scratch_shapes=[pltpu.VMEM((tm, d), BF16)],
        compiler_params=_cp(("parallel", "arbitrary")),
        name="ffn_sublayer",
    )(x, g, sh, sc, g2, gt, w1, w3, w2)


def _rope_slab(t, c_ref, s1_ref, s2_ref):
    return (t * c_ref[...] + pltpu.roll(t, LANE - ROPE_DIM // 2, 1) * s1_ref[...]
            + pltpu.roll(t, ROPE_DIM // 2, 1) * s2_ref[...])


def _oddin_kernel(x_ref, g_ref, sh_ref, sc_ref, win_ref, qg_ref, wuq_ref, kvg_ref, wuk_ref, wuv_ref,
                  c_ref, s1_ref, s2_ref, q_ref, k_ref, v_ref, u_ref, *, q_lora, kv_lora, scale):
    h = _modnorm(x_ref[...], g_ref[...], sh_ref[...], sc_ref[...]).astype(BF16)
    z = jnp.dot(h, win_ref[...], preferred_element_type=F32)
    o_kv, o_pe, o_u = q_lora, q_lora + kv_lora, q_lora + kv_lora + LANE
    u_ref[...] = z[:, o_u:]

    qn = (_rms(z[:, :q_lora]) * qg_ref[...]).astype(BF16)
    q = jnp.dot(qn, wuq_ref[...], preferred_element_type=F32)
    kvn = (_rms(z[:, o_kv:o_pe]) * kvg_ref[...]).astype(BF16)
    kn = jnp.dot(kvn, wuk_ref[...], preferred_element_type=F32)
    v_ref[...] = jnp.dot(kvn, wuv_ref[...], preferred_element_type=F32).astype(v_ref.dtype)
    kpe = _rope_slab(z[:, o_pe:o_u], c_ref, s1_ref, s2_ref).astype(k_ref.dtype)

    for hh in range(MLA_HEADS):
        b = hh * MLA_QK_PAD
        q_ref[:, b:b + NOPE_DIM] = (q[:, b:b + NOPE_DIM] * scale).astype(q_ref.dtype)
        qpe = _rope_slab(q[:, b + NOPE_DIM:b + MLA_QK_PAD], c_ref, s1_ref, s2_ref)
        q_ref[:, b + NOPE_DIM:b + MLA_QK_PAD] = (qpe * scale).astype(q_ref.dtype)
        k_ref[:, b:b + NOPE_DIM] = kn[:, hh * NOPE_DIM:(hh + 1) * NOPE_DIM].astype(k_ref.dtype)
        k_ref[:, b + NOPE_DIM:b + MLA_QK_PAD] = kpe


def _rope_tables(seq):
    pos = jnp.arange(seq, dtype=F32)
    inv_freq = ROPE_BASE ** (-jnp.arange(0, ROPE_DIM, 2, dtype=F32) / ROPE_DIM)
    ang = pos[:, None] * inv_freq[None, :]
    cos, sin = jnp.cos(ang), jnp.sin(ang)
    zero = jnp.zeros_like(cos)
    c = jnp.concatenate([cos, cos, zero, zero], axis=-1)
    s1 = jnp.concatenate([-sin, zero, zero, zero], axis=-1)
    s2 = jnp.concatenate([zero, sin, zero, zero], axis=-1)
    return c, s1, s2


def odd_in_proj(x, g, sh, sc, w_in, q_norm_g, w_uq, kv_norm_g, w_ukv, tm=256):
    s, d = x.shape
    q_lora, kv_lora = q_norm_g.shape[0], kv_norm_g.shape[0]
    s5_dim = w_in.shape[1] - q_lora - kv_lora - ROPE_DIM
    tm = min(tm, s)
    w_in_p = jnp.concatenate(
        [w_in[:, :q_lora + kv_lora + ROPE_DIM], jnp.zeros((d, LANE - ROPE_DIM), w_in.dtype),
         w_in[:, q_lora + kv_lora + ROPE_DIM:]], axis=1).astype(BF16)
    wq = w_uq.reshape(q_lora, MLA_HEADS, NOPE_DIM + ROPE_DIM)
    wq = jnp.pad(wq, ((0, 0), (0, 0), (0, MLA_QK_PAD - NOPE_DIM - ROPE_DIM)))
    wq = wq.reshape(q_lora, MLA_HEADS * MLA_QK_PAD).astype(BF16)
    wkv = w_ukv.reshape(kv_lora, MLA_HEADS, NOPE_DIM + V_DIM)
    wuk = wkv[:, :, :NOPE_DIM].reshape(kv_lora, MLA_HEADS * NOPE_DIM).astype(BF16)
    wuv = wkv[:, :, NOPE_DIM:].reshape(kv_lora, MLA_HEADS * V_DIM).astype(BF16)
    c, s1, s2 = _rope_tables(s)
    scale = (NOPE_DIM + ROPE_DIM) ** -0.5 * math.log2(math.e)

    full = lambda a: pl.BlockSpec(a.shape, lambda i: (0, 0))
    row = lambda w: pl.BlockSpec((tm, w), lambda i: (i, 0))
    vec = pl.BlockSpec((1, d), lambda i: (0, 0))
    qg = q_norm_g.reshape(1, q_lora)
    kvg = kv_norm_g.reshape(1, kv_lora)
    return pl.pallas_call(
        functools.partial(_oddin_kernel, q_lora=q_lora, kv_lora=kv_lora, scale=scale),
        out_shape=[jax.ShapeDtypeStruct((s, MLA_HEADS * MLA_QK_PAD), BF16),
                   jax.ShapeDtypeStruct((s, MLA_HEADS * MLA_QK_PAD), BF16),
                   jax.ShapeDtypeStruct((s, MLA_HEADS * V_DIM), BF16),
                   jax.ShapeDtypeStruct((s, s5_dim), F32)],
        grid=(s // tm,),
        in_specs=[row(d), vec, vec, vec, full(w_in_p), full(qg), full(wq), full(kvg), full(wuk), full(wuv),
                  row(LANE), row(LANE), row(LANE)],
        out_specs=[row(MLA_HEADS * MLA_QK_PAD), row(MLA_HEADS * MLA_QK_PAD), row(MLA_HEADS * V_DIM),
                   row(s5_dim)],
        compiler_params=_cp(("parallel",)),
        name="odd_in_proj",
    )(x, g, sh, sc, w_in_p, qg, wq, kvg, wuk, wuv, c, s1, s2)


def _flash_kernel(q_ref, k_ref, v_ref, o_ref, m_ref, acc_ref, s0, s1, *, rows, nk):
    t = pl.program_id(0)
    j1 = lax.rem(jnp.maximum(t - 1, 0), nk)
    tq = q_ref.shape[0]
    vd = v_ref.shape[1]

    @pl.when(t == 0)
    def _():
        m_ref[...] = jnp.full_like(m_ref, -jnp.inf)
        acc_ref[...] = jnp.zeros_like(acc_ref)
        s1[...] = jnp.full_like(s1, -jnp.inf)

    def stages(s_rd, s_wr):
        v = v_ref[...]
        v1 = jnp.concatenate([v, jnp.ones_like(v)], axis=1)
        k = k_ref[...]
        live = t >= 1
        fresh = j1 == 0
        for c in range(tq // rows):
            rs = slice(c * rows, (c + 1) * rows)
            s = s_rd[rs, :]
            m_prev = jnp.where(fresh, -jnp.inf, m_ref[rs, :])
            m_cand = jnp.maximum(m_prev, jnp.max(s, axis=-1, keepdims=True))
            m_new = jnp.where(live, m_cand, m_prev)
            m_sub = jnp.where(live, m_cand, 0.0)
            p = jnp.exp2((s - m_sub[:, :1]).astype(BF16))
            alpha = jnp.where(live, jnp.exp2(m_prev - m_new), 1.0)
            m_ref[rs, :] = m_new
            pv = jnp.dot(p, v1, preferred_element_type=F32)
            acc_ref[rs, :vd] = alpha * acc_ref[rs, :vd] + pv[:, :vd]
            acc_ref[rs, vd:] = alpha * acc_ref[rs, vd:] + pv[:, vd:]
            s_wr[rs, :] = lax.dot_general(q_ref[rs, :], k, (((1,), (1,)), ((), ())),
                                          preferred_element_type=F32)

    parity = lax.rem(t, 2)

    @pl.when(parity == 0)
    def _():
        stages(s1, s0)

    @pl.when(parity == 1)
    def _():
        stages(s0, s1)

    @pl.when((t >= 1) & (j1 == nk - 1))
    def _():
        o_ref[...] = (acc_ref[:, :vd] / acc_ref[:, vd:]).astype(o_ref.dtype)


def mla_attention(q, k, v, tq=1024, tk=1024, rows=512):
    s = q.shape[0]
    tq, tk = min(tq, s), min(tk, s)
    rows = min(rows, tq)
    ni, nk = s // tq, s // tk
    n = MLA_HEADS * ni * nk

    def item(t, lag):
        w = jnp.clip(t - lag, 0, n - 1)
        return w // (ni * nk), lax.rem(w, ni * nk) // nk, lax.rem(w, nk)

    def q_map(t):
        h, i, _ = item(t, 0)
        return i, h

    def k_map(t):
        h, _, j = item(t, 0)
        return j, h

    def v_map(t):
        h, _, j = item(t, 1)
        return j, h

    def o_map(t):
        h, i, _ = item(t, 1)
        return i, h

    return pl.pallas_call(
        functools.partial(_flash_kernel, rows=rows, nk=nk),
        out_shape=jax.ShapeDtypeStruct((s, MLA_HEADS * V_DIM), BF16),
        grid=(n + 1,),
        in_specs=[pl.BlockSpec((tq, MLA_QK_PAD), q_map),
                  pl.BlockSpec((tk, MLA_QK_PAD), k_map),
                  pl.BlockSpec((tk, V_DIM), v_map)],
        out_specs=pl.BlockSpec((tq, V_DIM), o_map),
        scratch_shapes=[pltpu.VMEM((tq, V_DIM), F32), pltpu.VMEM((tq, 2 * V_DIM), F32),
                        pltpu.VMEM((tq, tk), F32), pltpu.VMEM((tq, tk), F32)],
        compiler_params=_cp(("arbitrary",)),
        name="mla_flash_attention",
    )(q, k, v)


def _s5_matrices(a_re, a_im, log_dt, b_re, b_im, c_re, c_im, chunk):
    L = chunk
    A = lax.complex(a_re.astype(F32), a_im.astype(F32))
    dt = jnp.exp(log_dt.astype(F32))[..., None]
    adt = A * dt
    a_bar = jnp.exp(adt)
    b_bar = ((a_bar - 1.0) / A)[..., None] * lax.complex(b_re.astype(F32), b_im.astype(F32))
    c_c = lax.complex(c_re.astype(F32), c_im.astype(F32))
    kk = jnp.arange(L + 1, dtype=F32)
    apow = jnp.exp(adt[:, :, None, :] * kk[None, None, :, None].astype(jnp.complex64))
    g, p, gc = b_bar.shape[1], b_bar.shape[2], b_bar.shape[3]

    ker = jnp.real(jnp.einsum('dgcp,dgkp,dgpi->dgkci', c_c, apow[:, :, :L], b_bar))
    kb = ker[1][:, ::-1]
    diag = jnp.concatenate([kb[:, :L - 1], ker[0][:, :1] + kb[:, L - 1:], ker[0][:, 1:]], axis=1)
    t_mat = _toeplitz(jnp.transpose(diag, (0, 2, 3, 1)), L, L)
    t_mat = jnp.transpose(t_mat, (0, 3, 2, 4, 1)).reshape(g, L * gc, L * gc)

    pf = apow[0][:, ::-1][:, 1:][:, :, :, None] * b_bar[0][:, None]
    pb = apow[1][:, :L][:, :, :, None] * b_bar[1][:, None]
    def p_lay(m):
        return jnp.transpose(m, (0, 1, 3, 2)).reshape(g, L * gc, p)
    p_mat = jnp.concatenate([p_lay(jnp.real(pf)), p_lay(jnp.imag(pf)),
                             p_lay(jnp.real(pb)), p_lay(jnp.imag(pb))], axis=-1)

    wf = c_c[0][:, None] * apow[0][:, 1:][:, :, None, :]
    wb = c_c[1][:, None] * apow[1][:, ::-1][:, :L][:, :, None, :]
    def q_lay(m):
        return jnp.transpose(m, (0, 3, 1, 2)).reshape(g, p, L * gc)
    q_mat = jnp.concatenate([q_lay(jnp.real(wf)), q_lay(-jnp.imag(wf)),
                             q_lay(jnp.real(wb)), q_lay(-jnp.imag(wb))], axis=1)

    al = apow[:, :, L]
    dec = jnp.stack([jnp.real(al[0]), jnp.imag(al[0]), jnp.real(al[1]), jnp.imag(al[1])]).reshape(4, g * p)
    return t_mat, p_mat, q_mat, dec


def _s5_state_kernel(u_ref, p_ref, fre_ref, fim_ref, bre_ref, bim_ref):
    outs = [[], [], [], []]
    for gi in range(u_ref.shape[0]):
        xe = jnp.dot(u_ref[gi].astype(BF16), p_ref[gi], preferred_element_type=F32)
        for part in range(4):
            outs[part].append(xe[:, part * S5_STATE:(part + 1) * S5_STATE])
    for part, ref in enumerate((fre_ref, fim_ref, bre_ref, bim_ref)):
        ref[...] = jnp.concatenate(outs[part], axis=-1)


def _s5_scan_kernel(fre_ref, fim_ref, bre_ref, bim_ref, dec_ref, ofre_ref, ofim_ref, obre_ref, obim_ref):
    nc = fre_ref.shape[0]
    w = fre_ref.shape[1]
    far, fai, bar, bai = (dec_ref[i:i + 1, :] for i in range(4))

    def fwd(c, st):
        re, im = st
        ofre_ref[pl.ds(c, 1), :] = re
        ofim_ref[pl.ds(c, 1), :] = im
        return (far * re - fai * im + fre_ref[pl.ds(c, 1), :],
                far * im + fai * re + fim_ref[pl.ds(c, 1), :])

    def bwd(i, st):
        c = nc - 1 - i
        re, im = st
        obre_ref[pl.ds(c, 1), :] = re
        obim_ref[pl.ds(c, 1), :] = im
        return (bar * re - bai * im + bre_ref[pl.ds(c, 1), :],
                bar * im + bai * re + bim_ref[pl.ds(c, 1), :])

    zero = (jnp.zeros((1, w), F32), jnp.zeros((1, w), F32))
    lax.fori_loop(0, nc, fwd, zero)
    lax.fori_loop(0, nc, bwd, zero)


def _s5_out_kernel(u_ref, t_ref, q_ref, fre_ref, fim_ref, bre_ref, bim_ref, y_ref):
    for gi in range(u_ref.shape[0]):
        lanes = slice(gi * S5_STATE, (gi + 1) * S5_STATE)
        y = jnp.dot(u_ref[gi].astype(BF16), t_ref[gi], preferred_element_type=F32)
        xin = jnp.concatenate([r[:, lanes] for r in (fre_ref, fim_ref, bre_ref, bim_ref)], axis=-1)
        y_ref[gi] = y + jnp.dot(xin.astype(BF16), q_ref[gi], preferred_element_type=F32)


def s5_bidirectional(u, a_re, a_im, log_dt, b_re, b_im, c_re, c_im, chunk=S5_CHUNK):
    s, dim = u.shape
    gc, p = S5_GROUP, S5_STATE
    g = dim // gc
    chunk = min(chunk, s)
    nc = s // chunk
    gp = LANE // p
    t_mat, p_mat, q_mat, dec = _s5_matrices(a_re, a_im, log_dt, b_re, b_im, c_re, c_im, chunk)
    t_mat, p_mat, q_mat = t_mat.astype(BF16), p_mat.astype(BF16), q_mat.astype(BF16)
    u_t = jnp.transpose(u.reshape(nc, chunk, g, gc), (2, 0, 1, 3)).reshape(g, nc, chunk * gc)

    grp = lambda a, b: pl.BlockSpec((gp, a, b), lambda i: (i, 0, 0))
    st = pl.BlockSpec((nc, LANE), lambda i: (0, i))
    st_shape = jax.ShapeDtypeStruct((nc, g * p), F32)
    xe = pl.pallas_call(
        _s5_state_kernel,
        out_shape=[st_shape] * 4,
        grid=(g // gp,),
        in_specs=[grp(nc, chunk * gc), grp(chunk * gc, 4 * p)],
        out_specs=[st] * 4,
        compiler_params=_cp(("parallel",)),
        name="s5_chunk_states",
    )(u_t, p_mat)

    whole = lambda shape: pl.BlockSpec(shape, lambda: (0,) * len(shape))
    xin = pl.pallas_call(
        _s5_scan_kernel,
        out_shape=[st_shape] * 4,
        in_specs=[whole((nc, g * p))] * 4 + [whole((4, g * p))],
        out_specs=[whole((nc, g * p))] * 4,
        compiler_params=pltpu.CompilerParams(vmem_limit_bytes=VMEM_LIMIT_MB * 1024 * 1024),
        name="s5_boundary_scan",
    )(*xe, dec)

    y_t = pl.pallas_call(
        _s5_out_kernel,
        out_shape=jax.ShapeDtypeStruct((g, nc, chunk * gc), F32),
        grid=(g // gp,),
        in_specs=[grp(nc, chunk * gc), grp(chunk * gc, chunk * gc), grp(4 * p, chunk * gc)] + [st] * 4,
        out_specs=grp(nc, chunk * gc),
        compiler_params=_cp(("parallel",)),
        name="s5_outputs",
    )(u_t, t_mat, q_mat, *xin)
    return jnp.transpose(y_t.reshape(g, nc, chunk, gc), (1, 2, 0, 3)).reshape(s, dim)


def _s5_gate_kernel(y_ref, u_ref, d_ref, w_ref, o_ref):
    y = y_ref[...] + d_ref[...] * u_ref[...]
    y = 0.5 * y * (1.0 + jnp.tanh(math.sqrt(2.0 / math.pi) * (y + 0.044715 * (y * y * y))))
    z = jnp.dot(y.astype(BF16), w_ref[...], preferred_element_type=F32)
    o_ref[...] = (y * _sigmoid(z)).astype(o_ref.dtype)


def s5_gate(y, u, d_skip, w_glu, tm=1024):
    s, dim = y.shape
    tm = min(tm, s)
    row = pl.BlockSpec((tm, dim), lambda i: (i, 0))
    return pl.pallas_call(
        _s5_gate_kernel,
        out_shape=jax.ShapeDtypeStruct((s, dim), BF16),
        grid=(s // tm,),
        in_specs=[row, row, pl.BlockSpec((1, dim), lambda i: (0, 0)), pl.BlockSpec((dim, dim), lambda i: (0, 0))],
        out_specs=row,
        compiler_params=_cp(("parallel",)),
        name="s5_gate",
    )(y, u, d_skip.reshape(1, dim), w_glu.astype(BF16))


def _router_kernel(x_ref, g_ref, sh_ref, sc_ref, rw_ref, h_ref, info_ref):
    h = _modnorm(x_ref[...], g_ref[...], sh_ref[...], sc_ref[...])
    half = h.shape[1] // 2
    bits = lambda v: lax.bitcast_convert_type(v.astype(BF16).astype(F32), jnp.uint32)
    h_ref[...] = (bits(h[:, :half]) >> 16) | (bits(h[:, half:]) & jnp.uint32(0xFFFF0000))
    logits = jnp.dot(h, rw_ref[...], preferred_element_type=F32, precision=lax.Precision.HIGHEST)
    lane = lax.broadcasted_iota(jnp.int32, logits.shape, 1)
    logits = jnp.where(lane < N_EXPERTS, logits, -jnp.inf)
    m1 = jnp.max(logits, axis=-1, keepdims=True)
    i1 = jnp.min(jnp.where(logits == m1, lane, LANE), axis=-1, keepdims=True)
    rest = jnp.where(lane == i1, -jnp.inf, logits)
    m2 = jnp.max(rest, axis=-1, keepdims=True)
    i2 = jnp.min(jnp.where(rest == m2, lane, LANE), axis=-1, keepdims=True)
    e = jnp.exp(m2 - m1)
    g1 = 1.0 / (1.0 + e)
    g2 = e / (1.0 + e)
    info = jnp.where(lane == 0, i1.astype(F32),
                     jnp.where(lane == 1, i2.astype(F32),
                               jnp.where(lane == 2, g1, jnp.where(lane == 3, g2, 0.0))))
    info_ref[...] = info


def moe_router(x, g, sh, sc, router_w, tm=512):
    s, d = x.shape
    tm = min(tm, s)
    rw = jnp.pad(router_w, ((0, 0), (0, LANE - router_w.shape[1])))
    vec = pl.BlockSpec((1, d), lambda i: (0, 0))
    return pl.pallas_call(
        _router_kernel,
        out_shape=[jax.ShapeDtypeStruct((s, d // 2), jnp.uint32), jax.ShapeDtypeStruct((s, LANE), F32)],
        grid=(s // tm,),
        in_specs=[pl.BlockSpec((tm, d), lambda i: (i, 0)), vec, vec, vec,
                  pl.BlockSpec((d, LANE), lambda i: (0, 0))],
        out_specs=[pl.BlockSpec((tm, d // 2), lambda i: (i, 0)), pl.BlockSpec((tm, LANE), lambda i: (i, 0))],
        compiler_params=_cp(("parallel",)),
        name="moe_router",
    )(x, g, sh, sc, rw)


def _expert_kernel(te_ref, tv_ref, src_ref, hp_hbm, w1_ref, w3_ref, w2_ref, o_ref, acc_ref, gbuf, h_ref, sem):
    t = pl.program_id(0)
    f = pl.program_id(1)
    nt = pl.num_programs(0)
    nf = pl.num_programs(1)
    tm = acc_ref.shape[0]
    per_step = tm // nf
    valid = tv_ref[t] > 0
    slot = lax.rem(t, 2)

    def row_copy(tile, buf, r):
        return pltpu.make_async_copy(hp_hbm.at[pl.ds(src_ref[tile * tm + r], 1)],
                                     gbuf.at[buf, pl.ds(r, 1)], sem.at[buf])

    def wait_half(buf):
        pltpu.make_async_copy(hp_hbm.at[pl.ds(0, tm)], gbuf.at[buf], sem.at[buf]).wait()

    @pl.when((t == 0) & (f == 0))
    def _():
        def issue(r, carry):
            row_copy(0, 0, r).start()
            return carry
        lax.fori_loop(0, tm, issue, 0)

    @pl.when((f == 0) & ((t == 0) | (tv_ref[jnp.maximum(t - 1, 0)] > 0)))
    def _():
        wait_half(slot)

    @pl.when((f == 0) & valid)
    def _():
        w = gbuf[slot]
        half = w.shape[1]
        h_ref[:, :half] = lax.bitcast_convert_type(w << 16, F32).astype(BF16)
        h_ref[:, half:] = lax.bitcast_convert_type(w & jnp.uint32(0xFFFF0000), F32).astype(BF16)

    @pl.when(f == 0)
    def _():
        acc_ref[...] = jnp.zeros_like(acc_ref)

    def compute(rows):
        for u in range(per_step):
            row_copy(t + 1, 1 - slot, f * per_step + u).start()
        h = h_ref[:rows, :]
        u = jnp.dot(h, w1_ref[...].astype(BF16), preferred_element_type=F32)
        v = jnp.dot(h, w3_ref[...].astype(BF16), preferred_element_type=F32)
        acc_ref[:rows, :] += jnp.dot((_silu(u) * v).astype(BF16), w2_ref[...].astype(BF16),
                                     preferred_element_type=F32)

    @pl.when(tv_ref[t] > tm // 2)
    def _():
        compute(tm)

    @pl.when(valid & (tv_ref[t] <= tm // 2))
    def _():
        compute(tm // 2)

    @pl.when(f == nf - 1)
    def _():
        o_ref[...] = acc_ref[...].astype(o_ref.dtype)

    @pl.when((t == nt - 1) & (f == nf - 1) & valid)
    def _():
        wait_half(1 - slot)


def expert_ffn(h_packed, src, tile_expert, tile_valid, w1, w3, w2, tm, tf=512):
    dh = h_packed.shape[1]
    d = 2 * dh
    n = src.shape[0] - tm
    f = w1.shape[2]
    tf = min(tf, f)
    nf = f // tf
    assert tm % nf == 0 and n % tm == 0

    def fblk(j, t, tv):
        return jnp.where(tv[t] > 0, j, nf - 1)

    return pl.pallas_call(
        _expert_kernel,
        out_shape=jax.ShapeDtypeStruct((n, d), BF16),
        grid_spec=pltpu.PrefetchScalarGridSpec(
            num_scalar_prefetch=3,
            grid=(n // tm, nf),
            in_specs=[pl.BlockSpec(memory_space=pl.ANY),
                      pl.BlockSpec((None, d, tf), lambda t, j, te, tv, sr: (te[t], 0, fblk(j, t, tv))),
                      pl.BlockSpec((None, d, tf), lambda t, j, te, tv, sr: (te[t], 0, fblk(j, t, tv))),
                      pl.BlockSpec((None, tf, d), lambda t, j, te, tv, sr: (te[t], fblk(j, t, tv), 0))],
            out_specs=pl.BlockSpec((tm, d), lambda t, j, te, tv, sr: (t, 0)),
            scratch_shapes=[pltpu.VMEM((tm, d), F32), pltpu.VMEM((2, tm, dh), jnp.uint32),
                            pltpu.VMEM((tm, d), BF16), pltpu.SemaphoreType.DMA((2,))]),
        compiler_params=_cp(("arbitrary", "arbitrary")),
        name="expert_ffn",
    )(tile_expert, tile_valid, src, h_packed, w1, w3, w2)


def _dispatch(idx, tm):
    s = idx.shape[0]
    e_flat = idx.reshape(-1)
    onehot = (e_flat[:, None] == jnp.arange(N_EXPERTS)[None, :]).astype(jnp.int32)
    csum = jnp.cumsum(onehot, axis=0)
    rank = jnp.sum((csum - onehot) * onehot, axis=1)
    counts = csum[-1]
    padded = ((counts + tm - 1) // tm) * tm
    gend = jnp.cumsum(padded)
    slot = (gend - padded)[e_flat] + rank
    nt = -(-(2 * s) // tm) + N_EXPERTS
    src = jnp.zeros(((nt + 1) * tm,), jnp.int32).at[slot].set(jnp.arange(2 * s, dtype=jnp.int32) // 2)
    tstart = jnp.arange(nt, dtype=jnp.int32) * tm
    valid = tstart < gend[-1]
    te = jnp.minimum(jnp.sum((tstart[:, None] >= gend[None, :]).astype(jnp.int32), axis=1), N_EXPERTS - 1)
    nvalid = gend[-1] // tm
    fill = jnp.clip((gend - padded + counts)[te] - tstart, 0, tm)
    fill = jnp.where(valid, fill, 0)
    te = jnp.where(valid, te, te[jnp.maximum(nvalid - 1, 0)])
    return slot.reshape(s, 2), src, te.astype(jnp.int32), fill.astype(jnp.int32)


def _combine_kernel(ya_ref, yb_ref, info_ref, x_ref, g_ref, gt_ref, o_ref):
    info = info_ref[...]
    y = info[:, 2:3] * ya_ref[...].astype(F32) + info[:, 3:4] * yb_ref[...].astype(F32)
    o_ref[...] = x_ref[...] + gt_ref[...] * (_rms(y) * g_ref[...])


def moe_combine(ya, yb, info, x, g, gt, tm=512):
    s, d = x.shape
    tm = min(tm, s)
    row = lambda w: pl.BlockSpec((tm, w), lambda i: (i, 0))
    vec = pl.BlockSpec((1, d), lambda i: (0, 0))
    return pl.pallas_call(
        _combine_kernel,
        out_shape=jax.ShapeDtypeStruct((s, d), F32),
        grid=(s // tm,),
        in_specs=[row(d), row(d), row(LANE), row(d), vec, vec],
        out_specs=row(d),
        compiler_params=_cp(("parallel",)),
        name="moe_combine",
    )(ya, yb, info, x, g, gt)


def moe_sublayer(x, g, sh, sc, g2, gt, router_w, w1, w3, w2, tm_e=EXPERT_TILE):
    s = x.shape[0]
    tm_e = min(tm_e, s)
    h_packed, info = moe_router(x, g, sh, sc, router_w)
    idx = info[:, :2].astype(jnp.int32)
    slot, src, te, tv = _dispatch(idx, tm_e)
    y_sorted = expert_ffn(h_packed, src, te, tv, w1, w3, w2, tm_e)
    ya = jnp.take(y_sorted, slot[:, 0], axis=0)
    yb = jnp.take(y_sorted, slot[:, 1], axis=0)
    return moe_combine(ya, yb, info, x, g2, gt)


def kernel(x, c, t5_table, norm_g, ada_w, ada_b, e_w_in, e_conv_w, e_conv_b, e_cln_g, e_cln_b, e_w_out, e_ffn_w1, e_ffn_w3, e_ffn_w2, o_w_in, o_q_norm_g, o_w_uq, o_kv_norm_g, o_w_ukv, s5_a_re, s5_a_im, s5_log_dt, s5_b_re, s5_b_im, s5_c_re, s5_c_im, s5_d, s5_w_glu, o_w_out, router_w, moe_w1, moe_w3, moe_w2):
    bsz, seq, d = x.shape
    assert bsz == 1
    depth = norm_g.shape[0]
    conv_dim = e_conv_w.shape[2]
    xs = x.reshape(seq, d)
    mod = adaln(c, ada_w, ada_b)
    for layer in range(depth):
        i = layer // 2
        sh_m, sc_m, gt_m, sh_f, sc_f, gt_f = (mod[layer, :, k * d:(k + 1) * d] for k in range(6))
        g = [norm_g[layer, k].reshape(1, d) for k in range(4)]
        if layer % 2 == 0:
            z = norm_mod_matmul(xs, g[0], sh_m, sc_m, e_w_in[i].astype(BF16))
            a = conformer_conv(z, e_conv_w[i], e_conv_b[i], e_cln_g[i], e_cln_b[i])
            o = dilated_attention(z, t5_table, 2 * conv_dim, (z.shape[1] - 2 * conv_dim) // 3)
            xs = out_proj_residual(a, o, e_w_out[i].astype(BF16), xs, g[1], gt_m)
            xs = ffn_sublayer(xs, g[2], sh_f, sc_f, g[3], gt_f, e_ffn_w1[i].astype(BF16),
                              e_ffn_w3[i].astype(BF16), e_ffn_w2[i].astype(BF16))
        else:
            q, k, v, u = odd_in_proj(xs, g[0], sh_m, sc_m, o_w_in[i], o_q_norm_g[i], o_w_uq[i],
                                     o_kv_norm_g[i], o_w_ukv[i])
            o_mla = mla_attention(q, k, v)
            y = s5_bidirectional(u, s5_a_re[i], s5_a_im[i], s5_log_dt[i], s5_b_re[i], s5_b_im[i],
                                 s5_c_re[i], s5_c_im[i])
            y = s5_gate(y, u, s5_d[i], s5_w_glu[i])
            xs = out_proj_residual(o_mla, y, o_w_out[i].astype(BF16), xs, g[1], gt_m)
            xs = moe_sublayer(xs, g[2], sh_f, sc_f, g[3], gt_f, router_w[i], moe_w1[i], moe_w3[i], moe_w2[i])
    return xs.reshape(bsz, seq, d)
```

```python
import functools
import math

import jax
import jax.numpy as jnp
from jax import lax
from jax.experimental import pallas as pl
from jax.experimental.pallas import tpu as pltpu

F32 = jnp.float32
BF16 = jnp.bfloat16

RMS_EPS = 1e-6
LN_EPS = 1e-5
NEG_INF = -1e30

CONV_WIDTH = 31
CONV_HALO = 16
DIL_HEAD_DIM = 64
DIL_CONFIGS = ((128, 1), (512, 4), (2048, 16))
DIL_HALF = 64
DIL_TILE = 1024
DIL_QB = 256
DIL_MERGE = 4
N_BUCKETS = 32
T5_MAX_DIST = DIL_CONFIGS[-1][0] // 2
MLA_HEADS = 12
NOPE_DIM = 128
ROPE_DIM = 64
V_DIM = 128
ROPE_BASE = 10000.0
MLA_QK_PAD = 256
S5_GROUP = 16
S5_STATE = 64
S5_CHUNK = 16
N_EXPERTS = 8
EXPERT_TILE = 896
LANE = 128
SUBLANES = 8
VMEM_LIMIT_MB = 56


def _cp(sem, vmem_mb=VMEM_LIMIT_MB):
    return pltpu.CompilerParams(dimension_semantics=sem, vmem_limit_bytes=vmem_mb * 1024 * 1024)


def _rms(x):
    return x * lax.rsqrt(jnp.mean(x * x, axis=-1, keepdims=True) + RMS_EPS)


def _modnorm(x, g, sh, sc):
    return (_rms(x) * g) * (1.0 + sc) + sh


def _sigmoid(x):
    return 1.0 / (1.0 + jnp.exp(-x))


def _silu(x):
    return x * _sigmoid(x)


def _adaln_kernel(c_ref, w_ref, b_ref, o_ref, cb_ref):
    @pl.when((pl.program_id(0) == 0) & (pl.program_id(1) == 0))
    def _():
        cc = c_ref[...]
        cb_ref[...] = jnp.broadcast_to(_silu(cc), cb_ref.shape)

    cb = cb_ref[...]
    for j in range(o_ref.shape[-1] // LANE):
        sl = slice(j * LANE, (j + 1) * LANE)
        o_ref[:, sl] = jnp.sum(w_ref[:, sl] * cb, axis=0, keepdims=True) + b_ref[:, sl]


def adaln(c, ada_w, ada_b, tn=1024):
    nl, d, n = ada_w.shape
    out = pl.pallas_call(
        _adaln_kernel,
        out_shape=jax.ShapeDtypeStruct((nl, 1, n), F32),
        grid=(nl, n // tn),
        in_specs=[pl.BlockSpec((d, 1), lambda l, j: (0, 0)),
                  pl.BlockSpec((None, d, tn), lambda l, j: (l, 0, j)),
                  pl.BlockSpec((None, 1, tn), lambda l, j: (l, 0, j))],
        out_specs=pl.BlockSpec((None, 1, tn), lambda l, j: (l, 0, j)),
        scratch_shapes=[pltpu.VMEM((d, LANE), F32)],
        compiler_params=_cp(("arbitrary", "arbitrary")),
        name="adaln",
    )(c.reshape(d, 1), ada_w, ada_b.reshape(nl, 1, n))
    return out


def _nmm_kernel(x_ref, g_ref, sh_ref, sc_ref, w_ref, o_ref, h_ref):
    @pl.when(pl.program_id(1) == 0)
    def _():
        h_ref[...] = _modnorm(x_ref[...], g_ref[...], sh_ref[...], sc_ref[...]).astype(h_ref.dtype)

    o_ref[...] = jnp.dot(h_ref[...], w_ref[...], preferred_element_type=F32).astype(o_ref.dtype)


def norm_mod_matmul(x, g, sh, sc, w, tm=1024, tn=512):
    s, d = x.shape
    n = w.shape[1]
    tm, tn = min(tm, s), min(tn, n)
    vec = pl.BlockSpec((1, d), lambda i, j: (0, 0))
    return pl.pallas_call(
        _nmm_kernel,
        out_shape=jax.ShapeDtypeStruct((s, n), BF16),
        grid=(s // tm, n // tn),
        in_specs=[pl.BlockSpec((tm, d), lambda i, j: (i, 0)), vec, vec, vec,
                  pl.BlockSpec((d, tn), lambda i, j: (0, j))],
        out_specs=pl.BlockSpec((tm, tn), lambda i, j: (i, j)),
        scratch_shapes=[pltpu.VMEM((tm, d), BF16)],
        compiler_params=_cp(("parallel", "arbitrary")),
        name="norm_mod_matmul",
    )(x, g, sh, sc, w)


def _conv_kernel(av_ref, ag_ref, avp_ref, agp_ref, avn_ref, agn_ref, w_ref, b_ref, lg_ref, lb_ref,
                 o_ref, buf_ref, sh_ref, *, rows):
    i = pl.program_id(0)
    n = pl.num_programs(0)
    ts = av_ref.shape[0]

    def glu(v_ref, g_ref):
        return v_ref[...].astype(F32) * _sigmoid(g_ref[...].astype(F32))

    buf_ref[0:CONV_HALO, :] = jnp.where(i > 0, glu(avp_ref, agp_ref), 0.0)
    buf_ref[CONV_HALO:CONV_HALO + ts, :] = glu(av_ref, ag_ref)
    buf_ref[CONV_HALO + ts:2 * CONV_HALO + ts, :] = jnp.where(i < n - 1, glu(avn_ref, agn_ref), 0.0)
    buf_ref[2 * CONV_HALO + ts:, :] = jnp.zeros((SUBLANES, buf_ref.shape[1]), F32)

    span = ts + 2 * CONV_HALO
    for o in range(SUBLANES):
        sh_ref[o] = buf_ref[o:o + span, :]

    off = CONV_HALO - CONV_WIDTH // 2
    for r in range(ts // rows):
        acc = jnp.broadcast_to(b_ref[...], (rows, b_ref.shape[-1]))
        for k in range(CONV_WIDTH):
            lo = r * rows + k + off
            base = lo - lo % SUBLANES
            acc = acc + w_ref[k:k + 1, :] * sh_ref[lo % SUBLANES, base:base + rows, :]
        mu = jnp.mean(acc, axis=-1, keepdims=True)
        dlt = acc - mu
        var = jnp.mean(dlt * dlt, axis=-1, keepdims=True)
        y = dlt * lax.rsqrt(var + LN_EPS) * lg_ref[...] + lb_ref[...]
        o_ref[r * rows:(r + 1) * rows, :] = _silu(y).astype(o_ref.dtype)


def conformer_conv(z, conv_w, conv_b, cln_g, cln_b, ts=256, rows=32):
    s = z.shape[0]
    c = conv_w.shape[1]
    ts = min(ts, s)
    hb = ts // CONV_HALO
    nh = s // CONV_HALO
    main = lambda col: pl.BlockSpec((ts, c), lambda i: (i, col))
    prev = lambda col: pl.BlockSpec((CONV_HALO, c), lambda i: (jnp.maximum(i * hb - 1, 0), col))
    nxt = lambda col: pl.BlockSpec((CONV_HALO, c), lambda i: (jnp.minimum((i + 1) * hb, nh - 1), col))
    vec = lambda r: pl.BlockSpec((r, c), lambda i: (0, 0))
    return pl.pallas_call(
        functools.partial(_conv_kernel, rows=min(rows, ts)),
        out_shape=jax.ShapeDtypeStruct((s, c), BF16),
        grid=(s // ts,),
        in_specs=[main(0), main(1), prev(0), prev(1), nxt(0), nxt(1),
                  vec(CONV_WIDTH), vec(1), vec(1), vec(1)],
        out_specs=pl.BlockSpec((ts, c), lambda i: (i, 0)),
        scratch_shapes=[pltpu.VMEM((ts + 2 * CONV_HALO + SUBLANES, c), F32),
                        pltpu.VMEM((SUBLANES, ts + 2 * CONV_HALO, c), F32)],
        compiler_params=_cp(("parallel",)),
        name="conformer_conv",
    )(z, z, z, z, z, z, conv_w, conv_b.reshape(1, c), cln_g.reshape(1, c), cln_b.reshape(1, c))


def _t5_bucket(rel):
    half = N_BUCKETS // 2
    exact = half // 2
    n = jnp.abs(rel)
    large = exact + (jnp.log(jnp.maximum(n, 1).astype(F32) / exact)
                     / math.log(T5_MAX_DIST / exact) * (half - exact)).astype(jnp.int32)
    large = jnp.minimum(large, half - 1)
    return jnp.where(rel > 0, half, 0) + jnp.where(n < exact, n, large)


def _toeplitz(w, rows, cols):
    n = w.shape[-1]
    wp = jnp.concatenate([w, jnp.zeros(w.shape[:-1] + (1,), w.dtype)], axis=-1)
    flat = jnp.tile(wp, (1,) * (w.ndim - 1) + (rows,))[..., :rows * n]
    return flat.reshape(w.shape[:-1] + (rows, n))[..., rows - 1:rows - 1 + cols]


def _band_diag(t5_table, dil, qb):
    kl = qb + 2 * DIL_HALF
    n = -(-(qb + kl - 1) // LANE) * LANE
    dist = jnp.arange(n) - (qb - 1) - DIL_HALF
    return jnp.where((jnp.abs(dist) <= DIL_HALF)[None], t5_table[_t5_bucket(dist * dil)].T.astype(F32), NEG_INF)


def _band_bias_rows(diag_row, qb):
    n = diag_row.shape[1]
    rolled = pltpu.roll(jnp.broadcast_to(diag_row, (qb, n)), n - (qb - 1), 1, stride=1, stride_axis=0)
    return rolled[:, :qb + 2 * DIL_HALF]


def _band_group(q, kw, vw, bias_ref, pen):
    r = q.shape[0]
    is_a = lax.broadcasted_iota(jnp.int32, (1, LANE), 1) < DIL_HEAD_DIM
    q2 = jnp.concatenate([jnp.where(is_a, q, 0.0), jnp.where(is_a, 0.0, q)], axis=0).astype(BF16)
    s = lax.dot_general(q2, kw.astype(BF16), (((1,), (1,)), ((), ())), preferred_element_type=F32)
    s = s + bias_ref[...]
    if pen is not None:
        s = s + pen
    m = jnp.max(s, axis=-1, keepdims=True)
    p = jnp.exp(s - m)
    l = jnp.sum(p, axis=-1, keepdims=True)
    pv = jnp.dot(p.astype(BF16), vw.astype(BF16), preferred_element_type=F32)
    pick = lambda x: jnp.where(is_a, x[:r], x[r:])
    return pick(pv), pick(m), pick(l)


def _dil_kernel(q_ref, kp_ref, kc_ref, kn_ref, vp_ref, vc_ref, vn_ref, d1_ref, d4_ref, d16_ref, o_ref,
                qf, kf, vf, a1, m1, l1, a4, m4, l4, a16, m16, l16, b1_ref, b4_ref, b16_ref):
    i = pl.program_id(1)
    first = i == 0
    last = i == pl.num_programs(1) - 1
    t = DIL_TILE
    half = DIL_HALF

    @pl.when(first)
    def _():
        q16 = t // 16
        b16_ref[...] = jnp.full(b16_ref.shape, NEG_INF, F32)
        for h in range(2):
            b1_ref[h * DIL_QB:(h + 1) * DIL_QB, :] = _band_bias_rows(d1_ref[h:h + 1, :], DIL_QB)
            b4_ref[h * DIL_QB:(h + 1) * DIL_QB, :] = _band_bias_rows(d4_ref[h:h + 1, :], DIL_QB)
            blk = _band_bias_rows(d16_ref[h:h + 1, :], q16)
            for u in range(DIL_MERGE):
                r0 = (h * DIL_MERGE + u) * q16
                b16_ref[r0:r0 + q16, u * blk.shape[1]:(u + 1) * blk.shape[1]] = blk

    qf[...] = q_ref[...].astype(F32) * (DIL_HEAD_DIM ** -0.5)
    for dst, (p_ref, c_ref, n_ref) in ((kf, (kp_ref, kc_ref, kn_ref)), (vf, (vp_ref, vc_ref, vn_ref))):
        dst[0:t, :] = p_ref[...].astype(F32)
        dst[t:2 * t, :] = c_ref[...].astype(F32)
        dst[2 * t:3 * t, :] = n_ref[...].astype(F32)

    def store(refs, rows, vals):
        for ref, val in zip(refs, vals):
            ref[rows, :] = val

    qb = DIL_QB
    kl = qb + 2 * half
    col = lax.broadcasted_iota(jnp.int32, (1, kl), 1)
    lo = jnp.where((col < half) & first, NEG_INF, 0.0)
    hi = jnp.where((col >= qb + half) & last, NEG_INF, 0.0)

    nb = t // qb
    for b in range(nb):
        pen = lo + hi if nb == 1 else (lo if b == 0 else (hi if b == nb - 1 else None))
        k0 = t + b * qb - half
        rows = slice(b * qb, (b + 1) * qb)
        store((a1, m1, l1), rows, _band_group(qf[rows, :], kf[k0:k0 + kl, :], vf[k0:k0 + kl, :], b1_ref, pen))

    dil = 4
    for r in range(dil):
        qrows = pl.ds(r, qb, stride=dil)
        krows = pl.ds(t - half * dil + r, kl, stride=dil)
        store((a4, m4, l4), qrows, _band_group(qf[qrows, :], kf[krows, :], vf[krows, :], b4_ref, lo + hi))

    dil = 16
    q16 = t // dil
    k16 = q16 + 2 * half
    col = lax.rem(lax.broadcasted_iota(jnp.int32, (1, DIL_MERGE * k16), 1), k16)
    pen16 = (jnp.where((col < half) & first, NEG_INF, 0.0)
             + jnp.where((col >= q16 + half) & last, NEG_INF, 0.0))
    for g in range(dil // DIL_MERGE):
        res = range(g * DIL_MERGE, (g + 1) * DIL_MERGE)
        qrows = [pl.ds(r, q16, stride=dil) for r in res]
        krows = [pl.ds(t - half * dil + r, k16, stride=dil) for r in res]
        out = _band_group(jnp.concatenate([qf[rr, :] for rr in qrows], axis=0),
                          jnp.concatenate([kf[rr, :] for rr in krows], axis=0),
                          jnp.concatenate([vf[rr, :] for rr in krows], axis=0), b16_ref, pen16)
        for u, rr in enumerate(qrows):
            store((a16, m16, l16), rr, [x[u * q16:(u + 1) * q16] for x in out])

    mm = jnp.maximum(jnp.maximum(m1[...], m4[...]), m16[...])
    w1, w4, w16 = jnp.exp(m1[...] - mm), jnp.exp(m4[...] - mm), jnp.exp(m16[...] - mm)
    num = w1 * a1[...] + w4 * a4[...] + w16 * a16[...]
    den = w1 * l1[...] + w4 * l4[...] + w16 * l16[...]
    o_ref[...] = (num / den).astype(o_ref.dtype)


def dilated_attention(z, t5_table, col0, width):
    s, zw = z.shape
    t = DIL_TILE
    assert s % t == 0 and col0 % LANE == 0 and width % LANE == 0 and t // 4 == DIL_QB
    nt = s // t
    nh = t5_table.shape[1]
    cq, ck, cv = ((col0 + k * width) // LANE for k in range(3))
    q16 = t // 16
    diags = [_band_diag(t5_table, dil, qb).reshape(nh // 2, 2, -1)
             for dil, qb in ((1, DIL_QB), (4, DIL_QB), (16, q16))]
    kl, k16 = DIL_QB + 2 * DIL_HALF, q16 + 2 * DIL_HALF

    def blk(col, shift):
        return pl.BlockSpec((t, LANE), lambda hg, i: (jnp.clip(i + shift, 0, nt - 1), col + hg))

    diag_spec = lambda d: pl.BlockSpec((None,) + d.shape[1:], lambda hg, i: (hg, 0, 0))
    return pl.pallas_call(
        _dil_kernel,
        out_shape=jax.ShapeDtypeStruct((s, width), BF16),
        grid=(width // LANE, nt),
        in_specs=[blk(cq, 0), blk(ck, -1), blk(ck, 0), blk(ck, 1), blk(cv, -1), blk(cv, 0), blk(cv, 1)]
                 + [diag_spec(d) for d in diags],
        out_specs=pl.BlockSpec((t, LANE), lambda hg, i: (i, hg)),
        scratch_shapes=[pltpu.VMEM((t, LANE), F32), pltpu.VMEM((3 * t, LANE), F32),
                        pltpu.VMEM((3 * t, LANE), F32)] + [pltpu.VMEM((t, LANE), F32)] * 9
                       + [pltpu.VMEM((2 * DIL_QB, kl), F32), pltpu.VMEM((2 * DIL_QB, kl), F32),
                          pltpu.VMEM((2 * DIL_MERGE * q16, DIL_MERGE * k16), F32)],
        compiler_params=_cp(("parallel", "arbitrary")),
        name="dilated_attention",
    )(*([z] * 7 + diags))


def _oproj_kernel(a1_ref, a2_ref, w_ref, x_ref, g_ref, gt_ref, o_ref):
    k1 = a1_ref.shape[1]
    y = jnp.dot(a1_ref[...], w_ref[0:k1, :], preferred_element_type=F32)
    y = y + jnp.dot(a2_ref[...], w_ref[k1:, :], preferred_element_type=F32)
    o_ref[...] = x_ref[...] + gt_ref[...] * (_rms(y) * g_ref[...])


def out_proj_residual(a1, a2, w, x, g, gt, tm=512):
    s, d = x.shape
    k1, k2 = a1.shape[1], a2.shape[1]
    tm = min(tm, s)
    vec = pl.BlockSpec((1, d), lambda i: (0, 0))
    return pl.pallas_call(
        _oproj_kernel,
        out_shape=jax.ShapeDtypeStruct((s, d), F32),
        grid=(s // tm,),
        in_specs=[pl.BlockSpec((tm, k1), lambda i: (i, 0)), pl.BlockSpec((tm, k2), lambda i: (i, 0)),
                  pl.BlockSpec((k1 + k2, d), lambda i: (0, 0)),
                  pl.BlockSpec((tm, d), lambda i: (i, 0)), vec, vec],
        out_specs=pl.BlockSpec((tm, d), lambda i: (i, 0)),
        compiler_params=_cp(("parallel",)),
        name="out_proj_residual",
    )(a1, a2, w, x, g, gt)


def _ffn_kernel(x_ref, g_ref, sh_ref, sc_ref, g2_ref, gt_ref, w1_ref, w3_ref, w2_ref, o_ref, h_ref):
    f = pl.program_id(1)

    @pl.when(f == 0)
    def _():
        h_ref[...] = _modnorm(x_ref[...], g_ref[...], sh_ref[...], sc_ref[...]).astype(h_ref.dtype)
        o_ref[...] = jnp.zeros_like(o_ref)

    h = h_ref[...]
    u = jnp.dot(h, w1_ref[...], preferred_element_type=F32)
    v = jnp.dot(h, w3_ref[...], preferred_element_type=F32)
    o_ref[...] += jnp.dot((_silu(u) * v).astype(BF16), w2_ref[...], preferred_element_type=F32)

    @pl.when(f == pl.num_programs(1) - 1)
    def _():
        o_ref[...] = x_ref[...] + gt_ref[...] * (_rms(o_ref[...]) * g2_ref[...])


def ffn_sublayer(x, g, sh, sc, g2, gt, w1, w3, w2, tm=1024, tf=256):
    s, d = x.shape
    f = w1.shape[1]
    tm, tf = min(tm, s), min(tf, f)
    vec = pl.BlockSpec((1, d), lambda i, j: (0, 0))
    return pl.pallas_call(
        _ffn_kernel,
        out_shape=jax.ShapeDtypeStruct((s, d), F32),
        grid=(s // tm, f // tf),
        in_specs=[pl.BlockSpec((tm, d), lambda i, j: (i, 0)), vec, vec, vec, vec, vec,
                  pl.BlockSpec((d, tf), lambda i, j: (0, j)), pl.BlockSpec((d, tf), lambda i, j: (0, j)),
                  pl.BlockSpec((tf, d), lambda i, j: (j, 0))],
        out_specs=pl.BlockSpec((tm, d), lambda i, j: (i, 0)),
        scratch_shapes=[pltpu.VMEM((tm, d), BF16)],
        compiler_params=_cp(("parallel", "arbitrary")),
        name="ffn_sublayer",
    )(x, g, sh, sc, g2, gt, w1, w3, w2)


def _rope_slab(t, c_ref, s1_ref, s2_ref):
    return (t * c_ref[...] + pltpu.roll(t, LANE - ROPE_DIM // 2, 1) * s1_ref[...]
            + pltpu.roll(t, ROPE_DIM // 2, 1) * s2_ref[...])


def _oddin_kernel(x_ref, g_ref, sh_ref, sc_ref, win_ref, qg_ref, wuq_ref, kvg_ref, wuk_ref, wuv_ref,
                  c_ref, s1_ref, s2_ref, q_ref, k_ref, v_ref, u_ref, *, q_lora, kv_lora, scale):
    h = _modnorm(x_ref[...], g_ref[...], sh_ref[...], sc_ref[...]).astype(BF16)
    z = jnp.dot(h, win_ref[...], preferred_element_type=F32)
    o_kv, o_pe, o_u = q_lora, q_lora + kv_lora, q_lora + kv_lora + LANE
    u_ref[...] = z[:, o_u:]

    qn = (_rms(z[:, :q_lora]) * qg_ref[...]).astype(BF16)
    q = jnp.dot(qn, wuq_ref[...], preferred_element_type=F32)
    kvn = (_rms(z[:, o_kv:o_pe]) * kvg_ref[...]).astype(BF16)
    kn = jnp.dot(kvn, wuk_ref[...], preferred_element_type=F32)
    v_ref[...] = jnp.dot(kvn, wuv_ref[...], preferred_element_type=F32).astype(v_ref.dtype)
    kpe = _rope_slab(z[:, o_pe:o_u], c_ref, s1_ref, s2_ref).astype(k_ref.dtype)

    for hh in range(MLA_HEADS):
        b = hh * MLA_QK_PAD
        q_ref[:, b:b + NOPE_DIM] = (q[:, b:b + NOPE_DIM] * scale).astype(q_ref.dtype)
        qpe = _rope_slab(q[:, b + NOPE_DIM:b + MLA_QK_PAD], c_ref, s1_ref, s2_ref)
        q_ref[:, b + NOPE_DIM:b + MLA_QK_PAD] = (qpe * scale).astype(q_ref.dtype)
        k_ref[:, b:b + NOPE_DIM] = kn[:, hh * NOPE_DIM:(hh + 1) * NOPE_DIM].astype(k_ref.dtype)
        k_ref[:, b + NOPE_DIM:b + MLA_QK_PAD] = kpe


def _rope_tables(seq):
    pos = jnp.arange(seq, dtype=F32)
    inv_freq = ROPE_BASE ** (-jnp.arange(0, ROPE_DIM, 2, dtype=F32) / ROPE_DIM)
    ang = pos[:, None] * inv_freq[None, :]
    cos, sin = jnp.cos(ang), jnp.sin(ang)
    zero = jnp.zeros_like(cos)
    c = jnp.concatenate([cos, cos, zero, zero], axis=-1)
    s1 = jnp.concatenate([-sin, zero, zero, zero], axis=-1)
    s2 = jnp.concatenate([zero, sin, zero, zero], axis=-1)
    return c, s1, s2


def odd_in_proj(x, g, sh, sc, w_in, q_norm_g, w_uq, kv_norm_g, w_ukv, tm=256):
    s, d = x.shape
    q_lora, kv_lora = q_norm_g.shape[0], kv_norm_g.shape[0]
    s5_dim = w_in.shape[1] - q_lora - kv_lora - ROPE_DIM
    tm = min(tm, s)
    w_in_p = jnp.concatenate(
        [w_in[:, :q_lora + kv_lora + ROPE_DIM], jnp.zeros((d, LANE - ROPE_DIM), w_in.dtype),
         w_in[:, q_lora + kv_lora + ROPE_DIM:]], axis=1).astype(BF16)
    wq = w_uq.reshape(q_lora, MLA_HEADS, NOPE_DIM + ROPE_DIM)
    wq = jnp.pad(wq, ((0, 0), (0, 0), (0, MLA_QK_PAD - NOPE_DIM - ROPE_DIM)))
    wq = wq.reshape(q_lora, MLA_HEADS * MLA_QK_PAD).astype(BF16)
    wkv = w_ukv.reshape(kv_lora, MLA_HEADS, NOPE_DIM + V_DIM)
    wuk = wkv[:, :, :NOPE_DIM].reshape(kv_lora, MLA_HEADS * NOPE_DIM).astype(BF16)
    wuv = wkv[:, :, NOPE_DIM:].reshape(kv_lora, MLA_HEADS * V_DIM).astype(BF16)
    c, s1, s2 = _rope_tables(s)
    scale = (NOPE_DIM + ROPE_DIM) ** -0.5 * math.log2(math.e)

    full = lambda a: pl.BlockSpec(a.shape, lambda i: (0, 0))
    row = lambda w: pl.BlockSpec((tm, w), lambda i: (i, 0))
    vec = pl.BlockSpec((1, d), lambda i: (0, 0))
    qg = q_norm_g.reshape(1, q_lora)
    kvg = kv_norm_g.reshape(1, kv_lora)
    return pl.pallas_call(
        functools.partial(_oddin_kernel, q_lora=q_lora, kv_lora=kv_lora, scale=scale),
        out_shape=[jax.ShapeDtypeStruct((s, MLA_HEADS * MLA_QK_PAD), BF16),
                   jax.ShapeDtypeStruct((s, MLA_HEADS * MLA_QK_PAD), BF16),
                   jax.ShapeDtypeStruct((s, MLA_HEADS * V_DIM), BF16),
                   jax.ShapeDtypeStruct((s, s5_dim), F32)],
        grid=(s // tm,),
        in_specs=[row(d), vec, vec, vec, full(w_in_p), full(qg), full(wq), full(kvg), full(wuk), full(wuv),
                  row(LANE), row(LANE), row(LANE)],
        out_specs=[row(MLA_HEADS * MLA_QK_PAD), row(MLA_HEADS * MLA_QK_PAD), row(MLA_HEADS * V_DIM),
                   row(s5_dim)],
        compiler_params=_cp(("parallel",)),
        name="odd_in_proj",
    )(x, g, sh, sc, w_in_p, qg, wq, kvg, wuk, wuv, c, s1, s2)


def _flash_kernel(q_ref, k_ref, v_ref, o_ref, m_ref, acc_ref, s0, s1, *, rows, nk):
    t = pl.program_id(0)
    j1 = lax.rem(jnp.maximum(t - 1, 0), nk)
    tq = q_ref.shape[0]
    vd = v_ref.shape[1]

    @pl.when(t == 0)
    def _():
        m_ref[...] = jnp.full_like(m_ref, -jnp.inf)
        acc_ref[...] = jnp.zeros_like(acc_ref)
        s1[...] = jnp.full_like(s1, -jnp.inf)

    def stages(s_rd, s_wr):
        v = v_ref[...]
        v1 = jnp.concatenate([v, jnp.ones_like(v)], axis=1)
        k = k_ref[...]
        live = t >= 1
        fresh = j1 == 0
        for c in range(tq // rows):
            rs = slice(c * rows, (c + 1) * rows)
            s = s_rd[rs, :]
            m_prev = jnp.where(fresh, -jnp.inf, m_ref[rs, :])
            m_cand = jnp.maximum(m_prev, jnp.max(s, axis=-1, keepdims=True))
            m_new = jnp.where(live, m_cand, m_prev)
            m_sub = jnp.where(live, m_cand, 0.0)
            p = jnp.exp2((s - m_sub[:, :1]).astype(BF16))
            alpha = jnp.where(live, jnp.exp2(m_prev - m_new), 1.0)
            m_ref[rs, :] = m_new
            pv = jnp.dot(p, v1, preferred_element_type=F32)
            acc_ref[rs, :vd] = alpha * acc_ref[rs, :vd] + pv[:, :vd]
            acc_ref[rs, vd:] = alpha * acc_ref[rs, vd:] + pv[:, vd:]
            s_wr[rs, :] = lax.dot_general(q_ref[rs, :], k, (((1,), (1,)), ((), ())),
                                          preferred_element_type=F32)

    parity = lax.rem(t, 2)

    @pl.when(parity == 0)
    def _():
        stages(s1, s0)

    @pl.when(parity == 1)
    def _():
        stages(s0, s1)

    @pl.when((t >= 1) & (j1 == nk - 1))
    def _():
        o_ref[...] = (acc_ref[:, :vd] / acc_ref[:, vd:]).astype(o_ref.dtype)


def mla_attention(q, k, v, tq=2048, tk=2048, rows=512):
    s = q.shape[0]
    tq, tk = min(tq, s), min(tk, s)
    rows = min(rows, tq)
    ni, nk = s // tq, s // tk
    n = MLA_HEADS * ni * nk

    def item(t, lag):
        w = jnp.clip(t - lag, 0, n - 1)
        return w // (ni * nk), lax.rem(w, ni * nk) // nk, lax.rem(w, nk)

    def q_map(t):
        h, i, _ = item(t, 0)
        return i, h

    def k_map(t):
        h, _, j = item(t, 0)
        return j, h

    def v_map(t):
        h, _, j = item(t, 1)
        return j, h

    def o_map(t):
        h, i, _ = item(t, 1)
        return i, h

    return pl.pallas_call(
        functools.partial(_flash_kernel, rows=rows, nk=nk),
        out_shape=jax.ShapeDtypeStruct((s, MLA_HEADS * V_DIM), BF16),
        grid=(n + 1,),
        in_specs=[pl.BlockSpec((tq, MLA_QK_PAD), q_map),
                  pl.BlockSpec((tk, MLA_QK_PAD), k_map),
                  pl.BlockSpec((tk, V_DIM), v_map)],
        out_specs=pl.BlockSpec((tq, V_DIM), o_map),
        scratch_shapes=[pltpu.VMEM((tq, V_DIM), F32), pltpu.VMEM((tq, 2 * V_DIM), F32),
                        pltpu.VMEM((tq, tk), F32), pltpu.VMEM((tq, tk), F32)],
        compiler_params=_cp(("arbitrary",)),
        name="mla_flash_attention",
    )(q, k, v)


def _s5_matrices(a_re, a_im, log_dt, b_re, b_im, c_re, c_im, chunk):
    L = chunk
    A = lax.complex(a_re.astype(F32), a_im.astype(F32))
    dt = jnp.exp(log_dt.astype(F32))[..., None]
    adt = A * dt
    a_bar = jnp.exp(adt)
    b_bar = ((a_bar - 1.0) / A)[..., None] * lax.complex(b_re.astype(F32), b_im.astype(F32))
    c_c = lax.complex(c_re.astype(F32), c_im.astype(F32))
    kk = jnp.arange(L + 1, dtype=F32)
    apow = jnp.exp(adt[:, :, None, :] * kk[None, None, :, None].astype(jnp.complex64))
    g, p, gc = b_bar.shape[1], b_bar.shape[2], b_bar.shape[3]

    ker = jnp.real(jnp.einsum('dgcp,dgkp,dgpi->dgkci', c_c, apow[:, :, :L], b_bar))
    kb = ker[1][:, ::-1]
    diag = jnp.concatenate([kb[:, :L - 1], ker[0][:, :1] + kb[:, L - 1:], ker[0][:, 1:]], axis=1)
    t_mat = _toeplitz(jnp.transpose(diag, (0, 2, 3, 1)), L, L)
    t_mat = jnp.transpose(t_mat, (0, 3, 2, 4, 1)).reshape(g, L * gc, L * gc)

    pf = apow[0][:, ::-1][:, 1:][:, :, :, None] * b_bar[0][:, None]
    pb = apow[1][:, :L][:, :, :, None] * b_bar[1][:, None]
    def p_lay(m):
        return jnp.transpose(m, (0, 1, 3, 2)).reshape(g, L * gc, p)
    p_mat = jnp.concatenate([p_lay(jnp.real(pf)), p_lay(jnp.imag(pf)),
                             p_lay(jnp.real(pb)), p_lay(jnp.imag(pb))], axis=-1)

    wf = c_c[0][:, None] * apow[0][:, 1:][:, :, None, :]
    wb = c_c[1][:, None] * apow[1][:, ::-1][:, :L][:, :, None, :]
    def q_lay(m):
        return jnp.transpose(m, (0, 3, 1, 2)).reshape(g, p, L * gc)
    q_mat = jnp.concatenate([q_lay(jnp.real(wf)), q_lay(-jnp.imag(wf)),
                             q_lay(jnp.real(wb)), q_lay(-jnp.imag(wb))], axis=1)

    al = apow[:, :, L]
    dec = jnp.stack([jnp.real(al[0]), jnp.imag(al[0]), jnp.real(al[1]), jnp.imag(al[1])]).reshape(4, g * p)
    return t_mat, p_mat, q_mat, dec


def _s5_state_kernel(u_ref, p_ref, fre_ref, fim_ref, bre_ref, bim_ref):
    outs = [[], [], [], []]
    for gi in range(u_ref.shape[0]):
        xe = jnp.dot(u_ref[gi].astype(BF16), p_ref[gi], preferred_element_type=F32)
        for part in range(4):
            outs[part].append(xe[:, part * S5_STATE:(part + 1) * S5_STATE])
    for part, ref in enumerate((fre_ref, fim_ref, bre_ref, bim_ref)):
        ref[...] = jnp.concatenate(outs[part], axis=-1)


def _s5_scan_kernel(fre_ref, fim_ref, bre_ref, bim_ref, dec_ref, ofre_ref, ofim_ref, obre_ref, obim_ref):
    nc = fre_ref.shape[0]
    w = fre_ref.shape[1]
    far, fai, bar, bai = (dec_ref[i:i + 1, :] for i in range(4))

    def fwd(c, st):
        re, im = st
        ofre_ref[pl.ds(c, 1), :] = re
        ofim_ref[pl.ds(c, 1), :] = im
        return (far * re - fai * im + fre_ref[pl.ds(c, 1), :],
                far * im + fai * re + fim_ref[pl.ds(c, 1), :])

    def bwd(i, st):
        c = nc - 1 - i
        re, im = st
        obre_ref[pl.ds(c, 1), :] = re
        obim_ref[pl.ds(c, 1), :] = im
        return (bar * re - bai * im + bre_ref[pl.ds(c, 1), :],
                bar * im + bai * re + bim_ref[pl.ds(c, 1), :])

    zero = (jnp.zeros((1, w), F32), jnp.zeros((1, w), F32))
    lax.fori_loop(0, nc, fwd, zero)
    lax.fori_loop(0, nc, bwd, zero)


def _s5_out_kernel(u_ref, t_ref, q_ref, fre_ref, fim_ref, bre_ref, bim_ref, y_ref):
    for gi in range(u_ref.shape[0]):
        lanes = slice(gi * S5_STATE, (gi + 1) * S5_STATE)
        y = jnp.dot(u_ref[gi].astype(BF16), t_ref[gi], preferred_element_type=F32)
        xin = jnp.concatenate([r[:, lanes] for r in (fre_ref, fim_ref, bre_ref, bim_ref)], axis=-1)
        y_ref[gi] = y + jnp.dot(xin.astype(BF16), q_ref[gi], preferred_element_type=F32)


def s5_bidirectional(u, a_re, a_im, log_dt, b_re, b_im, c_re, c_im, chunk=S5_CHUNK):
    s, dim = u.shape
    gc, p = S5_GROUP, S5_STATE
    g = dim // gc
    chunk = min(chunk, s)
    nc = s // chunk
    gp = LANE // p
    t_mat, p_mat, q_mat, dec = _s5_matrices(a_re, a_im, log_dt, b_re, b_im, c_re, c_im, chunk)
    t_mat, p_mat, q_mat = t_mat.astype(BF16), p_mat.astype(BF16), q_mat.astype(BF16)
    u_t = jnp.transpose(u.reshape(nc, chunk, g, gc), (2, 0, 1, 3)).reshape(g, nc, chunk * gc)

    grp = lambda a, b: pl.BlockSpec((gp, a, b), lambda i: (i, 0, 0))
    st = pl.BlockSpec((nc, LANE), lambda i: (0, i))
    st_shape = jax.ShapeDtypeStruct((nc, g * p), F32)
    xe = pl.pallas_call(
        _s5_state_kernel,
        out_shape=[st_shape] * 4,
        grid=(g // gp,),
        in_specs=[grp(nc, chunk * gc), grp(chunk * gc, 4 * p)],
        out_specs=[st] * 4,
        compiler_params=_cp(("parallel",)),
        name="s5_chunk_states",
    )(u_t, p_mat)

    lanes = min(4 * LANE, g * p)
    cols = lambda rows: pl.BlockSpec((rows, lanes), lambda i: (0, i))
    xin = pl.pallas_call(
        _s5_scan_kernel,
        out_shape=[st_shape] * 4,
        grid=(g * p // lanes,),
        in_specs=[cols(nc)] * 4 + [cols(4)],
        out_specs=[cols(nc)] * 4,
        compiler_params=_cp(("parallel",)),
        name="s5_boundary_scan",
    )(*xe, dec)

    y_t = pl.pallas_call(
        _s5_out_kernel,
        out_shape=jax.ShapeDtypeStruct((g, nc, chunk * gc), F32),
        grid=(g // gp,),
        in_specs=[grp(nc, chunk * gc), grp(chunk * gc, chunk * gc), grp(4 * p, chunk * gc)] + [st] * 4,
        out_specs=grp(nc, chunk * gc),
        compiler_params=_cp(("parallel",)),
        name="s5_outputs",
    )(u_t, t_mat, q_mat, *xin)
    return jnp.transpose(y_t.reshape(g, nc, chunk, gc), (1, 2, 0, 3)).reshape(s, dim)


def _s5_gate_kernel(y_ref, u_ref, d_ref, w_ref, o_ref):
    y = y_ref[...] + d_ref[...] * u_ref[...]
    y = 0.5 * y * (1.0 + jnp.tanh(math.sqrt(2.0 / math.pi) * (y + 0.044715 * (y * y * y))))
    z = jnp.dot(y.astype(BF16), w_ref[...], preferred_element_type=F32)
    o_ref[...] = (y * _sigmoid(z)).astype(o_ref.dtype)


def s5_gate(y, u, d_skip, w_glu, tm=1024):
    s, dim = y.shape
    tm = min(tm, s)
    row = pl.BlockSpec((tm, dim), lambda i: (i, 0))
    return pl.pallas_call(
        _s5_gate_kernel,
        out_shape=jax.ShapeDtypeStruct((s, dim), BF16),
        grid=(s // tm,),
        in_specs=[row, row, pl.BlockSpec((1, dim), lambda i: (0, 0)), pl.BlockSpec((dim, dim), lambda i: (0, 0))],
        out_specs=row,
        compiler_params=_cp(("parallel",)),
        name="s5_gate",
    )(y, u, d_skip.reshape(1, dim), w_glu.astype(BF16))


def _router_kernel(x_ref, g_ref, sh_ref, sc_ref, rw_ref, h_ref, info_ref):
    h = _modnorm(x_ref[...], g_ref[...], sh_ref[...], sc_ref[...])
    half = h.shape[1] // 2
    bits = lambda v: lax.bitcast_convert_type(v.astype(BF16).astype(F32), jnp.uint32)
    h_ref[...] = (bits(h[:, :half]) >> 16) | (bits(h[:, half:]) & jnp.uint32(0xFFFF0000))
    logits = jnp.dot(h, rw_ref[...], preferred_element_type=F32, precision=lax.Precision.HIGHEST)
    lane = lax.broadcasted_iota(jnp.int32, logits.shape, 1)
    logits = jnp.where(lane < N_EXPERTS, logits, -jnp.inf)
    m1 = jnp.max(logits, axis=-1, keepdims=True)
    i1 = jnp.min(jnp.where(logits == m1, lane, LANE), axis=-1, keepdims=True)
    rest = jnp.where(lane == i1, -jnp.inf, logits)
    m2 = jnp.max(rest, axis=-1, keepdims=True)
    i2 = jnp.min(jnp.where(rest == m2, lane, LANE), axis=-1, keepdims=True)
    e = jnp.exp(m2 - m1)
    g1 = 1.0 / (1.0 + e)
    g2 = e / (1.0 + e)
    info = jnp.where(lane == 0, i1.astype(F32),
                     jnp.where(lane == 1, i2.astype(F32),
                               jnp.where(lane == 2, g1, jnp.where(lane == 3, g2, 0.0))))
    info_ref[...] = info


def moe_router(x, g, sh, sc, router_w, tm=512):
    s, d = x.shape
    tm = min(tm, s)
    rw = jnp.pad(router_w, ((0, 0), (0, LANE - router_w.shape[1])))
    vec = pl.BlockSpec((1, d), lambda i: (0, 0))
    return pl.pallas_call(
        _router_kernel,
        out_shape=[jax.ShapeDtypeStruct((s, d // 2), jnp.uint32), jax.ShapeDtypeStruct((s, LANE), F32)],
        grid=(s // tm,),
        in_specs=[pl.BlockSpec((tm, d), lambda i: (i, 0)), vec, vec, vec,
                  pl.BlockSpec((d, LANE), lambda i: (0, 0))],
        out_specs=[pl.BlockSpec((tm, d // 2), lambda i: (i, 0)), pl.BlockSpec((tm, LANE), lambda i: (i, 0))],
        compiler_params=_cp(("parallel",)),
        name="moe_router",
    )(x, g, sh, sc, rw)


def _expert_kernel(te_ref, tv_ref, src_ref, hp_hbm, w1_ref, w3_ref, w2_ref, o_ref, acc_ref, gbuf, h_ref, sem):
    t = pl.program_id(0)
    f = pl.program_id(1)
    nt = pl.num_programs(0)
    nf = pl.num_programs(1)
    tm = acc_ref.shape[0]
    per_step = tm // nf
    valid = tv_ref[t] > 0
    slot = lax.rem(t, 2)

    def row_copy(tile, buf, r):
        return pltpu.make_async_copy(hp_hbm.at[pl.ds(src_ref[tile * tm + r], 1)],
                                     gbuf.at[buf, pl.ds(r, 1)], sem.at[buf])

    def wait_half(buf):
        pltpu.make_async_copy(hp_hbm.at[pl.ds(0, tm)], gbuf.at[buf], sem.at[buf]).wait()

    @pl.when((t == 0) & (f == 0))
    def _():
        def issue(r, carry):
            row_copy(0, 0, r).start()
            return carry
        lax.fori_loop(0, tm, issue, 0)

    @pl.when((f == 0) & ((t == 0) | (tv_ref[jnp.maximum(t - 1, 0)] > 0)))
    def _():
        wait_half(slot)

    @pl.when((f == 0) & valid)
    def _():
        w = gbuf[slot]
        half = w.shape[1]
        h_ref[:, :half] = lax.bitcast_convert_type(w << 16, F32).astype(BF16)
        h_ref[:, half:] = lax.bitcast_convert_type(w & jnp.uint32(0xFFFF0000), F32).astype(BF16)

    @pl.when(f == 0)
    def _():
        acc_ref[...] = jnp.zeros_like(acc_ref)

    def compute(rows):
        for u in range(per_step):
            row_copy(t + 1, 1 - slot, f * per_step + u).start()
        h = h_ref[:rows, :]
        u = jnp.dot(h, w1_ref[...].astype(BF16), preferred_element_type=F32)
        v = jnp.dot(h, w3_ref[...].astype(BF16), preferred_element_type=F32)
        acc_ref[:rows, :] += jnp.dot((_silu(u) * v).astype(BF16), w2_ref[...].astype(BF16),
                                     preferred_element_type=F32)

    @pl.when(tv_ref[t] > tm // 2)
    def _():
        compute(tm)

    @pl.when(valid & (tv_ref[t] <= tm // 2))
    def _():
        compute(tm // 2)

    @pl.when(f == nf - 1)
    def _():
        o_ref[...] = acc_ref[...].astype(o_ref.dtype)

    @pl.when((t == nt - 1) & (f == nf - 1) & valid)
    def _():
        wait_half(1 - slot)


def expert_ffn(h_packed, src, tile_expert, tile_valid, w1, w3, w2, tm, tf=512):
    dh = h_packed.shape[1]
    d = 2 * dh
    n = src.shape[0] - tm
    f = w1.shape[2]
    tf = min(tf, f)
    nf = f // tf
    assert tm % nf == 0 and n % tm == 0

    def fblk(j, t, tv):
        return jnp.where(tv[t] > 0, j, nf - 1)

    return pl.pallas_call(
        _expert_kernel,
        out_shape=jax.ShapeDtypeStruct((n, d), BF16),
        grid_spec=pltpu.PrefetchScalarGridSpec(
            num_scalar_prefetch=3,
            grid=(n // tm, nf),
            in_specs=[pl.BlockSpec(memory_space=pl.ANY),
                      pl.BlockSpec((None, d, tf), lambda t, j, te, tv, sr: (te[t], 0, fblk(j, t, tv))),
                      pl.BlockSpec((None, d, tf), lambda t, j, te, tv, sr: (te[t], 0, fblk(j, t, tv))),
                      pl.BlockSpec((None, tf, d), lambda t, j, te, tv, sr: (te[t], fblk(j, t, tv), 0))],
            out_specs=pl.BlockSpec((tm, d), lambda t, j, te, tv, sr: (t, 0)),
            scratch_shapes=[pltpu.VMEM((tm, d), F32), pltpu.VMEM((2, tm, dh), jnp.uint32),
                            pltpu.VMEM((tm, d), BF16), pltpu.SemaphoreType.DMA((2,))]),
        compiler_params=_cp(("arbitrary", "arbitrary")),
        name="expert_ffn",
    )(tile_expert, tile_valid, src, h_packed, w1, w3, w2)


def _dispatch(idx, tm):
    s = idx.shape[0]
    e_flat = idx.reshape(-1)
    onehot = (e_flat[:, None] == jnp.arange(N_EXPERTS)[None, :]).astype(jnp.int32)
    csum = jnp.cumsum(onehot, axis=0)
    rank = jnp.sum((csum - onehot) * onehot, axis=1)
    counts = csum[-1]
    padded = ((counts + tm - 1) // tm) * tm
    gend = jnp.cumsum(padded)
    slot = (gend - padded)[e_flat] + rank
    nt = -(-(2 * s) // tm) + N_EXPERTS
    src = jnp.zeros(((nt + 1) * tm,), jnp.int32).at[slot].set(jnp.arange(2 * s, dtype=jnp.int32) // 2)
    tstart = jnp.arange(nt, dtype=jnp.int32) * tm
    valid = tstart < gend[-1]
    te = jnp.minimum(jnp.sum((tstart[:, None] >= gend[None, :]).astype(jnp.int32), axis=1), N_EXPERTS - 1)
    nvalid = gend[-1] // tm
    fill = jnp.clip((gend - padded + counts)[te] - tstart, 0, tm)
    fill = jnp.where(valid, fill, 0)
    te = jnp.where(valid, te, te[jnp.maximum(nvalid - 1, 0)])
    return slot.reshape(s, 2), src, te.astype(jnp.int32), fill.astype(jnp.int32)


def _combine_kernel(ya_ref, yb_ref, info_ref, x_ref, g_ref, gt_ref, o_ref):
    info = info_ref[...]
    y = info[:, 2:3] * ya_ref[...].astype(F32) + info[:, 3:4] * yb_ref[...].astype(F32)
    o_ref[...] = x_ref[...] + gt_ref[...] * (_rms(y) * g_ref[...])


def moe_combine(ya, yb, info, x, g, gt, tm=512):
    s, d = x.shape
    tm = min(tm, s)
    row = lambda w: pl.BlockSpec((tm, w), lambda i: (i, 0))
    vec = pl.BlockSpec((1, d), lambda i: (0, 0))
    return pl.pallas_call(
        _combine_kernel,
        out_shape=jax.ShapeDtypeStruct((s, d), F32),
        grid=(s // tm,),
        in_specs=[row(d), row(d), row(LANE), row(d), vec, vec],
        out_specs=row(d),
        compiler_params=_cp(("parallel",)),
        name="moe_combine",
    )(ya, yb, info, x, g, gt)


def moe_sublayer(x, g, sh, sc, g2, gt, router_w, w1, w3, w2, tm_e=EXPERT_TILE):
    s = x.shape[0]
    tm_e = min(tm_e, s)
    h_packed, info = moe_router(x, g, sh, sc, router_w)
    idx = info[:, :2].astype(jnp.int32)
    slot, src, te, tv = _dispatch(idx, tm_e)
    y_sorted = expert_ffn(h_packed, src, te, tv, w1, w3, w2, tm_e)
    ya = jnp.take(y_sorted, slot[:, 0], axis=0)
    yb = jnp.take(y_sorted, slot[:, 1], axis=0)
    return moe_combine(ya, yb, info, x, g2, gt)


def kernel(x, c, t5_table, norm_g, ada_w, ada_b, e_w_in, e_conv_w, e_conv_b, e_cln_g, e_cln_b, e_w_out, e_ffn_w1, e_ffn_w3, e_ffn_w2, o_w_in, o_q_norm_g, o_w_uq, o_kv_norm_g, o_w_ukv, s5_a_re, s5_a_im, s5_log_dt, s5_b_re, s5_b_im, s5_c_re, s5_c_im, s5_d, s5_w_glu, o_w_out, router_w, moe_w1, moe_w3, moe_w2):
    bsz, seq, d = x.shape
    assert bsz == 1
    depth = norm_g.shape[0]
    conv_dim = e_conv_w.shape[2]
    xs = x.reshape(seq, d)
    mod = adaln(c, ada_w, ada_b)
    for layer in range(depth):
        i = layer // 2
        sh_m, sc_m, gt_m, sh_f, sc_f, gt_f = (mod[layer, :, k * d:(k + 1) * d] for k in range(6))
        g = [norm_g[layer, k].reshape(1, d) for k in range(4)]
        if layer % 2 == 0:
            z = norm_mod_matmul(xs, g[0], sh_m, sc_m, e_w_in[i].astype(BF16))
            a = conformer_conv(z, e_conv_w[i], e_conv_b[i], e_cln_g[i], e_cln_b[i])
            o = dilated_attention(z, t5_table, 2 * conv_dim, (z.shape[1] - 2 * conv_dim) // 3)
            xs = out_proj_residual(a, o, e_w_out[i].astype(BF16), xs, g[1], gt_m)
            xs = ffn_sublayer(xs, g[2], sh_f, sc_f, g[3], gt_f, e_ffn_w1[i].astype(BF16),
                              e_ffn_w3[i].astype(BF16), e_ffn_w2[i].astype(BF16))
        else:
            q, k, v, u = odd_in_proj(xs, g[0], sh_m, sc_m, o_w_in[i], o_q_norm_g[i], o_w_uq[i],
                                     o_kv_norm_g[i], o_w_ukv[i])
            o_mla = mla_attention(q, k, v)
            y = s5_bidirectional(u, s5_a_re[i], s5_a_im[i], s5_log_dt[i], s5_b_re[i], s5_b_im[i],
                                 s5_c_re[i], s5_c_im[i])
            y = s5_gate(y, u, s5_d[i], s5_w_glu[i])
            xs = out_proj_residual(o_mla, y, o_w_out[i].astype(BF16), xs, g[1], gt_m)
            xs = moe_sublayer(xs, g[2], sh_f, sc_f, g[3], gt_f, router_w[i], moe_w1[i], moe_w3[i], moe_w2[i])
    return xs.reshape(bsz, seq, d)
```

```python
import functools
import math

import jax
import jax.numpy as jnp
from jax import lax
from jax.experimental import pallas as pl
from jax.experimental.pallas import tpu as pltpu

F32 = jnp.float32
BF16 = jnp.bfloat16

RMS_EPS = 1e-6
LN_EPS = 1e-5
NEG_INF = -1e30

CONV_WIDTH = 31
CONV_HALO = 16
DIL_HEAD_DIM = 64
DIL_CONFIGS = ((128, 1), (512, 4), (2048, 16))
DIL_HALF = 64
DIL_TILE = 1024
DIL_QB = 256
DIL_MERGE = 2
N_BUCKETS = 32
T5_MAX_DIST = DIL_CONFIGS[-1][0] // 2
MLA_HEADS = 12
NOPE_DIM = 128
ROPE_DIM = 64
V_DIM = 128
ROPE_BASE = 10000.0
MLA_QK_PAD = 256
S5_GROUP = 16
S5_STATE = 64
S5_CHUNK = 16
N_EXPERTS = 8
EXPERT_TILE = 896
LANE = 128
SUBLANES = 8
VMEM_LIMIT_MB = 56


def _cp(sem, vmem_mb=VMEM_LIMIT_MB):
    return pltpu.CompilerParams(dimension_semantics=sem, vmem_limit_bytes=vmem_mb * 1024 * 1024)


def _rms(x):
    return x * lax.rsqrt(jnp.mean(x * x, axis=-1, keepdims=True) + RMS_EPS)


def _modnorm(x, g, sh, sc):
    return (_rms(x) * g) * (1.0 + sc) + sh


def _sigmoid(x):
    return 1.0 / (1.0 + jnp.exp(-x))


def _silu(x):
    return x * _sigmoid(x)


def _adaln_kernel(c_ref, w_ref, b_ref, o_ref, cb_ref):
    @pl.when((pl.program_id(0) == 0) & (pl.program_id(1) == 0))
    def _():
        cc = c_ref[...]
        cb_ref[...] = jnp.broadcast_to(_silu(cc), cb_ref.shape)

    cb = cb_ref[...]
    for j in range(o_ref.shape[-1] // LANE):
        sl = slice(j * LANE, (j + 1) * LANE)
        o_ref[:, sl] = jnp.sum(w_ref[:, sl] * cb, axis=0, keepdims=True) + b_ref[:, sl]


def adaln(c, ada_w, ada_b, tn=1024):
    nl, d, n = ada_w.shape
    out = pl.pallas_call(
        _adaln_kernel,
        out_shape=jax.ShapeDtypeStruct((nl, 1, n), F32),
        grid=(nl, n // tn),
        in_specs=[pl.BlockSpec((d, 1), lambda l, j: (0, 0)),
                  pl.BlockSpec((None, d, tn), lambda l, j: (l, 0, j)),
                  pl.BlockSpec((None, 1, tn), lambda l, j: (l, 0, j))],
        out_specs=pl.BlockSpec((None, 1, tn), lambda l, j: (l, 0, j)),
        scratch_shapes=[pltpu.VMEM((d, LANE), F32)],
        compiler_params=_cp(("arbitrary", "arbitrary")),
        name="adaln",
    )(c.reshape(d, 1), ada_w, ada_b.reshape(nl, 1, n))
    return out


def _nmm_kernel(x_ref, g_ref, sh_ref, sc_ref, w_ref, o_ref, h_ref):
    @pl.when(pl.program_id(1) == 0)
    def _():
        h_ref[...] = _modnorm(x_ref[...], g_ref[...], sh_ref[...], sc_ref[...]).astype(h_ref.dtype)

    o_ref[...] = jnp.dot(h_ref[...], w_ref[...], preferred_element_type=F32).astype(o_ref.dtype)


def norm_mod_matmul(x, g, sh, sc, w, tm=1024, tn=512):
    s, d = x.shape
    n = w.shape[1]
    tm, tn = min(tm, s), min(tn, n)
    vec = pl.BlockSpec((1, d), lambda i, j: (0, 0))
    return pl.pallas_call(
        _nmm_kernel,
        out_shape=jax.ShapeDtypeStruct((s, n), BF16),
        grid=(s // tm, n // tn),
        in_specs=[pl.BlockSpec((tm, d), lambda i, j: (i, 0)), vec, vec, vec,
                  pl.BlockSpec((d, tn), lambda i, j: (0, j))],
        out_specs=pl.BlockSpec((tm, tn), lambda i, j: (i, j)),
        scratch_shapes=[pltpu.VMEM((tm, d), BF16)],
        compiler_params=_cp(("parallel", "arbitrary")),
        name="norm_mod_matmul",
    )(x, g, sh, sc, w)


def _conv_kernel(av_ref, ag_ref, avp_ref, agp_ref, avn_ref, agn_ref, w_ref, b_ref, lg_ref, lb_ref,
                 o_ref, buf_ref, sh_ref, *, rows):
    i = pl.program_id(0)
    n = pl.num_programs(0)
    ts = av_ref.shape[0]

    def glu(v_ref, g_ref):
        return v_ref[...].astype(F32) * _sigmoid(g_ref[...].astype(F32))

    buf_ref[0:CONV_HALO, :] = jnp.where(i > 0, glu(avp_ref, agp_ref), 0.0)
    buf_ref[CONV_HALO:CONV_HALO + ts, :] = glu(av_ref, ag_ref)
    buf_ref[CONV_HALO + ts:2 * CONV_HALO + ts, :] = jnp.where(i < n - 1, glu(avn_ref, agn_ref), 0.0)
    buf_ref[2 * CONV_HALO + ts:, :] = jnp.zeros((SUBLANES, buf_ref.shape[1]), F32)

    span = ts + 2 * CONV_HALO
    for o in range(SUBLANES):
        sh_ref[o] = buf_ref[o:o + span, :]

    off = CONV_HALO - CONV_WIDTH // 2
    for r in range(ts // rows):
        acc = jnp.broadcast_to(b_ref[...], (rows, b_ref.shape[-1]))
        for k in range(CONV_WIDTH):
            lo = r * rows + k + off
            base = lo - lo % SUBLANES
            acc = acc + w_ref[k:k + 1, :] * sh_ref[lo % SUBLANES, base:base + rows, :]
        mu = jnp.mean(acc, axis=-1, keepdims=True)
        dlt = acc - mu
        var = jnp.mean(dlt * dlt, axis=-1, keepdims=True)
        y = dlt * lax.rsqrt(var + LN_EPS) * lg_ref[...] + lb_ref[...]
        o_ref[r * rows:(r + 1) * rows, :] = _silu(y).astype(o_ref.dtype)


def conformer_conv(z, conv_w, conv_b, cln_g, cln_b, ts=256, rows=32):
    s = z.shape[0]
    c = conv_w.shape[1]
    ts = min(ts, s)
    hb = ts // CONV_HALO
    nh = s // CONV_HALO
    main = lambda col: pl.BlockSpec((ts, c), lambda i: (i, col))
    prev = lambda col: pl.BlockSpec((CONV_HALO, c), lambda i: (jnp.maximum(i * hb - 1, 0), col))
    nxt = lambda col: pl.BlockSpec((CONV_HALO, c), lambda i: (jnp.minimum((i + 1) * hb, nh - 1), col))
    vec = lambda r: pl.BlockSpec((r, c), lambda i: (0, 0))
    return pl.pallas_call(
        functools.partial(_conv_kernel, rows=min(rows, ts)),
        out_shape=jax.ShapeDtypeStruct((s, c), BF16),
        grid=(s // ts,),
        in_specs=[main(0), main(1), prev(0), prev(1), nxt(0), nxt(1),
                  vec(CONV_WIDTH), vec(1), vec(1), vec(1)],
        out_specs=pl.BlockSpec((ts, c), lambda i: (i, 0)),
        scratch_shapes=[pltpu.VMEM((ts + 2 * CONV_HALO + SUBLANES, c), F32),
                        pltpu.VMEM((SUBLANES, ts + 2 * CONV_HALO, c), F32)],
        compiler_params=_cp(("parallel",)),
        name="conformer_conv",
    )(z, z, z, z, z, z, conv_w, conv_b.reshape(1, c), cln_g.reshape(1, c), cln_b.reshape(1, c))


def _t5_bucket(rel):
    half = N_BUCKETS // 2
    exact = half // 2
    n = jnp.abs(rel)
    large = exact + (jnp.log(jnp.maximum(n, 1).astype(F32) / exact)
                     / math.log(T5_MAX_DIST / exact) * (half - exact)).astype(jnp.int32)
    large = jnp.minimum(large, half - 1)
    return jnp.where(rel > 0, half, 0) + jnp.where(n < exact, n, large)


def _toeplitz(w, rows, cols):
    n = w.shape[-1]
    wp = jnp.concatenate([w, jnp.zeros(w.shape[:-1] + (1,), w.dtype)], axis=-1)
    flat = jnp.tile(wp, (1,) * (w.ndim - 1) + (rows,))[..., :rows * n]
    return flat.reshape(w.shape[:-1] + (rows, n))[..., rows - 1:rows - 1 + cols]


def _band_diag(t5_table, dil, qb):
    kl = qb + 2 * DIL_HALF
    n = -(-(qb + kl - 1) // LANE) * LANE
    dist = jnp.arange(n) - (qb - 1) - DIL_HALF
    return jnp.where((jnp.abs(dist) <= DIL_HALF)[None], t5_table[_t5_bucket(dist * dil)].T.astype(F32), NEG_INF)


def _band_bias_rows(diag_row, qb):
    n = diag_row.shape[1]
    rolled = pltpu.roll(jnp.broadcast_to(diag_row, (qb, n)), n - (qb - 1), 1, stride=1, stride_axis=0)
    return rolled[:, :qb + 2 * DIL_HALF]


def _band_group(q, kw, vw, bias_ref, pen):
    r = q.shape[0]
    is_a = lax.broadcasted_iota(jnp.int32, (1, LANE), 1) < DIL_HEAD_DIM
    q2 = jnp.concatenate([jnp.where(is_a, q, 0.0), jnp.where(is_a, 0.0, q)], axis=0).astype(BF16)
    s = lax.dot_general(q2, kw.astype(BF16), (((1,), (1,)), ((), ())), preferred_element_type=F32)
    s = s + bias_ref[...]
    if pen is not None:
        s = s + pen
    m = jnp.max(s, axis=-1, keepdims=True)
    p = jnp.exp(s - m)
    l = jnp.sum(p, axis=-1, keepdims=True)
    pv = jnp.dot(p.astype(BF16), vw.astype(BF16), preferred_element_type=F32)
    pick = lambda x: jnp.where(is_a, x[:r], x[r:])
    return pick(pv), pick(m), pick(l)


def _dil_kernel(q_ref, kp_ref, kc_ref, kn_ref, vp_ref, vc_ref, vn_ref, d1_ref, d4_ref, d16_ref, o_ref,
                qf, kf, vf, a1, m1, l1, a4, m4, l4, a16, m16, l16, b1_ref, b4_ref, b16_ref):
    i = pl.program_id(1)
    first = i == 0
    last = i == pl.num_programs(1) - 1
    t = DIL_TILE
    half = DIL_HALF

    @pl.when(first)
    def _():
        q16 = t // 16
        b16_ref[...] = jnp.full(b16_ref.shape, NEG_INF, F32)
        for h in range(2):
            b1_ref[h * DIL_QB:(h + 1) * DIL_QB, :] = _band_bias_rows(d1_ref[h:h + 1, :], DIL_QB)
            b4_ref[h * DIL_QB:(h + 1) * DIL_QB, :] = _band_bias_rows(d4_ref[h:h + 1, :], DIL_QB)
            blk = _band_bias_rows(d16_ref[h:h + 1, :], q16)
            for u in range(DIL_MERGE):
                r0 = (h * DIL_MERGE + u) * q16
                b16_ref[r0:r0 + q16, u * blk.shape[1]:(u + 1) * blk.shape[1]] = blk

    qf[...] = q_ref[...].astype(F32) * (DIL_HEAD_DIM ** -0.5)
    for dst, (p_ref, c_ref, n_ref) in ((kf, (kp_ref, kc_ref, kn_ref)), (vf, (vp_ref, vc_ref, vn_ref))):
        dst[0:t, :] = p_ref[...].astype(F32)
        dst[t:2 * t, :] = c_ref[...].astype(F32)
        dst[2 * t:3 * t, :] = n_ref[...].astype(F32)

    def store(refs, rows, vals):
        for ref, val in zip(refs, vals):
            ref[rows, :] = val

    qb = DIL_QB
    kl = qb + 2 * half
    col = lax.broadcasted_iota(jnp.int32, (1, kl), 1)
    lo = jnp.where((col < half) & first, NEG_INF, 0.0)
    hi = jnp.where((col >= qb + half) & last, NEG_INF, 0.0)

    nb = t // qb
    for b in range(nb):
        pen = lo + hi if nb == 1 else (lo if b == 0 else (hi if b == nb - 1 else None))
        k0 = t + b * qb - half
        rows = slice(b * qb, (b + 1) * qb)
        store((a1, m1, l1), rows, _band_group(qf[rows, :], kf[k0:k0 + kl, :], vf[k0:k0 + kl, :], b1_ref, pen))

    dil = 4
    for r in range(dil):
        qrows = pl.ds(r, qb, stride=dil)
        krows = pl.ds(t - half * dil + r, kl, stride=dil)
        store((a4, m4, l4), qrows, _band_group(qf[qrows, :], kf[krows, :], vf[krows, :], b4_ref, lo + hi))

    dil = 16
    q16 = t // dil
    k16 = q16 + 2 * half
    col = lax.rem(lax.broadcasted_iota(jnp.int32, (1, DIL_MERGE * k16), 1), k16)
    pen16 = (jnp.where((col < half) & first, NEG_INF, 0.0)
             + jnp.where((col >= q16 + half) & last, NEG_INF, 0.0))
    for g in range(dil // DIL_MERGE):
        res = range(g * DIL_MERGE, (g + 1) * DIL_MERGE)
        qrows = [pl.ds(r, q16, stride=dil) for r in res]
        krows = [pl.ds(t - half * dil + r, k16, stride=dil) for r in res]
        out = _band_group(jnp.concatenate([qf[rr, :] for rr in qrows], axis=0),
                          jnp.concatenate([kf[rr, :] for rr in krows], axis=0),
                          jnp.concatenate([vf[rr, :] for rr in krows], axis=0), b16_ref, pen16)
        for u, rr in enumerate(qrows):
            store((a16, m16, l16), rr, [x[u * q16:(u + 1) * q16] for x in out])

    mm = jnp.maximum(jnp.maximum(m1[...], m4[...]), m16[...])
    w1, w4, w16 = jnp.exp(m1[...] - mm), jnp.exp(m4[...] - mm), jnp.exp(m16[...] - mm)
    num = w1 * a1[...] + w4 * a4[...] + w16 * a16[...]
    den = w1 * l1[...] + w4 * l4[...] + w16 * l16[...]
    o_ref[...] = (num / den).astype(o_ref.dtype)


def dilated_attention(z, t5_table, col0, width):
    s, zw = z.shape
    t = DIL_TILE
    assert s % t == 0 and col0 % LANE == 0 and width % LANE == 0 and t // 4 == DIL_QB
    nt = s // t
    nh = t5_table.shape[1]
    cq, ck, cv = ((col0 + k * width) // LANE for k in range(3))
    q16 = t // 16
    diags = [_band_diag(t5_table, dil, qb).reshape(nh // 2, 2, -1)
             for dil, qb in ((1, DIL_QB), (4, DIL_QB), (16, q16))]
    kl, k16 = DIL_QB + 2 * DIL_HALF, q16 + 2 * DIL_HALF

    def blk(col, shift):
        return pl.BlockSpec((t, LANE), lambda hg, i: (jnp.clip(i + shift, 0, nt - 1), col + hg))

    diag_spec = lambda d: pl.BlockSpec((None,) + d.shape[1:], lambda hg, i: (hg, 0, 0))
    return pl.pallas_call(
        _dil_kernel,
        out_shape=jax.ShapeDtypeStruct((s, width), BF16),
        grid=(width // LANE, nt),
        in_specs=[blk(cq, 0), blk(ck, -1), blk(ck, 0), blk(ck, 1), blk(cv, -1), blk(cv, 0), blk(cv, 1)]
                 + [diag_spec(d) for d in diags],
        out_specs=pl.BlockSpec((t, LANE), lambda hg, i: (i, hg)),
        scratch_shapes=[pltpu.VMEM((t, LANE), F32), pltpu.VMEM((3 * t, LANE), F32),
                        pltpu.VMEM((3 * t, LANE), F32)] + [pltpu.VMEM((t, LANE), F32)] * 9
                       + [pltpu.VMEM((2 * DIL_QB, kl), F32), pltpu.VMEM((2 * DIL_QB, kl), F32),
                          pltpu.VMEM((2 * DIL_MERGE * q16, DIL_MERGE * k16), F32)],
        compiler_params=_cp(("parallel", "arbitrary")),
        name="dilated_attention",
    )(*([z] * 7 + diags))


def _oproj_kernel(a1_ref, a2_ref, w_ref, x_ref, g_ref, gt_ref, o_ref):
    k1 = a1_ref.shape[1]
    y = jnp.dot(a1_ref[...], w_ref[0:k1, :], preferred_element_type=F32)
    y = y + jnp.dot(a2_ref[...], w_ref[k1:, :], preferred_element_type=F32)
    o_ref[...] = x_ref[...] + gt_ref[...] * (_rms(y) * g_ref[...])


def out_proj_residual(a1, a2, w, x, g, gt, tm=512):
    s, d = x.shape
    k1, k2 = a1.shape[1], a2.shape[1]
    tm = min(tm, s)
    vec = pl.BlockSpec((1, d), lambda i: (0, 0))
    return pl.pallas_call(
        _oproj_kernel,
        out_shape=jax.ShapeDtypeStruct((s, d), F32),
        grid=(s // tm,),
        in_specs=[pl.BlockSpec((tm, k1), lambda i: (i, 0)), pl.BlockSpec((tm, k2), lambda i: (i, 0)),
                  pl.BlockSpec((k1 + k2, d), lambda i: (0, 0)),
                  pl.BlockSpec((tm, d), lambda i: (i, 0)), vec, vec],
        out_specs=pl.BlockSpec((tm, d), lambda i: (i, 0)),
        compiler_params=_cp(("parallel",)),
        name="out_proj_residual",
    )(a1, a2, w, x, g, gt)


def _ffn_kernel(x_ref, g_ref, sh_ref, sc_ref, g2_ref, gt_ref, w1_ref, w3_ref, w2_ref, o_ref, h_ref):
    f = pl.program_id(1)

    @pl.when(f == 0)
    def _():
        h_ref[...] = _modnorm(x_ref[...], g_ref[...], sh_ref[...], sc_ref[...]).astype(h_ref.dtype)
        o_ref[...] = jnp.zeros_like(o_ref)

    h = h_ref[...]
    u = jnp.dot(h, w1_ref[...], preferred_element_type=F32)
    v = jnp.dot(h, w3_ref[...], preferred_element_type=F32)
    o_ref[...] += jnp.dot((_silu(u) * v).astype(BF16), w2_ref[...], preferred_element_type=F32)

    @pl.when(f == pl.num_programs(1) - 1)
    def _():
        o_ref[...] = x_ref[...] + gt_ref[...] * (_rms(o_ref[...]) * g2_ref[...])


def ffn_sublayer(x, g, sh, sc, g2, gt, w1, w3, w2, tm=1024, tf=256):
    s, d = x.shape
    f = w1.shape[1]
    tm, tf = min(tm, s), min(tf, f)
    vec = pl.BlockSpec((1, d), lambda i, j: (0, 0))
    return pl.pallas_call(
        _ffn_kernel,
        out_shape=jax.ShapeDtypeStruct((s, d), F32),
        grid=(s // tm, f // tf),
        in_specs=[pl.BlockSpec((tm, d), lambda i, j: (i, 0)), vec, vec, vec, vec, vec,
                  pl.BlockSpec((d, tf), lambda i, j: (0, j)), pl.BlockSpec((d, tf), lambda i, j: (0, j)),
                  pl.BlockSpec((tf, d), lambda i, j: (j, 0))],
        out_specs=pl.BlockSpec((tm, d), lambda i, j: (i, 0)),
        scratch_shapes=[pltpu.VMEM((tm, d), BF16)],
        compiler_params=_cp(("parallel", "arbitrary")),
        name="ffn_sublayer",
    )(x, g, sh, sc, g2, gt, w1, w3, w2)


def _rope_slab(t, c_ref, s1_ref, s2_ref):
    return (t * c_ref[...] + pltpu.roll(t, LANE - ROPE_DIM // 2, 1) * s1_ref[...]
            + pltpu.roll(t, ROPE_DIM // 2, 1) * s2_ref[...])


def _oddin_kernel(x_ref, g_ref, sh_ref, sc_ref, win_ref, qg_ref, wuq_ref, kvg_ref, wuk_ref, wuv_ref,
                  c_ref, s1_ref, s2_ref, q_ref, k_ref, v_ref, u_ref, *, q_lora, kv_lora, scale):
    h = _modnorm(x_ref[...], g_ref[...], sh_ref[...], sc_ref[...]).astype(BF16)
    z = jnp.dot(h, win_ref[...], preferred_element_type=F32)
    o_kv, o_pe, o_u = q_lora, q_lora + kv_lora, q_lora + kv_lora + LANE
    u_ref[...] = z[:, o_u:]

    qn = (_rms(z[:, :q_lora]) * qg_ref[...]).astype(BF16)
    q = jnp.dot(qn, wuq_ref[...], preferred_element_type=F32)
    kvn = (_rms(z[:, o_kv:o_pe]) * kvg_ref[...]).astype(BF16)
    kn = jnp.dot(kvn, wuk_ref[...], preferred_element_type=F32)
    v_ref[...] = jnp.dot(kvn, wuv_ref[...], preferred_element_type=F32).astype(v_ref.dtype)
    kpe = _rope_slab(z[:, o_pe:o_u], c_ref, s1_ref, s2_ref).astype(k_ref.dtype)

    for hh in range(MLA_HEADS):
        b = hh * MLA_QK_PAD
        q_ref[:, b:b + NOPE_DIM] = (q[:, b:b + NOPE_DIM] * scale).astype(q_ref.dtype)
        qpe = _rope_slab(q[:, b + NOPE_DIM:b + MLA_QK_PAD], c_ref, s1_ref, s2_ref)
        q_ref[:, b + NOPE_DIM:b + MLA_QK_PAD] = (qpe * scale).astype(q_ref.dtype)
        k_ref[:, b:b + NOPE_DIM] = kn[:, hh * NOPE_DIM:(hh + 1) * NOPE_DIM].astype(k_ref.dtype)
        k_ref[:, b + NOPE_DIM:b + MLA_QK_PAD] = kpe


def _rope_tables(seq):
    pos = jnp.arange(seq, dtype=F32)
    inv_freq = ROPE_BASE ** (-jnp.arange(0, ROPE_DIM, 2, dtype=F32) / ROPE_DIM)
    ang = pos[:, None] * inv_freq[None, :]
    cos, sin = jnp.cos(ang), jnp.sin(ang)
    zero = jnp.zeros_like(cos)
    c = jnp.concatenate([cos, cos, zero, zero], axis=-1)
    s1 = jnp.concatenate([-sin, zero, zero, zero], axis=-1)
    s2 = jnp.concatenate([zero, sin, zero, zero], axis=-1)
    return c, s1, s2


def odd_in_proj(x, g, sh, sc, w_in, q_norm_g, w_uq, kv_norm_g, w_ukv, tm=256):
    s, d = x.shape
    q_lora, kv_lora = q_norm_g.shape[0], kv_norm_g.shape[0]
    s5_dim = w_in.shape[1] - q_lora - kv_lora - ROPE_DIM
    tm = min(tm, s)
    w_in_p = jnp.concatenate(
        [w_in[:, :q_lora + kv_lora + ROPE_DIM], jnp.zeros((d, LANE - ROPE_DIM), w_in.dtype),
         w_in[:, q_lora + kv_lora + ROPE_DIM:]], axis=1).astype(BF16)
    wq = w_uq.reshape(q_lora, MLA_HEADS, NOPE_DIM + ROPE_DIM)
    wq = jnp.pad(wq, ((0, 0), (0, 0), (0, MLA_QK_PAD - NOPE_DIM - ROPE_DIM)))
    wq = wq.reshape(q_lora, MLA_HEADS * MLA_QK_PAD).astype(BF16)
    wkv = w_ukv.reshape(kv_lora, MLA_HEADS, NOPE_DIM + V_DIM)
    wuk = wkv[:, :, :NOPE_DIM].reshape(kv_lora, MLA_HEADS * NOPE_DIM).astype(BF16)
    wuv = wkv[:, :, NOPE_DIM:].reshape(kv_lora, MLA_HEADS * V_DIM).astype(BF16)
    c, s1, s2 = _rope_tables(s)
    scale = (NOPE_DIM + ROPE_DIM) ** -0.5 * math.log2(math.e)

    full = lambda a: pl.BlockSpec(a.shape, lambda i: (0, 0))
    row = lambda w: pl.BlockSpec((tm, w), lambda i: (i, 0))
    vec = pl.BlockSpec((1, d), lambda i: (0, 0))
    qg = q_norm_g.reshape(1, q_lora)
    kvg = kv_norm_g.reshape(1, kv_lora)
    return pl.pallas_call(
        functools.partial(_oddin_kernel, q_lora=q_lora, kv_lora=kv_lora, scale=scale),
        out_shape=[jax.ShapeDtypeStruct((s, MLA_HEADS * MLA_QK_PAD), BF16),
                   jax.ShapeDtypeStruct((s, MLA_HEADS * MLA_QK_PAD), BF16),
                   jax.ShapeDtypeStruct((s, MLA_HEADS * V_DIM), BF16),
                   jax.ShapeDtypeStruct((s, s5_dim), F32)],
        grid=(s // tm,),
        in_specs=[row(d), vec, vec, vec, full(w_in_p), full(qg), full(wq), full(kvg), full(wuk), full(wuv),
                  row(LANE), row(LANE), row(LANE)],
        out_specs=[row(MLA_HEADS * MLA_QK_PAD), row(MLA_HEADS * MLA_QK_PAD), row(MLA_HEADS * V_DIM),
                   row(s5_dim)],
        compiler_params=_cp(("parallel",)),
        name="odd_in_proj",
    )(x, g, sh, sc, w_in_p, qg, wq, kvg, wuk, wuv, c, s1, s2)


def _flash_kernel(q_ref, k_ref, v_ref, o_ref, m_ref, acc_ref, s0, s1, *, rows, nk):
    t = pl.program_id(0)
    j1 = lax.rem(jnp.maximum(t - 1, 0), nk)
    tq = q_ref.shape[0]
    vd = v_ref.shape[1]

    @pl.when(t == 0)
    def _():
        m_ref[...] = jnp.full_like(m_ref, -jnp.inf)
        acc_ref[...] = jnp.zeros_like(acc_ref)
        s1[...] = jnp.full_like(s1, -jnp.inf)

    def stages(s_rd, s_wr):
        v = v_ref[...]
        v1 = jnp.concatenate([v, jnp.ones_like(v)], axis=1)
        k = k_ref[...]
        live = t >= 1
        fresh = j1 == 0
        for c in range(tq // rows):
            rs = slice(c * rows, (c + 1) * rows)
            s = s_rd[rs, :]
            m_prev = jnp.where(fresh, -jnp.inf, m_ref[rs, :])
            m_cand = jnp.maximum(m_prev, jnp.max(s, axis=-1, keepdims=True))
            m_new = jnp.where(live, m_cand, m_prev)
            m_sub = jnp.where(live, m_cand, 0.0)
            p = jnp.exp2((s - m_sub[:, :1]).astype(BF16))
            alpha = jnp.where(live, jnp.exp2(m_prev - m_new), 1.0)
            m_ref[rs, :] = m_new
            pv = jnp.dot(p, v1, preferred_element_type=F32)
            acc_ref[rs, :vd] = alpha * acc_ref[rs, :vd] + pv[:, :vd]
            acc_ref[rs, vd:] = alpha * acc_ref[rs, vd:] + pv[:, vd:]
            s_wr[rs, :] = lax.dot_general(q_ref[rs, :], k, (((1,), (1,)), ((), ())),
                                          preferred_element_type=F32)

    parity = lax.rem(t, 2)

    @pl.when(parity == 0)
    def _():
        stages(s1, s0)

    @pl.when(parity == 1)
    def _():
        stages(s0, s1)

    @pl.when((t >= 1) & (j1 == nk - 1))
    def _():
        o_ref[...] = (acc_ref[:, :vd] / acc_ref[:, vd:]).astype(o_ref.dtype)


def mla_attention(q, k, v, tq=2048, tk=2048, rows=512):
    s = q.shape[0]
    tq, tk = min(tq, s), min(tk, s)
    rows = min(rows, tq)
    ni, nk = s // tq, s // tk
    n = MLA_HEADS * ni * nk

    def item(t, lag):
        w = jnp.clip(t - lag, 0, n - 1)
        return w // (ni * nk), lax.rem(w, ni * nk) // nk, lax.rem(w, nk)

    def q_map(t):
        h, i, _ = item(t, 0)
        return i, h

    def k_map(t):
        h, _, j = item(t, 0)
        return j, h

    def v_map(t):
        h, _, j = item(t, 1)
        return j, h

    def o_map(t):
        h, i, _ = item(t, 1)
        return i, h

    return pl.pallas_call(
        functools.partial(_flash_kernel, rows=rows, nk=nk),
        out_shape=jax.ShapeDtypeStruct((s, MLA_HEADS * V_DIM), BF16),
        grid=(n + 1,),
        in_specs=[pl.BlockSpec((tq, MLA_QK_PAD), q_map),
                  pl.BlockSpec((tk, MLA_QK_PAD), k_map),
                  pl.BlockSpec((tk, V_DIM), v_map)],
        out_specs=pl.BlockSpec((tq, V_DIM), o_map),
        scratch_shapes=[pltpu.VMEM((tq, V_DIM), F32), pltpu.VMEM((tq, 2 * V_DIM), F32),
                        pltpu.VMEM((tq, tk), F32), pltpu.VMEM((tq, tk), F32)],
        compiler_params=_cp(("arbitrary",)),
        name="mla_flash_attention",
    )(q, k, v)


def _s5_matrices(a_re, a_im, log_dt, b_re, b_im, c_re, c_im, chunk):
    L = chunk
    A = lax.complex(a_re.astype(F32), a_im.astype(F32))
    dt = jnp.exp(log_dt.astype(F32))[..., None]
    adt = A * dt
    a_bar = jnp.exp(adt)
    b_bar = ((a_bar - 1.0) / A)[..., None] * lax.complex(b_re.astype(F32), b_im.astype(F32))
    c_c = lax.complex(c_re.astype(F32), c_im.astype(F32))
    kk = jnp.arange(L + 1, dtype=F32)
    apow = jnp.exp(adt[:, :, None, :] * kk[None, None, :, None].astype(jnp.complex64))
    g, p, gc = b_bar.shape[1], b_bar.shape[2], b_bar.shape[3]

    ker = jnp.real(jnp.einsum('dgcp,dgkp,dgpi->dgkci', c_c, apow[:, :, :L], b_bar))
    kb = ker[1][:, ::-1]
    diag = jnp.concatenate([kb[:, :L - 1], ker[0][:, :1] + kb[:, L - 1:], ker[0][:, 1:]], axis=1)
    t_mat = _toeplitz(jnp.transpose(diag, (0, 2, 3, 1)), L, L)
    t_mat = jnp.transpose(t_mat, (0, 3, 2, 4, 1)).reshape(g, L * gc, L * gc)

    pf = apow[0][:, ::-1][:, 1:][:, :, :, None] * b_bar[0][:, None]
    pb = apow[1][:, :L][:, :, :, None] * b_bar[1][:, None]
    def p_lay(m):
        return jnp.transpose(m, (0, 1, 3, 2)).reshape(g, L * gc, p)
    p_mat = jnp.concatenate([p_lay(jnp.real(pf)), p_lay(jnp.imag(pf)),
                             p_lay(jnp.real(pb)), p_lay(jnp.imag(pb))], axis=-1)

    wf = c_c[0][:, None] * apow[0][:, 1:][:, :, None, :]
    wb = c_c[1][:, None] * apow[1][:, ::-1][:, :L][:, :, None, :]
    def q_lay(m):
        return jnp.transpose(m, (0, 3, 1, 2)).reshape(g, p, L * gc)
    q_mat = jnp.concatenate([q_lay(jnp.real(wf)), q_lay(-jnp.imag(wf)),
                             q_lay(jnp.real(wb)), q_lay(-jnp.imag(wb))], axis=1)

    al = apow[:, :, L]
    dec = jnp.stack([jnp.real(al[0]), jnp.imag(al[0]), jnp.real(al[1]), jnp.imag(al[1])]).reshape(4, g * p)
    return t_mat, p_mat, q_mat, dec


def _s5_state_kernel(u_ref, p_ref, fre_ref, fim_ref, bre_ref, bim_ref):
    outs = [[], [], [], []]
    for gi in range(u_ref.shape[0]):
        xe = jnp.dot(u_ref[gi].astype(BF16), p_ref[gi], preferred_element_type=F32)
        for part in range(4):
            outs[part].append(xe[:, part * S5_STATE:(part + 1) * S5_STATE])
    for part, ref in enumerate((fre_ref, fim_ref, bre_ref, bim_ref)):
        ref[...] = jnp.concatenate(outs[part], axis=-1)


def _s5_scan_kernel(fre_ref, fim_ref, bre_ref, bim_ref, dec_ref, ofre_ref, ofim_ref, obre_ref, obim_ref):
    nc = fre_ref.shape[0]
    w = fre_ref.shape[1]
    far, fai, bar, bai = (dec_ref[i:i + 1, :] for i in range(4))

    def fwd(c, st):
        re, im = st
        ofre_ref[pl.ds(c, 1), :] = re
        ofim_ref[pl.ds(c, 1), :] = im
        return (far * re - fai * im + fre_ref[pl.ds(c, 1), :],
                far * im + fai * re + fim_ref[pl.ds(c, 1), :])

    def bwd(i, st):
        c = nc - 1 - i
        re, im = st
        obre_ref[pl.ds(c, 1), :] = re
        obim_ref[pl.ds(c, 1), :] = im
        return (bar * re - bai * im + bre_ref[pl.ds(c, 1), :],
                bar * im + bai * re + bim_ref[pl.ds(c, 1), :])

    zero = (jnp.zeros((1, w), F32), jnp.zeros((1, w), F32))
    lax.fori_loop(0, nc, fwd, zero)
    lax.fori_loop(0, nc, bwd, zero)


def _s5_out_kernel(u_ref, t_ref, q_ref, fre_ref, fim_ref, bre_ref, bim_ref, y_ref):
    for gi in range(u_ref.shape[0]):
        lanes = slice(gi * S5_STATE, (gi + 1) * S5_STATE)
        y = jnp.dot(u_ref[gi].astype(BF16), t_ref[gi], preferred_element_type=F32)
        xin = jnp.concatenate([r[:, lanes] for r in (fre_ref, fim_ref, bre_ref, bim_ref)], axis=-1)
        y_ref[gi] = y + jnp.dot(xin.astype(BF16), q_ref[gi], preferred_element_type=F32)


def s5_bidirectional(u, a_re, a_im, log_dt, b_re, b_im, c_re, c_im, chunk=S5_CHUNK):
    s, dim = u.shape
    gc, p = S5_GROUP, S5_STATE
    g = dim // gc
    chunk = min(chunk, s)
    nc = s // chunk
    gp = LANE // p
    t_mat, p_mat, q_mat, dec = _s5_matrices(a_re, a_im, log_dt, b_re, b_im, c_re, c_im, chunk)
    t_mat, p_mat, q_mat = t_mat.astype(BF16), p_mat.astype(BF16), q_mat.astype(BF16)
    u_t = jnp.transpose(u.astype(BF16).reshape(nc, chunk, g, gc), (2, 0, 1, 3)).reshape(g, nc, chunk * gc)

    grp = lambda a, b: pl.BlockSpec((gp, a, b), lambda i: (i, 0, 0))
    st = pl.BlockSpec((nc, LANE), lambda i: (0, i))
    st_shape = jax.ShapeDtypeStruct((nc, g * p), F32)
    xe = pl.pallas_call(
        _s5_state_kernel,
        out_shape=[st_shape] * 4,
        grid=(g // gp,),
        in_specs=[grp(nc, chunk * gc), grp(chunk * gc, 4 * p)],
        out_specs=[st] * 4,
        compiler_params=_cp(("parallel",)),
        name="s5_chunk_states",
    )(u_t, p_mat)

    lanes = min(4 * LANE, g * p)
    cols = lambda rows: pl.BlockSpec((rows, lanes), lambda i: (0, i))
    xin = pl.pallas_call(
        _s5_scan_kernel,
        out_shape=[st_shape] * 4,
        grid=(g * p // lanes,),
        in_specs=[cols(nc)] * 4 + [cols(4)],
        out_specs=[cols(nc)] * 4,
        compiler_params=_cp(("parallel",)),
        name="s5_boundary_scan",
    )(*xe, dec)

    y_t = pl.pallas_call(
        _s5_out_kernel,
        out_shape=jax.ShapeDtypeStruct((g, nc, chunk * gc), F32),
        grid=(g // gp,),
        in_specs=[grp(nc, chunk * gc), grp(chunk * gc, chunk * gc), grp(4 * p, chunk * gc)] + [st] * 4,
        out_specs=grp(nc, chunk * gc),
        compiler_params=_cp(("parallel",)),
        name="s5_outputs",
    )(u_t, t_mat, q_mat, *xin)
    return jnp.transpose(y_t.reshape(g, nc, chunk, gc), (1, 2, 0, 3)).reshape(s, dim)


def _s5_gate_kernel(y_ref, u_ref, d_ref, w_ref, o_ref):
    y = y_ref[...] + d_ref[...] * u_ref[...]
    y = 0.5 * y * (1.0 + jnp.tanh(math.sqrt(2.0 / math.pi) * (y + 0.044715 * (y * y * y))))
    z = jnp.dot(y.astype(BF16), w_ref[...], preferred_element_type=F32)
    o_ref[...] = (y * _sigmoid(z)).astype(o_ref.dtype)


def s5_gate(y, u, d_skip, w_glu, tm=1024):
    s, dim = y.shape
    tm = min(tm, s)
    row = pl.BlockSpec((tm, dim), lambda i: (i, 0))
    return pl.pallas_call(
        _s5_gate_kernel,
        out_shape=jax.ShapeDtypeStruct((s, dim), BF16),
        grid=(s // tm,),
        in_specs=[row, row, pl.BlockSpec((1, dim), lambda i: (0, 0)), pl.BlockSpec((dim, dim), lambda i: (0, 0))],
        out_specs=row,
        compiler_params=_cp(("parallel",)),
        name="s5_gate",
    )(y, u, d_skip.reshape(1, dim), w_glu.astype(BF16))


def _router_kernel(x_ref, g_ref, sh_ref, sc_ref, rw_ref, h_ref, info_ref):
    h = _modnorm(x_ref[...], g_ref[...], sh_ref[...], sc_ref[...])
    half = h.shape[1] // 2
    bits = lambda v: lax.bitcast_convert_type(v.astype(BF16).astype(F32), jnp.uint32)
    h_ref[...] = (bits(h[:, :half]) >> 16) | (bits(h[:, half:]) & jnp.uint32(0xFFFF0000))
    logits = jnp.dot(h, rw_ref[...], preferred_element_type=F32, precision=lax.Precision.HIGHEST)
    lane = lax.broadcasted_iota(jnp.int32, logits.shape, 1)
    logits = jnp.where(lane < N_EXPERTS, logits, -jnp.inf)
    m1 = jnp.max(logits, axis=-1, keepdims=True)
    i1 = jnp.min(jnp.where(logits == m1, lane, LANE), axis=-1, keepdims=True)
    rest = jnp.where(lane == i1, -jnp.inf, logits)
    m2 = jnp.max(rest, axis=-1, keepdims=True)
    i2 = jnp.min(jnp.where(rest == m2, lane, LANE), axis=-1, keepdims=True)
    e = jnp.exp(m2 - m1)
    g1 = 1.0 / (1.0 + e)
    g2 = e / (1.0 + e)
    info = jnp.where(lane == 0, i1.astype(F32),
                     jnp.where(lane == 1, i2.astype(F32),
                               jnp.where(lane == 2, g1, jnp.where(lane == 3, g2, 0.0))))
    info_ref[...] = info


def moe_router(x, g, sh, sc, router_w, tm=512):
    s, d = x.shape
    tm = min(tm, s)
    rw = jnp.pad(router_w, ((0, 0), (0, LANE - router_w.shape[1])))
    vec = pl.BlockSpec((1, d), lambda i: (0, 0))
    return pl.pallas_call(
        _router_kernel,
        out_shape=[jax.ShapeDtypeStruct((s, d // 2), jnp.uint32), jax.ShapeDtypeStruct((s, LANE), F32)],
        grid=(s // tm,),
        in_specs=[pl.BlockSpec((tm, d), lambda i: (i, 0)), vec, vec, vec,
                  pl.BlockSpec((d, LANE), lambda i: (0, 0))],
        out_specs=[pl.BlockSpec((tm, d // 2), lambda i: (i, 0)), pl.BlockSpec((tm, LANE), lambda i: (i, 0))],
        compiler_params=_cp(("parallel",)),
        name="moe_router",
    )(x, g, sh, sc, rw)


def _expert_kernel(te_ref, tv_ref, src_ref, hp_hbm, w1_ref, w3_ref, w2_ref, o_ref, acc_ref, gbuf, h_ref, sem):
    t = pl.program_id(0)
    f = pl.program_id(1)
    nt = pl.num_programs(0)
    nf = pl.num_programs(1)
    tm = acc_ref.shape[0]
    per_step = tm // nf
    valid = tv_ref[t] > 0
    slot = lax.rem(t, 2)

    def row_copy(tile, buf, r):
        return pltpu.make_async_copy(hp_hbm.at[pl.ds(src_ref[tile * tm + r], 1)],
                                     gbuf.at[buf, pl.ds(r, 1)], sem.at[buf])

    def wait_half(buf):
        pltpu.make_async_copy(hp_hbm.at[pl.ds(0, tm)], gbuf.at[buf], sem.at[buf]).wait()

    @pl.when((t == 0) & (f == 0))
    def _():
        def issue(r, carry):
            row_copy(0, 0, r).start()
            return carry
        lax.fori_loop(0, tm, issue, 0)

    @pl.when((f == 0) & ((t == 0) | (tv_ref[jnp.maximum(t - 1, 0)] > 0)))
    def _():
        wait_half(slot)

    @pl.when((f == 0) & valid)
    def _():
        w = gbuf[slot]
        half = w.shape[1]
        h_ref[:, :half] = lax.bitcast_convert_type(w << 16, F32).astype(BF16)
        h_ref[:, half:] = lax.bitcast_convert_type(w & jnp.uint32(0xFFFF0000), F32).astype(BF16)

    @pl.when(f == 0)
    def _():
        acc_ref[...] = jnp.zeros_like(acc_ref)

    def compute(rows):
        for u in range(per_step):
            row_copy(t + 1, 1 - slot, f * per_step + u).start()
        h = h_ref[:rows, :]
        u = jnp.dot(h, w1_ref[...].astype(BF16), preferred_element_type=F32)
        v = jnp.dot(h, w3_ref[...].astype(BF16), preferred_element_type=F32)
        acc_ref[:rows, :] += jnp.dot((_silu(u) * v).astype(BF16), w2_ref[...].astype(BF16),
                                     preferred_element_type=F32)

    @pl.when(tv_ref[t] > tm // 2)
    def _():
        compute(tm)

    @pl.when(valid & (tv_ref[t] <= tm // 2))
    def _():
        compute(tm // 2)

    @pl.when(f == nf - 1)
    def _():
        o_ref[...] = acc_ref[...].astype(o_ref.dtype)

    @pl.when((t == nt - 1) & (f == nf - 1) & valid)
    def _():
        wait_half(1 - slot)


def expert_ffn(h_packed, src, tile_expert, tile_valid, w1, w3, w2, tm, tf=512):
    dh = h_packed.shape[1]
    d = 2 * dh
    n = src.shape[0] - tm
    f = w1.shape[2]
    tf = min(tf, f)
    nf = f // tf
    assert tm % nf == 0 and n % tm == 0

    def fblk(j, t, tv):
        return jnp.where(tv[t] > 0, j, nf - 1)

    return pl.pallas_call(
        _expert_kernel,
        out_shape=jax.ShapeDtypeStruct((n, d), BF16),
        grid_spec=pltpu.PrefetchScalarGridSpec(
            num_scalar_prefetch=3,
            grid=(n // tm, nf),
            in_specs=[pl.BlockSpec(memory_space=pl.ANY),
                      pl.BlockSpec((None, d, tf), lambda t, j, te, tv, sr: (te[t], 0, fblk(j, t, tv))),
                      pl.BlockSpec((None, d, tf), lambda t, j, te, tv, sr: (te[t], 0, fblk(j, t, tv))),
                      pl.BlockSpec((None, tf, d), lambda t, j, te, tv, sr: (te[t], fblk(j, t, tv), 0))],
            out_specs=pl.BlockSpec((tm, d), lambda t, j, te, tv, sr: (t, 0)),
            scratch_shapes=[pltpu.VMEM((tm, d), F32), pltpu.VMEM((2, tm, dh), jnp.uint32),
                            pltpu.VMEM((tm, d), BF16), pltpu.SemaphoreType.DMA((2,))]),
        compiler_params=_cp(("arbitrary", "arbitrary")),
        name="expert_ffn",
    )(tile_expert, tile_valid, src, h_packed, w1, w3, w2)


def _dispatch(idx, tm):
    s = idx.shape[0]
    e_flat = idx.reshape(-1)
    onehot = (e_flat[:, None] == jnp.arange(N_EXPERTS)[None, :]).astype(jnp.int32)
    csum = jnp.cumsum(onehot, axis=0)
    rank = jnp.sum((csum - onehot) * onehot, axis=1)
    counts = csum[-1]
    padded = ((counts + tm - 1) // tm) * tm
    gend = jnp.cumsum(padded)
    slot = (gend - padded)[e_flat] + rank
    nt = -(-(2 * s) // tm) + N_EXPERTS
    src = jnp.zeros(((nt + 1) * tm,), jnp.int32).at[slot].set(jnp.arange(2 * s, dtype=jnp.int32) // 2)
    tstart = jnp.arange(nt, dtype=jnp.int32) * tm
    valid = tstart < gend[-1]
    te = jnp.minimum(jnp.sum((tstart[:, None] >= gend[None, :]).astype(jnp.int32), axis=1), N_EXPERTS - 1)
    nvalid = gend[-1] // tm
    fill = jnp.clip((gend - padded + counts)[te] - tstart, 0, tm)
    fill = jnp.where(valid, fill, 0)
    te = jnp.where(valid, te, te[jnp.maximum(nvalid - 1, 0)])
    return slot.reshape(s, 2), src, te.astype(jnp.int32), fill.astype(jnp.int32)


def _combine_kernel(ya_ref, yb_ref, info_ref, x_ref, g_ref, gt_ref, o_ref):
    info = info_ref[...]
    y = info[:, 2:3] * ya_ref[...].astype(F32) + info[:, 3:4] * yb_ref[...].astype(F32)
    o_ref[...] = x_ref[...] + gt_ref[...] * (_rms(y) * g_ref[...])


def moe_combine(ya, yb, info, x, g, gt, tm=512):
    s, d = x.shape
    tm = min(tm, s)
    row = lambda w: pl.BlockSpec((tm, w), lambda i: (i, 0))
    vec = pl.BlockSpec((1, d), lambda i: (0, 0))
    return pl.pallas_call(
        _combine_kernel,
        out_shape=jax.ShapeDtypeStruct((s, d), F32),
        grid=(s // tm,),
        in_specs=[row(d), row(d), row(LANE), row(d), vec, vec],
        out_specs=row(d),
        compiler_params=_cp(("parallel",)),
        name="moe_combine",
    )(ya, yb, info, x, g, gt)


def moe_sublayer(x, g, sh, sc, g2, gt, router_w, w1, w3, w2, tm_e=EXPERT_TILE):
    s = x.shape[0]
    tm_e = min(tm_e, s)
    h_packed, info = moe_router(x, g, sh, sc, router_w)
    idx = info[:, :2].astype(jnp.int32)
    slot, src, te, tv = _dispatch(idx, tm_e)
    y_sorted = expert_ffn(h_packed, src, te, tv, w1, w3, w2, tm_e)
    ya = jnp.take(y_sorted, slot[:, 0], axis=0)
    yb = jnp.take(y_sorted, slot[:, 1], axis=0)
    return moe_combine(ya, yb, info, x, g2, gt)


def kernel(x, c, t5_table, norm_g, ada_w, ada_b, e_w_in, e_conv_w, e_conv_b, e_cln_g, e_cln_b, e_w_out, e_ffn_w1, e_ffn_w3, e_ffn_w2, o_w_in, o_q_norm_g, o_w_uq, o_kv_norm_g, o_w_ukv, s5_a_re, s5_a_im, s5_log_dt, s5_b_re, s5_b_im, s5_c_re, s5_c_im, s5_d, s5_w_glu, o_w_out, router_w, moe_w1, moe_w3, moe_w2):
    bsz, seq, d = x.shape
    assert bsz == 1
    depth = norm_g.shape[0]
    conv_dim = e_conv_w.shape[2]
    xs = x.reshape(seq, d)
    mod = adaln(c, ada_w, ada_b)
    for layer in range(depth):
        i = layer // 2
        sh_m, sc_m, gt_m, sh_f, sc_f, gt_f = (mod[layer, :, k * d:(k + 1) * d] for k in range(6))
        g = [norm_g[layer, k].reshape(1, d) for k in range(4)]
        if layer % 2 == 0:
            z = norm_mod_matmul(xs, g[0], sh_m, sc_m, e_w_in[i].astype(BF16))
            a = conformer_conv(z, e_conv_w[i], e_conv_b[i], e_cln_g[i], e_cln_b[i])
            o = dilated_attention(z, t5_table, 2 * conv_dim, (z.shape[1] - 2 * conv_dim) // 3)
            xs = out_proj_residual(a, o, e_w_out[i].astype(BF16), xs, g[1], gt_m)
            xs = ffn_sublayer(xs, g[2], sh_f, sc_f, g[3], gt_f, e_ffn_w1[i].astype(BF16),
                              e_ffn_w3[i].astype(BF16), e_ffn_w2[i].astype(BF16))
        else:
            q, k, v, u = odd_in_proj(xs, g[0], sh_m, sc_m, o_w_in[i], o_q_norm_g[i], o_w_uq[i],
                                     o_kv_norm_g[i], o_w_ukv[i])
            o_mla = mla_attention(q, k, v)
            y = s5_bidirectional(u, s5_a_re[i], s5_a_im[i], s5_log_dt[i], s5_b_re[i], s5_b_im[i],
                                 s5_c_re[i], s5_c_im[i])
            y = s5_gate(y, u, s5_d[i], s5_w_glu[i])
            xs = out_proj_residual(o_mla, y, o_w_out[i].astype(BF16), xs, g[1], gt_m)
            xs = moe_sublayer(xs, g[2], sh_f, sc_f, g[3], gt_f, router_w[i], moe_w1[i], moe_w3[i], moe_w2[i])
    return xs.reshape(bsz, seq, d)
```

```python
import functools
import math

import jax
import jax.numpy as jnp
from jax import lax
from jax.experimental import pallas as pl
from jax.experimental.pallas import tpu as pltpu

F32 = jnp.float32
BF16 = jnp.bfloat16

RMS_EPS = 1e-6
LN_EPS = 1e-5
NEG_INF = -1e30
LOG2E = math.log2(math.e)

CONV_WIDTH = 31
CONV_HALO = 16
DIL_HEAD_DIM = 64
DIL_CONFIGS = ((128, 1), (512, 4), (2048, 16))
DIL_HALF = 64
DIL_TILE = 1024
DIL_QB = 256
DIL_MERGE = 2
N_BUCKETS = 32
T5_MAX_DIST = DIL_CONFIGS[-1][0] // 2
MLA_HEADS = 12
NOPE_DIM = 128
ROPE_DIM = 64
V_DIM = 128
ROPE_BASE = 10000.0
MLA_QK_PAD = 256
S5_GROUP = 16
S5_STATE = 64
S5_CHUNK = 16
N_EXPERTS = 8
EXPERT_TILE = 896
LANE = 128
SUBLANES = 8
VMEM_LIMIT_MB = 56


def _cp(sem, vmem_mb=VMEM_LIMIT_MB):
    return pltpu.CompilerParams(dimension_semantics=sem, vmem_limit_bytes=vmem_mb * 1024 * 1024)


def _rms(x):
    return x * lax.rsqrt(jnp.mean(x * x, axis=-1, keepdims=True) + RMS_EPS)


def _modnorm(x, g, sh, sc):
    return (_rms(x) * g) * (1.0 + sc) + sh


def _sigmoid(x):
    return 1.0 / (1.0 + jnp.exp(-x))


def _silu(x):
    return x * _sigmoid(x)


def _adaln_kernel(c_ref, w_ref, b_ref, o_ref, cb_ref):
    @pl.when((pl.program_id(0) == 0) & (pl.program_id(1) == 0))
    def _():
        cc = c_ref[...]
        cb_ref[...] = jnp.broadcast_to(_silu(cc), cb_ref.shape)

    cb = cb_ref[...]
    for j in range(o_ref.shape[-1] // LANE):
        sl = slice(j * LANE, (j + 1) * LANE)
        o_ref[:, sl] = jnp.sum(w_ref[:, sl] * cb, axis=0, keepdims=True) + b_ref[:, sl]


def adaln(c, ada_w, ada_b, tn=1024):
    nl, d, n = ada_w.shape
    out = pl.pallas_call(
        _adaln_kernel,
        out_shape=jax.ShapeDtypeStruct((nl, 1, n), F32),
        grid=(nl, n // tn),
        in_specs=[pl.BlockSpec((d, 1), lambda l, j: (0, 0)),
                  pl.BlockSpec((None, d, tn), lambda l, j: (l, 0, j)),
                  pl.BlockSpec((None, 1, tn), lambda l, j: (l, 0, j))],
        out_specs=pl.BlockSpec((None, 1, tn), lambda l, j: (l, 0, j)),
        scratch_shapes=[pltpu.VMEM((d, LANE), F32)],
        compiler_params=_cp(("arbitrary", "arbitrary")),
        name="adaln",
    )(c.reshape(d, 1), ada_w, ada_b.reshape(nl, 1, n))
    return out


def _nmm_kernel(x_ref, g_ref, sh_ref, sc_ref, w_ref, o_ref, h_ref):
    @pl.when(pl.program_id(1) == 0)
    def _():
        h_ref[...] = _modnorm(x_ref[...], g_ref[...], sh_ref[...], sc_ref[...]).astype(h_ref.dtype)

    o_ref[...] = jnp.dot(h_ref[...], w_ref[...].astype(BF16), preferred_element_type=F32).astype(o_ref.dtype)


def norm_mod_matmul(x, g, sh, sc, w, tm=1024, tn=512):
    s, d = x.shape
    n = w.shape[1]
    tm, tn = min(tm, s), min(tn, n)
    vec = pl.BlockSpec((1, d), lambda i, j: (0, 0))
    return pl.pallas_call(
        _nmm_kernel,
        out_shape=jax.ShapeDtypeStruct((s, n), BF16),
        grid=(s // tm, n // tn),
        in_specs=[pl.BlockSpec((tm, d), lambda i, j: (i, 0)), vec, vec, vec,
                  pl.BlockSpec((d, tn), lambda i, j: (0, j))],
        out_specs=pl.BlockSpec((tm, tn), lambda i, j: (i, j)),
        scratch_shapes=[pltpu.VMEM((tm, d), BF16)],
        compiler_params=_cp(("parallel", "arbitrary")),
        name="norm_mod_matmul",
    )(x, g, sh, sc, w)


def _conv_kernel(av_ref, ag_ref, avp_ref, agp_ref, avn_ref, agn_ref, w_ref, b_ref, lg_ref, lb_ref,
                 o_ref, buf_ref, sh_ref, *, rows):
    i = pl.program_id(0)
    n = pl.num_programs(0)
    ts = av_ref.shape[0]

    def glu(v_ref, g_ref):
        return v_ref[...].astype(F32) * _sigmoid(g_ref[...].astype(F32))

    buf_ref[0:CONV_HALO, :] = jnp.where(i > 0, glu(avp_ref, agp_ref), 0.0)
    buf_ref[CONV_HALO:CONV_HALO + ts, :] = glu(av_ref, ag_ref)
    buf_ref[CONV_HALO + ts:2 * CONV_HALO + ts, :] = jnp.where(i < n - 1, glu(avn_ref, agn_ref), 0.0)
    buf_ref[2 * CONV_HALO + ts:, :] = jnp.zeros((SUBLANES, buf_ref.shape[1]), F32)

    span = ts + 2 * CONV_HALO
    for o in range(SUBLANES):
        sh_ref[o] = buf_ref[o:o + span, :]

    off = CONV_HALO - CONV_WIDTH // 2
    for r in range(ts // rows):
        acc = jnp.broadcast_to(b_ref[...], (rows, b_ref.shape[-1]))
        for k in range(CONV_WIDTH):
            lo = r * rows + k + off
            base = lo - lo % SUBLANES
            acc = acc + w_ref[k:k + 1, :] * sh_ref[lo % SUBLANES, base:base + rows, :]
        mu = jnp.mean(acc, axis=-1, keepdims=True)
        dlt = acc - mu
        var = jnp.mean(dlt * dlt, axis=-1, keepdims=True)
        y = dlt * lax.rsqrt(var + LN_EPS) * lg_ref[...] + lb_ref[...]
        o_ref[r * rows:(r + 1) * rows, :] = _silu(y).astype(o_ref.dtype)


def conformer_conv(z, conv_w, conv_b, cln_g, cln_b, ts=256, rows=32):
    s = z.shape[0]
    c = conv_w.shape[1]
    ts = min(ts, s)
    hb = ts // CONV_HALO
    nh = s // CONV_HALO
    main = lambda col: pl.BlockSpec((ts, c), lambda i: (i, col))
    prev = lambda col: pl.BlockSpec((CONV_HALO, c), lambda i: (jnp.maximum(i * hb - 1, 0), col))
    nxt = lambda col: pl.BlockSpec((CONV_HALO, c), lambda i: (jnp.minimum((i + 1) * hb, nh - 1), col))
    vec = lambda r: pl.BlockSpec((r, c), lambda i: (0, 0))
    return pl.pallas_call(
        functools.partial(_conv_kernel, rows=min(rows, ts)),
        out_shape=jax.ShapeDtypeStruct((s, c), BF16),
        grid=(s // ts,),
        in_specs=[main(0), main(1), prev(0), prev(1), nxt(0), nxt(1),
                  vec(CONV_WIDTH), vec(1), vec(1), vec(1)],
        out_specs=pl.BlockSpec((ts, c), lambda i: (i, 0)),
        scratch_shapes=[pltpu.VMEM((ts + 2 * CONV_HALO + SUBLANES, c), F32),
                        pltpu.VMEM((SUBLANES, ts + 2 * CONV_HALO, c), F32)],
        compiler_params=_cp(("parallel",)),
        name="conformer_conv",
    )(z, z, z, z, z, z, conv_w, conv_b.reshape(1, c), cln_g.reshape(1, c), cln_b.reshape(1, c))


def _t5_bucket(rel):
    half = N_BUCKETS // 2
    exact = half // 2
    n = jnp.abs(rel)
    large = exact + (jnp.log(jnp.maximum(n, 1).astype(F32) / exact)
                     / math.log(T5_MAX_DIST / exact) * (half - exact)).astype(jnp.int32)
    large = jnp.minimum(large, half - 1)
    return jnp.where(rel > 0, half, 0) + jnp.where(n < exact, n, large)


def _toeplitz(w, rows, cols):
    n = w.shape[-1]
    wp = jnp.concatenate([w, jnp.zeros(w.shape[:-1] + (1,), w.dtype)], axis=-1)
    flat = jnp.tile(wp, (1,) * (w.ndim - 1) + (rows,))[..., :rows * n]
    return flat.reshape(w.shape[:-1] + (rows, n))[..., rows - 1:rows - 1 + cols]


def _band_diag(t5_table, dil, qb):
    kl = qb + 2 * DIL_HALF
    n = -(-(qb + kl - 1) // LANE) * LANE
    dist = jnp.arange(n) - (qb - 1) - DIL_HALF
    return jnp.where((jnp.abs(dist) <= DIL_HALF)[None], t5_table[_t5_bucket(dist * dil)].T.astype(F32), NEG_INF)


def _band_bias_rows(diag_row, qb):
    n = diag_row.shape[1]
    rolled = pltpu.roll(jnp.broadcast_to(diag_row, (qb, n)), n - (qb - 1), 1, stride=1, stride_axis=0)
    return rolled[:, :qb + 2 * DIL_HALF]


def _band_group(q, kw, vw, bias_ref, pen):
    r = q.shape[0]
    is_a = lax.broadcasted_iota(jnp.int32, (1, LANE), 1) < DIL_HEAD_DIM
    q2 = jnp.concatenate([jnp.where(is_a, q, 0.0), jnp.where(is_a, 0.0, q)], axis=0).astype(BF16)
    s = lax.dot_general(q2, kw.astype(BF16), (((1,), (1,)), ((), ())), preferred_element_type=F32)
    s = s + bias_ref[...]
    if pen is not None:
        s = s + pen
    m = jnp.max(s, axis=-1, keepdims=True)
    p = jnp.exp2((s - m).astype(BF16))
    vb = vw.astype(BF16)
    pv = jnp.dot(p, jnp.concatenate([vb, jnp.ones_like(vb)], axis=1), preferred_element_type=F32)
    pick = lambda x: jnp.where(is_a, x[:r], x[r:])
    return pick(pv[:, :LANE]), pick(m), pick(pv[:, LANE:])


def _dil_kernel(q_ref, kp_ref, kc_ref, kn_ref, vp_ref, vc_ref, vn_ref, d1_ref, d4_ref, d16_ref, o_ref,
                qf, kf, vf, a1, m1, l1, a4, m4, l4, a16, m16, l16, b1_ref, b4_ref, b16_ref):
    i = pl.program_id(1)
    first = i == 0
    last = i == pl.num_programs(1) - 1
    t = DIL_TILE
    half = DIL_HALF

    @pl.when(first)
    def _():
        q16 = t // 16
        b16_ref[...] = jnp.full(b16_ref.shape, NEG_INF, F32)
        for h in range(2):
            b1_ref[h * DIL_QB:(h + 1) * DIL_QB, :] = _band_bias_rows(d1_ref[h:h + 1, :], DIL_QB)
            b4_ref[h * DIL_QB:(h + 1) * DIL_QB, :] = _band_bias_rows(d4_ref[h:h + 1, :], DIL_QB)
            blk = _band_bias_rows(d16_ref[h:h + 1, :], q16)
            for u in range(DIL_MERGE):
                r0 = (h * DIL_MERGE + u) * q16
                b16_ref[r0:r0 + q16, u * blk.shape[1]:(u + 1) * blk.shape[1]] = blk

    qf[...] = q_ref[...].astype(F32) * (DIL_HEAD_DIM ** -0.5 * LOG2E)
    for dst, (p_ref, c_ref, n_ref) in ((kf, (kp_ref, kc_ref, kn_ref)), (vf, (vp_ref, vc_ref, vn_ref))):
        dst[0:t, :] = p_ref[...].astype(F32)
        dst[t:2 * t, :] = c_ref[...].astype(F32)
        dst[2 * t:3 * t, :] = n_ref[...].astype(F32)

    def store(refs, rows, vals):
        for ref, val in zip(refs, vals):
            ref[rows, :] = val

    qb = DIL_QB
    kl = qb + 2 * half
    col = lax.broadcasted_iota(jnp.int32, (1, kl), 1)
    lo = jnp.where((col < half) & first, NEG_INF, 0.0)
    hi = jnp.where((col >= qb + half) & last, NEG_INF, 0.0)

    nb = t // qb
    for b in range(nb):
        pen = lo + hi if nb == 1 else (lo if b == 0 else (hi if b == nb - 1 else None))
        k0 = t + b * qb - half
        rows = slice(b * qb, (b + 1) * qb)
        store((a1, m1, l1), rows, _band_group(qf[rows, :], kf[k0:k0 + kl, :], vf[k0:k0 + kl, :], b1_ref, pen))

    dil = 4
    for r in range(dil):
        qrows = pl.ds(r, qb, stride=dil)
        krows = pl.ds(t - half * dil + r, kl, stride=dil)
        store((a4, m4, l4), qrows, _band_group(qf[qrows, :], kf[krows, :], vf[krows, :], b4_ref, lo + hi))

    dil = 16
    q16 = t // dil
    k16 = q16 + 2 * half
    col = lax.rem(lax.broadcasted_iota(jnp.int32, (1, DIL_MERGE * k16), 1), k16)
    pen16 = (jnp.where((col < half) & first, NEG_INF, 0.0)
             + jnp.where((col >= q16 + half) & last, NEG_INF, 0.0))
    for g in range(dil // DIL_MERGE):
        res = range(g * DIL_MERGE, (g + 1) * DIL_MERGE)
        qrows = [pl.ds(r, q16, stride=dil) for r in res]
        krows = [pl.ds(t - half * dil + r, k16, stride=dil) for r in res]
        out = _band_group(jnp.concatenate([qf[rr, :] for rr in qrows], axis=0),
                          jnp.concatenate([kf[rr, :] for rr in krows], axis=0),
                          jnp.concatenate([vf[rr, :] for rr in krows], axis=0), b16_ref, pen16)
        for u, rr in enumerate(qrows):
            store((a16, m16, l16), rr, [x[u * q16:(u + 1) * q16] for x in out])

    mm = jnp.maximum(jnp.maximum(m1[...], m4[...]), m16[...])
    w1, w4, w16 = jnp.exp2(m1[...] - mm), jnp.exp2(m4[...] - mm), jnp.exp2(m16[...] - mm)
    num = w1 * a1[...] + w4 * a4[...] + w16 * a16[...]
    den = w1 * l1[...] + w4 * l4[...] + w16 * l16[...]
    o_ref[...] = (num / den).astype(o_ref.dtype)


def dilated_attention(z, t5_table, col0, width):
    s, zw = z.shape
    t = DIL_TILE
    assert s % t == 0 and col0 % LANE == 0 and width % LANE == 0 and t // 4 == DIL_QB
    nt = s // t
    nh = t5_table.shape[1]
    cq, ck, cv = ((col0 + k * width) // LANE for k in range(3))
    q16 = t // 16
    diags = [(_band_diag(t5_table, dil, qb) * LOG2E).reshape(nh // 2, 2, -1)
             for dil, qb in ((1, DIL_QB), (4, DIL_QB), (16, q16))]
    kl, k16 = DIL_QB + 2 * DIL_HALF, q16 + 2 * DIL_HALF

    def blk(col, shift):
        return pl.BlockSpec((t, LANE), lambda hg, i: (jnp.clip(i + shift, 0, nt - 1), col + hg))

    diag_spec = lambda d: pl.BlockSpec((None,) + d.shape[1:], lambda hg, i: (hg, 0, 0))
    return pl.pallas_call(
        _dil_kernel,
        out_shape=jax.ShapeDtypeStruct((s, width), BF16),
        grid=(width // LANE, nt),
        in_specs=[blk(cq, 0), blk(ck, -1), blk(ck, 0), blk(ck, 1), blk(cv, -1), blk(cv, 0), blk(cv, 1)]
                 + [diag_spec(d) for d in diags],
        out_specs=pl.BlockSpec((t, LANE), lambda hg, i: (i, hg)),
        scratch_shapes=[pltpu.VMEM((t, LANE), F32), pltpu.VMEM((3 * t, LANE), F32),
                        pltpu.VMEM((3 * t, LANE), F32)] + [pltpu.VMEM((t, LANE), F32)] * 9
                       + [pltpu.VMEM((2 * DIL_QB, kl), F32), pltpu.VMEM((2 * DIL_QB, kl), F32),
                          pltpu.VMEM((2 * DIL_MERGE * q16, DIL_MERGE * k16), F32)],
        compiler_params=_cp(("parallel", "arbitrary")),
        name="dilated_attention",
    )(*([z] * 7 + diags))


def _oproj_kernel(a1_ref, a2_ref, w_ref, x_ref, g_ref, gt_ref, o_ref):
    k1 = a1_ref.shape[1]
    y = jnp.dot(a1_ref[...], w_ref[0:k1, :], preferred_element_type=F32)
    y = y + jnp.dot(a2_ref[...], w_ref[k1:, :], preferred_element_type=F32)
    o_ref[...] = x_ref[...] + gt_ref[...] * (_rms(y) * g_ref[...])


def out_proj_residual(a1, a2, w, x, g, gt, tm=512):
    s, d = x.shape
    k1, k2 = a1.shape[1], a2.shape[1]
    tm = min(tm, s)
    vec = pl.BlockSpec((1, d), lambda i: (0, 0))
    return pl.pallas_call(
        _oproj_kernel,
        out_shape=jax.ShapeDtypeStruct((s, d), F32),
        grid=(s // tm,),
        in_specs=[pl.BlockSpec((tm, k1), lambda i: (i, 0)), pl.BlockSpec((tm, k2), lambda i: (i, 0)),
                  pl.BlockSpec((k1 + k2, d), lambda i: (0, 0)),
                  pl.BlockSpec((tm, d), lambda i: (i, 0)), vec, vec],
        out_specs=pl.BlockSpec((tm, d), lambda i: (i, 0)),
        compiler_params=_cp(("parallel",)),
        name="out_proj_residual",
    )(a1, a2, w, x, g, gt)


def _ffn_kernel(x_ref, g_ref, sh_ref, sc_ref, g2_ref, gt_ref, w1_ref, w3_ref, w2_ref, o_ref, h_ref):
    f = pl.program_id(1)

    @pl.when(f == 0)
    def _():
        h_ref[...] = _modnorm(x_ref[...], g_ref[...], sh_ref[...], sc_ref[...]).astype(h_ref.dtype)
        o_ref[...] = jnp.zeros_like(o_ref)

    h = h_ref[...]
    u = jnp.dot(h, w1_ref[...], preferred_element_type=F32)
    v = jnp.dot(h, w3_ref[...], preferred_element_type=F32)
    o_ref[...] += jnp.dot((_silu(u) * v).astype(BF16), w2_ref[...], preferred_element_type=F32)

    @pl.when(f == pl.num_programs(1) - 1)
    def _():
        o_ref[...] = x_ref[...] + gt_ref[...] * (_rms(o_ref[...]) * g2_ref[...])


def ffn_sublayer(x, g, sh, sc, g2, gt, w1, w3, w2, tm=1024, tf=256):
    s, d = x.shape
    f = w1.shape[1]
    tm, tf = min(tm, s), min(tf, f)
    vec = pl.BlockSpec((1, d), lambda i, j: (0, 0))
    return pl.pallas_call(
        _ffn_kernel,
        out_shape=jax.ShapeDtypeStruct((s, d), F32),
        grid=(s // tm, f // tf),
        in_specs=[pl.BlockSpec((tm, d), lambda i, j: (i, 0)), vec, vec, vec, vec, vec,
                  pl.BlockSpec((d, tf), lambda i, j: (0, j)), pl.BlockSpec((d, tf), lambda i, j: (0, j)),
                  pl.BlockSpec((tf, d), lambda i, j: (j, 0))],
        out_specs=pl.BlockSpec((tm, d), lambda i, j: (i, 0)),
        scratch_shapes=[pltpu.VMEM((tm, d), BF16)],
        compiler_params=_cp(("parallel", "arbitrary")),
        name="ffn_sublayer",
    )(x, g, sh, sc, g2, gt, w1, w3, w2)


def _rope_slab(t, c_ref, s1_ref, s2_ref):
    return (t * c_ref[...] + pltpu.roll(t, LANE - ROPE_DIM // 2, 1) * s1_ref[...]
            + pltpu.roll(t, ROPE_DIM // 2, 1) * s2_ref[...])


def _oddin_kernel(x_ref, g_ref, sh_ref, sc_ref, win_ref, qg_ref, wuq_ref, kvg_ref, wuk_ref, wuv_ref,
                  c_ref, s1_ref, s2_ref, q_ref, k_ref, v_ref, u_ref, *, q_lora, kv_lora, scale):
    h = _modnorm(x_ref[...], g_ref[...], sh_ref[...], sc_ref[...]).astype(BF16)
    z = jnp.dot(h, win_ref[...], preferred_element_type=F32)
    o_kv, o_pe, o_u = q_lora, q_lora + kv_lora, q_lora + kv_lora + LANE
    u_ref[...] = z[:, o_u:]

    qn = (_rms(z[:, :q_lora]) * qg_ref[...]).astype(BF16)
    q = jnp.dot(qn, wuq_ref[...], preferred_element_type=F32)
    kvn = (_rms(z[:, o_kv:o_pe]) * kvg_ref[...]).astype(BF16)
    kn = jnp.dot(kvn, wuk_ref[...], preferred_element_type=F32)
    v_ref[...] = jnp.dot(kvn, wuv_ref[...], preferred_element_type=F32).astype(v_ref.dtype)
    kpe = _rope_slab(z[:, o_pe:o_u], c_ref, s1_ref, s2_ref).astype(k_ref.dtype)

    for hh in range(MLA_HEADS):
        b = hh * MLA_QK_PAD
        q_ref[:, b:b + NOPE_DIM] = (q[:, b:b + NOPE_DIM] * scale).astype(q_ref.dtype)
        qpe = _rope_slab(q[:, b + NOPE_DIM:b + MLA_QK_PAD], c_ref, s1_ref, s2_ref)
        q_ref[:, b + NOPE_DIM:b + MLA_QK_PAD] = (qpe * scale).astype(q_ref.dtype)
        k_ref[:, b:b + NOPE_DIM] = kn[:, hh * NOPE_DIM:(hh + 1) * NOPE_DIM].astype(k_ref.dtype)
        k_ref[:, b + NOPE_DIM:b + MLA_QK_PAD] = kpe


def _rope_tables(seq):
    pos = jnp.arange(seq, dtype=F32)
    inv_freq = ROPE_BASE ** (-jnp.arange(0, ROPE_DIM, 2, dtype=F32) / ROPE_DIM)
    ang = pos[:, None] * inv_freq[None, :]
    cos, sin = jnp.cos(ang), jnp.sin(ang)
    zero = jnp.zeros_like(cos)
    c = jnp.concatenate([cos, cos, zero, zero], axis=-1)
    s1 = jnp.concatenate([-sin, zero, zero, zero], axis=-1)
    s2 = jnp.concatenate([zero, sin, zero, zero], axis=-1)
    return c, s1, s2


def odd_in_proj(x, g, sh, sc, w_in, q_norm_g, w_uq, kv_norm_g, w_ukv, tm=256):
    s, d = x.shape
    q_lora, kv_lora = q_norm_g.shape[0], kv_norm_g.shape[0]
    s5_dim = w_in.shape[1] - q_lora - kv_lora - ROPE_DIM
    tm = min(tm, s)
    w_in_p = jnp.concatenate(
        [w_in[:, :q_lora + kv_lora + ROPE_DIM], jnp.zeros((d, LANE - ROPE_DIM), w_in.dtype),
         w_in[:, q_lora + kv_lora + ROPE_DIM:]], axis=1).astype(BF16)
    wq = w_uq.reshape(q_lora, MLA_HEADS, NOPE_DIM + ROPE_DIM)
    wq = jnp.pad(wq, ((0, 0), (0, 0), (0, MLA_QK_PAD - NOPE_DIM - ROPE_DIM)))
    wq = wq.reshape(q_lora, MLA_HEADS * MLA_QK_PAD).astype(BF16)
    wkv = w_ukv.reshape(kv_lora, MLA_HEADS, NOPE_DIM + V_DIM)
    wuk = wkv[:, :, :NOPE_DIM].reshape(kv_lora, MLA_HEADS * NOPE_DIM).astype(BF16)
    wuv = wkv[:, :, NOPE_DIM:].reshape(kv_lora, MLA_HEADS * V_DIM).astype(BF16)
    c, s1, s2 = _rope_tables(s)
    scale = (NOPE_DIM + ROPE_DIM) ** -0.5 * LOG2E

    full = lambda a: pl.BlockSpec(a.shape, lambda i: (0, 0))
    row = lambda w: pl.BlockSpec((tm, w), lambda i: (i, 0))
    vec = pl.BlockSpec((1, d), lambda i: (0, 0))
    qg = q_norm_g.reshape(1, q_lora)
    kvg = kv_norm_g.reshape(1, kv_lora)
    return pl.pallas_call(
        functools.partial(_oddin_kernel, q_lora=q_lora, kv_lora=kv_lora, scale=scale),
        out_shape=[jax.ShapeDtypeStruct((s, MLA_HEADS * MLA_QK_PAD), BF16),
                   jax.ShapeDtypeStruct((s, MLA_HEADS * MLA_QK_PAD), BF16),
                   jax.ShapeDtypeStruct((s, MLA_HEADS * V_DIM), BF16),
                   jax.ShapeDtypeStruct((s, s5_dim), F32)],
        grid=(s // tm,),
        in_specs=[row(d), vec, vec, vec, full(w_in_p), full(qg), full(wq), full(kvg), full(wuk), full(wuv),
                  row(LANE), row(LANE), row(LANE)],
        out_specs=[row(MLA_HEADS * MLA_QK_PAD), row(MLA_HEADS * MLA_QK_PAD), row(MLA_HEADS * V_DIM),
                   row(s5_dim)],
        compiler_params=_cp(("parallel",)),
        name="odd_in_proj",
    )(x, g, sh, sc, w_in_p, qg, wq, kvg, wuk, wuv, c, s1, s2)


def _flash_kernel(q_ref, k_ref, v_ref, o_ref, m_ref, acc_ref, s0, s1, *, rows, nk):
    t = pl.program_id(0)
    j1 = lax.rem(jnp.maximum(t - 1, 0), nk)
    tq = q_ref.shape[0]
    vd = v_ref.shape[1]

    @pl.when(t == 0)
    def _():
        m_ref[...] = jnp.full_like(m_ref, -jnp.inf)
        acc_ref[...] = jnp.zeros_like(acc_ref)
        s1[...] = jnp.full_like(s1, -jnp.inf)

    def stages(s_rd, s_wr):
        v = v_ref[...]
        v1 = jnp.concatenate([v, jnp.ones_like(v)], axis=1)
        k = k_ref[...]
        live = t >= 1
        fresh = j1 == 0
        for c in range(tq // rows):
            rs = slice(c * rows, (c + 1) * rows)
            s = s_rd[rs, :]
            m_prev = jnp.where(fresh, -jnp.inf, m_ref[rs, :])
            m_cand = jnp.maximum(m_prev, jnp.max(s, axis=-1, keepdims=True))
            m_new = jnp.where(live, m_cand, m_prev)
            m_sub = jnp.where(live, m_cand, 0.0)
            p = jnp.exp2((s - m_sub[:, :1]).astype(BF16))
            alpha = jnp.where(live, jnp.exp2(m_prev - m_new), 1.0)
            m_ref[rs, :] = m_new
            pv = jnp.dot(p, v1, preferred_element_type=F32)
            acc_ref[rs, :vd] = alpha * acc_ref[rs, :vd] + pv[:, :vd]
            acc_ref[rs, vd:] = alpha * acc_ref[rs, vd:] + pv[:, vd:]
            s_wr[rs, :] = lax.dot_general(q_ref[rs, :], k, (((1,), (1,)), ((), ())),
                                          preferred_element_type=F32)

    parity = lax.rem(t, 2)

    @pl.when(parity == 0)
    def _():
        stages(s1, s0)

    @pl.when(parity == 1)
    def _():
        stages(s0, s1)

    @pl.when((t >= 1) & (j1 == nk - 1))
    def _():
        o_ref[...] = (acc_ref[:, :vd] / acc_ref[:, vd:]).astype(o_ref.dtype)


def mla_attention(q, k, v, tq=2048, tk=2048, rows=512):
    s = q.shape[0]
    tq, tk = min(tq, s), min(tk, s)
    rows = min(rows, tq)
    ni, nk = s // tq, s // tk
    n = MLA_HEADS * ni * nk

    def item(t, lag):
        w = jnp.clip(t - lag, 0, n - 1)
        return w // (ni * nk), lax.rem(w, ni * nk) // nk, lax.rem(w, nk)

    def q_map(t):
        h, i, _ = item(t, 0)
        return i, h

    def k_map(t):
        h, _, j = item(t, 0)
        return j, h

    def v_map(t):
        h, _, j = item(t, 1)
        return j, h

    def o_map(t):
        h, i, _ = item(t, 1)
        return i, h

    return pl.pallas_call(
        functools.partial(_flash_kernel, rows=rows, nk=nk),
        out_shape=jax.ShapeDtypeStruct((s, MLA_HEADS * V_DIM), BF16),
        grid=(n + 1,),
        in_specs=[pl.BlockSpec((tq, MLA_QK_PAD), q_map),
                  pl.BlockSpec((tk, MLA_QK_PAD), k_map),
                  pl.BlockSpec((tk, V_DIM), v_map)],
        out_specs=pl.BlockSpec((tq, V_DIM), o_map),
        scratch_shapes=[pltpu.VMEM((tq, V_DIM), F32), pltpu.VMEM((tq, 2 * V_DIM), F32),
                        pltpu.VMEM((tq, tk), F32), pltpu.VMEM((tq, tk), F32)],
        compiler_params=_cp(("arbitrary",)),
        name="mla_flash_attention",
    )(q, k, v)


def _s5_matrices(a_re, a_im, log_dt, b_re, b_im, c_re, c_im, chunk):
    L = chunk
    A = lax.complex(a_re.astype(F32), a_im.astype(F32))
    dt = jnp.exp(log_dt.astype(F32))[..., None]
    adt = A * dt
    a_bar = jnp.exp(adt)
    b_bar = ((a_bar - 1.0) / A)[..., None] * lax.complex(b_re.astype(F32), b_im.astype(F32))
    c_c = lax.complex(c_re.astype(F32), c_im.astype(F32))
    kk = jnp.arange(L + 1, dtype=F32)
    apow = jnp.exp(adt[:, :, None, :] * kk[None, None, :, None].astype(jnp.complex64))
    g, p, gc = b_bar.shape[1], b_bar.shape[2], b_bar.shape[3]

    ker = jnp.real(jnp.einsum('dgcp,dgkp,dgpi->dgkci', c_c, apow[:, :, :L], b_bar))
    kb = ker[1][:, ::-1]
    diag = jnp.concatenate([kb[:, :L - 1], ker[0][:, :1] + kb[:, L - 1:], ker[0][:, 1:]], axis=1)
    t_mat = _toeplitz(jnp.transpose(diag, (0, 2, 3, 1)), L, L)
    t_mat = jnp.transpose(t_mat, (0, 3, 2, 4, 1)).reshape(g, L * gc, L * gc)

    pf = apow[0][:, ::-1][:, 1:][:, :, :, None] * b_bar[0][:, None]
    pb = apow[1][:, :L][:, :, :, None] * b_bar[1][:, None]
    def p_lay(m):
        return jnp.transpose(m, (0, 1, 3, 2)).reshape(g, L * gc, p)
    p_mat = jnp.concatenate([p_lay(jnp.real(pf)), p_lay(jnp.imag(pf)),
                             p_lay(jnp.real(pb)), p_lay(jnp.imag(pb))], axis=-1)

    wf = c_c[0][:, None] * apow[0][:, 1:][:, :, None, :]
    wb = c_c[1][:, None] * apow[1][:, ::-1][:, :L][:, :, None, :]
    def q_lay(m):
        return jnp.transpose(m, (0, 3, 1, 2)).reshape(g, p, L * gc)
    q_mat = jnp.concatenate([q_lay(jnp.real(wf)), q_lay(-jnp.imag(wf)),
                             q_lay(jnp.real(wb)), q_lay(-jnp.imag(wb))], axis=1)

    al = apow[:, :, L]
    dec = jnp.stack([jnp.real(al[0]), jnp.imag(al[0]), jnp.real(al[1]), jnp.imag(al[1])]).reshape(4, g * p)
    return t_mat, p_mat, q_mat, dec


def _s5_state_kernel(u_ref, p_ref, fre_ref, fim_ref, bre_ref, bim_ref):
    outs = [[], [], [], []]
    for gi in range(u_ref.shape[0]):
        xe = jnp.dot(u_ref[gi].astype(BF16), p_ref[gi], preferred_element_type=F32)
        for part in range(4):
            outs[part].append(xe[:, part * S5_STATE:(part + 1) * S5_STATE])
    for part, ref in enumerate((fre_ref, fim_ref, bre_ref, bim_ref)):
        ref[...] = jnp.concatenate(outs[part], axis=-1)


def _s5_scan_kernel(fre_ref, fim_ref, bre_ref, bim_ref, dec_ref, ofre_ref, ofim_ref, obre_ref, obim_ref):
    nc = fre_ref.shape[0]
    w = fre_ref.shape[1]
    far, fai, bar, bai = (dec_ref[i:i + 1, :] for i in range(4))

    def fwd(c, st):
        re, im = st
        ofre_ref[pl.ds(c, 1), :] = re
        ofim_ref[pl.ds(c, 1), :] = im
        return (far * re - fai * im + fre_ref[pl.ds(c, 1), :],
                far * im + fai * re + fim_ref[pl.ds(c, 1), :])

    def bwd(i, st):
        c = nc - 1 - i
        re, im = st
        obre_ref[pl.ds(c, 1), :] = re
        obim_ref[pl.ds(c, 1), :] = im
        return (bar * re - bai * im + bre_ref[pl.ds(c, 1), :],
                bar * im + bai * re + bim_ref[pl.ds(c, 1), :])

    zero = (jnp.zeros((1, w), F32), jnp.zeros((1, w), F32))
    lax.fori_loop(0, nc, fwd, zero)
    lax.fori_loop(0, nc, bwd, zero)


def _s5_out_kernel(u_ref, t_ref, q_ref, fre_ref, fim_ref, bre_ref, bim_ref, y_ref):
    for gi in range(u_ref.shape[0]):
        lanes = slice(gi * S5_STATE, (gi + 1) * S5_STATE)
        y = jnp.dot(u_ref[gi].astype(BF16), t_ref[gi], preferred_element_type=F32)
        xin = jnp.concatenate([r[:, lanes] for r in (fre_ref, fim_ref, bre_ref, bim_ref)], axis=-1)
        y_ref[gi] = y + jnp.dot(xin.astype(BF16), q_ref[gi], preferred_element_type=F32)


def s5_bidirectional(u, a_re, a_im, log_dt, b_re, b_im, c_re, c_im, chunk=S5_CHUNK):
    s, dim = u.shape
    gc, p = S5_GROUP, S5_STATE
    g = dim // gc
    chunk = min(chunk, s)
    nc = s // chunk
    gp = LANE // p
    t_mat, p_mat, q_mat, dec = _s5_matrices(a_re, a_im, log_dt, b_re, b_im, c_re, c_im, chunk)
    t_mat, p_mat, q_mat = t_mat.astype(BF16), p_mat.astype(BF16), q_mat.astype(BF16)
    u_t = jnp.transpose(u.astype(BF16).reshape(nc, chunk, g, gc), (2, 0, 1, 3)).reshape(g, nc, chunk * gc)

    grp = lambda a, b: pl.BlockSpec((gp, a, b), lambda i: (i, 0, 0))
    st = pl.BlockSpec((nc, LANE), lambda i: (0, i))
    st_shape = jax.ShapeDtypeStruct((nc, g * p), F32)
    xe = pl.pallas_call(
        _s5_state_kernel,
        out_shape=[st_shape] * 4,
        grid=(g // gp,),
        in_specs=[grp(nc, chunk * gc), grp(chunk * gc, 4 * p)],
        out_specs=[st] * 4,
        compiler_params=_cp(("parallel",)),
        name="s5_chunk_states",
    )(u_t, p_mat)

    lanes = min(4 * LANE, g * p)
    cols = lambda rows: pl.BlockSpec((rows, lanes), lambda i: (0, i))
    xin = pl.pallas_call(
        _s5_scan_kernel,
        out_shape=[st_shape] * 4,
        grid=(g * p // lanes,),
        in_specs=[cols(nc)] * 4 + [cols(4)],
        out_specs=[cols(nc)] * 4,
        compiler_params=_cp(("parallel",)),
        name="s5_boundary_scan",
    )(*xe, dec)

    y_t = pl.pallas_call(
        _s5_out_kernel,
        out_shape=jax.ShapeDtypeStruct((g, nc, chunk * gc), F32),
        grid=(g // gp,),
        in_specs=[grp(nc, chunk * gc), grp(chunk * gc, chunk * gc), grp(4 * p, chunk * gc)] + [st] * 4,
        out_specs=grp(nc, chunk * gc),
        compiler_params=_cp(("parallel",)),
        name="s5_outputs",
    )(u_t, t_mat, q_mat, *xin)
    return jnp.transpose(y_t.reshape(g, nc, chunk, gc), (1, 2, 0, 3)).reshape(s, dim)


def _s5_gate_kernel(y_ref, u_ref, d_ref, w_ref, o_ref):
    y = y_ref[...] + d_ref[...] * u_ref[...]
    y = 0.5 * y * (1.0 + jnp.tanh(math.sqrt(2.0 / math.pi) * (y + 0.044715 * (y * y * y))))
    z = jnp.dot(y.astype(BF16), w_ref[...], preferred_element_type=F32)
    o_ref[...] = (y * _sigmoid(z)).astype(o_ref.dtype)


def s5_gate(y, u, d_skip, w_glu, tm=1024):
    s, dim = y.shape
    tm = min(tm, s)
    row = pl.BlockSpec((tm, dim), lambda i: (i, 0))
    return pl.pallas_call(
        _s5_gate_kernel,
        out_shape=jax.ShapeDtypeStruct((s, dim), BF16),
        grid=(s // tm,),
        in_specs=[row, row, pl.BlockSpec((1, dim), lambda i: (0, 0)), pl.BlockSpec((dim, dim), lambda i: (0, 0))],
        out_specs=row,
        compiler_params=_cp(("parallel",)),
        name="s5_gate",
    )(y, u, d_skip.reshape(1, dim), w_glu.astype(BF16))


def _router_kernel(x_ref, g_ref, sh_ref, sc_ref, rw_ref, h_ref, info_ref):
    h = _modnorm(x_ref[...], g_ref[...], sh_ref[...], sc_ref[...])
    half = h.shape[1] // 2
    bits = lambda v: lax.bitcast_convert_type(v.astype(BF16).astype(F32), jnp.uint32)
    h_ref[...] = (bits(h[:, :half]) >> 16) | (bits(h[:, half:]) & jnp.uint32(0xFFFF0000))
    logits = jnp.dot(h, rw_ref[...], preferred_element_type=F32, precision=lax.Precision.HIGHEST)
    lane = lax.broadcasted_iota(jnp.int32, logits.shape, 1)
    logits = jnp.where(lane < N_EXPERTS, logits, -jnp.inf)
    m1 = jnp.max(logits, axis=-1, keepdims=True)
    i1 = jnp.min(jnp.where(logits == m1, lane, LANE), axis=-1, keepdims=True)
    rest = jnp.where(lane == i1, -jnp.inf, logits)
    m2 = jnp.max(rest, axis=-1, keepdims=True)
    i2 = jnp.min(jnp.where(rest == m2, lane, LANE), axis=-1, keepdims=True)
    e = jnp.exp(m2 - m1)
    g1 = 1.0 / (1.0 + e)
    g2 = e / (1.0 + e)
    info = jnp.where(lane == 0, i1.astype(F32),
                     jnp.where(lane == 1, i2.astype(F32),
                               jnp.where(lane == 2, g1, jnp.where(lane == 3, g2, 0.0))))
    info_ref[...] = info


def moe_router(x, g, sh, sc, router_w, tm=512):
    s, d = x.shape
    tm = min(tm, s)
    rw = jnp.pad(router_w, ((0, 0), (0, LANE - router_w.shape[1])))
    vec = pl.BlockSpec((1, d), lambda i: (0, 0))
    return pl.pallas_call(
        _router_kernel,
        out_shape=[jax.ShapeDtypeStruct((s, d // 2), jnp.uint32), jax.ShapeDtypeStruct((s, LANE), F32)],
        grid=(s // tm,),
        in_specs=[pl.BlockSpec((tm, d), lambda i: (i, 0)), vec, vec, vec,
                  pl.BlockSpec((d, LANE), lambda i: (0, 0))],
        out_specs=[pl.BlockSpec((tm, d // 2), lambda i: (i, 0)), pl.BlockSpec((tm, LANE), lambda i: (i, 0))],
        compiler_params=_cp(("parallel",)),
        name="moe_router",
    )(x, g, sh, sc, rw)


def _expert_kernel(te_ref, tv_ref, src_ref, hp_hbm, w1_ref, w3_ref, w2_ref, o_ref, acc_ref, gbuf, h_ref, sem):
    t = pl.program_id(0)
    f = pl.program_id(1)
    nt = pl.num_programs(0)
    nf = pl.num_programs(1)
    tm = acc_ref.shape[0]
    per_step = tm // nf
    valid = tv_ref[t] > 0
    slot = lax.rem(t, 2)

    def row_copy(tile, buf, r):
        return pltpu.make_async_copy(hp_hbm.at[pl.ds(src_ref[tile * tm + r], 1)],
                                     gbuf.at[buf, pl.ds(r, 1)], sem.at[buf])

    def wait_half(buf):
        pltpu.make_async_copy(hp_hbm.at[pl.ds(0, tm)], gbuf.at[buf], sem.at[buf]).wait()

    @pl.when((t == 0) & (f == 0))
    def _():
        def issue(r, carry):
            row_copy(0, 0, r).start()
            return carry
        lax.fori_loop(0, tm, issue, 0)

    @pl.when((f == 0) & ((t == 0) | (tv_ref[jnp.maximum(t - 1, 0)] > 0)))
    def _():
        wait_half(slot)

    @pl.when((f == 0) & valid)
    def _():
        w = gbuf[slot]
        half = w.shape[1]
        h_ref[:, :half] = lax.bitcast_convert_type(w << 16, F32).astype(BF16)
        h_ref[:, half:] = lax.bitcast_convert_type(w & jnp.uint32(0xFFFF0000), F32).astype(BF16)

    @pl.when(f == 0)
    def _():
        acc_ref[...] = jnp.zeros_like(acc_ref)

    def compute(rows):
        for u in range(per_step):
            row_copy(t + 1, 1 - slot, f * per_step + u).start()
        h = h_ref[:rows, :]
        u = jnp.dot(h, w1_ref[...].astype(BF16), preferred_element_type=F32)
        v = jnp.dot(h, w3_ref[...].astype(BF16), preferred_element_type=F32)
        acc_ref[:rows, :] += jnp.dot((_silu(u) * v).astype(BF16), w2_ref[...].astype(BF16),
                                     preferred_element_type=F32)

    @pl.when(tv_ref[t] > tm // 2)
    def _():
        compute(tm)

    @pl.when(valid & (tv_ref[t] <= tm // 2))
    def _():
        compute(tm // 2)

    @pl.when(f == nf - 1)
    def _():
        o_ref[...] = acc_ref[...].astype(o_ref.dtype)

    @pl.when((t == nt - 1) & (f == nf - 1) & valid)
    def _():
        wait_half(1 - slot)


def expert_ffn(h_packed, src, tile_expert, tile_valid, w1, w3, w2, tm, tf=512):
    dh = h_packed.shape[1]
    d = 2 * dh
    n = src.shape[0] - tm
    f = w1.shape[2]
    tf = min(tf, f)
    nf = f // tf
    assert tm % nf == 0 and n % tm == 0

    def fblk(j, t, tv):
        return jnp.where(tv[t] > 0, j, nf - 1)

    return pl.pallas_call(
        _expert_kernel,
        out_shape=jax.ShapeDtypeStruct((n, d), BF16),
        grid_spec=pltpu.PrefetchScalarGridSpec(
            num_scalar_prefetch=3,
            grid=(n // tm, nf),
            in_specs=[pl.BlockSpec(memory_space=pl.ANY),
                      pl.BlockSpec((None, d, tf), lambda t, j, te, tv, sr: (te[t], 0, fblk(j, t, tv))),
                      pl.BlockSpec((None, d, tf), lambda t, j, te, tv, sr: (te[t], 0, fblk(j, t, tv))),
                      pl.BlockSpec((None, tf, d), lambda t, j, te, tv, sr: (te[t], fblk(j, t, tv), 0))],
            out_specs=pl.BlockSpec((tm, d), lambda t, j, te, tv, sr: (t, 0)),
            scratch_shapes=[pltpu.VMEM((tm, d), F32), pltpu.VMEM((2, tm, dh), jnp.uint32),
                            pltpu.VMEM((tm, d), BF16), pltpu.SemaphoreType.DMA((2,))]),
        compiler_params=_cp(("arbitrary", "arbitrary")),
        name="expert_ffn",
    )(tile_expert, tile_valid, src, h_packed, w1, w3, w2)


def _dispatch(idx, tm):
    s = idx.shape[0]
    e_flat = idx.reshape(-1)
    onehot = (e_flat[:, None] == jnp.arange(N_EXPERTS)[None, :]).astype(jnp.int32)
    csum = jnp.cumsum(onehot, axis=0)
    rank = jnp.sum((csum - onehot) * onehot, axis=1)
    counts = csum[-1]
    padded = ((counts + tm - 1) // tm) * tm
    gend = jnp.cumsum(padded)
    slot = (gend - padded)[e_flat] + rank
    nt = -(-(2 * s) // tm) + N_EXPERTS
    src = jnp.zeros(((nt + 1) * tm,), jnp.int32).at[slot].set(jnp.arange(2 * s, dtype=jnp.int32) // 2)
    tstart = jnp.arange(nt, dtype=jnp.int32) * tm
    valid = tstart < gend[-1]
    te = jnp.minimum(jnp.sum((tstart[:, None] >= gend[None, :]).astype(jnp.int32), axis=1), N_EXPERTS - 1)
    nvalid = gend[-1] // tm
    fill = jnp.clip((gend - padded + counts)[te] - tstart, 0, tm)
    fill = jnp.where(valid, fill, 0)
    te = jnp.where(valid, te, te[jnp.maximum(nvalid - 1, 0)])
    return slot.reshape(s, 2), src, te.astype(jnp.int32), fill.astype(jnp.int32)


def _combine_kernel(ya_ref, yb_ref, info_ref, x_ref, g_ref, gt_ref, o_ref):
    info = info_ref[...]
    y = info[:, 2:3] * ya_ref[...].astype(F32) + info[:, 3:4] * yb_ref[...].astype(F32)
    o_ref[...] = x_ref[...] + gt_ref[...] * (_rms(y) * g_ref[...])


def moe_combine(ya, yb, info, x, g, gt, tm=512):
    s, d = x.shape
    tm = min(tm, s)
    row = lambda w: pl.BlockSpec((tm, w), lambda i: (i, 0))
    vec = pl.BlockSpec((1, d), lambda i: (0, 0))
    return pl.pallas_call(
        _combine_kernel,
        out_shape=jax.ShapeDtypeStruct((s, d), F32),
        grid=(s // tm,),
        in_specs=[row(d), row(d), row(LANE), row(d), vec, vec],
        out_specs=row(d),
        compiler_params=_cp(("parallel",)),
        name="moe_combine",
    )(ya, yb, info, x, g, gt)


def moe_sublayer(x, g, sh, sc, g2, gt, router_w, w1, w3, w2, tm_e=EXPERT_TILE):
    s = x.shape[0]
    tm_e = min(tm_e, s)
    h_packed, info = moe_router(x, g, sh, sc, router_w)
    idx = info[:, :2].astype(jnp.int32)
    slot, src, te, tv = _dispatch(idx, tm_e)
    y_sorted = expert_ffn(h_packed, src, te, tv, w1, w3, w2, tm_e)
    ya = jnp.take(y_sorted, slot[:, 0], axis=0)
    yb = jnp.take(y_sorted, slot[:, 1], axis=0)
    return moe_combine(ya, yb, info, x, g2, gt)


def kernel(x, c, t5_table, norm_g, ada_w, ada_b, e_w_in, e_conv_w, e_conv_b, e_cln_g, e_cln_b, e_w_out, e_ffn_w1, e_ffn_w3, e_ffn_w2, o_w_in, o_q_norm_g, o_w_uq, o_kv_norm_g, o_w_ukv, s5_a_re, s5_a_im, s5_log_dt, s5_b_re, s5_b_im, s5_c_re, s5_c_im, s5_d, s5_w_glu, o_w_out, router_w, moe_w1, moe_w3, moe_w2):
    bsz, seq, d = x.shape
    assert bsz == 1
    depth = norm_g.shape[0]
    conv_dim = e_conv_w.shape[2]
    xs = x.reshape(seq, d)
    mod = adaln(c, ada_w, ada_b)
    for layer in range(depth):
        i = layer // 2
        sh_m, sc_m, gt_m, sh_f, sc_f, gt_f = (mod[layer, :, k * d:(k + 1) * d] for k in range(6))
        g = [norm_g[layer, k].reshape(1, d) for k in range(4)]
        if layer % 2 == 0:
            z = norm_mod_matmul(xs, g[0], sh_m, sc_m, e_w_in[i])
            a = conformer_conv(z, e_conv_w[i], e_conv_b[i], e_cln_g[i], e_cln_b[i])
            o = dilated_attention(z, t5_table, 2 * conv_dim, (z.shape[1] - 2 * conv_dim) // 3)
            xs = out_proj_residual(a, o, e_w_out[i].astype(BF16), xs, g[1], gt_m)
            xs = ffn_sublayer(xs, g[2], sh_f, sc_f, g[3], gt_f, e_ffn_w1[i].astype(BF16),
                              e_ffn_w3[i].astype(BF16), e_ffn_w2[i].astype(BF16))
        else:
            q, k, v, u = odd_in_proj(xs, g[0], sh_m, sc_m, o_w_in[i], o_q_norm_g[i], o_w_uq[i],
                                     o_kv_norm_g[i], o_w_ukv[i])
            o_mla = mla_attention(q, k, v)
            y = s5_bidirectional(u, s5_a_re[i], s5_a_im[i], s5_log_dt[i], s5_b_re[i], s5_b_im[i],
                                 s5_c_re[i], s5_c_im[i])
            y = s5_gate(y, u, s5_d[i], s5_w_glu[i])
            xs = out_proj_residual(o_mla, y, o_w_out[i].astype(BF16), xs, g[1], gt_m)
            xs = moe_sublayer(xs, g[2], sh_f, sc_f, g[3], gt_f, router_w[i], moe_w1[i], moe_w3[i], moe_w2[i])
    return xs.reshape(bsz, seq, d)
```

```python
import functools
import math

import jax
import jax.numpy as jnp
from jax import lax
from jax.experimental import pallas as pl
from jax.experimental.pallas import tpu as pltpu

F32 = jnp.float32
BF16 = jnp.bfloat16

RMS_EPS = 1e-6
LN_EPS = 1e-5
NEG_INF = -1e30
LOG2E = math.log2(math.e)

CONV_WIDTH = 31
CONV_HALO = 16
DIL_HEAD_DIM = 64
DIL_CONFIGS = ((128, 1), (512, 4), (2048, 16))
DIL_HALF = 64
DIL_TILE = 1024
DIL_QB = 256
DIL_MERGE = 2
N_BUCKETS = 32
T5_MAX_DIST = DIL_CONFIGS[-1][0] // 2
MLA_HEADS = 12
NOPE_DIM = 128
ROPE_DIM = 64
V_DIM = 128
ROPE_BASE = 10000.0
MLA_QK_PAD = 256
S5_GROUP = 16
S5_STATE = 64
S5_CHUNK = 16
N_EXPERTS = 8
EXPERT_TILE = 896
LANE = 128
SUBLANES = 8
VMEM_LIMIT_MB = 56


def _cp(sem, vmem_mb=VMEM_LIMIT_MB):
    return pltpu.CompilerParams(dimension_semantics=sem, vmem_limit_bytes=vmem_mb * 1024 * 1024)


def _rms(x):
    return x * lax.rsqrt(jnp.mean(x * x, axis=-1, keepdims=True) + RMS_EPS)


def _modnorm(x, g, sh, sc):
    return (_rms(x) * g) * (1.0 + sc) + sh


def _sigmoid(x):
    return 1.0 / (1.0 + jnp.exp(-x))


def _silu(x):
    return x * _sigmoid(x)


def _adaln_kernel(c_ref, w_ref, b_ref, o_ref, cb_ref):
    @pl.when((pl.program_id(0) == 0) & (pl.program_id(1) == 0))
    def _():
        cc = c_ref[...]
        cb_ref[...] = jnp.broadcast_to(_silu(cc), cb_ref.shape)

    cb = cb_ref[...]
    for j in range(o_ref.shape[-1] // LANE):
        sl = slice(j * LANE, (j + 1) * LANE)
        o_ref[:, sl] = jnp.sum(w_ref[:, sl] * cb, axis=0, keepdims=True) + b_ref[:, sl]


def adaln(c, ada_w, ada_b, tn=1024):
    nl, d, n = ada_w.shape
    out = pl.pallas_call(
        _adaln_kernel,
        out_shape=jax.ShapeDtypeStruct((nl, 1, n), F32),
        grid=(nl, n // tn),
        in_specs=[pl.BlockSpec((d, 1), lambda l, j: (0, 0)),
                  pl.BlockSpec((None, d, tn), lambda l, j: (l, 0, j)),
                  pl.BlockSpec((None, 1, tn), lambda l, j: (l, 0, j))],
        out_specs=pl.BlockSpec((None, 1, tn), lambda l, j: (l, 0, j)),
        scratch_shapes=[pltpu.VMEM((d, LANE), F32)],
        compiler_params=_cp(("arbitrary", "arbitrary")),
        name="adaln",
    )(c.reshape(d, 1), ada_w, ada_b.reshape(nl, 1, n))
    return out


def _nmm_kernel(x_ref, g_ref, sh_ref, sc_ref, w_ref, o_ref, h_ref):
    @pl.when(pl.program_id(1) == 0)
    def _():
        h_ref[...] = _modnorm(x_ref[...], g_ref[...], sh_ref[...], sc_ref[...]).astype(h_ref.dtype)

    o_ref[...] = jnp.dot(h_ref[...], w_ref[...], preferred_element_type=F32).astype(o_ref.dtype)


def norm_mod_matmul(x, g, sh, sc, w, tm=1024, tn=1024):
    s, d = x.shape
    n = w.shape[1]
    tm, tn = min(tm, s), min(tn, n)
    vec = pl.BlockSpec((1, d), lambda i, j: (0, 0))
    return pl.pallas_call(
        _nmm_kernel,
        out_shape=jax.ShapeDtypeStruct((s, n), BF16),
        grid=(s // tm, n // tn),
        in_specs=[pl.BlockSpec((tm, d), lambda i, j: (i, 0)), vec, vec, vec,
                  pl.BlockSpec((d, tn), lambda i, j: (0, j))],
        out_specs=pl.BlockSpec((tm, tn), lambda i, j: (i, j)),
        scratch_shapes=[pltpu.VMEM((tm, d), BF16)],
        compiler_params=_cp(("parallel", "arbitrary")),
        name="norm_mod_matmul",
    )(x, g, sh, sc, w)


def _conv_kernel(av_ref, ag_ref, avp_ref, agp_ref, avn_ref, agn_ref, w_ref, b_ref, lg_ref, lb_ref,
                 o_ref, buf_ref, sh_ref, *, rows):
    i = pl.program_id(0)
    n = pl.num_programs(0)
    ts = av_ref.shape[0]

    def glu(v_ref, g_ref):
        return v_ref[...].astype(F32) * _sigmoid(g_ref[...].astype(F32))

    buf_ref[0:CONV_HALO, :] = jnp.where(i > 0, glu(avp_ref, agp_ref), 0.0)
    buf_ref[CONV_HALO:CONV_HALO + ts, :] = glu(av_ref, ag_ref)
    buf_ref[CONV_HALO + ts:2 * CONV_HALO + ts, :] = jnp.where(i < n - 1, glu(avn_ref, agn_ref), 0.0)
    buf_ref[2 * CONV_HALO + ts:, :] = jnp.zeros((SUBLANES, buf_ref.shape[1]), F32)

    span = ts + 2 * CONV_HALO
    for o in range(SUBLANES):
        sh_ref[o] = buf_ref[o:o + span, :]

    off = CONV_HALO - CONV_WIDTH // 2
    for r in range(ts // rows):
        acc = jnp.broadcast_to(b_ref[...], (rows, b_ref.shape[-1]))
        for k in range(CONV_WIDTH):
            lo = r * rows + k + off
            base = lo - lo % SUBLANES
            acc = acc + w_ref[k:k + 1, :] * sh_ref[lo % SUBLANES, base:base + rows, :]
        mu = jnp.mean(acc, axis=-1, keepdims=True)
        dlt = acc - mu
        var = jnp.mean(dlt * dlt, axis=-1, keepdims=True)
        y = dlt * lax.rsqrt(var + LN_EPS) * lg_ref[...] + lb_ref[...]
        o_ref[r * rows:(r + 1) * rows, :] = _silu(y).astype(o_ref.dtype)


def conformer_conv(z, conv_w, conv_b, cln_g, cln_b, ts=256, rows=32):
    s = z.shape[0]
    c = conv_w.shape[1]
    ts = min(ts, s)
    hb = ts // CONV_HALO
    nh = s // CONV_HALO
    main = lambda col: pl.BlockSpec((ts, c), lambda i: (i, col))
    prev = lambda col: pl.BlockSpec((CONV_HALO, c), lambda i: (jnp.maximum(i * hb - 1, 0), col))
    nxt = lambda col: pl.BlockSpec((CONV_HALO, c), lambda i: (jnp.minimum((i + 1) * hb, nh - 1), col))
    vec = lambda r: pl.BlockSpec((r, c), lambda i: (0, 0))
    return pl.pallas_call(
        functools.partial(_conv_kernel, rows=min(rows, ts)),
        out_shape=jax.ShapeDtypeStruct((s, c), BF16),
        grid=(s // ts,),
        in_specs=[main(0), main(1), prev(0), prev(1), nxt(0), nxt(1),
                  vec(CONV_WIDTH), vec(1), vec(1), vec(1)],
        out_specs=pl.BlockSpec((ts, c), lambda i: (i, 0)),
        scratch_shapes=[pltpu.VMEM((ts + 2 * CONV_HALO + SUBLANES, c), F32),
                        pltpu.VMEM((SUBLANES, ts + 2 * CONV_HALO, c), F32)],
        compiler_params=_cp(("parallel",)),
        name="conformer_conv",
    )(z, z, z, z, z, z, conv_w, conv_b.reshape(1, c), cln_g.reshape(1, c), cln_b.reshape(1, c))


def _t5_bucket(rel):
    half = N_BUCKETS // 2
    exact = half // 2
    n = jnp.abs(rel)
    large = exact + (jnp.log(jnp.maximum(n, 1).astype(F32) / exact)
                     / math.log(T5_MAX_DIST / exact) * (half - exact)).astype(jnp.int32)
    large = jnp.minimum(large, half - 1)
    return jnp.where(rel > 0, half, 0) + jnp.where(n < exact, n, large)


def _toeplitz(w, rows, cols):
    n = w.shape[-1]
    wp = jnp.concatenate([w, jnp.zeros(w.shape[:-1] + (1,), w.dtype)], axis=-1)
    flat = jnp.tile(wp, (1,) * (w.ndim - 1) + (rows,))[..., :rows * n]
    return flat.reshape(w.shape[:-1] + (rows, n))[..., rows - 1:rows - 1 + cols]


def _band_diag(t5_table, dil, qb):
    kl = qb + 2 * DIL_HALF
    n = -(-(qb + kl - 1) // LANE) * LANE
    dist = jnp.arange(n) - (qb - 1) - DIL_HALF
    return jnp.where((jnp.abs(dist) <= DIL_HALF)[None], t5_table[_t5_bucket(dist * dil)].T.astype(F32), NEG_INF)


def _band_bias_rows(diag_row, qb):
    n = diag_row.shape[1]
    rolled = pltpu.roll(jnp.broadcast_to(diag_row, (qb, n)), n - (qb - 1), 1, stride=1, stride_axis=0)
    return rolled[:, :qb + 2 * DIL_HALF]


def _band_group(q, kw, vw, bias_ref, pen):
    r = q.shape[0]
    is_a = lax.broadcasted_iota(jnp.int32, (1, LANE), 1) < DIL_HEAD_DIM
    q2 = jnp.concatenate([jnp.where(is_a, q, 0.0), jnp.where(is_a, 0.0, q)], axis=0).astype(BF16)
    s = lax.dot_general(q2, kw.astype(BF16), (((1,), (1,)), ((), ())), preferred_element_type=F32)
    s = s + bias_ref[...]
    if pen is not None:
        s = s + pen
    m = jnp.max(s, axis=-1, keepdims=True)
    p = jnp.exp2((s - m).astype(BF16))
    vb = vw.astype(BF16)
    pv = jnp.dot(p, jnp.concatenate([vb, jnp.ones_like(vb)], axis=1), preferred_element_type=F32)
    pick = lambda x: jnp.where(is_a, x[:r], x[r:])
    return pick(pv[:, :LANE]), pick(m), pick(pv[:, LANE:])


def _dil_kernel(q_ref, kp_ref, kc_ref, kn_ref, vp_ref, vc_ref, vn_ref, d1_ref, d4_ref, d16_ref, o_ref,
                qf, kf, vf, a1, m1, l1, a4, m4, l4, a16, m16, l16, b1_ref, b4_ref, b16_ref):
    i = pl.program_id(1)
    first = i == 0
    last = i == pl.num_programs(1) - 1
    t = DIL_TILE
    half = DIL_HALF

    @pl.when(first)
    def _():
        q16 = t // 16
        b16_ref[...] = jnp.full(b16_ref.shape, NEG_INF, F32)
        for h in range(2):
            b1_ref[h * DIL_QB:(h + 1) * DIL_QB, :] = _band_bias_rows(d1_ref[h:h + 1, :], DIL_QB)
            b4_ref[h * DIL_QB:(h + 1) * DIL_QB, :] = _band_bias_rows(d4_ref[h:h + 1, :], DIL_QB)
            blk = _band_bias_rows(d16_ref[h:h + 1, :], q16)
            for u in range(DIL_MERGE):
                r0 = (h * DIL_MERGE + u) * q16
                b16_ref[r0:r0 + q16, u * blk.shape[1]:(u + 1) * blk.shape[1]] = blk

    qf[...] = q_ref[...].astype(F32) * (DIL_HEAD_DIM ** -0.5 * LOG2E)
    for dst, (p_ref, c_ref, n_ref) in ((kf, (kp_ref, kc_ref, kn_ref)), (vf, (vp_ref, vc_ref, vn_ref))):
        dst[0:t, :] = p_ref[...].astype(F32)
        dst[t:2 * t, :] = c_ref[...].astype(F32)
        dst[2 * t:3 * t, :] = n_ref[...].astype(F32)

    def store(refs, rows, vals):
        for ref, val in zip(refs, vals):
            ref[rows, :] = val

    qb = DIL_QB
    kl = qb + 2 * half
    col = lax.broadcasted_iota(jnp.int32, (1, kl), 1)
    lo = jnp.where((col < half) & first, NEG_INF, 0.0)
    hi = jnp.where((col >= qb + half) & last, NEG_INF, 0.0)

    nb = t // qb
    for b in range(nb):
        pen = lo + hi if nb == 1 else (lo if b == 0 else (hi if b == nb - 1 else None))
        k0 = t + b * qb - half
        rows = slice(b * qb, (b + 1) * qb)
        store((a1, m1, l1), rows, _band_group(qf[rows, :], kf[k0:k0 + kl, :], vf[k0:k0 + kl, :], b1_ref, pen))

    dil = 4
    for r in range(dil):
        qrows = pl.ds(r, qb, stride=dil)
        krows = pl.ds(t - half * dil + r, kl, stride=dil)
        store((a4, m4, l4), qrows, _band_group(qf[qrows, :], kf[krows, :], vf[krows, :], b4_ref, lo + hi))

    dil = 16
    q16 = t // dil
    k16 = q16 + 2 * half
    col = lax.rem(lax.broadcasted_iota(jnp.int32, (1, DIL_MERGE * k16), 1), k16)
    pen16 = (jnp.where((col < half) & first, NEG_INF, 0.0)
             + jnp.where((col >= q16 + half) & last, NEG_INF, 0.0))
    for g in range(dil // DIL_MERGE):
        res = range(g * DIL_MERGE, (g + 1) * DIL_MERGE)
        qrows = [pl.ds(r, q16, stride=dil) for r in res]
        krows = [pl.ds(t - half * dil + r, k16, stride=dil) for r in res]
        out = _band_group(jnp.concatenate([qf[rr, :] for rr in qrows], axis=0),
                          jnp.concatenate([kf[rr, :] for rr in krows], axis=0),
                          jnp.concatenate([vf[rr, :] for rr in krows], axis=0), b16_ref, pen16)
        for u, rr in enumerate(qrows):
            store((a16, m16, l16), rr, [x[u * q16:(u + 1) * q16] for x in out])

    mm = jnp.maximum(jnp.maximum(m1[...], m4[...]), m16[...])
    w1, w4, w16 = jnp.exp2(m1[...] - mm), jnp.exp2(m4[...] - mm), jnp.exp2(m16[...] - mm)
    num = w1 * a1[...] + w4 * a4[...] + w16 * a16[...]
    den = w1 * l1[...] + w4 * l4[...] + w16 * l16[...]
    o_ref[...] = (num / den).astype(o_ref.dtype)


def dilated_attention(z, t5_table, col0, width):
    s, zw = z.shape
    t = DIL_TILE
    assert s % t == 0 and col0 % LANE == 0 and width % LANE == 0 and t // 4 == DIL_QB
    nt = s // t
    nh = t5_table.shape[1]
    cq, ck, cv = ((col0 + k * width) // LANE for k in range(3))
    q16 = t // 16
    diags = [(_band_diag(t5_table, dil, qb) * LOG2E).reshape(nh // 2, 2, -1)
             for dil, qb in ((1, DIL_QB), (4, DIL_QB), (16, q16))]
    kl, k16 = DIL_QB + 2 * DIL_HALF, q16 + 2 * DIL_HALF

    def blk(col, shift):
        return pl.BlockSpec((t, LANE), lambda hg, i: (jnp.clip(i + shift, 0, nt - 1), col + hg))

    diag_spec = lambda d: pl.BlockSpec((None,) + d.shape[1:], lambda hg, i: (hg, 0, 0))
    return pl.pallas_call(
        _dil_kernel,
        out_shape=jax.ShapeDtypeStruct((s, width), BF16),
        grid=(width // LANE, nt),
        in_specs=[blk(cq, 0), blk(ck, -1), blk(ck, 0), blk(ck, 1), blk(cv, -1), blk(cv, 0), blk(cv, 1)]
                 + [diag_spec(d) for d in diags],
        out_specs=pl.BlockSpec((t, LANE), lambda hg, i: (i, hg)),
        scratch_shapes=[pltpu.VMEM((t, LANE), F32), pltpu.VMEM((3 * t, LANE), F32),
                        pltpu.VMEM((3 * t, LANE), F32)] + [pltpu.VMEM((t, LANE), F32)] * 9
                       + [pltpu.VMEM((2 * DIL_QB, kl), F32), pltpu.VMEM((2 * DIL_QB, kl), F32),
                          pltpu.VMEM((2 * DIL_MERGE * q16, DIL_MERGE * k16), F32)],
        compiler_params=_cp(("parallel", "arbitrary")),
        name="dilated_attention",
    )(*([z] * 7 + diags))


def _oproj_kernel(a1_ref, a2_ref, w_ref, x_ref, g_ref, gt_ref, o_ref):
    k1 = a1_ref.shape[1]
    y = jnp.dot(a1_ref[...], w_ref[0:k1, :], preferred_element_type=F32)
    y = y + jnp.dot(a2_ref[...], w_ref[k1:, :], preferred_element_type=F32)
    o_ref[...] = x_ref[...] + gt_ref[...] * (_rms(y) * g_ref[...])


def out_proj_residual(a1, a2, w, x, g, gt, tm=512):
    s, d = x.shape
    k1, k2 = a1.shape[1], a2.shape[1]
    tm = min(tm, s)
    vec = pl.BlockSpec((1, d), lambda i: (0, 0))
    return pl.pallas_call(
        _oproj_kernel,
        out_shape=jax.ShapeDtypeStruct((s, d), F32),
        grid=(s // tm,),
        in_specs=[pl.BlockSpec((tm, k1), lambda i: (i, 0)), pl.BlockSpec((tm, k2), lambda i: (i, 0)),
                  pl.BlockSpec((k1 + k2, d), lambda i: (0, 0)),
                  pl.BlockSpec((tm, d), lambda i: (i, 0)), vec, vec],
        out_specs=pl.BlockSpec((tm, d), lambda i: (i, 0)),
        compiler_params=_cp(("parallel",)),
        name="out_proj_residual",
    )(a1, a2, w, x, g, gt)


def _ffn_kernel(x_ref, g_ref, sh_ref, sc_ref, g2_ref, gt_ref, w13_ref, w2_ref, o_ref, h_ref):
    f = pl.program_id(1)
    tf = w2_ref.shape[0]

    @pl.when(f == 0)
    def _():
        h_ref[...] = _modnorm(x_ref[...], g_ref[...], sh_ref[...], sc_ref[...]).astype(h_ref.dtype)
        o_ref[...] = jnp.zeros_like(o_ref)

    uv = jnp.dot(h_ref[...], w13_ref[...], preferred_element_type=F32)
    u, v = uv[:, :tf], uv[:, tf:]
    o_ref[...] += jnp.dot((_silu(u) * v).astype(BF16), w2_ref[...], preferred_element_type=F32)

    @pl.when(f == pl.num_programs(1) - 1)
    def _():
        o_ref[...] = x_ref[...] + gt_ref[...] * (_rms(o_ref[...]) * g2_ref[...])


def ffn_sublayer(x, g, sh, sc, g2, gt, w1, w3, w2, tm=1024, tf=256):
    s, d = x.shape
    f = w1.shape[1]
    tm, tf = min(tm, s), min(tf, f)
    nf = f // tf
    w13 = jnp.stack([w1.reshape(d, nf, tf), w3.reshape(d, nf, tf)], axis=2).reshape(d, 2 * f)
    vec = pl.BlockSpec((1, d), lambda i, j: (0, 0))
    return pl.pallas_call(
        _ffn_kernel,
        out_shape=jax.ShapeDtypeStruct((s, d), F32),
        grid=(s // tm, nf),
        in_specs=[pl.BlockSpec((tm, d), lambda i, j: (i, 0)), vec, vec, vec, vec, vec,
                  pl.BlockSpec((d, 2 * tf), lambda i, j: (0, j)),
                  pl.BlockSpec((tf, d), lambda i, j: (j, 0))],
        out_specs=pl.BlockSpec((tm, d), lambda i, j: (i, 0)),
        scratch_shapes=[pltpu.VMEM((tm, d), BF16)],
        compiler_params=_cp(("parallel", "arbitrary")),
        name="ffn_sublayer",
    )(x, g, sh, sc, g2, gt, w13, w2)


def _rope_slab(t, c_ref, s1_ref, s2_ref):
    return (t * c_ref[...] + pltpu.roll(t, LANE - ROPE_DIM // 2, 1) * s1_ref[...]
            + pltpu.roll(t, ROPE_DIM // 2, 1) * s2_ref[...])


def _oddin_kernel(x_ref, g_ref, sh_ref, sc_ref, win_ref, qg_ref, wuq_ref, kvg_ref, wuk_ref, wuv_ref,
                  c_ref, s1_ref, s2_ref, q_ref, k_ref, v_ref, u_ref, *, q_lora, kv_lora, scale):
    h = _modnorm(x_ref[...], g_ref[...], sh_ref[...], sc_ref[...]).astype(BF16)
    z = jnp.dot(h, win_ref[...], preferred_element_type=F32)
    o_kv, o_pe, o_u = q_lora, q_lora + kv_lora, q_lora + kv_lora + LANE
    u_ref[...] = z[:, o_u:]

    qn = (_rms(z[:, :q_lora]) * qg_ref[...]).astype(BF16)
    q = jnp.dot(qn, wuq_ref[...], preferred_element_type=F32)
    kvn = (_rms(z[:, o_kv:o_pe]) * kvg_ref[...]).astype(BF16)
    kn = jnp.dot(kvn, wuk_ref[...], preferred_element_type=F32)
    v_ref[...] = jnp.dot(kvn, wuv_ref[...], preferred_element_type=F32).astype(v_ref.dtype)
    kpe = _rope_slab(z[:, o_pe:o_u], c_ref, s1_ref, s2_ref).astype(k_ref.dtype)

    for hh in range(MLA_HEADS):
        b = hh * MLA_QK_PAD
        q_ref[:, b:b + NOPE_DIM] = (q[:, b:b + NOPE_DIM] * scale).astype(q_ref.dtype)
        qpe = _rope_slab(q[:, b + NOPE_DIM:b + MLA_QK_PAD], c_ref, s1_ref, s2_ref)
        q_ref[:, b + NOPE_DIM:b + MLA_QK_PAD] = (qpe * scale).astype(q_ref.dtype)
        k_ref[:, b:b + NOPE_DIM] = kn[:, hh * NOPE_DIM:(hh + 1) * NOPE_DIM].astype(k_ref.dtype)
        k_ref[:, b + NOPE_DIM:b + MLA_QK_PAD] = kpe


def _rope_tables(seq):
    pos = jnp.arange(seq, dtype=F32)
    inv_freq = ROPE_BASE ** (-jnp.arange(0, ROPE_DIM, 2, dtype=F32) / ROPE_DIM)
    ang = pos[:, None] * inv_freq[None, :]
    cos, sin = jnp.cos(ang), jnp.sin(ang)
    zero = jnp.zeros_like(cos)
    c = jnp.concatenate([cos, cos, zero, zero], axis=-1)
    s1 = jnp.concatenate([-sin, zero, zero, zero], axis=-1)
    s2 = jnp.concatenate([zero, sin, zero, zero], axis=-1)
    return c, s1, s2


def odd_in_proj(x, g, sh, sc, w_in, q_norm_g, w_uq, kv_norm_g, w_ukv, tm=256):
    s, d = x.shape
    q_lora, kv_lora = q_norm_g.shape[0], kv_norm_g.shape[0]
    s5_dim = w_in.shape[1] - q_lora - kv_lora - ROPE_DIM
    tm = min(tm, s)
    w_in_p = jnp.concatenate(
        [w_in[:, :q_lora + kv_lora + ROPE_DIM], jnp.zeros((d, LANE - ROPE_DIM), w_in.dtype),
         w_in[:, q_lora + kv_lora + ROPE_DIM:]], axis=1).astype(BF16)
    wq = w_uq.reshape(q_lora, MLA_HEADS, NOPE_DIM + ROPE_DIM)
    wq = jnp.pad(wq, ((0, 0), (0, 0), (0, MLA_QK_PAD - NOPE_DIM - ROPE_DIM)))
    wq = wq.reshape(q_lora, MLA_HEADS * MLA_QK_PAD).astype(BF16)
    wkv = w_ukv.reshape(kv_lora, MLA_HEADS, NOPE_DIM + V_DIM)
    wuk = wkv[:, :, :NOPE_DIM].reshape(kv_lora, MLA_HEADS * NOPE_DIM).astype(BF16)
    wuv = wkv[:, :, NOPE_DIM:].reshape(kv_lora, MLA_HEADS * V_DIM).astype(BF16)
    c, s1, s2 = _rope_tables(s)
    scale = (NOPE_DIM + ROPE_DIM) ** -0.5 * LOG2E

    full = lambda a: pl.BlockSpec(a.shape, lambda i: (0, 0))
    row = lambda w: pl.BlockSpec((tm, w), lambda i: (i, 0))
    vec = pl.BlockSpec((1, d), lambda i: (0, 0))
    qg = q_norm_g.reshape(1, q_lora)
    kvg = kv_norm_g.reshape(1, kv_lora)
    return pl.pallas_call(
        functools.partial(_oddin_kernel, q_lora=q_lora, kv_lora=kv_lora, scale=scale),
        out_shape=[jax.ShapeDtypeStruct((s, MLA_HEADS * MLA_QK_PAD), BF16),
                   jax.ShapeDtypeStruct((s, MLA_HEADS * MLA_QK_PAD), BF16),
                   jax.ShapeDtypeStruct((s, MLA_HEADS * V_DIM), BF16),
                   jax.ShapeDtypeStruct((s, s5_dim), F32)],
        grid=(s // tm,),
        in_specs=[row(d), vec, vec, vec, full(w_in_p), full(qg), full(wq), full(kvg), full(wuk), full(wuv),
                  row(LANE), row(LANE), row(LANE)],
        out_specs=[row(MLA_HEADS * MLA_QK_PAD), row(MLA_HEADS * MLA_QK_PAD), row(MLA_HEADS * V_DIM),
                   row(s5_dim)],
        compiler_params=_cp(("parallel",)),
        name="odd_in_proj",
    )(x, g, sh, sc, w_in_p, qg, wq, kvg, wuk, wuv, c, s1, s2)


def _flash_kernel(q_ref, k_ref, v_ref, o_ref, m_ref, acc_ref, s0, s1, *, rows, nk):
    t = pl.program_id(0)
    j1 = lax.rem(jnp.maximum(t - 1, 0), nk)
    tq = q_ref.shape[0]
    vd = v_ref.shape[1]

    @pl.when(t == 0)
    def _():
        m_ref[...] = jnp.full_like(m_ref, -jnp.inf)
        acc_ref[...] = jnp.zeros_like(acc_ref)
        s1[...] = jnp.full_like(s1, -jnp.inf)

    def stages(s_rd, s_wr):
        v = v_ref[...]
        v1 = jnp.concatenate([v, jnp.ones_like(v)], axis=1)
        k = k_ref[...]
        live = t >= 1
        fresh = j1 == 0
        for c in range(tq // rows):
            rs = slice(c * rows, (c + 1) * rows)
            s = s_rd[rs, :]
            m_prev = jnp.where(fresh, -jnp.inf, m_ref[rs, :])
            m_cand = jnp.maximum(m_prev, jnp.max(s, axis=-1, keepdims=True))
            m_new = jnp.where(live, m_cand, m_prev)
            m_sub = jnp.where(live, m_cand, 0.0)
            p = jnp.exp2((s - m_sub[:, :1]).astype(BF16))
            alpha = jnp.where(live, jnp.exp2(m_prev - m_new), 1.0)
            m_ref[rs, :] = m_new
            pv = jnp.dot(p, v1, preferred_element_type=F32)
            acc_ref[rs, :vd] = alpha * acc_ref[rs, :vd] + pv[:, :vd]
            acc_ref[rs, vd:] = alpha * acc_ref[rs, vd:] + pv[:, vd:]
            s_wr[rs, :] = lax.dot_general(q_ref[rs, :], k, (((1,), (1,)), ((), ())),
                                          preferred_element_type=F32)

    parity = lax.rem(t, 2)

    @pl.when(parity == 0)
    def _():
        stages(s1, s0)

    @pl.when(parity == 1)
    def _():
        stages(s0, s1)

    @pl.when((t >= 1) & (j1 == nk - 1))
    def _():
        o_ref[...] = (acc_ref[:, :vd] / acc_ref[:, vd:]).astype(o_ref.dtype)


def mla_attention(q, k, v, tq=2048, tk=2048, rows=512):
    s = q.shape[0]
    tq, tk = min(tq, s), min(tk, s)
    rows = min(rows, tq)
    ni, nk = s // tq, s // tk
    n = MLA_HEADS * ni * nk

    def item(t, lag):
        w = jnp.clip(t - lag, 0, n - 1)
        return w // (ni * nk), lax.rem(w, ni * nk) // nk, lax.rem(w, nk)

    def q_map(t):
        h, i, _ = item(t, 0)
        return i, h

    def k_map(t):
        h, _, j = item(t, 0)
        return j, h

    def v_map(t):
        h, _, j = item(t, 1)
        return j, h

    def o_map(t):
        h, i, _ = item(t, 1)
        return i, h

    return pl.pallas_call(
        functools.partial(_flash_kernel, rows=rows, nk=nk),
        out_shape=jax.ShapeDtypeStruct((s, MLA_HEADS * V_DIM), BF16),
        grid=(n + 1,),
        in_specs=[pl.BlockSpec((tq, MLA_QK_PAD), q_map),
                  pl.BlockSpec((tk, MLA_QK_PAD), k_map),
                  pl.BlockSpec((tk, V_DIM), v_map)],
        out_specs=pl.BlockSpec((tq, V_DIM), o_map),
        scratch_shapes=[pltpu.VMEM((tq, V_DIM), F32), pltpu.VMEM((tq, 2 * V_DIM), F32),
                        pltpu.VMEM((tq, tk), F32), pltpu.VMEM((tq, tk), F32)],
        compiler_params=_cp(("arbitrary",)),
        name="mla_flash_attention",
    )(q, k, v)


def _s5_matrices(a_re, a_im, log_dt, b_re, b_im, c_re, c_im, chunk):
    L = chunk
    A = lax.complex(a_re.astype(F32), a_im.astype(F32))
    dt = jnp.exp(log_dt.astype(F32))[..., None]
    adt = A * dt
    a_bar = jnp.exp(adt)
    b_bar = ((a_bar - 1.0) / A)[..., None] * lax.complex(b_re.astype(F32), b_im.astype(F32))
    c_c = lax.complex(c_re.astype(F32), c_im.astype(F32))
    kk = jnp.arange(L + 1, dtype=F32)
    apow = jnp.exp(adt[:, :, None, :] * kk[None, None, :, None].astype(jnp.complex64))
    g, p, gc = b_bar.shape[1], b_bar.shape[2], b_bar.shape[3]

    ker = jnp.real(jnp.einsum('dgcp,dgkp,dgpi->dgkci', c_c, apow[:, :, :L], b_bar))
    kb = ker[1][:, ::-1]
    diag = jnp.concatenate([kb[:, :L - 1], ker[0][:, :1] + kb[:, L - 1:], ker[0][:, 1:]], axis=1)
    t_mat = _toeplitz(jnp.transpose(diag, (0, 2, 3, 1)), L, L)
    t_mat = jnp.transpose(t_mat, (0, 3, 2, 4, 1)).reshape(g, L * gc, L * gc)

    pf = apow[0][:, ::-1][:, 1:][:, :, :, None] * b_bar[0][:, None]
    pb = apow[1][:, :L][:, :, :, None] * b_bar[1][:, None]
    def p_lay(m):
        return jnp.transpose(m, (0, 1, 3, 2)).reshape(g, L * gc, p)
    p_mat = jnp.concatenate([p_lay(jnp.real(pf)), p_lay(jnp.imag(pf)),
                             p_lay(jnp.real(pb)), p_lay(jnp.imag(pb))], axis=-1)

    wf = c_c[0][:, None] * apow[0][:, 1:][:, :, None, :]
    wb = c_c[1][:, None] * apow[1][:, ::-1][:, :L][:, :, None, :]
    def q_lay(m):
        return jnp.transpose(m, (0, 3, 1, 2)).reshape(g, p, L * gc)
    q_mat = jnp.concatenate([q_lay(jnp.real(wf)), q_lay(-jnp.imag(wf)),
                             q_lay(jnp.real(wb)), q_lay(-jnp.imag(wb))], axis=1)

    al = apow[:, :, L]
    dec = jnp.stack([jnp.real(al[0]), jnp.imag(al[0]), jnp.real(al[1]), jnp.imag(al[1])]).reshape(4, g * p)
    return t_mat, p_mat, q_mat, dec


def _s5_state_kernel(u_ref, p_ref, fre_ref, fim_ref, bre_ref, bim_ref):
    outs = [[], [], [], []]
    for gi in range(u_ref.shape[0]):
        xe = jnp.dot(u_ref[gi].astype(BF16), p_ref[gi], preferred_element_type=F32)
        for part in range(4):
            outs[part].append(xe[:, part * S5_STATE:(part + 1) * S5_STATE])
    for part, ref in enumerate((fre_ref, fim_ref, bre_ref, bim_ref)):
        ref[...] = jnp.concatenate(outs[part], axis=-1)


def _s5_scan_kernel(fre_ref, fim_ref, bre_ref, bim_ref, dec_ref, ofre_ref, ofim_ref, obre_ref, obim_ref):
    nc = fre_ref.shape[0]
    w = fre_ref.shape[1]
    far, fai, bar, bai = (dec_ref[i:i + 1, :] for i in range(4))

    def fwd(c, st):
        re, im = st
        ofre_ref[pl.ds(c, 1), :] = re
        ofim_ref[pl.ds(c, 1), :] = im
        return (far * re - fai * im + fre_ref[pl.ds(c, 1), :],
                far * im + fai * re + fim_ref[pl.ds(c, 1), :])

    def bwd(i, st):
        c = nc - 1 - i
        re, im = st
        obre_ref[pl.ds(c, 1), :] = re
        obim_ref[pl.ds(c, 1), :] = im
        return (bar * re - bai * im + bre_ref[pl.ds(c, 1), :],
                bar * im + bai * re + bim_ref[pl.ds(c, 1), :])

    zero = (jnp.zeros((1, w), F32), jnp.zeros((1, w), F32))
    lax.fori_loop(0, nc, fwd, zero)
    lax.fori_loop(0, nc, bwd, zero)


def _s5_out_kernel(u_ref, t_ref, q_ref, fre_ref, fim_ref, bre_ref, bim_ref, y_ref):
    for gi in range(u_ref.shape[0]):
        lanes = slice(gi * S5_STATE, (gi + 1) * S5_STATE)
        y = jnp.dot(u_ref[gi].astype(BF16), t_ref[gi], preferred_element_type=F32)
        xin = jnp.concatenate([r[:, lanes] for r in (fre_ref, fim_ref, bre_ref, bim_ref)], axis=-1)
        y_ref[gi] = y + jnp.dot(xin.astype(BF16), q_ref[gi], preferred_element_type=F32)


def s5_bidirectional(u, a_re, a_im, log_dt, b_re, b_im, c_re, c_im, chunk=S5_CHUNK):
    s, dim = u.shape
    gc, p = S5_GROUP, S5_STATE
    g = dim // gc
    chunk = min(chunk, s)
    nc = s // chunk
    gp = LANE // p
    t_mat, p_mat, q_mat, dec = _s5_matrices(a_re, a_im, log_dt, b_re, b_im, c_re, c_im, chunk)
    t_mat, p_mat, q_mat = t_mat.astype(BF16), p_mat.astype(BF16), q_mat.astype(BF16)
    u_t = jnp.transpose(u.astype(BF16).reshape(nc, chunk, g, gc), (2, 0, 1, 3)).reshape(g, nc, chunk * gc)

    grp = lambda a, b: pl.BlockSpec((gp, a, b), lambda i: (i, 0, 0))
    st = pl.BlockSpec((nc, LANE), lambda i: (0, i))
    st_shape = jax.ShapeDtypeStruct((nc, g * p), F32)
    xe = pl.pallas_call(
        _s5_state_kernel,
        out_shape=[st_shape] * 4,
        grid=(g // gp,),
        in_specs=[grp(nc, chunk * gc), grp(chunk * gc, 4 * p)],
        out_specs=[st] * 4,
        compiler_params=_cp(("parallel",)),
        name="s5_chunk_states",
    )(u_t, p_mat)

    lanes = min(4 * LANE, g * p)
    cols = lambda rows: pl.BlockSpec((rows, lanes), lambda i: (0, i))
    xin = pl.pallas_call(
        _s5_scan_kernel,
        out_shape=[st_shape] * 4,
        grid=(g * p // lanes,),
        in_specs=[cols(nc)] * 4 + [cols(4)],
        out_specs=[cols(nc)] * 4,
        compiler_params=_cp(("parallel",)),
        name="s5_boundary_scan",
    )(*xe, dec)

    y_t = pl.pallas_call(
        _s5_out_kernel,
        out_shape=jax.ShapeDtypeStruct((g, nc, chunk * gc), F32),
        grid=(g // gp,),
        in_specs=[grp(nc, chunk * gc), grp(chunk * gc, chunk * gc), grp(4 * p, chunk * gc)] + [st] * 4,
        out_specs=grp(nc, chunk * gc),
        compiler_params=_cp(("parallel",)),
        name="s5_outputs",
    )(u_t, t_mat, q_mat, *xin)
    return jnp.transpose(y_t.reshape(g, nc, chunk, gc), (1, 2, 0, 3)).reshape(s, dim)


def _s5_gate_kernel(y_ref, u_ref, d_ref, w_ref, o_ref):
    y = y_ref[...] + d_ref[...] * u_ref[...]
    y = 0.5 * y * (1.0 + jnp.tanh(math.sqrt(2.0 / math.pi) * (y + 0.044715 * (y * y * y))))
    z = jnp.dot(y.astype(BF16), w_ref[...], preferred_element_type=F32)
    o_ref[...] = (y * _sigmoid(z)).astype(o_ref.dtype)


def s5_gate(y, u, d_skip, w_glu, tm=1024):
    s, dim = y.shape
    tm = min(tm, s)
    row = pl.BlockSpec((tm, dim), lambda i: (i, 0))
    return pl.pallas_call(
        _s5_gate_kernel,
        out_shape=jax.ShapeDtypeStruct((s, dim), BF16),
        grid=(s // tm,),
        in_specs=[row, row, pl.BlockSpec((1, dim), lambda i: (0, 0)), pl.BlockSpec((dim, dim), lambda i: (0, 0))],
        out_specs=row,
        compiler_params=_cp(("parallel",)),
        name="s5_gate",
    )(y, u, d_skip.reshape(1, dim), w_glu.astype(BF16))


def _router_kernel(x_ref, g_ref, sh_ref, sc_ref, rw_ref, h_ref, info_ref):
    h = _modnorm(x_ref[...], g_ref[...], sh_ref[...], sc_ref[...])
    half = h.shape[1] // 2
    bits = lambda v: lax.bitcast_convert_type(v.astype(BF16).astype(F32), jnp.uint32)
    h_ref[...] = (bits(h[:, :half]) >> 16) | (bits(h[:, half:]) & jnp.uint32(0xFFFF0000))
    logits = jnp.dot(h, rw_ref[...], preferred_element_type=F32, precision=lax.Precision.HIGHEST)
    lane = lax.broadcasted_iota(jnp.int32, logits.shape, 1)
    logits = jnp.where(lane < N_EXPERTS, logits, -jnp.inf)
    m1 = jnp.max(logits, axis=-1, keepdims=True)
    i1 = jnp.min(jnp.where(logits == m1, lane, LANE), axis=-1, keepdims=True)
    rest = jnp.where(lane == i1, -jnp.inf, logits)
    m2 = jnp.max(rest, axis=-1, keepdims=True)
    i2 = jnp.min(jnp.where(rest == m2, lane, LANE), axis=-1, keepdims=True)
    e = jnp.exp(m2 - m1)
    g1 = 1.0 / (1.0 + e)
    g2 = e / (1.0 + e)
    info = jnp.where(lane == 0, i1.astype(F32),
                     jnp.where(lane == 1, i2.astype(F32),
                               jnp.where(lane == 2, g1, jnp.where(lane == 3, g2, 0.0))))
    info_ref[...] = info


def moe_router(x, g, sh, sc, router_w, tm=512):
    s, d = x.shape
    tm = min(tm, s)
    rw = jnp.pad(router_w, ((0, 0), (0, LANE - router_w.shape[1])))
    vec = pl.BlockSpec((1, d), lambda i: (0, 0))
    return pl.pallas_call(
        _router_kernel,
        out_shape=[jax.ShapeDtypeStruct((s, d // 2), jnp.uint32), jax.ShapeDtypeStruct((s, LANE), F32)],
        grid=(s // tm,),
        in_specs=[pl.BlockSpec((tm, d), lambda i: (i, 0)), vec, vec, vec,
                  pl.BlockSpec((d, LANE), lambda i: (0, 0))],
        out_specs=[pl.BlockSpec((tm, d // 2), lambda i: (i, 0)), pl.BlockSpec((tm, LANE), lambda i: (i, 0))],
        compiler_params=_cp(("parallel",)),
        name="moe_router",
    )(x, g, sh, sc, rw)


def _expert_kernel(te_ref, tv_ref, src_ref, hp_hbm, w1_ref, w3_ref, w2_ref, o_ref, acc_ref, gbuf, h_ref, sem):
    t = pl.program_id(0)
    f = pl.program_id(1)
    nt = pl.num_programs(0)
    nf = pl.num_programs(1)
    tm = acc_ref.shape[0]
    per_step = tm // nf
    valid = tv_ref[t] > 0
    slot = lax.rem(t, 2)

    def row_copy(tile, buf, r):
        return pltpu.make_async_copy(hp_hbm.at[pl.ds(src_ref[tile * tm + r], 1)],
                                     gbuf.at[buf, pl.ds(r, 1)], sem.at[buf])

    def wait_half(buf):
        pltpu.make_async_copy(hp_hbm.at[pl.ds(0, tm)], gbuf.at[buf], sem.at[buf]).wait()

    @pl.when((t == 0) & (f == 0))
    def _():
        def issue(r, carry):
            row_copy(0, 0, r).start()
            return carry
        lax.fori_loop(0, tm, issue, 0)

    @pl.when((f == 0) & ((t == 0) | (tv_ref[jnp.maximum(t - 1, 0)] > 0)))
    def _():
        wait_half(slot)

    @pl.when((f == 0) & valid)
    def _():
        w = gbuf[slot]
        half = w.shape[1]
        h_ref[:, :half] = lax.bitcast_convert_type(w << 16, F32).astype(BF16)
        h_ref[:, half:] = lax.bitcast_convert_type(w & jnp.uint32(0xFFFF0000), F32).astype(BF16)

    @pl.when(f == 0)
    def _():
        acc_ref[...] = jnp.zeros_like(acc_ref)

    def compute(rows):
        for u in range(per_step):
            row_copy(t + 1, 1 - slot, f * per_step + u).start()
        h = h_ref[:rows, :]
        u = jnp.dot(h, w1_ref[...].astype(BF16), preferred_element_type=F32)
        v = jnp.dot(h, w3_ref[...].astype(BF16), preferred_element_type=F32)
        acc_ref[:rows, :] += jnp.dot((_silu(u) * v).astype(BF16), w2_ref[...].astype(BF16),
                                     preferred_element_type=F32)

    @pl.when(tv_ref[t] > tm // 2)
    def _():
        compute(tm)

    @pl.when(valid & (tv_ref[t] <= tm // 2))
    def _():
        compute(tm // 2)

    @pl.when(f == nf - 1)
    def _():
        o_ref[...] = acc_ref[...].astype(o_ref.dtype)

    @pl.when((t == nt - 1) & (f == nf - 1) & valid)
    def _():
        wait_half(1 - slot)


def expert_ffn(h_packed, src, tile_expert, tile_valid, w1, w3, w2, tm, tf=512):
    dh = h_packed.shape[1]
    d = 2 * dh
    n = src.shape[0] - tm
    f = w1.shape[2]
    tf = min(tf, f)
    nf = f // tf
    assert tm % nf == 0 and n % tm == 0

    def fblk(j, t, tv):
        return jnp.where(tv[t] > 0, j, nf - 1)

    return pl.pallas_call(
        _expert_kernel,
        out_shape=jax.ShapeDtypeStruct((n, d), BF16),
        grid_spec=pltpu.PrefetchScalarGridSpec(
            num_scalar_prefetch=3,
            grid=(n // tm, nf),
            in_specs=[pl.BlockSpec(memory_space=pl.ANY),
                      pl.BlockSpec((None, d, tf), lambda t, j, te, tv, sr: (te[t], 0, fblk(j, t, tv))),
                      pl.BlockSpec((None, d, tf), lambda t, j, te, tv, sr: (te[t], 0, fblk(j, t, tv))),
                      pl.BlockSpec((None, tf, d), lambda t, j, te, tv, sr: (te[t], fblk(j, t, tv), 0))],
            out_specs=pl.BlockSpec((tm, d), lambda t, j, te, tv, sr: (t, 0)),
            scratch_shapes=[pltpu.VMEM((tm, d), F32), pltpu.VMEM((2, tm, dh), jnp.uint32),
                            pltpu.VMEM((tm, d), BF16), pltpu.SemaphoreType.DMA((2,))]),
        compiler_params=_cp(("arbitrary", "arbitrary")),
        name="expert_ffn",
    )(tile_expert, tile_valid, src, h_packed, w1, w3, w2)


def _dispatch(idx, tm):
    s = idx.shape[0]
    e_flat = idx.reshape(-1)
    onehot = (e_flat[:, None] == jnp.arange(N_EXPERTS)[None, :]).astype(jnp.int32)
    csum = jnp.cumsum(onehot, axis=0)
    rank = jnp.sum((csum - onehot) * onehot, axis=1)
    counts = csum[-1]
    padded = ((counts + tm - 1) // tm) * tm
    gend = jnp.cumsum(padded)
    slot = (gend - padded)[e_flat] + rank
    nt = -(-(2 * s) // tm) + N_EXPERTS
    src = jnp.zeros(((nt + 1) * tm,), jnp.int32).at[slot].set(jnp.arange(2 * s, dtype=jnp.int32) // 2)
    tstart = jnp.arange(nt, dtype=jnp.int32) * tm
    valid = tstart < gend[-1]
    te = jnp.minimum(jnp.sum((tstart[:, None] >= gend[None, :]).astype(jnp.int32), axis=1), N_EXPERTS - 1)
    nvalid = gend[-1] // tm
    fill = jnp.clip((gend - padded + counts)[te] - tstart, 0, tm)
    fill = jnp.where(valid, fill, 0)
    te = jnp.where(valid, te, te[jnp.maximum(nvalid - 1, 0)])
    return slot.reshape(s, 2), src, te.astype(jnp.int32), fill.astype(jnp.int32)


def _combine_kernel(ya_ref, yb_ref, info_ref, x_ref, g_ref, gt_ref, o_ref):
    info = info_ref[...]
    y = info[:, 2:3] * ya_ref[...].astype(F32) + info[:, 3:4] * yb_ref[...].astype(F32)
    o_ref[...] = x_ref[...] + gt_ref[...] * (_rms(y) * g_ref[...])


def moe_combine(ya, yb, info, x, g, gt, tm=512):
    s, d = x.shape
    tm = min(tm, s)
    row = lambda w: pl.BlockSpec((tm, w), lambda i: (i, 0))
    vec = pl.BlockSpec((1, d), lambda i: (0, 0))
    return pl.pallas_call(
        _combine_kernel,
        out_shape=jax.ShapeDtypeStruct((s, d), F32),
        grid=(s // tm,),
        in_specs=[row(d), row(d), row(LANE), row(d), vec, vec],
        out_specs=row(d),
        compiler_params=_cp(("parallel",)),
        name="moe_combine",
    )(ya, yb, info, x, g, gt)


def moe_sublayer(x, g, sh, sc, g2, gt, router_w, w1, w3, w2, tm_e=EXPERT_TILE):
    s = x.shape[0]
    tm_e = min(tm_e, s)
    h_packed, info = moe_router(x, g, sh, sc, router_w)
    idx = info[:, :2].astype(jnp.int32)
    slot, src, te, tv = _dispatch(idx, tm_e)
    y_sorted = expert_ffn(h_packed, src, te, tv, w1, w3, w2, tm_e)
    ya = jnp.take(y_sorted, slot[:, 0], axis=0)
    yb = jnp.take(y_sorted, slot[:, 1], axis=0)
    return moe_combine(ya, yb, info, x, g2, gt)


def kernel(x, c, t5_table, norm_g, ada_w, ada_b, e_w_in, e_conv_w, e_conv_b, e_cln_g, e_cln_b, e_w_out, e_ffn_w1, e_ffn_w3, e_ffn_w2, o_w_in, o_q_norm_g, o_w_uq, o_kv_norm_g, o_w_ukv, s5_a_re, s5_a_im, s5_log_dt, s5_b_re, s5_b_im, s5_c_re, s5_c_im, s5_d, s5_w_glu, o_w_out, router_w, moe_w1, moe_w3, moe_w2):
    bsz, seq, d = x.shape
    assert bsz == 1
    depth = norm_g.shape[0]
    conv_dim = e_conv_w.shape[2]
    xs = x.reshape(seq, d)
    mod = adaln(c, ada_w, ada_b)
    for layer in range(depth):
        i = layer // 2
        sh_m, sc_m, gt_m, sh_f, sc_f, gt_f = (mod[layer, :, k * d:(k + 1) * d] for k in range(6))
        g = [norm_g[layer, k].reshape(1, d) for k in range(4)]
        if layer % 2 == 0:
            z = norm_mod_matmul(xs, g[0], sh_m, sc_m, e_w_in[i].astype(BF16))
            a = conformer_conv(z, e_conv_w[i], e_conv_b[i], e_cln_g[i], e_cln_b[i])
            o = dilated_attention(z, t5_table, 2 * conv_dim, (z.shape[1] - 2 * conv_dim) // 3)
            xs = out_proj_residual(a, o, e_w_out[i].astype(BF16), xs, g[1], gt_m)
            xs = ffn_sublayer(xs, g[2], sh_f, sc_f, g[3], gt_f, e_ffn_w1[i].astype(BF16),
                              e_ffn_w3[i].astype(BF16), e_ffn_w2[i].astype(BF16))
        else:
            q, k, v, u = odd_in_proj(xs, g[0], sh_m, sc_m, o_w_in[i], o_q_norm_g[i], o_w_uq[i],
                                     o_kv_norm_g[i], o_w_ukv[i])
            o_mla = mla_attention(q, k, v)
            y = s5_bidirectional(u, s5_a_re[i], s5_a_im[i], s5_log_dt[i], s5_b_re[i], s5_b_im[i],
                                 s5_c_re[i], s5_c_im[i])
            y = s5_gate(y, u, s5_d[i], s5_w_glu[i])
            xs = out_proj_residual(o_mla, y, o_w_out[i].astype(BF16), xs, g[1], gt_m)
            xs = moe_sublayer(xs, g[2], sh_f, sc_f, g[3], gt_f, router_w[i], moe_w1[i], moe_w3[i], moe_w2[i])
    return xs.reshape(bsz, seq, d)
```

```python
import functools
import math

import jax
import jax.numpy as jnp
from jax import lax
from jax.experimental import pallas as pl
from jax.experimental.pallas import tpu as pltpu

F32 = jnp.float32
BF16 = jnp.bfloat16

RMS_EPS = 1e-6
LN_EPS = 1e-5
NEG_INF = -1e30
LOG2E = math.log2(math.e)

CONV_WIDTH = 31
CONV_HALO = 16
DIL_HEAD_DIM = 64
DIL_CONFIGS = ((128, 1), (512, 4), (2048, 16))
DIL_HALF = 64
DIL_TILE = 1024
DIL_QB = 256
DIL_MERGE = 2
N_BUCKETS = 32
T5_MAX_DIST = DIL_CONFIGS[-1][0] // 2
MLA_HEADS = 12
NOPE_DIM = 128
ROPE_DIM = 64
V_DIM = 128
ROPE_BASE = 10000.0
MLA_QK_PAD = 256
S5_GROUP = 16
S5_STATE = 64
S5_CHUNK = 16
N_EXPERTS = 8
EXPERT_TILE = 896
LANE = 128
SUBLANES = 8
VMEM_LIMIT_MB = 56


def _cp(sem, vmem_mb=VMEM_LIMIT_MB):
    return pltpu.CompilerParams(dimension_semantics=sem, vmem_limit_bytes=vmem_mb * 1024 * 1024)


def _rms(x):
    return x * lax.rsqrt(jnp.mean(x * x, axis=-1, keepdims=True) + RMS_EPS)


def _modnorm(x, g, sh, sc):
    return (_rms(x) * g) * (1.0 + sc) + sh


def _sigmoid(x):
    return 1.0 / (1.0 + jnp.exp(-x))


def _silu(x):
    return x * _sigmoid(x)


def _adaln_kernel(c_ref, w_ref, b_ref, o_ref, cb_ref):
    @pl.when((pl.program_id(0) == 0) & (pl.program_id(1) == 0))
    def _():
        cc = c_ref[...]
        cb_ref[...] = jnp.broadcast_to(_silu(cc), cb_ref.shape)

    cb = cb_ref[...]
    for j in range(o_ref.shape[-1] // LANE):
        sl = slice(j * LANE, (j + 1) * LANE)
        o_ref[:, sl] = jnp.sum(w_ref[:, sl] * cb, axis=0, keepdims=True) + b_ref[:, sl]


def adaln(c, ada_w, ada_b, tn=1024):
    nl, d, n = ada_w.shape
    out = pl.pallas_call(
        _adaln_kernel,
        out_shape=jax.ShapeDtypeStruct((nl, 1, n), F32),
        grid=(nl, n // tn),
        in_specs=[pl.BlockSpec((d, 1), lambda l, j: (0, 0)),
                  pl.BlockSpec((None, d, tn), lambda l, j: (l, 0, j)),
                  pl.BlockSpec((None, 1, tn), lambda l, j: (l, 0, j))],
        out_specs=pl.BlockSpec((None, 1, tn), lambda l, j: (l, 0, j)),
        scratch_shapes=[pltpu.VMEM((d, LANE), F32)],
        compiler_params=_cp(("arbitrary", "arbitrary")),
        name="adaln",
    )(c.reshape(d, 1), ada_w, ada_b.reshape(nl, 1, n))
    return out


def _nmm_kernel(x_ref, g_ref, sh_ref, sc_ref, w_ref, o_ref, h_ref):
    @pl.when(pl.program_id(1) == 0)
    def _():
        h_ref[...] = _modnorm(x_ref[...], g_ref[...], sh_ref[...], sc_ref[...]).astype(h_ref.dtype)

    o_ref[...] = jnp.dot(h_ref[...], w_ref[...], preferred_element_type=F32).astype(o_ref.dtype)


def norm_mod_matmul(x, g, sh, sc, w, tm=1024, tn=1024):
    s, d = x.shape
    n = w.shape[1]
    tm, tn = min(tm, s), min(tn, n)
    vec = pl.BlockSpec((1, d), lambda i, j: (0, 0))
    return pl.pallas_call(
        _nmm_kernel,
        out_shape=jax.ShapeDtypeStruct((s, n), BF16),
        grid=(s // tm, n // tn),
        in_specs=[pl.BlockSpec((tm, d), lambda i, j: (i, 0)), vec, vec, vec,
                  pl.BlockSpec((d, tn), lambda i, j: (0, j))],
        out_specs=pl.BlockSpec((tm, tn), lambda i, j: (i, j)),
        scratch_shapes=[pltpu.VMEM((tm, d), BF16)],
        compiler_params=_cp(("parallel", "arbitrary")),
        name="norm_mod_matmul",
    )(x, g, sh, sc, w)


def _conv_kernel(av_ref, ag_ref, avp_ref, agp_ref, avn_ref, agn_ref, w_ref, b_ref, lg_ref, lb_ref,
                 o_ref, buf_ref, sh_ref, *, rows):
    i = pl.program_id(0)
    n = pl.num_programs(0)
    ts = av_ref.shape[0]

    def glu(v_ref, g_ref):
        return v_ref[...].astype(F32) * _sigmoid(g_ref[...].astype(F32))

    buf_ref[0:CONV_HALO, :] = jnp.where(i > 0, glu(avp_ref, agp_ref), 0.0)
    buf_ref[CONV_HALO:CONV_HALO + ts, :] = glu(av_ref, ag_ref)
    buf_ref[CONV_HALO + ts:2 * CONV_HALO + ts, :] = jnp.where(i < n - 1, glu(avn_ref, agn_ref), 0.0)
    buf_ref[2 * CONV_HALO + ts:, :] = jnp.zeros((SUBLANES, buf_ref.shape[1]), F32)

    span = ts + 2 * CONV_HALO
    for o in range(SUBLANES):
        sh_ref[o] = buf_ref[o:o + span, :]

    off = CONV_HALO - CONV_WIDTH // 2
    for r in range(ts // rows):
        acc = jnp.broadcast_to(b_ref[...], (rows, b_ref.shape[-1]))
        for k in range(CONV_WIDTH):
            lo = r * rows + k + off
            base = lo - lo % SUBLANES
            acc = acc + w_ref[k:k + 1, :] * sh_ref[lo % SUBLANES, base:base + rows, :]
        mu = jnp.mean(acc, axis=-1, keepdims=True)
        dlt = acc - mu
        var = jnp.mean(dlt * dlt, axis=-1, keepdims=True)
        y = dlt * lax.rsqrt(var + LN_EPS) * lg_ref[...] + lb_ref[...]
        o_ref[r * rows:(r + 1) * rows, :] = _silu(y).astype(o_ref.dtype)


def conformer_conv(z, conv_w, conv_b, cln_g, cln_b, ts=256, rows=32):
    s = z.shape[0]
    c = conv_w.shape[1]
    ts = min(ts, s)
    hb = ts // CONV_HALO
    nh = s // CONV_HALO
    main = lambda col: pl.BlockSpec((ts, c), lambda i: (i, col))
    prev = lambda col: pl.BlockSpec((CONV_HALO, c), lambda i: (jnp.maximum(i * hb - 1, 0), col))
    nxt = lambda col: pl.BlockSpec((CONV_HALO, c), lambda i: (jnp.minimum((i + 1) * hb, nh - 1), col))
    vec = lambda r: pl.BlockSpec((r, c), lambda i: (0, 0))
    return pl.pallas_call(
        functools.partial(_conv_kernel, rows=min(rows, ts)),
        out_shape=jax.ShapeDtypeStruct((s, c), BF16),
        grid=(s // ts,),
        in_specs=[main(0), main(1), prev(0), prev(1), nxt(0), nxt(1),
                  vec(CONV_WIDTH), vec(1), vec(1), vec(1)],
        out_specs=pl.BlockSpec((ts, c), lambda i: (i, 0)),
        scratch_shapes=[pltpu.VMEM((ts + 2 * CONV_HALO + SUBLANES, c), F32),
                        pltpu.VMEM((SUBLANES, ts + 2 * CONV_HALO, c), F32)],
        compiler_params=_cp(("parallel",)),
        name="conformer_conv",
    )(z, z, z, z, z, z, conv_w, conv_b.reshape(1, c), cln_g.reshape(1, c), cln_b.reshape(1, c))


def _t5_bucket(rel):
    half = N_BUCKETS // 2
    exact = half // 2
    n = jnp.abs(rel)
    large = exact + (jnp.log(jnp.maximum(n, 1).astype(F32) / exact)
                     / math.log(T5_MAX_DIST / exact) * (half - exact)).astype(jnp.int32)
    large = jnp.minimum(large, half - 1)
    return jnp.where(rel > 0, half, 0) + jnp.where(n < exact, n, large)


def _toeplitz(w, rows, cols):
    n = w.shape[-1]
    wp = jnp.concatenate([w, jnp.zeros(w.shape[:-1] + (1,), w.dtype)], axis=-1)
    flat = jnp.tile(wp, (1,) * (w.ndim - 1) + (rows,))[..., :rows * n]
    return flat.reshape(w.shape[:-1] + (rows, n))[..., rows - 1:rows - 1 + cols]


def _band_diag(t5_table, dil, qb):
    kl = qb + 2 * DIL_HALF
    n = -(-(qb + kl - 1) // LANE) * LANE
    dist = jnp.arange(n) - (qb - 1) - DIL_HALF
    return jnp.where((jnp.abs(dist) <= DIL_HALF)[None], t5_table[_t5_bucket(dist * dil)].T.astype(F32), NEG_INF)


def _band_bias_rows(diag_row, qb):
    n = diag_row.shape[1]
    rolled = pltpu.roll(jnp.broadcast_to(diag_row, (qb, n)), n - (qb - 1), 1, stride=1, stride_axis=0)
    return rolled[:, :qb + 2 * DIL_HALF]


def _band_group(q, kw, vw, bias_ref, pen):
    r = q.shape[0]
    is_a = lax.broadcasted_iota(jnp.int32, (1, LANE), 1) < DIL_HEAD_DIM
    q2 = jnp.concatenate([jnp.where(is_a, q, 0.0), jnp.where(is_a, 0.0, q)], axis=0).astype(BF16)
    s = lax.dot_general(q2, kw.astype(BF16), (((1,), (1,)), ((), ())), preferred_element_type=F32)
    s = s + bias_ref[...]
    if pen is not None:
        s = s + pen
    m = jnp.max(s, axis=-1, keepdims=True)
    p = jnp.exp2((s - m).astype(BF16))
    vb = vw.astype(BF16)
    pv = jnp.dot(p, jnp.concatenate([vb, jnp.ones_like(vb)], axis=1), preferred_element_type=F32)
    pick = lambda x: jnp.where(is_a, x[:r], x[r:])
    return pick(pv[:, :LANE]), pick(m), pick(pv[:, LANE:])


def _dil_kernel(q_ref, kp_ref, kc_ref, kn_ref, vp_ref, vc_ref, vn_ref, d1_ref, d4_ref, d16_ref, o_ref,
                qf, kf, vf, a1, m1, l1, a4, m4, l4, a16, m16, l16, b1_ref, b4_ref, b16_ref):
    i = pl.program_id(1)
    first = i == 0
    last = i == pl.num_programs(1) - 1
    t = DIL_TILE
    half = DIL_HALF

    @pl.when(first)
    def _():
        q16 = t // 16
        b16_ref[...] = jnp.full(b16_ref.shape, NEG_INF, F32)
        for h in range(2):
            b1_ref[h * DIL_QB:(h + 1) * DIL_QB, :] = _band_bias_rows(d1_ref[h:h + 1, :], DIL_QB)
            b4_ref[h * DIL_QB:(h + 1) * DIL_QB, :] = _band_bias_rows(d4_ref[h:h + 1, :], DIL_QB)
            blk = _band_bias_rows(d16_ref[h:h + 1, :], q16)
            for u in range(DIL_MERGE):
                r0 = (h * DIL_MERGE + u) * q16
                b16_ref[r0:r0 + q16, u * blk.shape[1]:(u + 1) * blk.shape[1]] = blk

    qf[...] = q_ref[...].astype(F32) * (DIL_HEAD_DIM ** -0.5 * LOG2E)
    for dst, (p_ref, c_ref, n_ref) in ((kf, (kp_ref, kc_ref, kn_ref)), (vf, (vp_ref, vc_ref, vn_ref))):
        dst[0:t, :] = p_ref[...].astype(F32)
        dst[t:2 * t, :] = c_ref[...].astype(F32)
        dst[2 * t:3 * t, :] = n_ref[...].astype(F32)

    def store(refs, rows, vals):
        for ref, val in zip(refs, vals):
            ref[rows, :] = val

    qb = DIL_QB
    kl = qb + 2 * half
    col = lax.broadcasted_iota(jnp.int32, (1, kl), 1)
    lo = jnp.where((col < half) & first, NEG_INF, 0.0)
    hi = jnp.where((col >= qb + half) & last, NEG_INF, 0.0)

    nb = t // qb
    for b in range(nb):
        pen = lo + hi if nb == 1 else (lo if b == 0 else (hi if b == nb - 1 else None))
        k0 = t + b * qb - half
        rows = slice(b * qb, (b + 1) * qb)
        store((a1, m1, l1), rows, _band_group(qf[rows, :], kf[k0:k0 + kl, :], vf[k0:k0 + kl, :], b1_ref, pen))

    dil = 4
    for r in range(dil):
        qrows = pl.ds(r, qb, stride=dil)
        krows = pl.ds(t - half * dil + r, kl, stride=dil)
        store((a4, m4, l4), qrows, _band_group(qf[qrows, :], kf[krows, :], vf[krows, :], b4_ref, lo + hi))

    dil = 16
    q16 = t // dil
    k16 = q16 + 2 * half
    col = lax.rem(lax.broadcasted_iota(jnp.int32, (1, DIL_MERGE * k16), 1), k16)
    pen16 = (jnp.where((col < half) & first, NEG_INF, 0.0)
             + jnp.where((col >= q16 + half) & last, NEG_INF, 0.0))
    for g in range(dil // DIL_MERGE):
        res = range(g * DIL_MERGE, (g + 1) * DIL_MERGE)
        qrows = [pl.ds(r, q16, stride=dil) for r in res]
        krows = [pl.ds(t - half * dil + r, k16, stride=dil) for r in res]
        out = _band_group(jnp.concatenate([qf[rr, :] for rr in qrows], axis=0),
                          jnp.concatenate([kf[rr, :] for rr in krows], axis=0),
                          jnp.concatenate([vf[rr, :] for rr in krows], axis=0), b16_ref, pen16)
        for u, rr in enumerate(qrows):
            store((a16, m16, l16), rr, [x[u * q16:(u + 1) * q16] for x in out])

    mm = jnp.maximum(jnp.maximum(m1[...], m4[...]), m16[...])
    w1, w4, w16 = jnp.exp2(m1[...] - mm), jnp.exp2(m4[...] - mm), jnp.exp2(m16[...] - mm)
    num = w1 * a1[...] + w4 * a4[...] + w16 * a16[...]
    den = w1 * l1[...] + w4 * l4[...] + w16 * l16[...]
    o_ref[...] = (num / den).astype(o_ref.dtype)


def dilated_attention(z, t5_table, col0, width):
    s, zw = z.shape
    t = DIL_TILE
    assert s % t == 0 and col0 % LANE == 0 and width % LANE == 0 and t // 4 == DIL_QB
    nt = s // t
    nh = t5_table.shape[1]
    cq, ck, cv = ((col0 + k * width) // LANE for k in range(3))
    q16 = t // 16
    diags = [(_band_diag(t5_table, dil, qb) * LOG2E).reshape(nh // 2, 2, -1)
             for dil, qb in ((1, DIL_QB), (4, DIL_QB), (16, q16))]
    kl, k16 = DIL_QB + 2 * DIL_HALF, q16 + 2 * DIL_HALF

    def blk(col, shift):
        return pl.BlockSpec((t, LANE), lambda hg, i: (jnp.clip(i + shift, 0, nt - 1), col + hg))

    diag_spec = lambda d: pl.BlockSpec((None,) + d.shape[1:], lambda hg, i: (hg, 0, 0))
    return pl.pallas_call(
        _dil_kernel,
        out_shape=jax.ShapeDtypeStruct((s, width), BF16),
        grid=(width // LANE, nt),
        in_specs=[blk(cq, 0), blk(ck, -1), blk(ck, 0), blk(ck, 1), blk(cv, -1), blk(cv, 0), blk(cv, 1)]
                 + [diag_spec(d) for d in diags],
        out_specs=pl.BlockSpec((t, LANE), lambda hg, i: (i, hg)),
        scratch_shapes=[pltpu.VMEM((t, LANE), F32), pltpu.VMEM((3 * t, LANE), F32),
                        pltpu.VMEM((3 * t, LANE), F32)] + [pltpu.VMEM((t, LANE), F32)] * 9
                       + [pltpu.VMEM((2 * DIL_QB, kl), F32), pltpu.VMEM((2 * DIL_QB, kl), F32),
                          pltpu.VMEM((2 * DIL_MERGE * q16, DIL_MERGE * k16), F32)],
        compiler_params=_cp(("parallel", "arbitrary")),
        name="dilated_attention",
    )(*([z] * 7 + diags))


def _oproj_kernel(a1_ref, a2_ref, w_ref, x_ref, g_ref, gt_ref, o_ref):
    k1 = a1_ref.shape[1]
    y = jnp.dot(a1_ref[...], w_ref[0:k1, :], preferred_element_type=F32)
    y = y + jnp.dot(a2_ref[...], w_ref[k1:, :], preferred_element_type=F32)
    o_ref[...] = x_ref[...] + gt_ref[...] * (_rms(y) * g_ref[...])


def out_proj_residual(a1, a2, w, x, g, gt, tm=512):
    s, d = x.shape
    k1, k2 = a1.shape[1], a2.shape[1]
    tm = min(tm, s)
    vec = pl.BlockSpec((1, d), lambda i: (0, 0))
    return pl.pallas_call(
        _oproj_kernel,
        out_shape=jax.ShapeDtypeStruct((s, d), F32),
        grid=(s // tm,),
        in_specs=[pl.BlockSpec((tm, k1), lambda i: (i, 0)), pl.BlockSpec((tm, k2), lambda i: (i, 0)),
                  pl.BlockSpec((k1 + k2, d), lambda i: (0, 0)),
                  pl.BlockSpec((tm, d), lambda i: (i, 0)), vec, vec],
        out_specs=pl.BlockSpec((tm, d), lambda i: (i, 0)),
        compiler_params=_cp(("parallel",)),
        name="out_proj_residual",
    )(a1, a2, w, x, g, gt)


def _ffn_kernel(x_ref, g_ref, sh_ref, sc_ref, g2_ref, gt_ref, w1_ref, w3_ref, w2_ref, o_ref, h_ref):
    f = pl.program_id(1)

    @pl.when(f == 0)
    def _():
        h_ref[...] = _modnorm(x_ref[...], g_ref[...], sh_ref[...], sc_ref[...]).astype(h_ref.dtype)
        o_ref[...] = jnp.zeros_like(o_ref)

    h = h_ref[...]
    u = jnp.dot(h, w1_ref[...], preferred_element_type=F32)
    v = jnp.dot(h, w3_ref[...], preferred_element_type=F32)
    o_ref[...] += jnp.dot((_silu(u) * v).astype(BF16), w2_ref[...], preferred_element_type=F32)

    @pl.when(f == pl.num_programs(1) - 1)
    def _():
        o_ref[...] = x_ref[...] + gt_ref[...] * (_rms(o_ref[...]) * g2_ref[...])


def ffn_sublayer(x, g, sh, sc, g2, gt, w1, w3, w2, tm=1024, tf=256):
    s, d = x.shape
    f = w1.shape[1]
    tm, tf = min(tm, s), min(tf, f)
    vec = pl.BlockSpec((1, d), lambda i, j: (0, 0))
    return pl.pallas_call(
        _ffn_kernel,
        out_shape=jax.ShapeDtypeStruct((s, d), F32),
        grid=(s // tm, f // tf),
        in_specs=[pl.BlockSpec((tm, d), lambda i, j: (i, 0)), vec, vec, vec, vec, vec,
                  pl.BlockSpec((d, tf), lambda i, j: (0, j)), pl.BlockSpec((d, tf), lambda i, j: (0, j)),
                  pl.BlockSpec((tf, d), lambda i, j: (j, 0))],
        out_specs=pl.BlockSpec((tm, d), lambda i, j: (i, 0)),
        scratch_shapes=[pltpu.VMEM((tm, d), BF16)],
        compiler_params=_cp(("parallel", "arbitrary")),
        name="ffn_sublayer",
    )(x, g, sh, sc, g2, gt, w1, w3, w2)


def _rope_slab(t, c_ref, s1_ref, s2_ref):
    return (t * c_ref[...] + pltpu.roll(t, LANE - ROPE_DIM // 2, 1) * s1_ref[...]
            + pltpu.roll(t, ROPE_DIM // 2, 1) * s2_ref[...])


def _oddin_kernel(x_ref, g_ref, sh_ref, sc_ref, win_ref, qg_ref, wuq_ref, kvg_ref, wuk_ref, wuv_ref,
                  c_ref, s1_ref, s2_ref, q_ref, k_ref, v_ref, u_ref, *, q_lora, kv_lora, scale):
    h = _modnorm(x_ref[...], g_ref[...], sh_ref[...], sc_ref[...]).astype(BF16)
    z = jnp.dot(h, win_ref[...], preferred_element_type=F32)
    o_kv, o_pe, o_u = q_lora, q_lora + kv_lora, q_lora + kv_lora + LANE
    u_ref[...] = z[:, o_u:]

    qn = (_rms(z[:, :q_lora]) * qg_ref[...]).astype(BF16)
    q = jnp.dot(qn, wuq_ref[...], preferred_element_type=F32)
    kvn = (_rms(z[:, o_kv:o_pe]) * kvg_ref[...]).astype(BF16)
    kn = jnp.dot(kvn, wuk_ref[...], preferred_element_type=F32)
    v_ref[...] = jnp.dot(kvn, wuv_ref[...], preferred_element_type=F32).astype(v_ref.dtype)
    kpe = _rope_slab(z[:, o_pe:o_u], c_ref, s1_ref, s2_ref).astype(k_ref.dtype)

    for hh in range(MLA_HEADS):
        b = hh * MLA_QK_PAD
        q_ref[:, b:b + NOPE_DIM] = (q[:, b:b + NOPE_DIM] * scale).astype(q_ref.dtype)
        qpe = _rope_slab(q[:, b + NOPE_DIM:b + MLA_QK_PAD], c_ref, s1_ref, s2_ref)
        q_ref[:, b + NOPE_DIM:b + MLA_QK_PAD] = (qpe * scale).astype(q_ref.dtype)
        k_ref[:, b:b + NOPE_DIM] = kn[:, hh * NOPE_DIM:(hh + 1) * NOPE_DIM].astype(k_ref.dtype)
        k_ref[:, b + NOPE_DIM:b + MLA_QK_PAD] = kpe


def _rope_tables(seq):
    pos = jnp.arange(seq, dtype=F32)
    inv_freq = ROPE_BASE ** (-jnp.arange(0, ROPE_DIM, 2, dtype=F32) / ROPE_DIM)
    ang = pos[:, None] * inv_freq[None, :]
    cos, sin = jnp.cos(ang), jnp.sin(ang)
    zero = jnp.zeros_like(cos)
    c = jnp.concatenate([cos, cos, zero, zero], axis=-1)
    s1 = jnp.concatenate([-sin, zero, zero, zero], axis=-1)
    s2 = jnp.concatenate([zero, sin, zero, zero], axis=-1)
    return c, s1, s2


def odd_in_proj(x, g, sh, sc, w_in, q_norm_g, w_uq, kv_norm_g, w_ukv, tm=256):
    s, d = x.shape
    q_lora, kv_lora = q_norm_g.shape[0], kv_norm_g.shape[0]
    s5_dim = w_in.shape[1] - q_lora - kv_lora - ROPE_DIM
    tm = min(tm, s)
    w_in_p = jnp.concatenate(
        [w_in[:, :q_lora + kv_lora + ROPE_DIM], jnp.zeros((d, LANE - ROPE_DIM), w_in.dtype),
         w_in[:, q_lora + kv_lora + ROPE_DIM:]], axis=1).astype(BF16)
    wq = w_uq.reshape(q_lora, MLA_HEADS, NOPE_DIM + ROPE_DIM)
    wq = jnp.pad(wq, ((0, 0), (0, 0), (0, MLA_QK_PAD - NOPE_DIM - ROPE_DIM)))
    wq = wq.reshape(q_lora, MLA_HEADS * MLA_QK_PAD).astype(BF16)
    wkv = w_ukv.reshape(kv_lora, MLA_HEADS, NOPE_DIM + V_DIM)
    wuk = wkv[:, :, :NOPE_DIM].reshape(kv_lora, MLA_HEADS * NOPE_DIM).astype(BF16)
    wuv = wkv[:, :, NOPE_DIM:].reshape(kv_lora, MLA_HEADS * V_DIM).astype(BF16)
    c, s1, s2 = _rope_tables(s)
    scale = (NOPE_DIM + ROPE_DIM) ** -0.5 * LOG2E

    full = lambda a: pl.BlockSpec(a.shape, lambda i: (0, 0))
    row = lambda w: pl.BlockSpec((tm, w), lambda i: (i, 0))
    vec = pl.BlockSpec((1, d), lambda i: (0, 0))
    qg = q_norm_g.reshape(1, q_lora)
    kvg = kv_norm_g.reshape(1, kv_lora)
    return pl.pallas_call(
        functools.partial(_oddin_kernel, q_lora=q_lora, kv_lora=kv_lora, scale=scale),
        out_shape=[jax.ShapeDtypeStruct((s, MLA_HEADS * MLA_QK_PAD), BF16),
                   jax.ShapeDtypeStruct((s, MLA_HEADS * MLA_QK_PAD), BF16),
                   jax.ShapeDtypeStruct((s, MLA_HEADS * V_DIM), BF16),
                   jax.ShapeDtypeStruct((s, s5_dim), F32)],
        grid=(s // tm,),
        in_specs=[row(d), vec, vec, vec, full(w_in_p), full(qg), full(wq), full(kvg), full(wuk), full(wuv),
                  row(LANE), row(LANE), row(LANE)],
        out_specs=[row(MLA_HEADS * MLA_QK_PAD), row(MLA_HEADS * MLA_QK_PAD), row(MLA_HEADS * V_DIM),
                   row(s5_dim)],
        compiler_params=_cp(("parallel",)),
        name="odd_in_proj",
    )(x, g, sh, sc, w_in_p, qg, wq, kvg, wuk, wuv, c, s1, s2)


def _flash_kernel(q_ref, k_ref, v_ref, o_ref, m_ref, acc_ref, s0, s1, *, rows, nk):
    t = pl.program_id(0)
    j1 = lax.rem(jnp.maximum(t - 1, 0), nk)
    tq = q_ref.shape[0]
    vd = v_ref.shape[1]

    @pl.when(t == 0)
    def _():
        m_ref[...] = jnp.full_like(m_ref, -jnp.inf)
        acc_ref[...] = jnp.zeros_like(acc_ref)
        s1[...] = jnp.full_like(s1, -jnp.inf)

    def stages(s_rd, s_wr):
        v = v_ref[...]
        v1 = jnp.concatenate([v, jnp.ones_like(v)], axis=1)
        k = k_ref[...]
        live = t >= 1
        fresh = j1 == 0
        for c in range(tq // rows):
            rs = slice(c * rows, (c + 1) * rows)
            s = s_rd[rs, :]
            m_prev = jnp.where(fresh, -jnp.inf, m_ref[rs, :])
            m_cand = jnp.maximum(m_prev, jnp.max(s, axis=-1, keepdims=True))
            m_new = jnp.where(live, m_cand, m_prev)
            m_sub = jnp.where(live, m_cand, 0.0)
            p = jnp.exp2((s - m_sub[:, :1]).astype(BF16))
            alpha = jnp.where(live, jnp.exp2(m_prev - m_new), 1.0)
            m_ref[rs, :] = m_new
            pv = jnp.dot(p, v1, preferred_element_type=F32)
            acc_ref[rs, :vd] = alpha * acc_ref[rs, :vd] + pv[:, :vd]
            acc_ref[rs, vd:] = alpha * acc_ref[rs, vd:] + pv[:, vd:]
            s_wr[rs, :] = lax.dot_general(q_ref[rs, :], k, (((1,), (1,)), ((), ())),
                                          preferred_element_type=F32)

    parity = lax.rem(t, 2)

    @pl.when(parity == 0)
    def _():
        stages(s1, s0)

    @pl.when(parity == 1)
    def _():
        stages(s0, s1)

    @pl.when((t >= 1) & (j1 == nk - 1))
    def _():
        o_ref[...] = (acc_ref[:, :vd] / acc_ref[:, vd:]).astype(o_ref.dtype)


def mla_attention(q, k, v, tq=2048, tk=2048, rows=512):
    s = q.shape[0]
    tq, tk = min(tq, s), min(tk, s)
    rows = min(rows, tq)
    ni, nk = s // tq, s // tk
    n = MLA_HEADS * ni * nk

    def item(t, lag):
        w = jnp.clip(t - lag, 0, n - 1)
        return w // (ni * nk), lax.rem(w, ni * nk) // nk, lax.rem(w, nk)

    def q_map(t):
        h, i, _ = item(t, 0)
        return i, h

    def k_map(t):
        h, _, j = item(t, 0)
        return j, h

    def v_map(t):
        h, _, j = item(t, 1)
        return j, h

    def o_map(t):
        h, i, _ = item(t, 1)
        return i, h

    return pl.pallas_call(
        functools.partial(_flash_kernel, rows=rows, nk=nk),
        out_shape=jax.ShapeDtypeStruct((s, MLA_HEADS * V_DIM), BF16),
        grid=(n + 1,),
        in_specs=[pl.BlockSpec((tq, MLA_QK_PAD), q_map),
                  pl.BlockSpec((tk, MLA_QK_PAD), k_map),
                  pl.BlockSpec((tk, V_DIM), v_map)],
        out_specs=pl.BlockSpec((tq, V_DIM), o_map),
        scratch_shapes=[pltpu.VMEM((tq, V_DIM), F32), pltpu.VMEM((tq, 2 * V_DIM), F32),
                        pltpu.VMEM((tq, tk), F32), pltpu.VMEM((tq, tk), F32)],
        compiler_params=_cp(("arbitrary",)),
        name="mla_flash_attention",
    )(q, k, v)


def _s5_matrices(a_re, a_im, log_dt, b_re, b_im, c_re, c_im, chunk):
    L = chunk
    A = lax.complex(a_re.astype(F32), a_im.astype(F32))
    dt = jnp.exp(log_dt.astype(F32))[..., None]
    adt = A * dt
    a_bar = jnp.exp(adt)
    b_bar = ((a_bar - 1.0) / A)[..., None] * lax.complex(b_re.astype(F32), b_im.astype(F32))
    c_c = lax.complex(c_re.astype(F32), c_im.astype(F32))
    kk = jnp.arange(L + 1, dtype=F32)
    apow = jnp.exp(adt[:, :, None, :] * kk[None, None, :, None].astype(jnp.complex64))
    g, p, gc = b_bar.shape[1], b_bar.shape[2], b_bar.shape[3]

    ker = jnp.real(jnp.einsum('dgcp,dgkp,dgpi->dgkci', c_c, apow[:, :, :L], b_bar))
    kb = ker[1][:, ::-1]
    diag = jnp.concatenate([kb[:, :L - 1], ker[0][:, :1] + kb[:, L - 1:], ker[0][:, 1:]], axis=1)
    t_mat = _toeplitz(jnp.transpose(diag, (0, 2, 3, 1)), L, L)
    t_mat = jnp.transpose(t_mat, (0, 3, 2, 4, 1)).reshape(g, L * gc, L * gc)

    pf = apow[0][:, ::-1][:, 1:][:, :, :, None] * b_bar[0][:, None]
    pb = apow[1][:, :L][:, :, :, None] * b_bar[1][:, None]
    def p_lay(m):
        return jnp.transpose(m, (0, 1, 3, 2)).reshape(g, L * gc, p)
    p_mat = jnp.concatenate([p_lay(jnp.real(pf)), p_lay(jnp.imag(pf)),
                             p_lay(jnp.real(pb)), p_lay(jnp.imag(pb))], axis=-1)

    wf = c_c[0][:, None] * apow[0][:, 1:][:, :, None, :]
    wb = c_c[1][:, None] * apow[1][:, ::-1][:, :L][:, :, None, :]
    def q_lay(m):
        return jnp.transpose(m, (0, 3, 1, 2)).reshape(g, p, L * gc)
    q_mat = jnp.concatenate([q_lay(jnp.real(wf)), q_lay(-jnp.imag(wf)),
                             q_lay(jnp.real(wb)), q_lay(-jnp.imag(wb))], axis=1)

    al = apow[:, :, L]
    dec = jnp.stack([jnp.real(al[0]), jnp.imag(al[0]), jnp.real(al[1]), jnp.imag(al[1])]).reshape(4, g * p)
    return t_mat, p_mat, q_mat, dec


def _s5_state_kernel(u_ref, p_ref, fre_ref, fim_ref, bre_ref, bim_ref):
    outs = [[], [], [], []]
    for gi in range(u_ref.shape[0]):
        xe = jnp.dot(u_ref[gi].astype(BF16), p_ref[gi], preferred_element_type=F32)
        for part in range(4):
            outs[part].append(xe[:, part * S5_STATE:(part + 1) * S5_STATE])
    for part, ref in enumerate((fre_ref, fim_ref, bre_ref, bim_ref)):
        ref[...] = jnp.concatenate(outs[part], axis=-1)


def _s5_scan_kernel(fre_ref, fim_ref, bre_ref, bim_ref, dec_ref, ofre_ref, ofim_ref, obre_ref, obim_ref):
    nc = fre_ref.shape[0]
    w = fre_ref.shape[1]
    far, fai, bar, bai = (dec_ref[i:i + 1, :] for i in range(4))

    def fwd(c, st):
        re, im = st
        ofre_ref[pl.ds(c, 1), :] = re
        ofim_ref[pl.ds(c, 1), :] = im
        return (far * re - fai * im + fre_ref[pl.ds(c, 1), :],
                far * im + fai * re + fim_ref[pl.ds(c, 1), :])

    def bwd(i, st):
        c = nc - 1 - i
        re, im = st
        obre_ref[pl.ds(c, 1), :] = re
        obim_ref[pl.ds(c, 1), :] = im
        return (bar * re - bai * im + bre_ref[pl.ds(c, 1), :],
                bar * im + bai * re + bim_ref[pl.ds(c, 1), :])

    zero = (jnp.zeros((1, w), F32), jnp.zeros((1, w), F32))
    lax.fori_loop(0, nc, fwd, zero)
    lax.fori_loop(0, nc, bwd, zero)


def _s5_out_kernel(u_ref, t_ref, q_ref, fre_ref, fim_ref, bre_ref, bim_ref, y_ref):
    for gi in range(u_ref.shape[0]):
        lanes = slice(gi * S5_STATE, (gi + 1) * S5_STATE)
        y = jnp.dot(u_ref[gi].astype(BF16), t_ref[gi], preferred_element_type=F32)
        xin = jnp.concatenate([r[:, lanes] for r in (fre_ref, fim_ref, bre_ref, bim_ref)], axis=-1)
        y_ref[gi] = (y + jnp.dot(xin.astype(BF16), q_ref[gi], preferred_element_type=F32)).astype(y_ref.dtype)


def s5_bidirectional(u, a_re, a_im, log_dt, b_re, b_im, c_re, c_im, chunk=S5_CHUNK):
    s, dim = u.shape
    gc, p = S5_GROUP, S5_STATE
    g = dim // gc
    chunk = min(chunk, s)
    nc = s // chunk
    gp = LANE // p
    t_mat, p_mat, q_mat, dec = _s5_matrices(a_re, a_im, log_dt, b_re, b_im, c_re, c_im, chunk)
    t_mat, p_mat, q_mat = t_mat.astype(BF16), p_mat.astype(BF16), q_mat.astype(BF16)
    u_t = jnp.transpose(u.astype(BF16).reshape(nc, chunk, g, gc), (2, 0, 1, 3)).reshape(g, nc, chunk * gc)

    grp = lambda a, b: pl.BlockSpec((gp, a, b), lambda i: (i, 0, 0))
    st = pl.BlockSpec((nc, LANE), lambda i: (0, i))
    st_shape = jax.ShapeDtypeStruct((nc, g * p), F32)
    xe = pl.pallas_call(
        _s5_state_kernel,
        out_shape=[st_shape] * 4,
        grid=(g // gp,),
        in_specs=[grp(nc, chunk * gc), grp(chunk * gc, 4 * p)],
        out_specs=[st] * 4,
        compiler_params=_cp(("parallel",)),
        name="s5_chunk_states",
    )(u_t, p_mat)

    lanes = min(4 * LANE, g * p)
    cols = lambda rows: pl.BlockSpec((rows, lanes), lambda i: (0, i))
    xin = pl.pallas_call(
        _s5_scan_kernel,
        out_shape=[st_shape] * 4,
        grid=(g * p // lanes,),
        in_specs=[cols(nc)] * 4 + [cols(4)],
        out_specs=[cols(nc)] * 4,
        compiler_params=_cp(("parallel",)),
        name="s5_boundary_scan",
    )(*xe, dec)

    y_t = pl.pallas_call(
        _s5_out_kernel,
        out_shape=jax.ShapeDtypeStruct((g, nc, chunk * gc), BF16),
        grid=(g // gp,),
        in_specs=[grp(nc, chunk * gc), grp(chunk * gc, chunk * gc), grp(4 * p, chunk * gc)] + [st] * 4,
        out_specs=grp(nc, chunk * gc),
        compiler_params=_cp(("parallel",)),
        name="s5_outputs",
    )(u_t, t_mat, q_mat, *xin)
    return jnp.transpose(y_t.reshape(g, nc, chunk, gc), (1, 2, 0, 3)).reshape(s, dim)


def _s5_gate_kernel(y_ref, u_ref, d_ref, w_ref, o_ref):
    y = y_ref[...].astype(F32) + d_ref[...] * u_ref[...]
    y = 0.5 * y * (1.0 + jnp.tanh(math.sqrt(2.0 / math.pi) * (y + 0.044715 * (y * y * y))))
    z = jnp.dot(y.astype(BF16), w_ref[...], preferred_element_type=F32)
    o_ref[...] = (y * _sigmoid(z)).astype(o_ref.dtype)


def s5_gate(y, u, d_skip, w_glu, tm=1024):
    s, dim = y.shape
    tm = min(tm, s)
    row = pl.BlockSpec((tm, dim), lambda i: (i, 0))
    return pl.pallas_call(
        _s5_gate_kernel,
        out_shape=jax.ShapeDtypeStruct((s, dim), BF16),
        grid=(s // tm,),
        in_specs=[row, row, pl.BlockSpec((1, dim), lambda i: (0, 0)), pl.BlockSpec((dim, dim), lambda i: (0, 0))],
        out_specs=row,
        compiler_params=_cp(("parallel",)),
        name="s5_gate",
    )(y, u, d_skip.reshape(1, dim), w_glu.astype(BF16))


def _router_kernel(x_ref, g_ref, sh_ref, sc_ref, rw_ref, h_ref, info_ref):
    h = _modnorm(x_ref[...], g_ref[...], sh_ref[...], sc_ref[...])
    half = h.shape[1] // 2
    bits = lambda v: lax.bitcast_convert_type(v.astype(BF16).astype(F32), jnp.uint32)
    h_ref[...] = (bits(h[:, :half]) >> 16) | (bits(h[:, half:]) & jnp.uint32(0xFFFF0000))
    logits = jnp.dot(h, rw_ref[...], preferred_element_type=F32, precision=lax.Precision.HIGHEST)
    lane = lax.broadcasted_iota(jnp.int32, logits.shape, 1)
    logits = jnp.where(lane < N_EXPERTS, logits, -jnp.inf)
    m1 = jnp.max(logits, axis=-1, keepdims=True)
    i1 = jnp.min(jnp.where(logits == m1, lane, LANE), axis=-1, keepdims=True)
    rest = jnp.where(lane == i1, -jnp.inf, logits)
    m2 = jnp.max(rest, axis=-1, keepdims=True)
    i2 = jnp.min(jnp.where(rest == m2, lane, LANE), axis=-1, keepdims=True)
    e = jnp.exp(m2 - m1)
    g1 = 1.0 / (1.0 + e)
    g2 = e / (1.0 + e)
    info = jnp.where(lane == 0, i1.astype(F32),
                     jnp.where(lane == 1, i2.astype(F32),
                               jnp.where(lane == 2, g1, jnp.where(lane == 3, g2, 0.0))))
    info_ref[...] = info


def moe_router(x, g, sh, sc, router_w, tm=512):
    s, d = x.shape
    tm = min(tm, s)
    rw = jnp.pad(router_w, ((0, 0), (0, LANE - router_w.shape[1])))
    vec = pl.BlockSpec((1, d), lambda i: (0, 0))
    return pl.pallas_call(
        _router_kernel,
        out_shape=[jax.ShapeDtypeStruct((s, d // 2), jnp.uint32), jax.ShapeDtypeStruct((s, LANE), F32)],
        grid=(s // tm,),
        in_specs=[pl.BlockSpec((tm, d), lambda i: (i, 0)), vec, vec, vec,
                  pl.BlockSpec((d, LANE), lambda i: (0, 0))],
        out_specs=[pl.BlockSpec((tm, d // 2), lambda i: (i, 0)), pl.BlockSpec((tm, LANE), lambda i: (i, 0))],
        compiler_params=_cp(("parallel",)),
        name="moe_router",
    )(x, g, sh, sc, rw)


def _expert_kernel(te_ref, tv_ref, src_ref, hp_hbm, w1_ref, w3_ref, w2_ref, o_ref, acc_ref, gbuf, h_ref, sem):
    t = pl.program_id(0)
    f = pl.program_id(1)
    nt = pl.num_programs(0)
    nf = pl.num_programs(1)
    tm = acc_ref.shape[0]
    per_step = tm // nf
    valid = tv_ref[t] > 0
    slot = lax.rem(t, 2)

    def row_copy(tile, buf, r):
        return pltpu.make_async_copy(hp_hbm.at[pl.ds(src_ref[tile * tm + r], 1)],
                                     gbuf.at[buf, pl.ds(r, 1)], sem.at[buf])

    def wait_half(buf):
        pltpu.make_async_copy(hp_hbm.at[pl.ds(0, tm)], gbuf.at[buf], sem.at[buf]).wait()

    @pl.when((t == 0) & (f == 0))
    def _():
        def issue(r, carry):
            row_copy(0, 0, r).start()
            return carry
        lax.fori_loop(0, tm, issue, 0)

    @pl.when((f == 0) & ((t == 0) | (tv_ref[jnp.maximum(t - 1, 0)] > 0)))
    def _():
        wait_half(slot)

    @pl.when((f == 0) & valid)
    def _():
        w = gbuf[slot]
        half = w.shape[1]
        h_ref[:, :half] = lax.bitcast_convert_type(w << 16, F32).astype(BF16)
        h_ref[:, half:] = lax.bitcast_convert_type(w & jnp.uint32(0xFFFF0000), F32).astype(BF16)

    @pl.when(f == 0)
    def _():
        acc_ref[...] = jnp.zeros_like(acc_ref)

    def compute(rows):
        for u in range(per_step):
            row_copy(t + 1, 1 - slot, f * per_step + u).start()
        h = h_ref[:rows, :]
        u = jnp.dot(h, w1_ref[...].astype(BF16), preferred_element_type=F32)
        v = jnp.dot(h, w3_ref[...].astype(BF16), preferred_element_type=F32)
        acc_ref[:rows, :] += jnp.dot((_silu(u) * v).astype(BF16), w2_ref[...].astype(BF16),
                                     preferred_element_type=F32)

    @pl.when(tv_ref[t] > tm // 2)
    def _():
        compute(tm)

    @pl.when(valid & (tv_ref[t] <= tm // 2))
    def _():
        compute(tm // 2)

    @pl.when(f == nf - 1)
    def _():
        o_ref[...] = acc_ref[...].astype(o_ref.dtype)

    @pl.when((t == nt - 1) & (f == nf - 1) & valid)
    def _():
        wait_half(1 - slot)


def expert_ffn(h_packed, src, tile_expert, tile_valid, w1, w3, w2, tm, tf=512):
    dh = h_packed.shape[1]
    d = 2 * dh
    n = src.shape[0] - tm
    f = w1.shape[2]
    tf = min(tf, f)
    nf = f // tf
    assert tm % nf == 0 and n % tm == 0

    def fblk(j, t, tv):
        return jnp.where(tv[t] > 0, j, nf - 1)

    return pl.pallas_call(
        _expert_kernel,
        out_shape=jax.ShapeDtypeStruct((n, d), BF16),
        grid_spec=pltpu.PrefetchScalarGridSpec(
            num_scalar_prefetch=3,
            grid=(n // tm, nf),
            in_specs=[pl.BlockSpec(memory_space=pl.ANY),
                      pl.BlockSpec((None, d, tf), lambda t, j, te, tv, sr: (te[t], 0, fblk(j, t, tv))),
                      pl.BlockSpec((None, d, tf), lambda t, j, te, tv, sr: (te[t], 0, fblk(j, t, tv))),
                      pl.BlockSpec((None, tf, d), lambda t, j, te, tv, sr: (te[t], fblk(j, t, tv), 0))],
            out_specs=pl.BlockSpec((tm, d), lambda t, j, te, tv, sr: (t, 0)),
            scratch_shapes=[pltpu.VMEM((tm, d), F32), pltpu.VMEM((2, tm, dh), jnp.uint32),
                            pltpu.VMEM((tm, d), BF16), pltpu.SemaphoreType.DMA((2,))]),
        compiler_params=_cp(("arbitrary", "arbitrary")),
        name="expert_ffn",
    )(tile_expert, tile_valid, src, h_packed, w1, w3, w2)


def _dispatch(idx, tm):
    s = idx.shape[0]
    e_flat = idx.reshape(-1)
    onehot = (e_flat[:, None] == jnp.arange(N_EXPERTS)[None, :]).astype(jnp.int32)
    csum = jnp.cumsum(onehot, axis=0)
    rank = jnp.sum((csum - onehot) * onehot, axis=1)
    counts = csum[-1]
    padded = ((counts + tm - 1) // tm) * tm
    gend = jnp.cumsum(padded)
    slot = (gend - padded)[e_flat] + rank
    nt = -(-(2 * s) // tm) + N_EXPERTS
    src = jnp.zeros(((nt + 1) * tm,), jnp.int32).at[slot].set(jnp.arange(2 * s, dtype=jnp.int32) // 2)
    tstart = jnp.arange(nt, dtype=jnp.int32) * tm
    valid = tstart < gend[-1]
    te = jnp.minimum(jnp.sum((tstart[:, None] >= gend[None, :]).astype(jnp.int32), axis=1), N_EXPERTS - 1)
    nvalid = gend[-1] // tm
    fill = jnp.clip((gend - padded + counts)[te] - tstart, 0, tm)
    fill = jnp.where(valid, fill, 0)
    te = jnp.where(valid, te, te[jnp.maximum(nvalid - 1, 0)])
    return slot.reshape(s, 2), src, te.astype(jnp.int32), fill.astype(jnp.int32)


def _combine_kernel(ya_ref, yb_ref, info_ref, x_ref, g_ref, gt_ref, o_ref):
    info = info_ref[...]
    y = info[:, 2:3] * ya_ref[...].astype(F32) + info[:, 3:4] * yb_ref[...].astype(F32)
    o_ref[...] = x_ref[...] + gt_ref[...] * (_rms(y) * g_ref[...])


def moe_combine(ya, yb, info, x, g, gt, tm=512):
    s, d = x.shape
    tm = min(tm, s)
    row = lambda w: pl.BlockSpec((tm, w), lambda i: (i, 0))
    vec = pl.BlockSpec((1, d), lambda i: (0, 0))
    return pl.pallas_call(
        _combine_kernel,
        out_shape=jax.ShapeDtypeStruct((s, d), F32),
        grid=(s // tm,),
        in_specs=[row(d), row(d), row(LANE), row(d), vec, vec],
        out_specs=row(d),
        compiler_params=_cp(("parallel",)),
        name="moe_combine",
    )(ya, yb, info, x, g, gt)


def moe_sublayer(x, g, sh, sc, g2, gt, router_w, w1, w3, w2, tm_e=EXPERT_TILE):
    s = x.shape[0]
    tm_e = min(tm_e, s)
    h_packed, info = moe_router(x, g, sh, sc, router_w)
    idx = info[:, :2].astype(jnp.int32)
    slot, src, te, tv = _dispatch(idx, tm_e)
    y_sorted = expert_ffn(h_packed, src, te, tv, w1, w3, w2, tm_e)
    ya = jnp.take(y_sorted, slot[:, 0], axis=0)
    yb = jnp.take(y_sorted, slot[:, 1], axis=0)
    return moe_combine(ya, yb, info, x, g2, gt)


def kernel(x, c, t5_table, norm_g, ada_w, ada_b, e_w_in, e_conv_w, e_conv_b, e_cln_g, e_cln_b, e_w_out, e_ffn_w1, e_ffn_w3, e_ffn_w2, o_w_in, o_q_norm_g, o_w_uq, o_kv_norm_g, o_w_ukv, s5_a_re, s5_a_im, s5_log_dt, s5_b_re, s5_b_im, s5_c_re, s5_c_im, s5_d, s5_w_glu, o_w_out, router_w, moe_w1, moe_w3, moe_w2):
    bsz, seq, d = x.shape
    assert bsz == 1
    depth = norm_g.shape[0]
    conv_dim = e_conv_w.shape[2]
    xs = x.reshape(seq, d)
    mod = adaln(c, ada_w, ada_b)
    for layer in range(depth):
        i = layer // 2
        sh_m, sc_m, gt_m, sh_f, sc_f, gt_f = (mod[layer, :, k * d:(k + 1) * d] for k in range(6))
        g = [norm_g[layer, k].reshape(1, d) for k in range(4)]
        if layer % 2 == 0:
            z = norm_mod_matmul(xs, g[0], sh_m, sc_m, e_w_in[i].astype(BF16))
            a = conformer_conv(z, e_conv_w[i], e_conv_b[i], e_cln_g[i], e_cln_b[i])
            o = dilated_attention(z, t5_table, 2 * conv_dim, (z.shape[1] - 2 * conv_dim) // 3)
            xs = out_proj_residual(a, o, e_w_out[i].astype(BF16), xs, g[1], gt_m)
            xs = ffn_sublayer(xs, g[2], sh_f, sc_f, g[3], gt_f, e_ffn_w1[i].astype(BF16),
                              e_ffn_w3[i].astype(BF16), e_ffn_w2[i].astype(BF16))
        else:
            q, k, v, u = odd_in_proj(xs, g[0], sh_m, sc_m, o_w_in[i], o_q_norm_g[i], o_w_uq[i],
                                     o_kv_norm_g[i], o_w_ukv[i])
            o_mla = mla_attention(q, k, v)
            y = s5_bidirectional(u, s5_a_re[i], s5_a_im[i], s5_log_dt[i], s5_b_re[i], s5_b_im[i],
                                 s5_c_re[i], s5_c_im[i])
            y = s5_gate(y, u, s5_d[i], s5_w_glu[i])
            xs = out_proj_residual(o_mla, y, o_w_out[i].astype(BF16), xs, g[1], gt_m)
            xs = moe_sublayer(xs, g[2], sh_f, sc_f, g[3], gt_f, router_w[i], moe_w1[i], moe_w3[i], moe_w2[i])
    return xs.reshape(bsz, seq, d)
```

```python
import functools
import math

import jax
import jax.numpy as jnp
from jax import lax
from jax.experimental import pallas as pl
from jax.experimental.pallas import tpu as pltpu

F32 = jnp.float32
BF16 = jnp.bfloat16

RMS_EPS = 1e-6
LN_EPS = 1e-5
NEG_INF = -1e30
LOG2E = math.log2(math.e)

CONV_WIDTH = 31
CONV_HALO = 16
DIL_HEAD_DIM = 64
DIL_CONFIGS = ((128, 1), (512, 4), (2048, 16))
DIL_HALF = 64
DIL_TILE = 1024
DIL_QB = 256
DIL_MERGE = 2
N_BUCKETS = 32
T5_MAX_DIST = DIL_CONFIGS[-1][0] // 2
MLA_HEADS = 12
NOPE_DIM = 128
ROPE_DIM = 64
V_DIM = 128
ROPE_BASE = 10000.0
MLA_QK_PAD = 256
S5_GROUP = 16
S5_STATE = 64
S5_CHUNK = 16
N_EXPERTS = 8
EXPERT_TILE = 896
LANE = 128
SUBLANES = 8
VMEM_LIMIT_MB = 56


def _cp(sem, vmem_mb=VMEM_LIMIT_MB):
    return pltpu.CompilerParams(dimension_semantics=sem, vmem_limit_bytes=vmem_mb * 1024 * 1024)


def _rms(x):
    return x * lax.rsqrt(jnp.mean(x * x, axis=-1, keepdims=True) + RMS_EPS)


def _modnorm(x, g, sh, sc):
    return (_rms(x) * g) * (1.0 + sc) + sh


def _sigmoid(x):
    return 1.0 / (1.0 + jnp.exp(-x))


def _silu(x):
    return x * _sigmoid(x)


def _adaln_kernel(c_ref, w_ref, b_ref, o_ref, cb_ref):
    @pl.when((pl.program_id(0) == 0) & (pl.program_id(1) == 0))
    def _():
        cc = c_ref[...]
        cb_ref[...] = jnp.broadcast_to(_silu(cc), cb_ref.shape)

    cb = cb_ref[...]
    for j in range(o_ref.shape[-1] // LANE):
        sl = slice(j * LANE, (j + 1) * LANE)
        o_ref[:, sl] = jnp.sum(w_ref[:, sl] * cb, axis=0, keepdims=True) + b_ref[:, sl]


def adaln(c, ada_w, ada_b, tn=1024):
    nl, d, n = ada_w.shape
    out = pl.pallas_call(
        _adaln_kernel,
        out_shape=jax.ShapeDtypeStruct((nl, 1, n), F32),
        grid=(nl, n // tn),
        in_specs=[pl.BlockSpec((d, 1), lambda l, j: (0, 0)),
                  pl.BlockSpec((None, d, tn), lambda l, j: (l, 0, j)),
                  pl.BlockSpec((None, 1, tn), lambda l, j: (l, 0, j))],
        out_specs=pl.BlockSpec((None, 1, tn), lambda l, j: (l, 0, j)),
        scratch_shapes=[pltpu.VMEM((d, LANE), F32)],
        compiler_params=_cp(("arbitrary", "arbitrary")),
        name="adaln",
    )(c.reshape(d, 1), ada_w, ada_b.reshape(nl, 1, n))
    return out


def _nmm_kernel(x_ref, g_ref, sh_ref, sc_ref, w_ref, o_ref, h_ref):
    @pl.when(pl.program_id(1) == 0)
    def _():
        h_ref[...] = _modnorm(x_ref[...], g_ref[...], sh_ref[...], sc_ref[...]).astype(h_ref.dtype)

    o_ref[...] = jnp.dot(h_ref[...], w_ref[...], preferred_element_type=F32).astype(o_ref.dtype)


def norm_mod_matmul(x, g, sh, sc, w, tm=1024, tn=1024):
    s, d = x.shape
    n = w.shape[1]
    tm, tn = min(tm, s), min(tn, n)
    vec = pl.BlockSpec((1, d), lambda i, j: (0, 0))
    return pl.pallas_call(
        _nmm_kernel,
        out_shape=jax.ShapeDtypeStruct((s, n), BF16),
        grid=(s // tm, n // tn),
        in_specs=[pl.BlockSpec((tm, d), lambda i, j: (i, 0)), vec, vec, vec,
                  pl.BlockSpec((d, tn), lambda i, j: (0, j))],
        out_specs=pl.BlockSpec((tm, tn), lambda i, j: (i, j)),
        scratch_shapes=[pltpu.VMEM((tm, d), BF16)],
        compiler_params=_cp(("parallel", "arbitrary")),
        name="norm_mod_matmul",
    )(x, g, sh, sc, w)


def _conv_kernel(av_ref, ag_ref, avp_ref, agp_ref, avn_ref, agn_ref, w_ref, b_ref, lg_ref, lb_ref,
                 o_ref, buf_ref, sh_ref, *, rows):
    i = pl.program_id(0)
    n = pl.num_programs(0)
    ts = av_ref.shape[0]

    def glu(v_ref, g_ref):
        return v_ref[...].astype(F32) * _sigmoid(g_ref[...].astype(F32))

    buf_ref[0:CONV_HALO, :] = jnp.where(i > 0, glu(avp_ref, agp_ref), 0.0)
    buf_ref[CONV_HALO:CONV_HALO + ts, :] = glu(av_ref, ag_ref)
    buf_ref[CONV_HALO + ts:2 * CONV_HALO + ts, :] = jnp.where(i < n - 1, glu(avn_ref, agn_ref), 0.0)
    buf_ref[2 * CONV_HALO + ts:, :] = jnp.zeros((SUBLANES, buf_ref.shape[1]), F32)

    span = ts + 2 * CONV_HALO
    for o in range(SUBLANES):
        sh_ref[o] = buf_ref[o:o + span, :]

    off = CONV_HALO - CONV_WIDTH // 2
    for r in range(ts // rows):
        acc = jnp.broadcast_to(b_ref[...], (rows, b_ref.shape[-1]))
        for k in range(CONV_WIDTH):
            lo = r * rows + k + off
            base = lo - lo % SUBLANES
            acc = acc + w_ref[k:k + 1, :] * sh_ref[lo % SUBLANES, base:base + rows, :]
        mu = jnp.mean(acc, axis=-1, keepdims=True)
        dlt = acc - mu
        var = jnp.mean(dlt * dlt, axis=-1, keepdims=True)
        y = dlt * lax.rsqrt(var + LN_EPS) * lg_ref[...] + lb_ref[...]
        o_ref[r * rows:(r + 1) * rows, :] = _silu(y).astype(o_ref.dtype)


def conformer_conv(z, conv_w, conv_b, cln_g, cln_b, ts=256, rows=32):
    s = z.shape[0]
    c = conv_w.shape[1]
    ts = min(ts, s)
    hb = ts // CONV_HALO
    nh = s // CONV_HALO
    main = lambda col: pl.BlockSpec((ts, c), lambda i: (i, col))
    prev = lambda col: pl.BlockSpec((CONV_HALO, c), lambda i: (jnp.maximum(i * hb - 1, 0), col))
    nxt = lambda col: pl.BlockSpec((CONV_HALO, c), lambda i: (jnp.minimum((i + 1) * hb, nh - 1), col))
    vec = lambda r: pl.BlockSpec((r, c), lambda i: (0, 0))
    return pl.pallas_call(
        functools.partial(_conv_kernel, rows=min(rows, ts)),
        out_shape=jax.ShapeDtypeStruct((s, c), BF16),
        grid=(s // ts,),
        in_specs=[main(0), main(1), prev(0), prev(1), nxt(0), nxt(1),
                  vec(CONV_WIDTH), vec(1), vec(1), vec(1)],
        out_specs=pl.BlockSpec((ts, c), lambda i: (i, 0)),
        scratch_shapes=[pltpu.VMEM((ts + 2 * CONV_HALO + SUBLANES, c), F32),
                        pltpu.VMEM((SUBLANES, ts + 2 * CONV_HALO, c), F32)],
        compiler_params=_cp(("parallel",)),
        name="conformer_conv",
    )(z, z, z, z, z, z, conv_w, conv_b.reshape(1, c), cln_g.reshape(1, c), cln_b.reshape(1, c))


def _t5_bucket(rel):
    half = N_BUCKETS // 2
    exact = half // 2
    n = jnp.abs(rel)
    large = exact + (jnp.log(jnp.maximum(n, 1).astype(F32) / exact)
                     / math.log(T5_MAX_DIST / exact) * (half - exact)).astype(jnp.int32)
    large = jnp.minimum(large, half - 1)
    return jnp.where(rel > 0, half, 0) + jnp.where(n < exact, n, large)


def _toeplitz(w, rows, cols):
    n = w.shape[-1]
    wp = jnp.concatenate([w, jnp.zeros(w.shape[:-1] + (1,), w.dtype)], axis=-1)
    flat = jnp.tile(wp, (1,) * (w.ndim - 1) + (rows,))[..., :rows * n]
    return flat.reshape(w.shape[:-1] + (rows, n))[..., rows - 1:rows - 1 + cols]


def _band_diag(t5_table, dil, qb):
    kl = qb + 2 * DIL_HALF
    n = -(-(qb + kl - 1) // LANE) * LANE
    dist = jnp.arange(n) - (qb - 1) - DIL_HALF
    return jnp.where((jnp.abs(dist) <= DIL_HALF)[None], t5_table[_t5_bucket(dist * dil)].T.astype(F32), NEG_INF)


def _band_bias_rows(diag_row, qb):
    n = diag_row.shape[1]
    rolled = pltpu.roll(jnp.broadcast_to(diag_row, (qb, n)), n - (qb - 1), 1, stride=1, stride_axis=0)
    return rolled[:, :qb + 2 * DIL_HALF]


def _band_group(q, kw, vw, bias_ref, pen):
    r = q.shape[0]
    is_a = lax.broadcasted_iota(jnp.int32, (1, LANE), 1) < DIL_HEAD_DIM
    q2 = jnp.concatenate([jnp.where(is_a, q, 0.0), jnp.where(is_a, 0.0, q)], axis=0).astype(BF16)
    s = lax.dot_general(q2, kw.astype(BF16), (((1,), (1,)), ((), ())), preferred_element_type=F32)
    s = s + bias_ref[...]
    if pen is not None:
        s = s + pen
    m = jnp.max(s, axis=-1, keepdims=True)
    p = jnp.exp2((s - m).astype(BF16))
    vb = vw.astype(BF16)
    pv = jnp.dot(p, jnp.concatenate([vb, jnp.ones_like(vb)], axis=1), preferred_element_type=F32)
    pick = lambda x: jnp.where(is_a, x[:r], x[r:])
    return pick(pv[:, :LANE]), pick(m), pick(pv[:, LANE:])


def _dil_kernel(q_ref, kp_ref, kc_ref, kn_ref, vp_ref, vc_ref, vn_ref, d1_ref, d4_ref, d16_ref, o_ref,
                qf, kf, vf, a1, m1, l1, a4, m4, l4, a16, m16, l16, b1_ref, b4_ref, b16_ref):
    i = pl.program_id(1)
    first = i == 0
    last = i == pl.num_programs(1) - 1
    t = DIL_TILE
    half = DIL_HALF

    @pl.when(first)
    def _():
        q16 = t // 16
        b16_ref[...] = jnp.full(b16_ref.shape, NEG_INF, F32)
        for h in range(2):
            b1_ref[h * DIL_QB:(h + 1) * DIL_QB, :] = _band_bias_rows(d1_ref[h:h + 1, :], DIL_QB)
            b4_ref[h * DIL_QB:(h + 1) * DIL_QB, :] = _band_bias_rows(d4_ref[h:h + 1, :], DIL_QB)
            blk = _band_bias_rows(d16_ref[h:h + 1, :], q16)
            for u in range(DIL_MERGE):
                r0 = (h * DIL_MERGE + u) * q16
                b16_ref[r0:r0 + q16, u * blk.shape[1]:(u + 1) * blk.shape[1]] = blk

    qf[...] = q_ref[...].astype(F32) * (DIL_HEAD_DIM ** -0.5 * LOG2E)
    for dst, (p_ref, c_ref, n_ref) in ((kf, (kp_ref, kc_ref, kn_ref)), (vf, (vp_ref, vc_ref, vn_ref))):
        dst[0:t, :] = p_ref[...].astype(F32)
        dst[t:2 * t, :] = c_ref[...].astype(F32)
        dst[2 * t:3 * t, :] = n_ref[...].astype(F32)

    def store(refs, rows, vals):
        for ref, val in zip(refs, vals):
            ref[rows, :] = val

    qb = DIL_QB
    kl = qb + 2 * half
    col = lax.broadcasted_iota(jnp.int32, (1, kl), 1)
    lo = jnp.where((col < half) & first, NEG_INF, 0.0)
    hi = jnp.where((col >= qb + half) & last, NEG_INF, 0.0)

    nb = t // qb
    for b in range(nb):
        pen = lo + hi if nb == 1 else (lo if b == 0 else (hi if b == nb - 1 else None))
        k0 = t + b * qb - half
        rows = slice(b * qb, (b + 1) * qb)
        store((a1, m1, l1), rows, _band_group(qf[rows, :], kf[k0:k0 + kl, :], vf[k0:k0 + kl, :], b1_ref, pen))

    dil = 4
    for r in range(dil):
        qrows = pl.ds(r, qb, stride=dil)
        krows = pl.ds(t - half * dil + r, kl, stride=dil)
        store((a4, m4, l4), qrows, _band_group(qf[qrows, :], kf[krows, :], vf[krows, :], b4_ref, lo + hi))

    dil = 16
    q16 = t // dil
    k16 = q16 + 2 * half
    col = lax.rem(lax.broadcasted_iota(jnp.int32, (1, DIL_MERGE * k16), 1), k16)
    pen16 = (jnp.where((col < half) & first, NEG_INF, 0.0)
             + jnp.where((col >= q16 + half) & last, NEG_INF, 0.0))
    for g in range(dil // DIL_MERGE):
        res = range(g * DIL_MERGE, (g + 1) * DIL_MERGE)
        qrows = [pl.ds(r, q16, stride=dil) for r in res]
        krows = [pl.ds(t - half * dil + r, k16, stride=dil) for r in res]
        out = _band_group(jnp.concatenate([qf[rr, :] for rr in qrows], axis=0),
                          jnp.concatenate([kf[rr, :] for rr in krows], axis=0),
                          jnp.concatenate([vf[rr, :] for rr in krows], axis=0), b16_ref, pen16)
        for u, rr in enumerate(qrows):
            store((a16, m16, l16), rr, [x[u * q16:(u + 1) * q16] for x in out])

    mm = jnp.maximum(jnp.maximum(m1[...], m4[...]), m16[...])
    w1, w4, w16 = jnp.exp2(m1[...] - mm), jnp.exp2(m4[...] - mm), jnp.exp2(m16[...] - mm)
    num = w1 * a1[...] + w4 * a4[...] + w16 * a16[...]
    den = w1 * l1[...] + w4 * l4[...] + w16 * l16[...]
    o_ref[...] = (num / den).astype(o_ref.dtype)


def dilated_attention(z, t5_table, col0, width):
    s, zw = z.shape
    t = DIL_TILE
    assert s % t == 0 and col0 % LANE == 0 and width % LANE == 0 and t // 4 == DIL_QB
    nt = s // t
    nh = t5_table.shape[1]
    cq, ck, cv = ((col0 + k * width) // LANE for k in range(3))
    q16 = t // 16
    diags = [(_band_diag(t5_table, dil, qb) * LOG2E).reshape(nh // 2, 2, -1)
             for dil, qb in ((1, DIL_QB), (4, DIL_QB), (16, q16))]
    kl, k16 = DIL_QB + 2 * DIL_HALF, q16 + 2 * DIL_HALF

    def blk(col, shift):
        return pl.BlockSpec((t, LANE), lambda hg, i: (jnp.clip(i + shift, 0, nt - 1), col + hg))

    diag_spec = lambda d: pl.BlockSpec((None,) + d.shape[1:], lambda hg, i: (hg, 0, 0))
    return pl.pallas_call(
        _dil_kernel,
        out_shape=jax.ShapeDtypeStruct((s, width), BF16),
        grid=(width // LANE, nt),
        in_specs=[blk(cq, 0), blk(ck, -1), blk(ck, 0), blk(ck, 1), blk(cv, -1), blk(cv, 0), blk(cv, 1)]
                 + [diag_spec(d) for d in diags],
        out_specs=pl.BlockSpec((t, LANE), lambda hg, i: (i, hg)),
        scratch_shapes=[pltpu.VMEM((t, LANE), F32), pltpu.VMEM((3 * t, LANE), F32),
                        pltpu.VMEM((3 * t, LANE), F32)] + [pltpu.VMEM((t, LANE), F32)] * 9
                       + [pltpu.VMEM((2 * DIL_QB, kl), F32), pltpu.VMEM((2 * DIL_QB, kl), F32),
                          pltpu.VMEM((2 * DIL_MERGE * q16, DIL_MERGE * k16), F32)],
        compiler_params=_cp(("parallel", "arbitrary")),
        name="dilated_attention",
    )(*([z] * 7 + diags))


def _oproj_kernel(a1_ref, a2_ref, w_ref, x_ref, g_ref, gt_ref, o_ref):
    k1 = a1_ref.shape[1]
    y = jnp.dot(a1_ref[...], w_ref[0:k1, :], preferred_element_type=F32)
    y = y + jnp.dot(a2_ref[...], w_ref[k1:, :], preferred_element_type=F32)
    o_ref[...] = x_ref[...] + gt_ref[...] * (_rms(y) * g_ref[...])


def out_proj_residual(a1, a2, w, x, g, gt, tm=512):
    s, d = x.shape
    k1, k2 = a1.shape[1], a2.shape[1]
    tm = min(tm, s)
    vec = pl.BlockSpec((1, d), lambda i: (0, 0))
    return pl.pallas_call(
        _oproj_kernel,
        out_shape=jax.ShapeDtypeStruct((s, d), F32),
        grid=(s // tm,),
        in_specs=[pl.BlockSpec((tm, k1), lambda i: (i, 0)), pl.BlockSpec((tm, k2), lambda i: (i, 0)),
                  pl.BlockSpec((k1 + k2, d), lambda i: (0, 0)),
                  pl.BlockSpec((tm, d), lambda i: (i, 0)), vec, vec],
        out_specs=pl.BlockSpec((tm, d), lambda i: (i, 0)),
        compiler_params=_cp(("parallel",)),
        name="out_proj_residual",
    )(a1, a2, w, x, g, gt)


def _ffn_kernel(x_ref, g_ref, sh_ref, sc_ref, g2_ref, gt_ref, w1_ref, w3_ref, w2_ref, o_ref, h_ref):
    f = pl.program_id(1)

    @pl.when(f == 0)
    def _():
        h_ref[...] = _modnorm(x_ref[...], g_ref[...], sh_ref[...], sc_ref[...]).astype(h_ref.dtype)
        o_ref[...] = jnp.zeros_like(o_ref)

    h = h_ref[...]
    u = jnp.dot(h, w1_ref[...], preferred_element_type=F32)
    v = jnp.dot(h, w3_ref[...], preferred_element_type=F32)
    o_ref[...] += jnp.dot((_silu(u) * v).astype(BF16), w2_ref[...], preferred_element_type=F32)

    @pl.when(f == pl.num_programs(1) - 1)
    def _():
        o_ref[...] = x_ref[...] + gt_ref[...] * (_rms(o_ref[...]) * g2_ref[...])


def ffn_sublayer(x, g, sh, sc, g2, gt, w1, w3, w2, tm=1024, tf=256):
    s, d = x.shape
    f = w1.shape[1]
    tm, tf = min(tm, s), min(tf, f)
    vec = pl.BlockSpec((1, d), lambda i, j: (0, 0))
    return pl.pallas_call(
        _ffn_kernel,
        out_shape=jax.ShapeDtypeStruct((s, d), F32),
        grid=(s // tm, f // tf),
        in_specs=[pl.BlockSpec((tm, d), lambda i, j: (i, 0)), vec, vec, vec, vec, vec,
                  pl.BlockSpec((d, tf), lambda i, j: (0, j)), pl.BlockSpec((d, tf), lambda i, j: (0, j)),
                  pl.BlockSpec((tf, d), lambda i, j: (j, 0))],
        out_specs=pl.BlockSpec((tm, d), lambda i, j: (i, 0)),
        scratch_shapes=[pltpu.VMEM((tm, d), BF16)],
        compiler_params=_cp(("parallel", "arbitrary")),
        name="ffn_sublayer",
    )(x, g, sh, sc, g2, gt, w1, w3, w2)


def _rope_slab(t, c_ref, s1_ref, s2_ref):
    return (t * c_ref[...] + pltpu.roll(t, LANE - ROPE_DIM // 2, 1) * s1_ref[...]
            + pltpu.roll(t, ROPE_DIM // 2, 1) * s2_ref[...])


def _oddin_kernel(x_ref, g_ref, sh_ref, sc_ref, win_ref, qg_ref, wuq_ref, kvg_ref, wuk_ref, wuv_ref,
                  c_ref, s1_ref, s2_ref, q_ref, k_ref, v_ref, u_ref, *, q_lora, kv_lora, scale):
    h = _modnorm(x_ref[...], g_ref[...], sh_ref[...], sc_ref[...]).astype(BF16)
    z = jnp.dot(h, win_ref[...], preferred_element_type=F32)
    o_kv, o_pe, o_u = q_lora, q_lora + kv_lora, q_lora + kv_lora + LANE
    u_ref[...] = z[:, o_u:]

    qn = (_rms(z[:, :q_lora]) * qg_ref[...]).astype(BF16)
    q = jnp.dot(qn, wuq_ref[...], preferred_element_type=F32)
    kvn = (_rms(z[:, o_kv:o_pe]) * kvg_ref[...]).astype(BF16)
    kn = jnp.dot(kvn, wuk_ref[...], preferred_element_type=F32)
    v_ref[...] = jnp.dot(kvn, wuv_ref[...], preferred_element_type=F32).astype(v_ref.dtype)
    kpe = _rope_slab(z[:, o_pe:o_u], c_ref, s1_ref, s2_ref).astype(k_ref.dtype)

    for hh in range(MLA_HEADS):
        b = hh * MLA_QK_PAD
        q_ref[:, b:b + NOPE_DIM] = (q[:, b:b + NOPE_DIM] * scale).astype(q_ref.dtype)
        qpe = _rope_slab(q[:, b + NOPE_DIM:b + MLA_QK_PAD], c_ref, s1_ref, s2_ref)
        q_ref[:, b + NOPE_DIM:b + MLA_QK_PAD] = (qpe * scale).astype(q_ref.dtype)
        k_ref[:, b:b + NOPE_DIM] = kn[:, hh * NOPE_DIM:(hh + 1) * NOPE_DIM].astype(k_ref.dtype)
        k_ref[:, b + NOPE_DIM:b + MLA_QK_PAD] = kpe


def _rope_tables(seq):
    pos = jnp.arange(seq, dtype=F32)
    inv_freq = ROPE_BASE ** (-jnp.arange(0, ROPE_DIM, 2, dtype=F32) / ROPE_DIM)
    ang = pos[:, None] * inv_freq[None, :]
    cos, sin = jnp.cos(ang), jnp.sin(ang)
    zero = jnp.zeros_like(cos)
    c = jnp.concatenate([cos, cos, zero, zero], axis=-1)
    s1 = jnp.concatenate([-sin, zero, zero, zero], axis=-1)
    s2 = jnp.concatenate([zero, sin, zero, zero], axis=-1)
    return c, s1, s2


def odd_in_proj(x, g, sh, sc, w_in, q_norm_g, w_uq, kv_norm_g, w_ukv, tm=256):
    s, d = x.shape
    q_lora, kv_lora = q_norm_g.shape[0], kv_norm_g.shape[0]
    s5_dim = w_in.shape[1] - q_lora - kv_lora - ROPE_DIM
    tm = min(tm, s)
    w_in_p = jnp.concatenate(
        [w_in[:, :q_lora + kv_lora + ROPE_DIM], jnp.zeros((d, LANE - ROPE_DIM), w_in.dtype),
         w_in[:, q_lora + kv_lora + ROPE_DIM:]], axis=1).astype(BF16)
    wq = w_uq.reshape(q_lora, MLA_HEADS, NOPE_DIM + ROPE_DIM)
    wq = jnp.pad(wq, ((0, 0), (0, 0), (0, MLA_QK_PAD - NOPE_DIM - ROPE_DIM)))
    wq = wq.reshape(q_lora, MLA_HEADS * MLA_QK_PAD).astype(BF16)
    wkv = w_ukv.reshape(kv_lora, MLA_HEADS, NOPE_DIM + V_DIM)
    wuk = wkv[:, :, :NOPE_DIM].reshape(kv_lora, MLA_HEADS * NOPE_DIM).astype(BF16)
    wuv = wkv[:, :, NOPE_DIM:].reshape(kv_lora, MLA_HEADS * V_DIM).astype(BF16)
    c, s1, s2 = _rope_tables(s)
    scale = (NOPE_DIM + ROPE_DIM) ** -0.5 * LOG2E

    full = lambda a: pl.BlockSpec(a.shape, lambda i: (0, 0))
    row = lambda w: pl.BlockSpec((tm, w), lambda i: (i, 0))
    vec = pl.BlockSpec((1, d), lambda i: (0, 0))
    qg = q_norm_g.reshape(1, q_lora)
    kvg = kv_norm_g.reshape(1, kv_lora)
    return pl.pallas_call(
        functools.partial(_oddin_kernel, q_lora=q_lora, kv_lora=kv_lora, scale=scale),
        out_shape=[jax.ShapeDtypeStruct((s, MLA_HEADS * MLA_QK_PAD), BF16),
                   jax.ShapeDtypeStruct((s, MLA_HEADS * MLA_QK_PAD), BF16),
                   jax.ShapeDtypeStruct((s, MLA_HEADS * V_DIM), BF16),
                   jax.ShapeDtypeStruct((s, s5_dim), F32)],
        grid=(s // tm,),
        in_specs=[row(d), vec, vec, vec, full(w_in_p), full(qg), full(wq), full(kvg), full(wuk), full(wuv),
                  row(LANE), row(LANE), row(LANE)],
        out_specs=[row(MLA_HEADS * MLA_QK_PAD), row(MLA_HEADS * MLA_QK_PAD), row(MLA_HEADS * V_DIM),
                   row(s5_dim)],
        compiler_params=_cp(("parallel",)),
        name="odd_in_proj",
    )(x, g, sh, sc, w_in_p, qg, wq, kvg, wuk, wuv, c, s1, s2)


def _flash_kernel(q_ref, k_ref, v_ref, o_ref, m_ref, acc_ref, s0, s1, *, rows, nk):
    t = pl.program_id(0)
    j1 = lax.rem(jnp.maximum(t - 1, 0), nk)
    tq = q_ref.shape[0]
    vd = v_ref.shape[1]

    @pl.when(t == 0)
    def _():
        m_ref[...] = jnp.full_like(m_ref, -jnp.inf)
        acc_ref[...] = jnp.zeros_like(acc_ref)
        s1[...] = jnp.full_like(s1, -jnp.inf)

    def stages(s_rd, s_wr):
        v = v_ref[...]
        v1 = jnp.concatenate([v, jnp.ones_like(v)], axis=1)
        k = k_ref[...]
        live = t >= 1
        fresh = j1 == 0
        for c in range(tq // rows):
            rs = slice(c * rows, (c + 1) * rows)
            s = s_rd[rs, :]
            m_prev = jnp.where(fresh, -jnp.inf, m_ref[rs, :])
            m_cand = jnp.maximum(m_prev, jnp.max(s, axis=-1, keepdims=True))
            m_new = jnp.where(live, m_cand, m_prev)
            m_sub = jnp.where(live, m_cand, 0.0)
            p = jnp.exp2((s - m_sub[:, :1]).astype(BF16))
            alpha = jnp.where(live, jnp.exp2(m_prev - m_new), 1.0)
            m_ref[rs, :] = m_new
            pv = jnp.dot(p, v1, preferred_element_type=F32)
            acc_ref[rs, :vd] = alpha * acc_ref[rs, :vd] + pv[:, :vd]
            acc_ref[rs, vd:] = alpha * acc_ref[rs, vd:] + pv[:, vd:]
            s_wr[rs, :] = lax.dot_general(q_ref[rs, :], k, (((1,), (1,)), ((), ())),
                                          preferred_element_type=F32)

    parity = lax.rem(t, 2)

    @pl.when(parity == 0)
    def _():
        stages(s1, s0)

    @pl.when(parity == 1)
    def _():
        stages(s0, s1)

    @pl.when((t >= 1) & (j1 == nk - 1))
    def _():
        o_ref[...] = (acc_ref[:, :vd] / acc_ref[:, vd:]).astype(o_ref.dtype)


def mla_attention(q, k, v, tq=2048, tk=2048, rows=512):
    s = q.shape[0]
    tq, tk = min(tq, s), min(tk, s)
    rows = min(rows, tq)
    ni, nk = s // tq, s // tk
    n = MLA_HEADS * ni * nk

    def item(t, lag):
        w = jnp.clip(t - lag, 0, n - 1)
        return w // (ni * nk), lax.rem(w, ni * nk) // nk, lax.rem(w, nk)

    def q_map(t):
        h, i, _ = item(t, 0)
        return i, h

    def k_map(t):
        h, _, j = item(t, 0)
        return j, h

    def v_map(t):
        h, _, j = item(t, 1)
        return j, h

    def o_map(t):
        h, i, _ = item(t, 1)
        return i, h

    return pl.pallas_call(
        functools.partial(_flash_kernel, rows=rows, nk=nk),
        out_shape=jax.ShapeDtypeStruct((s, MLA_HEADS * V_DIM), BF16),
        grid=(n + 1,),
        in_specs=[pl.BlockSpec((tq, MLA_QK_PAD), q_map),
                  pl.BlockSpec((tk, MLA_QK_PAD), k_map),
                  pl.BlockSpec((tk, V_DIM), v_map)],
        out_specs=pl.BlockSpec((tq, V_DIM), o_map),
        scratch_shapes=[pltpu.VMEM((tq, V_DIM), F32), pltpu.VMEM((tq, 2 * V_DIM), F32),
                        pltpu.VMEM((tq, tk), F32), pltpu.VMEM((tq, tk), F32)],
        compiler_params=_cp(("arbitrary",)),
        name="mla_flash_attention",
    )(q, k, v)


def _s5_matrices(a_re, a_im, log_dt, b_re, b_im, c_re, c_im, chunk):
    L = chunk
    A = lax.complex(a_re.astype(F32), a_im.astype(F32))
    dt = jnp.exp(log_dt.astype(F32))[..., None]
    adt = A * dt
    a_bar = jnp.exp(adt)
    b_bar = ((a_bar - 1.0) / A)[..., None] * lax.complex(b_re.astype(F32), b_im.astype(F32))
    c_c = lax.complex(c_re.astype(F32), c_im.astype(F32))
    kk = jnp.arange(L + 1, dtype=F32)
    apow = jnp.exp(adt[:, :, None, :] * kk[None, None, :, None].astype(jnp.complex64))
    g, p, gc = b_bar.shape[1], b_bar.shape[2], b_bar.shape[3]

    ker = jnp.real(jnp.einsum('dgcp,dgkp,dgpi->dgkci', c_c, apow[:, :, :L], b_bar))
    kb = ker[1][:, ::-1]
    diag = jnp.concatenate([kb[:, :L - 1], ker[0][:, :1] + kb[:, L - 1:], ker[0][:, 1:]], axis=1)
    t_mat = _toeplitz(jnp.transpose(diag, (0, 2, 3, 1)), L, L)
    t_mat = jnp.transpose(t_mat, (0, 3, 2, 4, 1)).reshape(g, L * gc, L * gc)

    pf = apow[0][:, ::-1][:, 1:][:, :, :, None] * b_bar[0][:, None]
    pb = apow[1][:, :L][:, :, :, None] * b_bar[1][:, None]
    def p_lay(m):
        return jnp.transpose(m, (0, 1, 3, 2)).reshape(g, L * gc, p)
    p_mat = jnp.concatenate([p_lay(jnp.real(pf)), p_lay(jnp.imag(pf)),
                             p_lay(jnp.real(pb)), p_lay(jnp.imag(pb))], axis=-1)

    wf = c_c[0][:, None] * apow[0][:, 1:][:, :, None, :]
    wb = c_c[1][:, None] * apow[1][:, ::-1][:, :L][:, :, None, :]
    def q_lay(m):
        return jnp.transpose(m, (0, 3, 1, 2)).reshape(g, p, L * gc)
    q_mat = jnp.concatenate([q_lay(jnp.real(wf)), q_lay(-jnp.imag(wf)),
                             q_lay(jnp.real(wb)), q_lay(-jnp.imag(wb))], axis=1)

    al = apow[:, :, L]
    dec = jnp.stack([jnp.real(al[0]), jnp.imag(al[0]), jnp.real(al[1]), jnp.imag(al[1])]).reshape(4, g * p)
    return t_mat, p_mat, q_mat, dec


def _s5_state_kernel(u_ref, p_ref, fre_ref, fim_ref, bre_ref, bim_ref):
    outs = [[], [], [], []]
    for gi in range(u_ref.shape[0]):
        xe = jnp.dot(u_ref[gi].astype(BF16), p_ref[gi], preferred_element_type=F32)
        for part in range(4):
            outs[part].append(xe[:, part * S5_STATE:(part + 1) * S5_STATE])
    for part, ref in enumerate((fre_ref, fim_ref, bre_ref, bim_ref)):
        ref[...] = jnp.concatenate(outs[part], axis=-1)


def _s5_scan_kernel(fre_ref, fim_ref, bre_ref, bim_ref, dec_ref, ofre_ref, ofim_ref, obre_ref, obim_ref):
    nc = fre_ref.shape[0]
    w = fre_ref.shape[1]
    far, fai, bar, bai = (dec_ref[i:i + 1, :] for i in range(4))

    def fwd(c, st):
        re, im = st
        ofre_ref[pl.ds(c, 1), :] = re
        ofim_ref[pl.ds(c, 1), :] = im
        return (far * re - fai * im + fre_ref[pl.ds(c, 1), :],
                far * im + fai * re + fim_ref[pl.ds(c, 1), :])

    def bwd(i, st):
        c = nc - 1 - i
        re, im = st
        obre_ref[pl.ds(c, 1), :] = re
        obim_ref[pl.ds(c, 1), :] = im
        return (bar * re - bai * im + bre_ref[pl.ds(c, 1), :],
                bar * im + bai * re + bim_ref[pl.ds(c, 1), :])

    zero = (jnp.zeros((1, w), F32), jnp.zeros((1, w), F32))
    lax.fori_loop(0, nc, fwd, zero)
    lax.fori_loop(0, nc, bwd, zero)


def _s5_out_kernel(u_ref, t_ref, q_ref, fre_ref, fim_ref, bre_ref, bim_ref, y_ref):
    for gi in range(u_ref.shape[0]):
        lanes = slice(gi * S5_STATE, (gi + 1) * S5_STATE)
        y = jnp.dot(u_ref[gi].astype(BF16), t_ref[gi], preferred_element_type=F32)
        xin = jnp.concatenate([r[:, lanes] for r in (fre_ref, fim_ref, bre_ref, bim_ref)], axis=-1)
        y_ref[gi] = (y + jnp.dot(xin.astype(BF16), q_ref[gi], preferred_element_type=F32)).astype(y_ref.dtype)


def _s5_to_groups_kernel(u_ref, o_ref, cat_ref, *, chunk):
    nc = cat_ref.shape[0]
    gc = S5_GROUP
    for s in range(chunk):
        cat_ref[:, s * LANE:(s + 1) * LANE] = u_ref[pl.ds(s, nc, stride=chunk), :].astype(BF16)
    cat = cat_ref[...]
    r = lax.broadcasted_iota(jnp.int32, (chunk * LANE, chunk * gc), 0)
    c = lax.broadcasted_iota(jnp.int32, (chunk * LANE, chunk * gc), 1)
    lane_of_col = jnp.where(r // LANE == c // gc, lax.rem(c, gc), -LANE)
    for j in range(o_ref.shape[0]):
        sel = jnp.where(lax.rem(r, LANE) - j * gc == lane_of_col, 1.0, 0.0).astype(BF16)
        o_ref[j] = jnp.dot(cat, sel, preferred_element_type=F32).astype(o_ref.dtype)


def _s5_from_groups_kernel(y_ref, o_ref, cat_ref, *, chunk):
    nc = cat_ref.shape[0]
    gc = S5_GROUP
    w = chunk * gc
    for j in range(y_ref.shape[0]):
        cat_ref[:, j * w:(j + 1) * w] = y_ref[j]
    cat = cat_ref[...]
    r = lax.broadcasted_iota(jnp.int32, (cat.shape[1], LANE), 0)
    lane = lax.broadcasted_iota(jnp.int32, (cat.shape[1], LANE), 1)
    src_lane = (r // w) * gc + lax.rem(r, gc)
    t_of_row = lax.rem(r, w) // gc
    for t in range(chunk):
        sel = jnp.where((t_of_row == t) & (src_lane == lane), 1.0, 0.0).astype(BF16)
        o_ref[pl.ds(t, nc, stride=chunk), :] = jnp.dot(cat, sel, preferred_element_type=F32)


def _s5_relayout(x, to_groups, s, g, chunk):
    gc = S5_GROUP
    nc = s // chunk
    per = LANE // gc
    nat = pl.BlockSpec((s, LANE), lambda i: (0, i))
    grp = pl.BlockSpec((per, nc, chunk * gc), lambda i: (i, 0, 0))
    if to_groups:
        kern, in_spec, out_spec = _s5_to_groups_kernel, nat, grp
        out_shape = jax.ShapeDtypeStruct((g, nc, chunk * gc), BF16)
        scratch = pltpu.VMEM((nc, chunk * LANE), BF16)
    else:
        kern, in_spec, out_spec = _s5_from_groups_kernel, grp, nat
        out_shape = jax.ShapeDtypeStruct((s, g * gc), F32)
        scratch = pltpu.VMEM((nc, per * chunk * gc), BF16)
    return pl.pallas_call(
        functools.partial(kern, chunk=chunk),
        out_shape=out_shape,
        grid=(g // per,),
        in_specs=[in_spec],
        out_specs=out_spec,
        scratch_shapes=[scratch],
        compiler_params=_cp(("parallel",)),
        name="s5_to_groups" if to_groups else "s5_from_groups",
    )(x)


def s5_bidirectional(u, a_re, a_im, log_dt, b_re, b_im, c_re, c_im, chunk=S5_CHUNK):
    s, dim = u.shape
    gc, p = S5_GROUP, S5_STATE
    g = dim // gc
    chunk = min(chunk, s)
    nc = s // chunk
    gp = LANE // p
    t_mat, p_mat, q_mat, dec = _s5_matrices(a_re, a_im, log_dt, b_re, b_im, c_re, c_im, chunk)
    t_mat, p_mat, q_mat = t_mat.astype(BF16), p_mat.astype(BF16), q_mat.astype(BF16)
    u_t = _s5_relayout(u, True, s, g, chunk)

    grp = lambda a, b: pl.BlockSpec((gp, a, b), lambda i: (i, 0, 0))
    st = pl.BlockSpec((nc, LANE), lambda i: (0, i))
    st_shape = jax.ShapeDtypeStruct((nc, g * p), F32)
    xe = pl.pallas_call(
        _s5_state_kernel,
        out_shape=[st_shape] * 4,
        grid=(g // gp,),
        in_specs=[grp(nc, chunk * gc), grp(chunk * gc, 4 * p)],
        out_specs=[st] * 4,
        compiler_params=_cp(("parallel",)),
        name="s5_chunk_states",
    )(u_t, p_mat)

    lanes = min(4 * LANE, g * p)
    cols = lambda rows: pl.BlockSpec((rows, lanes), lambda i: (0, i))
    xin = pl.pallas_call(
        _s5_scan_kernel,
        out_shape=[st_shape] * 4,
        grid=(g * p // lanes,),
        in_specs=[cols(nc)] * 4 + [cols(4)],
        out_specs=[cols(nc)] * 4,
        compiler_params=_cp(("parallel",)),
        name="s5_boundary_scan",
    )(*xe, dec)

    y_t = pl.pallas_call(
        _s5_out_kernel,
        out_shape=jax.ShapeDtypeStruct((g, nc, chunk * gc), BF16),
        grid=(g // gp,),
        in_specs=[grp(nc, chunk * gc), grp(chunk * gc, chunk * gc), grp(4 * p, chunk * gc)] + [st] * 4,
        out_specs=grp(nc, chunk * gc),
        compiler_params=_cp(("parallel",)),
        name="s5_outputs",
    )(u_t, t_mat, q_mat, *xin)
    return _s5_relayout(y_t, False, s, g, chunk)


def _s5_gate_kernel(y_ref, u_ref, d_ref, w_ref, o_ref):
    y = y_ref[...].astype(F32) + d_ref[...] * u_ref[...]
    y = 0.5 * y * (1.0 + jnp.tanh(math.sqrt(2.0 / math.pi) * (y + 0.044715 * (y * y * y))))
    z = jnp.dot(y.astype(BF16), w_ref[...], preferred_element_type=F32)
    o_ref[...] = (y * _sigmoid(z)).astype(o_ref.dtype)


def s5_gate(y, u, d_skip, w_glu, tm=1024):
    s, dim = y.shape
    tm = min(tm, s)
    row = pl.BlockSpec((tm, dim), lambda i: (i, 0))
    return pl.pallas_call(
        _s5_gate_kernel,
        out_shape=jax.ShapeDtypeStruct((s, dim), BF16),
        grid=(s // tm,),
        in_specs=[row, row, pl.BlockSpec((1, dim), lambda i: (0, 0)), pl.BlockSpec((dim, dim), lambda i: (0, 0))],
        out_specs=row,
        compiler_params=_cp(("parallel",)),
        name="s5_gate",
    )(y, u, d_skip.reshape(1, dim), w_glu.astype(BF16))


def _router_kernel(x_ref, g_ref, sh_ref, sc_ref, rw_ref, h_ref, info_ref):
    h = _modnorm(x_ref[...], g_ref[...], sh_ref[...], sc_ref[...])
    half = h.shape[1] // 2
    bits = lambda v: lax.bitcast_convert_type(v.astype(BF16).astype(F32), jnp.uint32)
    h_ref[...] = (bits(h[:, :half]) >> 16) | (bits(h[:, half:]) & jnp.uint32(0xFFFF0000))
    logits = jnp.dot(h, rw_ref[...], preferred_element_type=F32, precision=lax.Precision.HIGHEST)
    lane = lax.broadcasted_iota(jnp.int32, logits.shape, 1)
    logits = jnp.where(lane < N_EXPERTS, logits, -jnp.inf)
    m1 = jnp.max(logits, axis=-1, keepdims=True)
    i1 = jnp.min(jnp.where(logits == m1, lane, LANE), axis=-1, keepdims=True)
    rest = jnp.where(lane == i1, -jnp.inf, logits)
    m2 = jnp.max(rest, axis=-1, keepdims=True)
    i2 = jnp.min(jnp.where(rest == m2, lane, LANE), axis=-1, keepdims=True)
    e = jnp.exp(m2 - m1)
    g1 = 1.0 / (1.0 + e)
    g2 = e / (1.0 + e)
    info = jnp.where(lane == 0, i1.astype(F32),
                     jnp.where(lane == 1, i2.astype(F32),
                               jnp.where(lane == 2, g1, jnp.where(lane == 3, g2, 0.0))))
    info_ref[...] = info


def moe_router(x, g, sh, sc, router_w, tm=512):
    s, d = x.shape
    tm = min(tm, s)
    rw = jnp.pad(router_w, ((0, 0), (0, LANE - router_w.shape[1])))
    vec = pl.BlockSpec((1, d), lambda i: (0, 0))
    return pl.pallas_call(
        _router_kernel,
        out_shape=[jax.ShapeDtypeStruct((s, d // 2), jnp.uint32), jax.ShapeDtypeStruct((s, LANE), F32)],
        grid=(s // tm,),
        in_specs=[pl.BlockSpec((tm, d), lambda i: (i, 0)), vec, vec, vec,
                  pl.BlockSpec((d, LANE), lambda i: (0, 0))],
        out_specs=[pl.BlockSpec((tm, d // 2), lambda i: (i, 0)), pl.BlockSpec((tm, LANE), lambda i: (i, 0))],
        compiler_params=_cp(("parallel",)),
        name="moe_router",
    )(x, g, sh, sc, rw)


def _expert_kernel(te_ref, tv_ref, src_ref, hp_hbm, w1_ref, w3_ref, w2_ref, o_ref, acc_ref, gbuf, h_ref, sem):
    t = pl.program_id(0)
    f = pl.program_id(1)
    nt = pl.num_programs(0)
    nf = pl.num_programs(1)
    tm = acc_ref.shape[0]
    per_step = tm // nf
    valid = tv_ref[t] > 0
    slot = lax.rem(t, 2)

    def row_copy(tile, buf, r):
        return pltpu.make_async_copy(hp_hbm.at[pl.ds(src_ref[tile * tm + r], 1)],
                                     gbuf.at[buf, pl.ds(r, 1)], sem.at[buf])

    def wait_half(buf):
        pltpu.make_async_copy(hp_hbm.at[pl.ds(0, tm)], gbuf.at[buf], sem.at[buf]).wait()

    @pl.when((t == 0) & (f == 0))
    def _():
        def issue(r, carry):
            row_copy(0, 0, r).start()
            return carry
        lax.fori_loop(0, tm, issue, 0)

    @pl.when((f == 0) & ((t == 0) | (tv_ref[jnp.maximum(t - 1, 0)] > 0)))
    def _():
        wait_half(slot)

    @pl.when((f == 0) & valid)
    def _():
        w = gbuf[slot]
        half = w.shape[1]
        h_ref[:, :half] = lax.bitcast_convert_type(w << 16, F32).astype(BF16)
        h_ref[:, half:] = lax.bitcast_convert_type(w & jnp.uint32(0xFFFF0000), F32).astype(BF16)

    @pl.when(f == 0)
    def _():
        acc_ref[...] = jnp.zeros_like(acc_ref)

    def compute(rows):
        for u in range(per_step):
            row_copy(t + 1, 1 - slot, f * per_step + u).start()
        h = h_ref[:rows, :]
        u = jnp.dot(h, w1_ref[...].astype(BF16), preferred_element_type=F32)
        v = jnp.dot(h, w3_ref[...].astype(BF16), preferred_element_type=F32)
        acc_ref[:rows, :] += jnp.dot((_silu(u) * v).astype(BF16), w2_ref[...].astype(BF16),
                                     preferred_element_type=F32)

    @pl.when(tv_ref[t] > tm // 2)
    def _():
        compute(tm)

    @pl.when(valid & (tv_ref[t] <= tm // 2))
    def _():
        compute(tm // 2)

    @pl.when(f == nf - 1)
    def _():
        o_ref[...] = acc_ref[...].astype(o_ref.dtype)

    @pl.when((t == nt - 1) & (f == nf - 1) & valid)
    def _():
        wait_half(1 - slot)


def expert_ffn(h_packed, src, tile_expert, tile_valid, w1, w3, w2, tm, tf=512):
    dh = h_packed.shape[1]
    d = 2 * dh
    n = src.shape[0] - tm
    f = w1.shape[2]
    tf = min(tf, f)
    nf = f // tf
    assert tm % nf == 0 and n % tm == 0

    def fblk(j, t, tv):
        return jnp.where(tv[t] > 0, j, nf - 1)

    return pl.pallas_call(
        _expert_kernel,
        out_shape=jax.ShapeDtypeStruct((n, d), BF16),
        grid_spec=pltpu.PrefetchScalarGridSpec(
            num_scalar_prefetch=3,
            grid=(n // tm, nf),
            in_specs=[pl.BlockSpec(memory_space=pl.ANY),
                      pl.BlockSpec((None, d, tf), lambda t, j, te, tv, sr: (te[t], 0, fblk(j, t, tv))),
                      pl.BlockSpec((None, d, tf), lambda t, j, te, tv, sr: (te[t], 0, fblk(j, t, tv))),
                      pl.BlockSpec((None, tf, d), lambda t, j, te, tv, sr: (te[t], fblk(j, t, tv), 0))],
            out_specs=pl.BlockSpec((tm, d), lambda t, j, te, tv, sr: (t, 0)),
            scratch_shapes=[pltpu.VMEM((tm, d), F32), pltpu.VMEM((2, tm, dh), jnp.uint32),
                            pltpu.VMEM((tm, d), BF16), pltpu.SemaphoreType.DMA((2,))]),
        compiler_params=_cp(("arbitrary", "arbitrary")),
        name="expert_ffn",
    )(tile_expert, tile_valid, src, h_packed, w1, w3, w2)


def _dispatch(idx, tm):
    s = idx.shape[0]
    e_flat = idx.reshape(-1)
    onehot = (e_flat[:, None] == jnp.arange(N_EXPERTS)[None, :]).astype(jnp.int32)
    csum = jnp.cumsum(onehot, axis=0)
    rank = jnp.sum((csum - onehot) * onehot, axis=1)
    counts = csum[-1]
    padded = ((counts + tm - 1) // tm) * tm
    gend = jnp.cumsum(padded)
    slot = (gend - padded)[e_flat] + rank
    nt = -(-(2 * s) // tm) + N_EXPERTS
    src = jnp.zeros(((nt + 1) * tm,), jnp.int32).at[slot].set(jnp.arange(2 * s, dtype=jnp.int32) // 2)
    tstart = jnp.arange(nt, dtype=jnp.int32) * tm
    valid = tstart < gend[-1]
    te = jnp.minimum(jnp.sum((tstart[:, None] >= gend[None, :]).astype(jnp.int32), axis=1), N_EXPERTS - 1)
    nvalid = gend[-1] // tm
    fill = jnp.clip((gend - padded + counts)[te] - tstart, 0, tm)
    fill = jnp.where(valid, fill, 0)
    te = jnp.where(valid, te, te[jnp.maximum(nvalid - 1, 0)])
    return slot.reshape(s, 2), src, te.astype(jnp.int32), fill.astype(jnp.int32)


def _combine_kernel(ya_ref, yb_ref, info_ref, x_ref, g_ref, gt_ref, o_ref):
    info = info_ref[...]
    y = info[:, 2:3] * ya_ref[...].astype(F32) + info[:, 3:4] * yb_ref[...].astype(F32)
    o_ref[...] = x_ref[...] + gt_ref[...] * (_rms(y) * g_ref[...])


def moe_combine(ya, yb, info, x, g, gt, tm=512):
    s, d = x.shape
    tm = min(tm, s)
    row = lambda w: pl.BlockSpec((tm, w), lambda i: (i, 0))
    vec = pl.BlockSpec((1, d), lambda i: (0, 0))
    return pl.pallas_call(
        _combine_kernel,
        out_shape=jax.ShapeDtypeStruct((s, d), F32),
        grid=(s // tm,),
        in_specs=[row(d), row(d), row(LANE), row(d), vec, vec],
        out_specs=row(d),
        compiler_params=_cp(("parallel",)),
        name="moe_combine",
    )(ya, yb, info, x, g, gt)


def moe_sublayer(x, g, sh, sc, g2, gt, router_w, w1, w3, w2, tm_e=EXPERT_TILE):
    s = x.shape[0]
    tm_e = min(tm_e, s)
    h_packed, info = moe_router(x, g, sh, sc, router_w)
    idx = info[:, :2].astype(jnp.int32)
    slot, src, te, tv = _dispatch(idx, tm_e)
    y_sorted = expert_ffn(h_packed, src, te, tv, w1, w3, w2, tm_e)
    ya = jnp.take(y_sorted, slot[:, 0], axis=0)
    yb = jnp.take(y_sorted, slot[:, 1], axis=0)
    return moe_combine(ya, yb, info, x, g2, gt)


def kernel(x, c, t5_table, norm_g, ada_w, ada_b, e_w_in, e_conv_w, e_conv_b, e_cln_g, e_cln_b, e_w_out, e_ffn_w1, e_ffn_w3, e_ffn_w2, o_w_in, o_q_norm_g, o_w_uq, o_kv_norm_g, o_w_ukv, s5_a_re, s5_a_im, s5_log_dt, s5_b_re, s5_b_im, s5_c_re, s5_c_im, s5_d, s5_w_glu, o_w_out, router_w, moe_w1, moe_w3, moe_w2):
    bsz, seq, d = x.shape
    assert bsz == 1
    depth = norm_g.shape[0]
    conv_dim = e_conv_w.shape[2]
    xs = x.reshape(seq, d)
    mod = adaln(c, ada_w, ada_b)
    for layer in range(depth):
        i = layer // 2
        sh_m, sc_m, gt_m, sh_f, sc_f, gt_f = (mod[layer, :, k * d:(k + 1) * d] for k in range(6))
        g = [norm_g[layer, k].reshape(1, d) for k in range(4)]
        if layer % 2 == 0:
            z = norm_mod_matmul(xs, g[0], sh_m, sc_m, e_w_in[i].astype(BF16))
            a = conformer_conv(z, e_conv_w[i], e_conv_b[i], e_cln_g[i], e_cln_b[i])
            o = dilated_attention(z, t5_table, 2 * conv_dim, (z.shape[1] - 2 * conv_dim) // 3)
            xs = out_proj_residual(a, o, e_w_out[i].astype(BF16), xs, g[1], gt_m)
            xs = ffn_sublayer(xs, g[2], sh_f, sc_f, g[3], gt_f, e_ffn_w1[i].astype(BF16),
                              e_ffn_w3[i].astype(BF16), e_ffn_w2[i].astype(BF16))
        else:
            q, k, v, u = odd_in_proj(xs, g[0], sh_m, sc_m, o_w_in[i], o_q_norm_g[i], o_w_uq[i],
                                     o_kv_norm_g[i], o_w_ukv[i])
            o_mla = mla_attention(q, k, v)
            y = s5_bidirectional(u, s5_a_re[i], s5_a_im[i], s5_log_dt[i], s5_b_re[i], s5_b_im[i],
                                 s5_c_re[i], s5_c_im[i])
            y = s5_gate(y, u, s5_d[i], s5_w_glu[i])
            xs = out_proj_residual(o_mla, y, o_w_out[i].astype(BF16), xs, g[1], gt_m)
            xs = moe_sublayer(xs, g[2], sh_f, sc_f, g[3], gt_f, router_w[i], moe_w1[i], moe_w3[i], moe_w2[i])
    return xs.reshape(bsz, seq, d)
```

```python
import functools
import math

import jax
import jax.numpy as jnp
from jax import lax
from jax.experimental import pallas as pl
from jax.experimental.pallas import tpu as pltpu

F32 = jnp.float32
BF16 = jnp.bfloat16

RMS_EPS = 1e-6
LN_EPS = 1e-5
NEG_INF = -1e30
LOG2E = math.log2(math.e)

CONV_WIDTH = 31
CONV_HALO = 16
DIL_HEAD_DIM = 64
DIL_CONFIGS = ((128, 1), (512, 4), (2048, 16))
DIL_HALF = 64
DIL_TILE = 1024
DIL_QB = 256
DIL_MERGE = 2
N_BUCKETS = 32
T5_MAX_DIST = DIL_CONFIGS[-1][0] // 2
MLA_HEADS = 12
NOPE_DIM = 128
ROPE_DIM = 64
V_DIM = 128
ROPE_BASE = 10000.0
MLA_QK_PAD = 256
S5_GROUP = 16
S5_STATE = 64
S5_CHUNK = 16
N_EXPERTS = 8
EXPERT_TILE = 896
LANE = 128
SUBLANES = 8
VMEM_LIMIT_MB = 56


def _cp(sem, vmem_mb=VMEM_LIMIT_MB):
    return pltpu.CompilerParams(dimension_semantics=sem, vmem_limit_bytes=vmem_mb * 1024 * 1024)


def _rms(x):
    return x * lax.rsqrt(jnp.mean(x * x, axis=-1, keepdims=True) + RMS_EPS)


def _modnorm(x, g, sh, sc):
    return (_rms(x) * g) * (1.0 + sc) + sh


def _sigmoid(x):
    return 1.0 / (1.0 + jnp.exp(-x))


def _silu(x):
    return x * _sigmoid(x)


def _adaln_kernel(c_ref, w_ref, b_ref, o_ref, cb_ref):
    @pl.when((pl.program_id(0) == 0) & (pl.program_id(1) == 0))
    def _():
        cc = c_ref[...]
        cb_ref[...] = jnp.broadcast_to(_silu(cc), cb_ref.shape)

    cb = cb_ref[...]
    for j in range(o_ref.shape[-1] // LANE):
        sl = slice(j * LANE, (j + 1) * LANE)
        o_ref[:, sl] = jnp.sum(w_ref[:, sl] * cb, axis=0, keepdims=True) + b_ref[:, sl]


def adaln(c, ada_w, ada_b, tn=1024):
    nl, d, n = ada_w.shape
    out = pl.pallas_call(
        _adaln_kernel,
        out_shape=jax.ShapeDtypeStruct((nl, 1, n), F32),
        grid=(nl, n // tn),
        in_specs=[pl.BlockSpec((d, 1), lambda l, j: (0, 0)),
                  pl.BlockSpec((None, d, tn), lambda l, j: (l, 0, j)),
                  pl.BlockSpec((None, 1, tn), lambda l, j: (l, 0, j))],
        out_specs=pl.BlockSpec((None, 1, tn), lambda l, j: (l, 0, j)),
        scratch_shapes=[pltpu.VMEM((d, LANE), F32)],
        compiler_params=_cp(("arbitrary", "arbitrary")),
        name="adaln",
    )(c.reshape(d, 1), ada_w, ada_b.reshape(nl, 1, n))
    return out


def _nmm_kernel(x_ref, g_ref, sh_ref, sc_ref, w_ref, o_ref, h_ref):
    @pl.when(pl.program_id(1) == 0)
    def _():
        h_ref[...] = _modnorm(x_ref[...], g_ref[...], sh_ref[...], sc_ref[...]).astype(h_ref.dtype)

    o_ref[...] = jnp.dot(h_ref[...], w_ref[...], preferred_element_type=F32).astype(o_ref.dtype)


def norm_mod_matmul(x, g, sh, sc, w, tm=1024, tn=1024):
    s, d = x.shape
    n = w.shape[1]
    tm, tn = min(tm, s), min(tn, n)
    vec = pl.BlockSpec((1, d), lambda i, j: (0, 0))
    return pl.pallas_call(
        _nmm_kernel,
        out_shape=jax.ShapeDtypeStruct((s, n), BF16),
        grid=(s // tm, n // tn),
        in_specs=[pl.BlockSpec((tm, d), lambda i, j: (i, 0)), vec, vec, vec,
                  pl.BlockSpec((d, tn), lambda i, j: (0, j))],
        out_specs=pl.BlockSpec((tm, tn), lambda i, j: (i, j)),
        scratch_shapes=[pltpu.VMEM((tm, d), BF16)],
        compiler_params=_cp(("parallel", "arbitrary")),
        name="norm_mod_matmul",
    )(x, g, sh, sc, w)


def _conv_kernel(av_ref, ag_ref, avp_ref, agp_ref, avn_ref, agn_ref, w_ref, b_ref, lg_ref, lb_ref,
                 o_ref, buf_ref, sh_ref, *, rows):
    i = pl.program_id(0)
    n = pl.num_programs(0)
    ts = av_ref.shape[0]

    def glu(v_ref, g_ref):
        return v_ref[...].astype(F32) * _sigmoid(g_ref[...].astype(F32))

    buf_ref[0:CONV_HALO, :] = jnp.where(i > 0, glu(avp_ref, agp_ref), 0.0)
    buf_ref[CONV_HALO:CONV_HALO + ts, :] = glu(av_ref, ag_ref)
    buf_ref[CONV_HALO + ts:2 * CONV_HALO + ts, :] = jnp.where(i < n - 1, glu(avn_ref, agn_ref), 0.0)
    buf_ref[2 * CONV_HALO + ts:, :] = jnp.zeros((SUBLANES, buf_ref.shape[1]), F32)

    span = ts + 2 * CONV_HALO
    for o in range(SUBLANES):
        sh_ref[o] = buf_ref[o:o + span, :]

    off = CONV_HALO - CONV_WIDTH // 2
    for r in range(ts // rows):
        acc = jnp.broadcast_to(b_ref[...], (rows, b_ref.shape[-1]))
        for k in range(CONV_WIDTH):
            lo = r * rows + k + off
            base = lo - lo % SUBLANES
            acc = acc + w_ref[k:k + 1, :] * sh_ref[lo % SUBLANES, base:base + rows, :]
        mu = jnp.mean(acc, axis=-1, keepdims=True)
        dlt = acc - mu
        var = jnp.mean(dlt * dlt, axis=-1, keepdims=True)
        y = dlt * lax.rsqrt(var + LN_EPS) * lg_ref[...] + lb_ref[...]
        o_ref[r * rows:(r + 1) * rows, :] = _silu(y).astype(o_ref.dtype)


def conformer_conv(z, conv_w, conv_b, cln_g, cln_b, ts=256, rows=32):
    s = z.shape[0]
    c = conv_w.shape[1]
    ts = min(ts, s)
    hb = ts // CONV_HALO
    nh = s // CONV_HALO
    main = lambda col: pl.BlockSpec((ts, c), lambda i: (i, col))
    prev = lambda col: pl.BlockSpec((CONV_HALO, c), lambda i: (jnp.maximum(i * hb - 1, 0), col))
    nxt = lambda col: pl.BlockSpec((CONV_HALO, c), lambda i: (jnp.minimum((i + 1) * hb, nh - 1), col))
    vec = lambda r: pl.BlockSpec((r, c), lambda i: (0, 0))
    return pl.pallas_call(
        functools.partial(_conv_kernel, rows=min(rows, ts)),
        out_shape=jax.ShapeDtypeStruct((s, c), BF16),
        grid=(s // ts,),
        in_specs=[main(0), main(1), prev(0), prev(1), nxt(0), nxt(1),
                  vec(CONV_WIDTH), vec(1), vec(1), vec(1)],
        out_specs=pl.BlockSpec((ts, c), lambda i: (i, 0)),
        scratch_shapes=[pltpu.VMEM((ts + 2 * CONV_HALO + SUBLANES, c), F32),
                        pltpu.VMEM((SUBLANES, ts + 2 * CONV_HALO, c), F32)],
        compiler_params=_cp(("parallel",)),
        name="conformer_conv",
    )(z, z, z, z, z, z, conv_w, conv_b.reshape(1, c), cln_g.reshape(1, c), cln_b.reshape(1, c))


def _t5_bucket(rel):
    half = N_BUCKETS // 2
    exact = half // 2
    n = jnp.abs(rel)
    large = exact + (jnp.log(jnp.maximum(n, 1).astype(F32) / exact)
                     / math.log(T5_MAX_DIST / exact) * (half - exact)).astype(jnp.int32)
    large = jnp.minimum(large, half - 1)
    return jnp.where(rel > 0, half, 0) + jnp.where(n < exact, n, large)


def _band_diag(t5_table, dil, qb):
    kl = qb + 2 * DIL_HALF
    n = -(-(qb + kl - 1) // LANE) * LANE
    dist = jnp.arange(n) - (qb - 1) - DIL_HALF
    return jnp.where((jnp.abs(dist) <= DIL_HALF)[None], t5_table[_t5_bucket(dist * dil)].T.astype(F32), NEG_INF)


def _band_bias_rows(diag_row, qb):
    n = diag_row.shape[1]
    rolled = pltpu.roll(jnp.broadcast_to(diag_row, (qb, n)), n - (qb - 1), 1, stride=1, stride_axis=0)
    return rolled[:, :qb + 2 * DIL_HALF]


def _band_group(q, kw, vw, bias_ref, pen):
    r = q.shape[0]
    is_a = lax.broadcasted_iota(jnp.int32, (1, LANE), 1) < DIL_HEAD_DIM
    q2 = jnp.concatenate([jnp.where(is_a, q, 0.0), jnp.where(is_a, 0.0, q)], axis=0).astype(BF16)
    s = lax.dot_general(q2, kw.astype(BF16), (((1,), (1,)), ((), ())), preferred_element_type=F32)
    s = s + bias_ref[...]
    if pen is not None:
        s = s + pen
    m = jnp.max(s, axis=-1, keepdims=True)
    p = jnp.exp2((s - m).astype(BF16))
    vb = vw.astype(BF16)
    pv = jnp.dot(p, jnp.concatenate([vb, jnp.ones_like(vb)], axis=1), preferred_element_type=F32)
    pick = lambda x: jnp.where(is_a, x[:r], x[r:])
    return pick(pv[:, :LANE]), pick(m), pick(pv[:, LANE:])


def _dil_kernel(q_ref, kp_ref, kc_ref, kn_ref, vp_ref, vc_ref, vn_ref, d1_ref, d4_ref, d16_ref, o_ref,
                qf, kf, vf, a1, m1, l1, a4, m4, l4, a16, m16, l16, b1_ref, b4_ref, b16_ref):
    i = pl.program_id(1)
    first = i == 0
    last = i == pl.num_programs(1) - 1
    t = DIL_TILE
    half = DIL_HALF

    @pl.when(first)
    def _():
        q16 = t // 16
        b16_ref[...] = jnp.full(b16_ref.shape, NEG_INF, F32)
        for h in range(2):
            b1_ref[h * DIL_QB:(h + 1) * DIL_QB, :] = _band_bias_rows(d1_ref[h:h + 1, :], DIL_QB)
            b4_ref[h * DIL_QB:(h + 1) * DIL_QB, :] = _band_bias_rows(d4_ref[h:h + 1, :], DIL_QB)
            blk = _band_bias_rows(d16_ref[h:h + 1, :], q16)
            for u in range(DIL_MERGE):
                r0 = (h * DIL_MERGE + u) * q16
                b16_ref[r0:r0 + q16, u * blk.shape[1]:(u + 1) * blk.shape[1]] = blk

    qf[...] = q_ref[...].astype(F32) * (DIL_HEAD_DIM ** -0.5 * LOG2E)
    for dst, (p_ref, c_ref, n_ref) in ((kf, (kp_ref, kc_ref, kn_ref)), (vf, (vp_ref, vc_ref, vn_ref))):
        dst[0:t, :] = p_ref[...].astype(F32)
        dst[t:2 * t, :] = c_ref[...].astype(F32)
        dst[2 * t:3 * t, :] = n_ref[...].astype(F32)

    def store(refs, rows, vals):
        for ref, val in zip(refs, vals):
            ref[rows, :] = val

    qb = DIL_QB
    kl = qb + 2 * half
    col = lax.broadcasted_iota(jnp.int32, (1, kl), 1)
    lo = jnp.where((col < half) & first, NEG_INF, 0.0)
    hi = jnp.where((col >= qb + half) & last, NEG_INF, 0.0)

    nb = t // qb
    for b in range(nb):
        pen = lo + hi if nb == 1 else (lo if b == 0 else (hi if b == nb - 1 else None))
        k0 = t + b * qb - half
        rows = slice(b * qb, (b + 1) * qb)
        store((a1, m1, l1), rows, _band_group(qf[rows, :], kf[k0:k0 + kl, :], vf[k0:k0 + kl, :], b1_ref, pen))

    dil = 4
    for r in range(dil):
        qrows = pl.ds(r, qb, stride=dil)
        krows = pl.ds(t - half * dil + r, kl, stride=dil)
        store((a4, m4, l4), qrows, _band_group(qf[qrows, :], kf[krows, :], vf[krows, :], b4_ref, lo + hi))

    dil = 16
    q16 = t // dil
    k16 = q16 + 2 * half
    col = lax.rem(lax.broadcasted_iota(jnp.int32, (1, DIL_MERGE * k16), 1), k16)
    pen16 = (jnp.where((col < half) & first, NEG_INF, 0.0)
             + jnp.where((col >= q16 + half) & last, NEG_INF, 0.0))
    for g in range(dil // DIL_MERGE):
        res = range(g * DIL_MERGE, (g + 1) * DIL_MERGE)
        qrows = [pl.ds(r, q16, stride=dil) for r in res]
        krows = [pl.ds(t - half * dil + r, k16, stride=dil) for r in res]
        out = _band_group(jnp.concatenate([qf[rr, :] for rr in qrows], axis=0),
                          jnp.concatenate([kf[rr, :] for rr in krows], axis=0),
                          jnp.concatenate([vf[rr, :] for rr in krows], axis=0), b16_ref, pen16)
        for u, rr in enumerate(qrows):
            store((a16, m16, l16), rr, [x[u * q16:(u + 1) * q16] for x in out])

    mm = jnp.maximum(jnp.maximum(m1[...], m4[...]), m16[...])
    w1, w4, w16 = jnp.exp2(m1[...] - mm), jnp.exp2(m4[...] - mm), jnp.exp2(m16[...] - mm)
    num = w1 * a1[...] + w4 * a4[...] + w16 * a16[...]
    den = w1 * l1[...] + w4 * l4[...] + w16 * l16[...]
    o_ref[...] = (num / den).astype(o_ref.dtype)


def dilated_attention(z, t5_table, col0, width):
    s, zw = z.shape
    t = DIL_TILE
    assert s % t == 0 and col0 % LANE == 0 and width % LANE == 0 and t // 4 == DIL_QB
    nt = s // t
    nh = t5_table.shape[1]
    cq, ck, cv = ((col0 + k * width) // LANE for k in range(3))
    q16 = t // 16
    diags = [(_band_diag(t5_table, dil, qb) * LOG2E).reshape(nh // 2, 2, -1)
             for dil, qb in ((1, DIL_QB), (4, DIL_QB), (16, q16))]
    kl, k16 = DIL_QB + 2 * DIL_HALF, q16 + 2 * DIL_HALF

    def blk(col, shift):
        return pl.BlockSpec((t, LANE), lambda hg, i: (jnp.clip(i + shift, 0, nt - 1), col + hg))

    diag_spec = lambda d: pl.BlockSpec((None,) + d.shape[1:], lambda hg, i: (hg, 0, 0))
    return pl.pallas_call(
        _dil_kernel,
        out_shape=jax.ShapeDtypeStruct((s, width), BF16),
        grid=(width // LANE, nt),
        in_specs=[blk(cq, 0), blk(ck, -1), blk(ck, 0), blk(ck, 1), blk(cv, -1), blk(cv, 0), blk(cv, 1)]
                 + [diag_spec(d) for d in diags],
        out_specs=pl.BlockSpec((t, LANE), lambda hg, i: (i, hg)),
        scratch_shapes=[pltpu.VMEM((t, LANE), F32), pltpu.VMEM((3 * t, LANE), F32),
                        pltpu.VMEM((3 * t, LANE), F32)] + [pltpu.VMEM((t, LANE), F32)] * 9
                       + [pltpu.VMEM((2 * DIL_QB, kl), F32), pltpu.VMEM((2 * DIL_QB, kl), F32),
                          pltpu.VMEM((2 * DIL_MERGE * q16, DIL_MERGE * k16), F32)],
        compiler_params=_cp(("parallel", "arbitrary")),
        name="dilated_attention",
    )(*([z] * 7 + diags))


def _oproj_kernel(a1_ref, a2_ref, w_ref, x_ref, g_ref, gt_ref, o_ref):
    k1 = a1_ref.shape[1]
    y = jnp.dot(a1_ref[...], w_ref[0:k1, :], preferred_element_type=F32)
    y = y + jnp.dot(a2_ref[...], w_ref[k1:, :], preferred_element_type=F32)
    o_ref[...] = x_ref[...] + gt_ref[...] * (_rms(y) * g_ref[...])


def out_proj_residual(a1, a2, w, x, g, gt, tm=512):
    s, d = x.shape
    k1, k2 = a1.shape[1], a2.shape[1]
    tm = min(tm, s)
    vec = pl.BlockSpec((1, d), lambda i: (0, 0))
    return pl.pallas_call(
        _oproj_kernel,
        out_shape=jax.ShapeDtypeStruct((s, d), F32),
        grid=(s // tm,),
        in_specs=[pl.BlockSpec((tm, k1), lambda i: (i, 0)), pl.BlockSpec((tm, k2), lambda i: (i, 0)),
                  pl.BlockSpec((k1 + k2, d), lambda i: (0, 0)),
                  pl.BlockSpec((tm, d), lambda i: (i, 0)), vec, vec],
        out_specs=pl.BlockSpec((tm, d), lambda i: (i, 0)),
        compiler_params=_cp(("parallel",)),
        name="out_proj_residual",
    )(a1, a2, w, x, g, gt)


def _ffn_kernel(x_ref, g_ref, sh_ref, sc_ref, g2_ref, gt_ref, w1_ref, w3_ref, w2_ref, o_ref, h_ref):
    f = pl.program_id(1)

    @pl.when(f == 0)
    def _():
        h_ref[...] = _modnorm(x_ref[...], g_ref[...], sh_ref[...], sc_ref[...]).astype(h_ref.dtype)
        o_ref[...] = jnp.zeros_like(o_ref)

    h = h_ref[...]
    u = jnp.dot(h, w1_ref[...], preferred_element_type=F32)
    v = jnp.dot(h, w3_ref[...], preferred_element_type=F32)
    o_ref[...] += jnp.dot((_silu(u) * v).astype(BF16), w2_ref[...], preferred_element_type=F32)

    @pl.when(f == pl.num_programs(1) - 1)
    def _():
        o_ref[...] = x_ref[...] + gt_ref[...] * (_rms(o_ref[...]) * g2_ref[...])


def ffn_sublayer(x, g, sh, sc, g2, gt, w1, w3, w2, tm=1024, tf=256):
    s, d = x.shape
    f = w1.shape[1]
    tm, tf = min(tm, s), min(tf, f)
    vec = pl.BlockSpec((1, d), lambda i, j: (0, 0))
    return pl.pallas_call(
        _ffn_kernel,
        out_shape=jax.ShapeDtypeStruct((s, d), F32),
        grid=(s // tm, f // tf),
        in_specs=[pl.BlockSpec((tm, d), lambda i, j: (i, 0)), vec, vec, vec, vec, vec,
                  pl.BlockSpec((d, tf), lambda i, j: (0, j)), pl.BlockSpec((d, tf), lambda i, j: (0, j)),
                  pl.BlockSpec((tf, d), lambda i, j: (j, 0))],
        out_specs=pl.BlockSpec((tm, d), lambda i, j: (i, 0)),
        scratch_shapes=[pltpu.VMEM((tm, d), BF16)],
        compiler_params=_cp(("parallel", "arbitrary")),
        name="ffn_sublayer",
    )(x, g, sh, sc, g2, gt, w1, w3, w2)


def _rope_coeffs(tab):
    half = ROPE_DIM // 2
    lane = lax.broadcasted_iota(jnp.int32, (1, LANE), 1)
    first, second = lane < half, (lane >= half) & (lane < 2 * half)
    c = jnp.where(first, tab, 0.0) + jnp.where(second, pltpu.roll(tab, half, 1), 0.0)
    s1 = jnp.where(first, -pltpu.roll(tab, LANE - half, 1), 0.0)
    s2 = jnp.where(second, tab, 0.0)
    return c, s1, s2


def _rope_slab(t, coeffs):
    c, s1, s2 = coeffs
    return t * c + pltpu.roll(t, LANE - ROPE_DIM // 2, 1) * s1 + pltpu.roll(t, ROPE_DIM // 2, 1) * s2


def _oddin_kernel(x_ref, g_ref, sh_ref, sc_ref, win_ref, qg_ref, wuq_ref, kvg_ref, wuk_ref, wuv_ref,
                  rope_ref, q_ref, k_ref, v_ref, u_ref, *, q_lora, kv_lora, scale):
    h = _modnorm(x_ref[...], g_ref[...], sh_ref[...], sc_ref[...]).astype(BF16)
    z = jnp.dot(h, win_ref[...], preferred_element_type=F32)
    o_kv, o_pe, o_u = q_lora, q_lora + kv_lora, q_lora + kv_lora + LANE
    u_ref[...] = z[:, o_u:]

    qn = (_rms(z[:, :q_lora]) * qg_ref[...]).astype(BF16)
    q = jnp.dot(qn, wuq_ref[...], preferred_element_type=F32)
    kvn = (_rms(z[:, o_kv:o_pe]) * kvg_ref[...]).astype(BF16)
    kn = jnp.dot(kvn, wuk_ref[...], preferred_element_type=F32)
    v_ref[...] = jnp.dot(kvn, wuv_ref[...], preferred_element_type=F32).astype(v_ref.dtype)
    coeffs = _rope_coeffs(rope_ref[...])
    kpe = _rope_slab(z[:, o_pe:o_u], coeffs).astype(k_ref.dtype)

    for hh in range(MLA_HEADS):
        b = hh * MLA_QK_PAD
        q_ref[:, b:b + NOPE_DIM] = (q[:, b:b + NOPE_DIM] * scale).astype(q_ref.dtype)
        qpe = _rope_slab(q[:, b + NOPE_DIM:b + MLA_QK_PAD], coeffs)
        q_ref[:, b + NOPE_DIM:b + MLA_QK_PAD] = (qpe * scale).astype(q_ref.dtype)
        k_ref[:, b:b + NOPE_DIM] = kn[:, hh * NOPE_DIM:(hh + 1) * NOPE_DIM].astype(k_ref.dtype)
        k_ref[:, b + NOPE_DIM:b + MLA_QK_PAD] = kpe


def _rope_tables(seq):
    pos = jnp.arange(seq, dtype=F32)
    inv_freq = ROPE_BASE ** (-jnp.arange(0, ROPE_DIM, 2, dtype=F32) / ROPE_DIM)
    ang = pos[:, None] * inv_freq[None, :]
    pad = jnp.zeros((seq, LANE - ROPE_DIM), F32)
    return jnp.concatenate([jnp.cos(ang), jnp.sin(ang), pad], axis=-1)


def odd_in_proj(x, g, sh, sc, w_in, q_norm_g, w_uq, kv_norm_g, w_ukv, tm=256):
    s, d = x.shape
    q_lora, kv_lora = q_norm_g.shape[0], kv_norm_g.shape[0]
    s5_dim = w_in.shape[1] - q_lora - kv_lora - ROPE_DIM
    tm = min(tm, s)
    w_in_p = jnp.concatenate(
        [w_in[:, :q_lora + kv_lora + ROPE_DIM], jnp.zeros((d, LANE - ROPE_DIM), w_in.dtype),
         w_in[:, q_lora + kv_lora + ROPE_DIM:]], axis=1).astype(BF16)
    wq = w_uq.reshape(q_lora, MLA_HEADS, NOPE_DIM + ROPE_DIM)
    wq = jnp.pad(wq, ((0, 0), (0, 0), (0, MLA_QK_PAD - NOPE_DIM - ROPE_DIM)))
    wq = wq.reshape(q_lora, MLA_HEADS * MLA_QK_PAD).astype(BF16)
    wkv = w_ukv.reshape(kv_lora, MLA_HEADS, NOPE_DIM + V_DIM)
    wuk = wkv[:, :, :NOPE_DIM].reshape(kv_lora, MLA_HEADS * NOPE_DIM).astype(BF16)
    wuv = wkv[:, :, NOPE_DIM:].reshape(kv_lora, MLA_HEADS * V_DIM).astype(BF16)
    rope = _rope_tables(s)
    scale = (NOPE_DIM + ROPE_DIM) ** -0.5 * LOG2E

    full = lambda a: pl.BlockSpec(a.shape, lambda i: (0, 0))
    row = lambda w: pl.BlockSpec((tm, w), lambda i: (i, 0))
    vec = pl.BlockSpec((1, d), lambda i: (0, 0))
    qg = q_norm_g.reshape(1, q_lora)
    kvg = kv_norm_g.reshape(1, kv_lora)
    return pl.pallas_call(
        functools.partial(_oddin_kernel, q_lora=q_lora, kv_lora=kv_lora, scale=scale),
        out_shape=[jax.ShapeDtypeStruct((s, MLA_HEADS * MLA_QK_PAD), BF16),
                   jax.ShapeDtypeStruct((s, MLA_HEADS * MLA_QK_PAD), BF16),
                   jax.ShapeDtypeStruct((s, MLA_HEADS * V_DIM), BF16),
                   jax.ShapeDtypeStruct((s, s5_dim), F32)],
        grid=(s // tm,),
        in_specs=[row(d), vec, vec, vec, full(w_in_p), full(qg), full(wq), full(kvg), full(wuk), full(wuv),
                  row(LANE)],
        out_specs=[row(MLA_HEADS * MLA_QK_PAD), row(MLA_HEADS * MLA_QK_PAD), row(MLA_HEADS * V_DIM),
                   row(s5_dim)],
        compiler_params=_cp(("parallel",)),
        name="odd_in_proj",
    )(x, g, sh, sc, w_in_p, qg, wq, kvg, wuk, wuv, rope)


def _flash_kernel(q_ref, k_ref, v_ref, o_ref, m_ref, acc_ref, s0, s1, *, rows, nk):
    t = pl.program_id(0)
    j1 = lax.rem(jnp.maximum(t - 1, 0), nk)
    tq = q_ref.shape[0]
    vd = v_ref.shape[1]

    @pl.when(t == 0)
    def _():
        m_ref[...] = jnp.full_like(m_ref, -jnp.inf)
        acc_ref[...] = jnp.zeros_like(acc_ref)
        s1[...] = jnp.full_like(s1, -jnp.inf)

    def stages(s_rd, s_wr):
        v = v_ref[...]
        v1 = jnp.concatenate([v, jnp.ones_like(v)], axis=1)
        k = k_ref[...]
        live = t >= 1
        fresh = j1 == 0
        for c in range(tq // rows):
            rs = slice(c * rows, (c + 1) * rows)
            s = s_rd[rs, :]
            m_prev = jnp.where(fresh, -jnp.inf, m_ref[rs, :])
            m_cand = jnp.maximum(m_prev, jnp.max(s, axis=-1, keepdims=True))
            m_new = jnp.where(live, m_cand, m_prev)
            m_sub = jnp.where(live, m_cand, 0.0)
            p = jnp.exp2((s - m_sub[:, :1]).astype(BF16))
            alpha = jnp.where(live, jnp.exp2(m_prev - m_new), 1.0)
            m_ref[rs, :] = m_new
            pv = jnp.dot(p, v1, preferred_element_type=F32)
            acc_ref[rs, :vd] = alpha * acc_ref[rs, :vd] + pv[:, :vd]
            acc_ref[rs, vd:] = alpha * acc_ref[rs, vd:] + pv[:, vd:]
            s_wr[rs, :] = lax.dot_general(q_ref[rs, :], k, (((1,), (1,)), ((), ())),
                                          preferred_element_type=F32)

    parity = lax.rem(t, 2)

    @pl.when(parity == 0)
    def _():
        stages(s1, s0)

    @pl.when(parity == 1)
    def _():
        stages(s0, s1)

    @pl.when((t >= 1) & (j1 == nk - 1))
    def _():
        o_ref[...] = (acc_ref[:, :vd] / acc_ref[:, vd:]).astype(o_ref.dtype)


def mla_attention(q, k, v, tq=2048, tk=2048, rows=512):
    s = q.shape[0]
    tq, tk = min(tq, s), min(tk, s)
    rows = min(rows, tq)
    ni, nk = s // tq, s // tk
    n = MLA_HEADS * ni * nk

    def item(t, lag):
        w = jnp.clip(t - lag, 0, n - 1)
        return w // (ni * nk), lax.rem(w, ni * nk) // nk, lax.rem(w, nk)

    def q_map(t):
        h, i, _ = item(t, 0)
        return i, h

    def k_map(t):
        h, _, j = item(t, 0)
        return j, h

    def v_map(t):
        h, _, j = item(t, 1)
        return j, h

    def o_map(t):
        h, i, _ = item(t, 1)
        return i, h

    return pl.pallas_call(
        functools.partial(_flash_kernel, rows=rows, nk=nk),
        out_shape=jax.ShapeDtypeStruct((s, MLA_HEADS * V_DIM), BF16),
        grid=(n + 1,),
        in_specs=[pl.BlockSpec((tq, MLA_QK_PAD), q_map),
                  pl.BlockSpec((tk, MLA_QK_PAD), k_map),
                  pl.BlockSpec((tk, V_DIM), v_map)],
        out_specs=pl.BlockSpec((tq, V_DIM), o_map),
        scratch_shapes=[pltpu.VMEM((tq, V_DIM), F32), pltpu.VMEM((tq, 2 * V_DIM), F32),
                        pltpu.VMEM((tq, tk), F32), pltpu.VMEM((tq, tk), F32)],
        compiler_params=_cp(("arbitrary",)),
        name="mla_flash_attention",
    )(q, k, v)


def _s5_matrices(a_re, a_im, log_dt, b_re, b_im, c_re, c_im, chunk):
    L = chunk
    A = lax.complex(a_re.astype(F32), a_im.astype(F32))
    dt = jnp.exp(log_dt.astype(F32))[..., None]
    adt = A * dt
    a_bar = jnp.exp(adt)
    b_bar = ((a_bar - 1.0) / A)[..., None] * lax.complex(b_re.astype(F32), b_im.astype(F32))
    c_c = lax.complex(c_re.astype(F32), c_im.astype(F32))
    kk = jnp.arange(L + 1, dtype=F32)
    apow = jnp.exp(adt[:, :, None, :] * kk[None, None, :, None].astype(jnp.complex64))
    g, p, gc = b_bar.shape[1], b_bar.shape[2], b_bar.shape[3]

    ker = jnp.real(jnp.einsum('dgcp,dgkp,dgpi->dgkci', c_c, apow[:, :, :L], b_bar))
    kb = ker[1][:, ::-1]
    diag = jnp.concatenate([kb[:, :L - 1], ker[0][:, :1] + kb[:, L - 1:], ker[0][:, 1:]], axis=1)
    lag = jnp.arange(L)[None, :] - jnp.arange(L)[:, None] + L - 1
    onehot = (jnp.arange(2 * L - 1)[:, None, None] == lag[None]).astype(F32)
    t_mat = jnp.einsum('gmci,mst->gsitc', diag, onehot, precision=lax.Precision.HIGHEST)
    t_mat = t_mat.reshape(g, L * gc, L * gc)

    pf = apow[0][:, ::-1][:, 1:][:, :, :, None] * b_bar[0][:, None]
    pb = apow[1][:, :L][:, :, :, None] * b_bar[1][:, None]
    def p_lay(m):
        return jnp.transpose(m, (0, 1, 3, 2)).reshape(g, L * gc, p)
    p_mat = jnp.concatenate([p_lay(jnp.real(pf)), p_lay(jnp.imag(pf)),
                             p_lay(jnp.real(pb)), p_lay(jnp.imag(pb))], axis=-1)

    wf = c_c[0][:, None] * apow[0][:, 1:][:, :, None, :]
    wb = c_c[1][:, None] * apow[1][:, ::-1][:, :L][:, :, None, :]
    def q_lay(m):
        return jnp.transpose(m, (0, 3, 1, 2)).reshape(g, p, L * gc)
    q_mat = jnp.concatenate([q_lay(jnp.real(wf)), q_lay(-jnp.imag(wf)),
                             q_lay(jnp.real(wb)), q_lay(-jnp.imag(wb))], axis=1)

    al = apow[:, :, L]
    dec = jnp.stack([jnp.real(al[0]), jnp.imag(al[0]), jnp.real(al[1]), jnp.imag(al[1])]).reshape(4, g * p)
    return t_mat, p_mat, q_mat, dec


def _s5_state_kernel(u_ref, p_ref, fre_ref, fim_ref, bre_ref, bim_ref):
    outs = [[], [], [], []]
    for gi in range(u_ref.shape[0]):
        xe = jnp.dot(u_ref[gi].astype(BF16), p_ref[gi], preferred_element_type=F32)
        for part in range(4):
            outs[part].append(xe[:, part * S5_STATE:(part + 1) * S5_STATE])
    for part, ref in enumerate((fre_ref, fim_ref, bre_ref, bim_ref)):
        ref[...] = jnp.concatenate(outs[part], axis=-1)


def _s5_scan_kernel(fre_ref, fim_ref, bre_ref, bim_ref, dec_ref, ofre_ref, ofim_ref, obre_ref, obim_ref):
    nc = fre_ref.shape[0]
    w = fre_ref.shape[1]
    far, fai, bar, bai = (dec_ref[i:i + 1, :] for i in range(4))

    def fwd(c, st):
        re, im = st
        ofre_ref[pl.ds(c, 1), :] = re
        ofim_ref[pl.ds(c, 1), :] = im
        return (far * re - fai * im + fre_ref[pl.ds(c, 1), :],
                far * im + fai * re + fim_ref[pl.ds(c, 1), :])

    def bwd(i, st):
        c = nc - 1 - i
        re, im = st
        obre_ref[pl.ds(c, 1), :] = re
        obim_ref[pl.ds(c, 1), :] = im
        return (bar * re - bai * im + bre_ref[pl.ds(c, 1), :],
                bar * im + bai * re + bim_ref[pl.ds(c, 1), :])

    zero = (jnp.zeros((1, w), F32), jnp.zeros((1, w), F32))
    lax.fori_loop(0, nc, fwd, zero)
    lax.fori_loop(0, nc, bwd, zero)


def _s5_out_kernel(u_ref, t_ref, q_ref, fre_ref, fim_ref, bre_ref, bim_ref, y_ref):
    for gi in range(u_ref.shape[0]):
        lanes = slice(gi * S5_STATE, (gi + 1) * S5_STATE)
        y = jnp.dot(u_ref[gi].astype(BF16), t_ref[gi], preferred_element_type=F32)
        xin = jnp.concatenate([r[:, lanes] for r in (fre_ref, fim_ref, bre_ref, bim_ref)], axis=-1)
        y_ref[gi] = (y + jnp.dot(xin.astype(BF16), q_ref[gi], preferred_element_type=F32)).astype(y_ref.dtype)


def _s5_to_groups_kernel(u_ref, o_ref, cat_ref, *, chunk):
    nc = cat_ref.shape[0]
    gc = S5_GROUP
    for s in range(chunk):
        cat_ref[:, s * LANE:(s + 1) * LANE] = u_ref[pl.ds(s, nc, stride=chunk), :].astype(BF16)
    cat = cat_ref[...]
    r = lax.broadcasted_iota(jnp.int32, (chunk * LANE, chunk * gc), 0)
    c = lax.broadcasted_iota(jnp.int32, (chunk * LANE, chunk * gc), 1)
    lane_of_col = jnp.where(r // LANE == c // gc, lax.rem(c, gc), -LANE)
    for j in range(o_ref.shape[0]):
        sel = jnp.where(lax.rem(r, LANE) - j * gc == lane_of_col, 1.0, 0.0).astype(BF16)
        o_ref[j] = jnp.dot(cat, sel, preferred_element_type=F32).astype(o_ref.dtype)


def _s5_from_groups_kernel(y_ref, o_ref, cat_ref, *, chunk):
    nc = cat_ref.shape[0]
    gc = S5_GROUP
    w = chunk * gc
    for j in range(y_ref.shape[0]):
        cat_ref[:, j * w:(j + 1) * w] = y_ref[j]
    cat = cat_ref[...]
    r = lax.broadcasted_iota(jnp.int32, (cat.shape[1], LANE), 0)
    lane = lax.broadcasted_iota(jnp.int32, (cat.shape[1], LANE), 1)
    src_lane = (r // w) * gc + lax.rem(r, gc)
    t_of_row = lax.rem(r, w) // gc
    for t in range(chunk):
        sel = jnp.where((t_of_row == t) & (src_lane == lane), 1.0, 0.0).astype(BF16)
        o_ref[pl.ds(t, nc, stride=chunk), :] = jnp.dot(cat, sel, preferred_element_type=F32)


def _s5_relayout(x, to_groups, s, g, chunk):
    gc = S5_GROUP
    nc = s // chunk
    per = LANE // gc
    nat = pl.BlockSpec((s, LANE), lambda i: (0, i))
    grp = pl.BlockSpec((per, nc, chunk * gc), lambda i: (i, 0, 0))
    if to_groups:
        kern, in_spec, out_spec = _s5_to_groups_kernel, nat, grp
        out_shape = jax.ShapeDtypeStruct((g, nc, chunk * gc), BF16)
        scratch = pltpu.VMEM((nc, chunk * LANE), BF16)
    else:
        kern, in_spec, out_spec = _s5_from_groups_kernel, grp, nat
        out_shape = jax.ShapeDtypeStruct((s, g * gc), F32)
        scratch = pltpu.VMEM((nc, per * chunk * gc), BF16)
    return pl.pallas_call(
        functools.partial(kern, chunk=chunk),
        out_shape=out_shape,
        grid=(g // per,),
        in_specs=[in_spec],
        out_specs=out_spec,
        scratch_shapes=[scratch],
        compiler_params=_cp(("parallel",)),
        name="s5_to_groups" if to_groups else "s5_from_groups",
    )(x)


def s5_bidirectional(u, a_re, a_im, log_dt, b_re, b_im, c_re, c_im, chunk=S5_CHUNK):
    s, dim = u.shape
    gc, p = S5_GROUP, S5_STATE
    g = dim // gc
    chunk = min(chunk, s)
    nc = s // chunk
    gp = LANE // p
    t_mat, p_mat, q_mat, dec = _s5_matrices(a_re, a_im, log_dt, b_re, b_im, c_re, c_im, chunk)
    t_mat, p_mat, q_mat = t_mat.astype(BF16), p_mat.astype(BF16), q_mat.astype(BF16)
    u_t = _s5_relayout(u, True, s, g, chunk)

    grp = lambda a, b: pl.BlockSpec((gp, a, b), lambda i: (i, 0, 0))
    st = pl.BlockSpec((nc, LANE), lambda i: (0, i))
    st_shape = jax.ShapeDtypeStruct((nc, g * p), F32)
    xe = pl.pallas_call(
        _s5_state_kernel,
        out_shape=[st_shape] * 4,
        grid=(g // gp,),
        in_specs=[grp(nc, chunk * gc), grp(chunk * gc, 4 * p)],
        out_specs=[st] * 4,
        compiler_params=_cp(("parallel",)),
        name="s5_chunk_states",
    )(u_t, p_mat)

    lanes = min(4 * LANE, g * p)
    cols = lambda rows: pl.BlockSpec((rows, lanes), lambda i: (0, i))
    xin = pl.pallas_call(
        _s5_scan_kernel,
        out_shape=[st_shape] * 4,
        grid=(g * p // lanes,),
        in_specs=[cols(nc)] * 4 + [cols(4)],
        out_specs=[cols(nc)] * 4,
        compiler_params=_cp(("parallel",)),
        name="s5_boundary_scan",
    )(*xe, dec)

    y_t = pl.pallas_call(
        _s5_out_kernel,
        out_shape=jax.ShapeDtypeStruct((g, nc, chunk * gc), BF16),
        grid=(g // gp,),
        in_specs=[grp(nc, chunk * gc), grp(chunk * gc, chunk * gc), grp(4 * p, chunk * gc)] + [st] * 4,
        out_specs=grp(nc, chunk * gc),
        compiler_params=_cp(("parallel",)),
        name="s5_outputs",
    )(u_t, t_mat, q_mat, *xin)
    return _s5_relayout(y_t, False, s, g, chunk)


def _s5_gate_kernel(y_ref, u_ref, d_ref, w_ref, o_ref):
    y = y_ref[...].astype(F32) + d_ref[...] * u_ref[...]
    y = 0.5 * y * (1.0 + jnp.tanh(math.sqrt(2.0 / math.pi) * (y + 0.044715 * (y * y * y))))
    z = jnp.dot(y.astype(BF16), w_ref[...], preferred_element_type=F32)
    o_ref[...] = (y * _sigmoid(z)).astype(o_ref.dtype)


def s5_gate(y, u, d_skip, w_glu, tm=1024):
    s, dim = y.shape
    tm = min(tm, s)
    row = pl.BlockSpec((tm, dim), lambda i: (i, 0))
    return pl.pallas_call(
        _s5_gate_kernel,
        out_shape=jax.ShapeDtypeStruct((s, dim), BF16),
        grid=(s // tm,),
        in_specs=[row, row, pl.BlockSpec((1, dim), lambda i: (0, 0)), pl.BlockSpec((dim, dim), lambda i: (0, 0))],
        out_specs=row,
        compiler_params=_cp(("parallel",)),
        name="s5_gate",
    )(y, u, d_skip.reshape(1, dim), w_glu.astype(BF16))


def _router_kernel(x_ref, g_ref, sh_ref, sc_ref, rw_ref, h_ref, info_ref):
    h = _modnorm(x_ref[...], g_ref[...], sh_ref[...], sc_ref[...])
    half = h.shape[1] // 2
    bits = lambda v: lax.bitcast_convert_type(v.astype(BF16).astype(F32), jnp.uint32)
    h_ref[...] = (bits(h[:, :half]) >> 16) | (bits(h[:, half:]) & jnp.uint32(0xFFFF0000))
    logits = jnp.dot(h, rw_ref[...], preferred_element_type=F32, precision=lax.Precision.HIGHEST)
    lane = lax.broadcasted_iota(jnp.int32, logits.shape, 1)
    logits = jnp.where(lane < N_EXPERTS, logits, -jnp.inf)
    m1 = jnp.max(logits, axis=-1, keepdims=True)
    i1 = jnp.min(jnp.where(logits == m1, lane, LANE), axis=-1, keepdims=True)
    rest = jnp.where(lane == i1, -jnp.inf, logits)
    m2 = jnp.max(rest, axis=-1, keepdims=True)
    i2 = jnp.min(jnp.where(rest == m2, lane, LANE), axis=-1, keepdims=True)
    e = jnp.exp(m2 - m1)
    g1 = 1.0 / (1.0 + e)
    g2 = e / (1.0 + e)
    info = jnp.where(lane == 0, i1.astype(F32),
                     jnp.where(lane == 1, i2.astype(F32),
                               jnp.where(lane == 2, g1, jnp.where(lane == 3, g2, 0.0))))
    info_ref[...] = info


def moe_router(x, g, sh, sc, router_w, tm=512):
    s, d = x.shape
    tm = min(tm, s)
    rw = jnp.pad(router_w, ((0, 0), (0, LANE - router_w.shape[1])))
    vec = pl.BlockSpec((1, d), lambda i: (0, 0))
    return pl.pallas_call(
        _router_kernel,
        out_shape=[jax.ShapeDtypeStruct((s, d // 2), jnp.uint32), jax.ShapeDtypeStruct((s, LANE), F32)],
        grid=(s // tm,),
        in_specs=[pl.BlockSpec((tm, d), lambda i: (i, 0)), vec, vec, vec,
                  pl.BlockSpec((d, LANE), lambda i: (0, 0))],
        out_specs=[pl.BlockSpec((tm, d // 2), lambda i: (i, 0)), pl.BlockSpec((tm, LANE), lambda i: (i, 0))],
        compiler_params=_cp(("parallel",)),
        name="moe_router",
    )(x, g, sh, sc, rw)


def _expert_kernel(te_ref, tv_ref, src_ref, hp_hbm, w1_ref, w3_ref, w2_ref, o_ref, acc_ref, gbuf, h_ref, sem):
    t = pl.program_id(0)
    f = pl.program_id(1)
    nt = pl.num_programs(0)
    nf = pl.num_programs(1)
    tm = acc_ref.shape[0]
    per_step = tm // nf
    valid = tv_ref[t] > 0
    slot = lax.rem(t, 2)

    def row_copy(tile, buf, r):
        return pltpu.make_async_copy(hp_hbm.at[pl.ds(src_ref[tile * tm + r], 1)],
                                     gbuf.at[buf, pl.ds(r, 1)], sem.at[buf])

    def wait_half(buf):
        pltpu.make_async_copy(hp_hbm.at[pl.ds(0, tm)], gbuf.at[buf], sem.at[buf]).wait()

    @pl.when((t == 0) & (f == 0))
    def _():
        def issue(r, carry):
            row_copy(0, 0, r).start()
            return carry
        lax.fori_loop(0, tm, issue, 0)

    @pl.when((f == 0) & ((t == 0) | (tv_ref[jnp.maximum(t - 1, 0)] > 0)))
    def _():
        wait_half(slot)

    @pl.when((f == 0) & valid)
    def _():
        w = gbuf[slot]
        half = w.shape[1]
        h_ref[:, :half] = lax.bitcast_convert_type(w << 16, F32).astype(BF16)
        h_ref[:, half:] = lax.bitcast_convert_type(w & jnp.uint32(0xFFFF0000), F32).astype(BF16)

    @pl.when(f == 0)
    def _():
        acc_ref[...] = jnp.zeros_like(acc_ref)

    def compute(rows):
        for u in range(per_step):
            row_copy(t + 1, 1 - slot, f * per_step + u).start()
        h = h_ref[:rows, :]
        u = jnp.dot(h, w1_ref[...].astype(BF16), preferred_element_type=F32)
        v = jnp.dot(h, w3_ref[...].astype(BF16), preferred_element_type=F32)
        acc_ref[:rows, :] += jnp.dot((_silu(u) * v).astype(BF16), w2_ref[...].astype(BF16),
                                     preferred_element_type=F32)

    @pl.when(tv_ref[t] > tm // 2)
    def _():
        compute(tm)

    @pl.when(valid & (tv_ref[t] <= tm // 2))
    def _():
        compute(tm // 2)

    @pl.when(f == nf - 1)
    def _():
        o_ref[...] = acc_ref[...].astype(o_ref.dtype)

    @pl.when((t == nt - 1) & (f == nf - 1) & valid)
    def _():
        wait_half(1 - slot)


def expert_ffn(h_packed, src, tile_expert, tile_valid, w1, w3, w2, tm, tf=512):
    dh = h_packed.shape[1]
    d = 2 * dh
    n = src.shape[0] - tm
    f = w1.shape[2]
    tf = min(tf, f)
    nf = f // tf
    assert tm % nf == 0 and n % tm == 0

    def fblk(j, t, tv):
        return jnp.where(tv[t] > 0, j, nf - 1)

    return pl.pallas_call(
        _expert_kernel,
        out_shape=jax.ShapeDtypeStruct((n, d), BF16),
        grid_spec=pltpu.PrefetchScalarGridSpec(
            num_scalar_prefetch=3,
            grid=(n // tm, nf),
            in_specs=[pl.BlockSpec(memory_space=pl.ANY),
                      pl.BlockSpec((None, d, tf), lambda t, j, te, tv, sr: (te[t], 0, fblk(j, t, tv))),
                      pl.BlockSpec((None, d, tf), lambda t, j, te, tv, sr: (te[t], 0, fblk(j, t, tv))),
                      pl.BlockSpec((None, tf, d), lambda t, j, te, tv, sr: (te[t], fblk(j, t, tv), 0))],
            out_specs=pl.BlockSpec((tm, d), lambda t, j, te, tv, sr: (t, 0)),
            scratch_shapes=[pltpu.VMEM((tm, d), F32), pltpu.VMEM((2, tm, dh), jnp.uint32),
                            pltpu.VMEM((tm, d), BF16), pltpu.SemaphoreType.DMA((2,))]),
        compiler_params=_cp(("arbitrary", "arbitrary")),
        name="expert_ffn",
    )(tile_expert, tile_valid, src, h_packed, w1, w3, w2)


def _dispatch(idx, tm):
    s = idx.shape[0]
    e_flat = idx.reshape(-1)
    onehot = (e_flat[None, :] == jnp.arange(N_EXPERTS)[:, None]).astype(jnp.int32)
    csum = jnp.cumsum(onehot, axis=1)
    rank = jnp.sum((csum - onehot) * onehot, axis=0)
    counts = csum[:, -1]
    padded = ((counts + tm - 1) // tm) * tm
    gend = jnp.cumsum(padded)
    slot = (gend - padded)[e_flat] + rank
    nt = -(-(2 * s) // tm) + N_EXPERTS
    src = jnp.zeros(((nt + 1) * tm,), jnp.int32).at[slot].set(jnp.arange(2 * s, dtype=jnp.int32) // 2)
    tstart = jnp.arange(nt, dtype=jnp.int32) * tm
    valid = tstart < gend[-1]
    te = jnp.minimum(jnp.sum((tstart[:, None] >= gend[None, :]).astype(jnp.int32), axis=1), N_EXPERTS - 1)
    nvalid = gend[-1] // tm
    fill = jnp.clip((gend - padded + counts)[te] - tstart, 0, tm)
    fill = jnp.where(valid, fill, 0)
    te = jnp.where(valid, te, te[jnp.maximum(nvalid - 1, 0)])
    return slot.reshape(s, 2), src, te.astype(jnp.int32), fill.astype(jnp.int32)


def _combine_kernel(ya_ref, yb_ref, info_ref, x_ref, g_ref, gt_ref, o_ref):
    info = info_ref[...]
    y = info[:, 2:3] * ya_ref[...].astype(F32) + info[:, 3:4] * yb_ref[...].astype(F32)
    o_ref[...] = x_ref[...] + gt_ref[...] * (_rms(y) * g_ref[...])


def moe_combine(ya, yb, info, x, g, gt, tm=512):
    s, d = x.shape
    tm = min(tm, s)
    row = lambda w: pl.BlockSpec((tm, w), lambda i: (i, 0))
    vec = pl.BlockSpec((1, d), lambda i: (0, 0))
    return pl.pallas_call(
        _combine_kernel,
        out_shape=jax.ShapeDtypeStruct((s, d), F32),
        grid=(s // tm,),
        in_specs=[row(d), row(d), row(LANE), row(d), vec, vec],
        out_specs=row(d),
        compiler_params=_cp(("parallel",)),
        name="moe_combine",
    )(ya, yb, info, x, g, gt)


def moe_sublayer(x, g, sh, sc, g2, gt, router_w, w1, w3, w2, tm_e=EXPERT_TILE):
    s = x.shape[0]
    tm_e = min(tm_e, s)
    h_packed, info = moe_router(x, g, sh, sc, router_w)
    idx = info[:, :2].astype(jnp.int32)
    slot, src, te, tv = _dispatch(idx, tm_e)
    y_sorted = expert_ffn(h_packed, src, te, tv, w1, w3, w2, tm_e)
    ya = jnp.take(y_sorted, slot[:, 0], axis=0)
    yb = jnp.take(y_sorted, slot[:, 1], axis=0)
    return moe_combine(ya, yb, info, x, g2, gt)


def kernel(x, c, t5_table, norm_g, ada_w, ada_b, e_w_in, e_conv_w, e_conv_b, e_cln_g, e_cln_b, e_w_out, e_ffn_w1, e_ffn_w3, e_ffn_w2, o_w_in, o_q_norm_g, o_w_uq, o_kv_norm_g, o_w_ukv, s5_a_re, s5_a_im, s5_log_dt, s5_b_re, s5_b_im, s5_c_re, s5_c_im, s5_d, s5_w_glu, o_w_out, router_w, moe_w1, moe_w3, moe_w2):
    bsz, seq, d = x.shape
    assert bsz == 1
    depth = norm_g.shape[0]
    conv_dim = e_conv_w.shape[2]
    xs = x.reshape(seq, d)
    mod = adaln(c, ada_w, ada_b)
    for layer in range(depth):
        i = layer // 2
        sh_m, sc_m, gt_m, sh_f, sc_f, gt_f = (mod[layer, :, k * d:(k + 1) * d] for k in range(6))
        g = [norm_g[layer, k].reshape(1, d) for k in range(4)]
        if layer % 2 == 0:
            z = norm_mod_matmul(xs, g[0], sh_m, sc_m, e_w_in[i].astype(BF16))
            a = conformer_conv(z, e_conv_w[i], e_conv_b[i], e_cln_g[i], e_cln_b[i])
            o = dilated_attention(z, t5_table, 2 * conv_dim, (z.shape[1] - 2 * conv_dim) // 3)
            xs = out_proj_residual(a, o, e_w_out[i].astype(BF16), xs, g[1], gt_m)
            xs = ffn_sublayer(xs, g[2], sh_f, sc_f, g[3], gt_f, e_ffn_w1[i].astype(BF16),
                              e_ffn_w3[i].astype(BF16), e_ffn_w2[i].astype(BF16))
        else:
            q, k, v, u = odd_in_proj(xs, g[0], sh_m, sc_m, o_w_in[i], o_q_norm_g[i], o_w_uq[i],
                                     o_kv_norm_g[i], o_w_ukv[i])
            o_mla = mla_attention(q, k, v)
            y = s5_bidirectional(u, s5_a_re[i], s5_a_im[i], s5_log_dt[i], s5_b_re[i], s5_b_im[i],
                                 s5_c_re[i], s5_c_im[i])
            y = s5_gate(y, u, s5_d[i], s5_w_glu[i])
            xs = out_proj_residual(o_mla, y, o_w_out[i].astype(BF16), xs, g[1], gt_m)
            xs = moe_sublayer(xs, g[2], sh_f, sc_f, g[3], gt_f, router_w[i], moe_w1[i], moe_w3[i], moe_w2[i])
    return xs.reshape(bsz, seq, d)
```

```python
import functools
import math

import jax
import jax.numpy as jnp
from jax import lax
from jax.experimental import pallas as pl
from jax.experimental.pallas import tpu as pltpu

F32 = jnp.float32
BF16 = jnp.bfloat16

RMS_EPS = 1e-6
LN_EPS = 1e-5
NEG_INF = -1e30
LOG2E = math.log2(math.e)

CONV_WIDTH = 31
CONV_HALO = 16
DIL_HEAD_DIM = 64
DIL_CONFIGS = ((128, 1), (512, 4), (2048, 16))
DIL_HALF = 64
DIL_TILE = 1024
DIL_QB = 256
DIL_MERGE = 2
N_BUCKETS = 32
T5_MAX_DIST = DIL_CONFIGS[-1][0] // 2
MLA_HEADS = 12
NOPE_DIM = 128
ROPE_DIM = 64
V_DIM = 128
ROPE_BASE = 10000.0
MLA_QK_PAD = 256
S5_GROUP = 16
S5_STATE = 64
S5_CHUNK = 16
N_EXPERTS = 8
EXPERT_TILE = 896
LANE = 128
SUBLANES = 8
VMEM_LIMIT_MB = 56


def _cp(sem, vmem_mb=VMEM_LIMIT_MB):
    return pltpu.CompilerParams(dimension_semantics=sem, vmem_limit_bytes=vmem_mb * 1024 * 1024)


def _rms(x):
    return x * lax.rsqrt(jnp.mean(x * x, axis=-1, keepdims=True) + RMS_EPS)


def _modnorm(x, g, sh, sc):
    return (_rms(x) * g) * (1.0 + sc) + sh


def _sigmoid(x):
    return 1.0 / (1.0 + jnp.exp(-x))


def _silu(x):
    return x * _sigmoid(x)


def _adaln_kernel(c_ref, w_ref, b_ref, o_ref, cb_ref):
    @pl.when((pl.program_id(0) == 0) & (pl.program_id(1) == 0))
    def _():
        cc = c_ref[...]
        cb_ref[...] = jnp.broadcast_to(_silu(cc), cb_ref.shape)

    cb = cb_ref[...]
    for j in range(o_ref.shape[-1] // LANE):
        sl = slice(j * LANE, (j + 1) * LANE)
        o_ref[:, sl] = jnp.sum(w_ref[:, sl] * cb, axis=0, keepdims=True) + b_ref[:, sl]


def adaln(c, ada_w, ada_b, tn=1024):
    nl, d, n = ada_w.shape
    out = pl.pallas_call(
        _adaln_kernel,
        out_shape=jax.ShapeDtypeStruct((nl, 1, n), F32),
        grid=(nl, n // tn),
        in_specs=[pl.BlockSpec((d, 1), lambda l, j: (0, 0)),
                  pl.BlockSpec((None, d, tn), lambda l, j: (l, 0, j)),
                  pl.BlockSpec((None, 1, tn), lambda l, j: (l, 0, j))],
        out_specs=pl.BlockSpec((None, 1, tn), lambda l, j: (l, 0, j)),
        scratch_shapes=[pltpu.VMEM((d, LANE), F32)],
        compiler_params=_cp(("arbitrary", "arbitrary")),
        name="adaln",
    )(c.reshape(d, 1), ada_w, ada_b.reshape(nl, 1, n))
    return out


def _nmm_kernel(x_ref, g_ref, sh_ref, sc_ref, w_ref, o_ref, h_ref):
    @pl.when(pl.program_id(1) == 0)
    def _():
        h_ref[...] = _modnorm(x_ref[...], g_ref[...], sh_ref[...], sc_ref[...]).astype(h_ref.dtype)

    o_ref[...] = jnp.dot(h_ref[...], w_ref[...], preferred_element_type=F32).astype(o_ref.dtype)


def norm_mod_matmul(x, g, sh, sc, w, tm=1024, tn=1024):
    s, d = x.shape
    n = w.shape[1]
    tm, tn = min(tm, s), min(tn, n)
    vec = pl.BlockSpec((1, d), lambda i, j: (0, 0))
    return pl.pallas_call(
        _nmm_kernel,
        out_shape=jax.ShapeDtypeStruct((s, n), BF16),
        grid=(s // tm, n // tn),
        in_specs=[pl.BlockSpec((tm, d), lambda i, j: (i, 0)), vec, vec, vec,
                  pl.BlockSpec((d, tn), lambda i, j: (0, j))],
        out_specs=pl.BlockSpec((tm, tn), lambda i, j: (i, j)),
        scratch_shapes=[pltpu.VMEM((tm, d), BF16)],
        compiler_params=_cp(("parallel", "arbitrary")),
        name="norm_mod_matmul",
    )(x, g, sh, sc, w)


def _conv_kernel(av_ref, ag_ref, avp_ref, agp_ref, avn_ref, agn_ref, w_ref, b_ref, lg_ref, lb_ref,
                 o_ref, buf_ref, sh_ref, *, rows):
    i = pl.program_id(0)
    n = pl.num_programs(0)
    ts = av_ref.shape[0]

    def glu(v_ref, g_ref):
        return v_ref[...].astype(F32) * _sigmoid(g_ref[...].astype(F32))

    buf_ref[0:CONV_HALO, :] = jnp.where(i > 0, glu(avp_ref, agp_ref), 0.0)
    buf_ref[CONV_HALO:CONV_HALO + ts, :] = glu(av_ref, ag_ref)
    buf_ref[CONV_HALO + ts:2 * CONV_HALO + ts, :] = jnp.where(i < n - 1, glu(avn_ref, agn_ref), 0.0)
    buf_ref[2 * CONV_HALO + ts:, :] = jnp.zeros((SUBLANES, buf_ref.shape[1]), F32)

    span = ts + 2 * CONV_HALO
    for o in range(SUBLANES):
        sh_ref[o] = buf_ref[o:o + span, :]

    off = CONV_HALO - CONV_WIDTH // 2
    for r in range(ts // rows):
        acc = jnp.broadcast_to(b_ref[...], (rows, b_ref.shape[-1]))
        for k in range(CONV_WIDTH):
            lo = r * rows + k + off
            base = lo - lo % SUBLANES
            acc = acc + w_ref[k:k + 1, :] * sh_ref[lo % SUBLANES, base:base + rows, :]
        mu = jnp.mean(acc, axis=-1, keepdims=True)
        dlt = acc - mu
        var = jnp.mean(dlt * dlt, axis=-1, keepdims=True)
        y = dlt * lax.rsqrt(var + LN_EPS) * lg_ref[...] + lb_ref[...]
        o_ref[r * rows:(r + 1) * rows, :] = _silu(y).astype(o_ref.dtype)


def conformer_conv(z, conv_w, conv_b, cln_g, cln_b, ts=256, rows=32):
    s = z.shape[0]
    c = conv_w.shape[1]
    ts = min(ts, s)
    hb = ts // CONV_HALO
    nh = s // CONV_HALO
    main = lambda col: pl.BlockSpec((ts, c), lambda i: (i, col))
    prev = lambda col: pl.BlockSpec((CONV_HALO, c), lambda i: (jnp.maximum(i * hb - 1, 0), col))
    nxt = lambda col: pl.BlockSpec((CONV_HALO, c), lambda i: (jnp.minimum((i + 1) * hb, nh - 1), col))
    vec = lambda r: pl.BlockSpec((r, c), lambda i: (0, 0))
    return pl.pallas_call(
        functools.partial(_conv_kernel, rows=min(rows, ts)),
        out_shape=jax.ShapeDtypeStruct((s, c), BF16),
        grid=(s // ts,),
        in_specs=[main(0), main(1), prev(0), prev(1), nxt(0), nxt(1),
                  vec(CONV_WIDTH), vec(1), vec(1), vec(1)],
        out_specs=pl.BlockSpec((ts, c), lambda i: (i, 0)),
        scratch_shapes=[pltpu.VMEM((ts + 2 * CONV_HALO + SUBLANES, c), F32),
                        pltpu.VMEM((SUBLANES, ts + 2 * CONV_HALO, c), F32)],
        compiler_params=_cp(("parallel",)),
        name="conformer_conv",
    )(z, z, z, z, z, z, conv_w, conv_b.reshape(1, c), cln_g.reshape(1, c), cln_b.reshape(1, c))


def _t5_bucket(rel):
    half = N_BUCKETS // 2
    exact = half // 2
    n = jnp.abs(rel)
    large = exact + (jnp.log(jnp.maximum(n, 1).astype(F32) / exact)
                     / math.log(T5_MAX_DIST / exact) * (half - exact)).astype(jnp.int32)
    large = jnp.minimum(large, half - 1)
    return jnp.where(rel > 0, half, 0) + jnp.where(n < exact, n, large)


def _band_diag(t5_table, dil, qb):
    kl = qb + 2 * DIL_HALF
    n = -(-(qb + kl - 1) // LANE) * LANE
    dist = jnp.arange(n) - (qb - 1) - DIL_HALF
    return jnp.where((jnp.abs(dist) <= DIL_HALF)[None], t5_table[_t5_bucket(dist * dil)].T.astype(F32), NEG_INF)


def _band_bias_rows(diag_row, qb):
    n = diag_row.shape[1]
    rolled = pltpu.roll(jnp.broadcast_to(diag_row, (qb, n)), n - (qb - 1), 1, stride=1, stride_axis=0)
    return rolled[:, :qb + 2 * DIL_HALF]


def _band_group(q, kw, vw, bias_ref, pen):
    r = q.shape[0]
    is_a = lax.broadcasted_iota(jnp.int32, (1, LANE), 1) < DIL_HEAD_DIM
    q2 = jnp.concatenate([jnp.where(is_a, q, 0.0), jnp.where(is_a, 0.0, q)], axis=0).astype(BF16)
    s = lax.dot_general(q2, kw.astype(BF16), (((1,), (1,)), ((), ())), preferred_element_type=F32)
    s = s + bias_ref[...]
    if pen is not None:
        s = s + pen
    m = jnp.max(s, axis=-1, keepdims=True)
    p = jnp.exp2((s - m).astype(BF16))
    vb = vw.astype(BF16)
    pv = jnp.dot(p, jnp.concatenate([vb, jnp.ones_like(vb)], axis=1), preferred_element_type=F32)
    pick = lambda x: jnp.where(is_a, x[:r], x[r:])
    return pick(pv[:, :LANE]), pick(m), pick(pv[:, LANE:])


def _dil_kernel(q_ref, kp_ref, kc_ref, kn_ref, vp_ref, vc_ref, vn_ref, d1_ref, d4_ref, d16_ref, o_ref,
                qf, kf, vf, a1, m1, l1, a4, m4, l4, a16, m16, l16, b1_ref, b4_ref, b16_ref):
    i = pl.program_id(1)
    first = i == 0
    last = i == pl.num_programs(1) - 1
    t = DIL_TILE
    half = DIL_HALF

    @pl.when(first)
    def _():
        q16 = t // 16
        b16_ref[...] = jnp.full(b16_ref.shape, NEG_INF, F32)
        for h in range(2):
            b1_ref[h * DIL_QB:(h + 1) * DIL_QB, :] = _band_bias_rows(d1_ref[h:h + 1, :], DIL_QB)
            b4_ref[h * DIL_QB:(h + 1) * DIL_QB, :] = _band_bias_rows(d4_ref[h:h + 1, :], DIL_QB)
            blk = _band_bias_rows(d16_ref[h:h + 1, :], q16)
            for u in range(DIL_MERGE):
                r0 = (h * DIL_MERGE + u) * q16
                b16_ref[r0:r0 + q16, u * blk.shape[1]:(u + 1) * blk.shape[1]] = blk

    qf[...] = q_ref[...].astype(F32) * (DIL_HEAD_DIM ** -0.5 * LOG2E)
    for dst, (p_ref, c_ref, n_ref) in ((kf, (kp_ref, kc_ref, kn_ref)), (vf, (vp_ref, vc_ref, vn_ref))):
        dst[0:t, :] = p_ref[...].astype(F32)
        dst[t:2 * t, :] = c_ref[...].astype(F32)
        dst[2 * t:3 * t, :] = n_ref[...].astype(F32)

    def store(refs, rows, vals):
        for ref, val in zip(refs, vals):
            ref[rows, :] = val

    qb = DIL_QB
    kl = qb + 2 * half
    col = lax.broadcasted_iota(jnp.int32, (1, kl), 1)
    lo = jnp.where((col < half) & first, NEG_INF, 0.0)
    hi = jnp.where((col >= qb + half) & last, NEG_INF, 0.0)

    nb = t // qb
    for b in range(nb):
        pen = lo + hi if nb == 1 else (lo if b == 0 else (hi if b == nb - 1 else None))
        k0 = t + b * qb - half
        rows = slice(b * qb, (b + 1) * qb)
        store((a1, m1, l1), rows, _band_group(qf[rows, :], kf[k0:k0 + kl, :], vf[k0:k0 + kl, :], b1_ref, pen))

    dil = 4
    for r in range(dil):
        qrows = pl.ds(r, qb, stride=dil)
        krows = pl.ds(t - half * dil + r, kl, stride=dil)
        store((a4, m4, l4), qrows, _band_group(qf[qrows, :], kf[krows, :], vf[krows, :], b4_ref, lo + hi))

    dil = 16
    q16 = t // dil
    k16 = q16 + 2 * half
    col = lax.rem(lax.broadcasted_iota(jnp.int32, (1, DIL_MERGE * k16), 1), k16)
    pen16 = (jnp.where((col < half) & first, NEG_INF, 0.0)
             + jnp.where((col >= q16 + half) & last, NEG_INF, 0.0))
    for g in range(dil // DIL_MERGE):
        res = range(g * DIL_MERGE, (g + 1) * DIL_MERGE)
        qrows = [pl.ds(r, q16, stride=dil) for r in res]
        krows = [pl.ds(t - half * dil + r, k16, stride=dil) for r in res]
        out = _band_group(jnp.concatenate([qf[rr, :] for rr in qrows], axis=0),
                          jnp.concatenate([kf[rr, :] for rr in krows], axis=0),
                          jnp.concatenate([vf[rr, :] for rr in krows], axis=0), b16_ref, pen16)
        for u, rr in enumerate(qrows):
            store((a16, m16, l16), rr, [x[u * q16:(u + 1) * q16] for x in out])

    mm = jnp.maximum(jnp.maximum(m1[...], m4[...]), m16[...])
    w1, w4, w16 = jnp.exp2(m1[...] - mm), jnp.exp2(m4[...] - mm), jnp.exp2(m16[...] - mm)
    num = w1 * a1[...] + w4 * a4[...] + w16 * a16[...]
    den = w1 * l1[...] + w4 * l4[...] + w16 * l16[...]
    o_ref[...] = (num / den).astype(o_ref.dtype)


def dilated_attention(z, t5_table, col0, width):
    s, zw = z.shape
    t = DIL_TILE
    assert s % t == 0 and col0 % LANE == 0 and width % LANE == 0 and t // 4 == DIL_QB
    nt = s // t
    nh = t5_table.shape[1]
    cq, ck, cv = ((col0 + k * width) // LANE for k in range(3))
    q16 = t // 16
    diags = [(_band_diag(t5_table, dil, qb) * LOG2E).reshape(nh // 2, 2, -1)
             for dil, qb in ((1, DIL_QB), (4, DIL_QB), (16, q16))]
    kl, k16 = DIL_QB + 2 * DIL_HALF, q16 + 2 * DIL_HALF

    def blk(col, shift):
        return pl.BlockSpec((t, LANE), lambda hg, i: (jnp.clip(i + shift, 0, nt - 1), col + hg))

    diag_spec = lambda d: pl.BlockSpec((None,) + d.shape[1:], lambda hg, i: (hg, 0, 0))
    return pl.pallas_call(
        _dil_kernel,
        out_shape=jax.ShapeDtypeStruct((s, width), BF16),
        grid=(width // LANE, nt),
        in_specs=[blk(cq, 0), blk(ck, -1), blk(ck, 0), blk(ck, 1), blk(cv, -1), blk(cv, 0), blk(cv, 1)]
                 + [diag_spec(d) for d in diags],
        out_specs=pl.BlockSpec((t, LANE), lambda hg, i: (i, hg)),
        scratch_shapes=[pltpu.VMEM((t, LANE), F32), pltpu.VMEM((3 * t, LANE), F32),
                        pltpu.VMEM((3 * t, LANE), F32)] + [pltpu.VMEM((t, LANE), F32)] * 9
                       + [pltpu.VMEM((2 * DIL_QB, kl), F32), pltpu.VMEM((2 * DIL_QB, kl), F32),
                          pltpu.VMEM((2 * DIL_MERGE * q16, DIL_MERGE * k16), F32)],
        compiler_params=_cp(("parallel", "arbitrary")),
        name="dilated_attention",
    )(*([z] * 7 + diags))


def _oproj_kernel(a1_ref, a2_ref, w_ref, x_ref, g_ref, gt_ref, o_ref):
    k1 = a1_ref.shape[1]
    y = jnp.dot(a1_ref[...], w_ref[0:k1, :], preferred_element_type=F32)
    y = y + jnp.dot(a2_ref[...], w_ref[k1:, :], preferred_element_type=F32)
    o_ref[...] = x_ref[...] + gt_ref[...] * (_rms(y) * g_ref[...])


def out_proj_residual(a1, a2, w, x, g, gt, tm=512):
    s, d = x.shape
    k1, k2 = a1.shape[1], a2.shape[1]
    tm = min(tm, s)
    vec = pl.BlockSpec((1, d), lambda i: (0, 0))
    return pl.pallas_call(
        _oproj_kernel,
        out_shape=jax.ShapeDtypeStruct((s, d), F32),
        grid=(s // tm,),
        in_specs=[pl.BlockSpec((tm, k1), lambda i: (i, 0)), pl.BlockSpec((tm, k2), lambda i: (i, 0)),
                  pl.BlockSpec((k1 + k2, d), lambda i: (0, 0)),
                  pl.BlockSpec((tm, d), lambda i: (i, 0)), vec, vec],
        out_specs=pl.BlockSpec((tm, d), lambda i: (i, 0)),
        compiler_params=_cp(("parallel",)),
        name="out_proj_residual",
    )(a1, a2, w, x, g, gt)


def _ffn_kernel(x_ref, g_ref, sh_ref, sc_ref, g2_ref, gt_ref, w1_ref, w3_ref, w2_ref, o_ref, h_ref):
    f = pl.program_id(1)

    @pl.when(f == 0)
    def _():
        h_ref[...] = _modnorm(x_ref[...], g_ref[...], sh_ref[...], sc_ref[...]).astype(h_ref.dtype)
        o_ref[...] = jnp.zeros_like(o_ref)

    h = h_ref[...]
    u = jnp.dot(h, w1_ref[...], preferred_element_type=F32)
    v = jnp.dot(h, w3_ref[...], preferred_element_type=F32)
    o_ref[...] += jnp.dot((_silu(u) * v).astype(BF16), w2_ref[...], preferred_element_type=F32)

    @pl.when(f == pl.num_programs(1) - 1)
    def _():
        o_ref[...] = x_ref[...] + gt_ref[...] * (_rms(o_ref[...]) * g2_ref[...])


def ffn_sublayer(x, g, sh, sc, g2, gt, w1, w3, w2, tm=1024, tf=256):
    s, d = x.shape
    f = w1.shape[1]
    tm, tf = min(tm, s), min(tf, f)
    vec = pl.BlockSpec((1, d), lambda i, j: (0, 0))
    return pl.pallas_call(
        _ffn_kernel,
        out_shape=jax.ShapeDtypeStruct((s, d), F32),
        grid=(s // tm, f // tf),
        in_specs=[pl.BlockSpec((tm, d), lambda i, j: (i, 0)), vec, vec, vec, vec, vec,
                  pl.BlockSpec((d, tf), lambda i, j: (0, j)), pl.BlockSpec((d, tf), lambda i, j: (0, j)),
                  pl.BlockSpec((tf, d), lambda i, j: (j, 0))],
        out_specs=pl.BlockSpec((tm, d), lambda i, j: (i, 0)),
        scratch_shapes=[pltpu.VMEM((tm, d), BF16)],
        compiler_params=_cp(("parallel", "arbitrary")),
        name="ffn_sublayer",
    )(x, g, sh, sc, g2, gt, w1, w3, w2)


def _rope_coeffs(tab):
    half = ROPE_DIM // 2
    lane = lax.broadcasted_iota(jnp.int32, (1, LANE), 1)
    first, second = lane < half, (lane >= half) & (lane < 2 * half)
    c = jnp.where(first, tab, 0.0) + jnp.where(second, pltpu.roll(tab, half, 1), 0.0)
    s1 = jnp.where(first, -pltpu.roll(tab, LANE - half, 1), 0.0)
    s2 = jnp.where(second, tab, 0.0)
    return c, s1, s2


def _rope_slab(t, coeffs):
    c, s1, s2 = coeffs
    return t * c + pltpu.roll(t, LANE - ROPE_DIM // 2, 1) * s1 + pltpu.roll(t, ROPE_DIM // 2, 1) * s2


def _oddin_kernel(x_ref, g_ref, sh_ref, sc_ref, win_ref, qg_ref, wuq_ref, kvg_ref, wuk_ref, wuv_ref,
                  rope_ref, q_ref, k_ref, v_ref, u_ref, *, q_lora, kv_lora, scale):
    h = _modnorm(x_ref[...], g_ref[...], sh_ref[...], sc_ref[...]).astype(BF16)
    z = jnp.dot(h, win_ref[...], preferred_element_type=F32)
    o_kv, o_pe, o_u = q_lora, q_lora + kv_lora, q_lora + kv_lora + LANE
    u_ref[...] = z[:, o_u:]

    qn = (_rms(z[:, :q_lora]) * qg_ref[...]).astype(BF16)
    q = jnp.dot(qn, wuq_ref[...], preferred_element_type=F32)
    kvn = (_rms(z[:, o_kv:o_pe]) * kvg_ref[...]).astype(BF16)
    kn = jnp.dot(kvn, wuk_ref[...], preferred_element_type=F32)
    v_ref[...] = jnp.dot(kvn, wuv_ref[...], preferred_element_type=F32).astype(v_ref.dtype)
    coeffs = _rope_coeffs(rope_ref[...])
    kpe = _rope_slab(z[:, o_pe:o_u], coeffs).astype(k_ref.dtype)

    for hh in range(MLA_HEADS):
        b = hh * MLA_QK_PAD
        q_ref[:, b:b + NOPE_DIM] = (q[:, b:b + NOPE_DIM] * scale).astype(q_ref.dtype)
        qpe = _rope_slab(q[:, b + NOPE_DIM:b + MLA_QK_PAD], coeffs)
        q_ref[:, b + NOPE_DIM:b + MLA_QK_PAD] = (qpe * scale).astype(q_ref.dtype)
        k_ref[:, b:b + NOPE_DIM] = kn[:, hh * NOPE_DIM:(hh + 1) * NOPE_DIM].astype(k_ref.dtype)
        k_ref[:, b + NOPE_DIM:b + MLA_QK_PAD] = kpe


def _rope_tables(seq):
    pos = jnp.arange(seq, dtype=F32)
    inv_freq = ROPE_BASE ** (-jnp.arange(0, ROPE_DIM, 2, dtype=F32) / ROPE_DIM)
    ang = pos[:, None] * inv_freq[None, :]
    pad = jnp.zeros((seq, LANE - ROPE_DIM), F32)
    return jnp.concatenate([jnp.cos(ang), jnp.sin(ang), pad], axis=-1)


def odd_in_proj(x, g, sh, sc, w_in, q_norm_g, w_uq, kv_norm_g, w_ukv, tm=256):
    s, d = x.shape
    q_lora, kv_lora = q_norm_g.shape[0], kv_norm_g.shape[0]
    s5_dim = w_in.shape[1] - q_lora - kv_lora - ROPE_DIM
    tm = min(tm, s)
    w_in_p = jnp.concatenate(
        [w_in[:, :q_lora + kv_lora + ROPE_DIM], jnp.zeros((d, LANE - ROPE_DIM), w_in.dtype),
         w_in[:, q_lora + kv_lora + ROPE_DIM:]], axis=1).astype(BF16)
    wq = w_uq.reshape(q_lora, MLA_HEADS, NOPE_DIM + ROPE_DIM)
    wq = jnp.pad(wq, ((0, 0), (0, 0), (0, MLA_QK_PAD - NOPE_DIM - ROPE_DIM)))
    wq = wq.reshape(q_lora, MLA_HEADS * MLA_QK_PAD).astype(BF16)
    wkv = w_ukv.reshape(kv_lora, MLA_HEADS, NOPE_DIM + V_DIM)
    wuk = wkv[:, :, :NOPE_DIM].reshape(kv_lora, MLA_HEADS * NOPE_DIM).astype(BF16)
    wuv = wkv[:, :, NOPE_DIM:].reshape(kv_lora, MLA_HEADS * V_DIM).astype(BF16)
    rope = _rope_tables(s)
    scale = (NOPE_DIM + ROPE_DIM) ** -0.5 * LOG2E

    full = lambda a: pl.BlockSpec(a.shape, lambda i: (0, 0))
    row = lambda w: pl.BlockSpec((tm, w), lambda i: (i, 0))
    vec = pl.BlockSpec((1, d), lambda i: (0, 0))
    qg = q_norm_g.reshape(1, q_lora)
    kvg = kv_norm_g.reshape(1, kv_lora)
    return pl.pallas_call(
        functools.partial(_oddin_kernel, q_lora=q_lora, kv_lora=kv_lora, scale=scale),
        out_shape=[jax.ShapeDtypeStruct((s, MLA_HEADS * MLA_QK_PAD), BF16),
                   jax.ShapeDtypeStruct((s, MLA_HEADS * MLA_QK_PAD), BF16),
                   jax.ShapeDtypeStruct((s, MLA_HEADS * V_DIM), BF16),
                   jax.ShapeDtypeStruct((s, s5_dim), F32)],
        grid=(s // tm,),
        in_specs=[row(d), vec, vec, vec, full(w_in_p), full(qg), full(wq), full(kvg), full(wuk), full(wuv),
                  row(LANE)],
        out_specs=[row(MLA_HEADS * MLA_QK_PAD), row(MLA_HEADS * MLA_QK_PAD), row(MLA_HEADS * V_DIM),
                   row(s5_dim)],
        compiler_params=_cp(("parallel",)),
        name="odd_in_proj",
    )(x, g, sh, sc, w_in_p, qg, wq, kvg, wuk, wuv, rope)


def _flash_kernel(q_ref, k_ref, v_ref, o_ref, m_ref, acc_ref, s0, s1, *, rows, nk):
    t = pl.program_id(0)
    j1 = lax.rem(jnp.maximum(t - 1, 0), nk)
    tq = q_ref.shape[0]
    vd = v_ref.shape[1]

    @pl.when(t == 0)
    def _():
        m_ref[...] = jnp.full_like(m_ref, -jnp.inf)
        acc_ref[...] = jnp.zeros_like(acc_ref)
        s1[...] = jnp.full_like(s1, -jnp.inf)

    def stages(s_rd, s_wr):
        v = v_ref[...]
        v1 = jnp.concatenate([v, jnp.ones_like(v)], axis=1)
        k = k_ref[...]
        live = t >= 1
        fresh = j1 == 0
        for c in range(tq // rows):
            rs = slice(c * rows, (c + 1) * rows)
            s = s_rd[rs, :]
            m_prev = jnp.where(fresh, -jnp.inf, m_ref[rs, :])
            m_cand = jnp.maximum(m_prev, jnp.max(s, axis=-1, keepdims=True))
            m_new = jnp.where(live, m_cand, m_prev)
            m_sub = jnp.where(live, m_cand, 0.0)
            p = jnp.exp2((s - m_sub[:, :1]).astype(BF16))
            alpha = jnp.where(live, jnp.exp2(m_prev - m_new), 1.0)
            m_ref[rs, :] = m_new
            pv = jnp.dot(p, v1, preferred_element_type=F32)
            acc_ref[rs, :vd] = alpha * acc_ref[rs, :vd] + pv[:, :vd]
            acc_ref[rs, vd:] = alpha * acc_ref[rs, vd:] + pv[:, vd:]
            s_wr[rs, :] = lax.dot_general(q_ref[rs, :], k, (((1,), (1,)), ((), ())),
                                          preferred_element_type=F32)

    parity = lax.rem(t, 2)

    @pl.when(parity == 0)
    def _():
        stages(s1, s0)

    @pl.when(parity == 1)
    def _():
        stages(s0, s1)

    @pl.when((t >= 1) & (j1 == nk - 1))
    def _():
        o_ref[...] = (acc_ref[:, :vd] / acc_ref[:, vd:]).astype(o_ref.dtype)


def mla_attention(q, k, v, tq=2048, tk=2048, rows=512):
    s = q.shape[0]
    tq, tk = min(tq, s), min(tk, s)
    rows = min(rows, tq)
    ni, nk = s // tq, s // tk
    n = MLA_HEADS * ni * nk

    def item(t, lag):
        w = jnp.clip(t - lag, 0, n - 1)
        return w // (ni * nk), lax.rem(w, ni * nk) // nk, lax.rem(w, nk)

    def q_map(t):
        h, i, _ = item(t, 0)
        return i, h

    def k_map(t):
        h, _, j = item(t, 0)
        return j, h

    def v_map(t):
        h, _, j = item(t, 1)
        return j, h

    def o_map(t):
        h, i, _ = item(t, 1)
        return i, h

    return pl.pallas_call(
        functools.partial(_flash_kernel, rows=rows, nk=nk),
        out_shape=jax.ShapeDtypeStruct((s, MLA_HEADS * V_DIM), BF16),
        grid=(n + 1,),
        in_specs=[pl.BlockSpec((tq, MLA_QK_PAD), q_map),
                  pl.BlockSpec((tk, MLA_QK_PAD), k_map),
                  pl.BlockSpec((tk, V_DIM), v_map)],
        out_specs=pl.BlockSpec((tq, V_DIM), o_map),
        scratch_shapes=[pltpu.VMEM((tq, V_DIM), F32), pltpu.VMEM((tq, 2 * V_DIM), F32),
                        pltpu.VMEM((tq, tk), F32), pltpu.VMEM((tq, tk), F32)],
        compiler_params=_cp(("arbitrary",)),
        name="mla_flash_attention",
    )(q, k, v)


def _s5_matrices(a_re, a_im, log_dt, b_re, b_im, c_re, c_im, chunk):
    L = chunk
    A = lax.complex(a_re.astype(F32), a_im.astype(F32))
    dt = jnp.exp(log_dt.astype(F32))[..., None]
    adt = A * dt
    a_bar = jnp.exp(adt)
    b_bar = ((a_bar - 1.0) / A)[..., None] * lax.complex(b_re.astype(F32), b_im.astype(F32))
    c_c = lax.complex(c_re.astype(F32), c_im.astype(F32))
    kk = jnp.arange(L + 1, dtype=F32)
    apow = jnp.exp(adt[:, :, None, :] * kk[None, None, :, None].astype(jnp.complex64))
    g, p, gc = b_bar.shape[1], b_bar.shape[2], b_bar.shape[3]

    ker = jnp.real(jnp.einsum('dgcp,dgkp,dgpi->dgkci', c_c, apow[:, :, :L], b_bar))
    kb = ker[1][:, ::-1]
    diag = jnp.concatenate([kb[:, :L - 1], ker[0][:, :1] + kb[:, L - 1:], ker[0][:, 1:]], axis=1)
    lag = jnp.arange(L)[None, :] - jnp.arange(L)[:, None] + L - 1
    onehot = (jnp.arange(2 * L - 1)[:, None, None] == lag[None]).astype(F32)
    t_mat = jnp.einsum('gmci,mst->gsitc', diag, onehot, precision=lax.Precision.HIGHEST)
    t_mat = t_mat.reshape(g, L * gc, L * gc)

    pf = apow[0][:, ::-1][:, 1:][:, :, :, None] * b_bar[0][:, None]
    pb = apow[1][:, :L][:, :, :, None] * b_bar[1][:, None]
    def p_lay(m):
        return jnp.transpose(m, (0, 1, 3, 2)).reshape(g, L * gc, p)
    p_mat = jnp.concatenate([p_lay(jnp.real(pf)), p_lay(jnp.imag(pf)),
                             p_lay(jnp.real(pb)), p_lay(jnp.imag(pb))], axis=-1)

    wf = c_c[0][:, None] * apow[0][:, 1:][:, :, None, :]
    wb = c_c[1][:, None] * apow[1][:, ::-1][:, :L][:, :, None, :]
    def q_lay(m):
        return jnp.transpose(m, (0, 3, 1, 2)).reshape(g, p, L * gc)
    q_mat = jnp.concatenate([q_lay(jnp.real(wf)), q_lay(-jnp.imag(wf)),
                             q_lay(jnp.real(wb)), q_lay(-jnp.imag(wb))], axis=1)

    al = apow[:, :, L]
    dec = jnp.stack([jnp.real(al[0]), jnp.imag(al[0]), jnp.real(al[1]), jnp.imag(al[1])]).reshape(4, g * p)
    return t_mat, p_mat, q_mat, dec


def _s5_state_kernel(u_ref, p_ref, fre_ref, fim_ref, bre_ref, bim_ref):
    outs = [[], [], [], []]
    for gi in range(u_ref.shape[0]):
        xe = jnp.dot(u_ref[gi].astype(BF16), p_ref[gi], preferred_element_type=F32)
        for part in range(4):
            outs[part].append(xe[:, part * S5_STATE:(part + 1) * S5_STATE])
    for part, ref in enumerate((fre_ref, fim_ref, bre_ref, bim_ref)):
        ref[...] = jnp.concatenate(outs[part], axis=-1)


def _s5_scan_kernel(fre_ref, fim_ref, bre_ref, bim_ref, dec_ref, ofre_ref, ofim_ref, obre_ref, obim_ref):
    nc = fre_ref.shape[0]
    w = fre_ref.shape[1]
    far, fai, bar, bai = (dec_ref[i:i + 1, :] for i in range(4))

    def fwd(c, st):
        re, im = st
        ofre_ref[pl.ds(c, 1), :] = re
        ofim_ref[pl.ds(c, 1), :] = im
        return (far * re - fai * im + fre_ref[pl.ds(c, 1), :],
                far * im + fai * re + fim_ref[pl.ds(c, 1), :])

    def bwd(i, st):
        c = nc - 1 - i
        re, im = st
        obre_ref[pl.ds(c, 1), :] = re
        obim_ref[pl.ds(c, 1), :] = im
        return (bar * re - bai * im + bre_ref[pl.ds(c, 1), :],
                bar * im + bai * re + bim_ref[pl.ds(c, 1), :])

    zero = (jnp.zeros((1, w), F32), jnp.zeros((1, w), F32))
    lax.fori_loop(0, nc, fwd, zero)
    lax.fori_loop(0, nc, bwd, zero)


def _s5_out_kernel(u_ref, t_ref, q_ref, fre_ref, fim_ref, bre_ref, bim_ref, y_ref):
    for gi in range(u_ref.shape[0]):
        lanes = slice(gi * S5_STATE, (gi + 1) * S5_STATE)
        y = jnp.dot(u_ref[gi].astype(BF16), t_ref[gi], preferred_element_type=F32)
        xin = jnp.concatenate([r[:, lanes] for r in (fre_ref, fim_ref, bre_ref, bim_ref)], axis=-1)
        y_ref[gi] = (y + jnp.dot(xin.astype(BF16), q_ref[gi], preferred_element_type=F32)).astype(y_ref.dtype)


def _s5_to_groups_kernel(u_ref, o_ref, cat_ref, *, chunk):
    nc = cat_ref.shape[0]
    gc = S5_GROUP
    for s in range(chunk):
        cat_ref[:, s * LANE:(s + 1) * LANE] = u_ref[pl.ds(s, nc, stride=chunk), :].astype(BF16)
    cat = cat_ref[...]
    r = lax.broadcasted_iota(jnp.int32, (chunk * LANE, chunk * gc), 0)
    c = lax.broadcasted_iota(jnp.int32, (chunk * LANE, chunk * gc), 1)
    lane_of_col = jnp.where(r // LANE == c // gc, lax.rem(c, gc), -LANE)
    for j in range(o_ref.shape[0]):
        sel = jnp.where(lax.rem(r, LANE) - j * gc == lane_of_col, 1.0, 0.0).astype(BF16)
        o_ref[j] = jnp.dot(cat, sel, preferred_element_type=F32).astype(o_ref.dtype)


def _s5_from_groups_kernel(y_ref, o_ref, cat_ref, *, chunk):
    nc = cat_ref.shape[0]
    gc = S5_GROUP
    w = chunk * gc
    for j in range(y_ref.shape[0]):
        cat_ref[:, j * w:(j + 1) * w] = y_ref[j]
    cat = cat_ref[...]
    r = lax.broadcasted_iota(jnp.int32, (cat.shape[1], LANE), 0)
    lane = lax.broadcasted_iota(jnp.int32, (cat.shape[1], LANE), 1)
    src_lane = (r // w) * gc + lax.rem(r, gc)
    t_of_row = lax.rem(r, w) // gc
    for t in range(chunk):
        sel = jnp.where((t_of_row == t) & (src_lane == lane), 1.0, 0.0).astype(BF16)
        o_ref[pl.ds(t, nc, stride=chunk), :] = jnp.dot(cat, sel, preferred_element_type=F32)


def _s5_relayout(x, to_groups, s, g, chunk):
    gc = S5_GROUP
    nc = s // chunk
    per = LANE // gc
    nat = pl.BlockSpec((s, LANE), lambda i: (0, i))
    grp = pl.BlockSpec((per, nc, chunk * gc), lambda i: (i, 0, 0))
    if to_groups:
        kern, in_spec, out_spec = _s5_to_groups_kernel, nat, grp
        out_shape = jax.ShapeDtypeStruct((g, nc, chunk * gc), BF16)
        scratch = pltpu.VMEM((nc, chunk * LANE), BF16)
    else:
        kern, in_spec, out_spec = _s5_from_groups_kernel, grp, nat
        out_shape = jax.ShapeDtypeStruct((s, g * gc), F32)
        scratch = pltpu.VMEM((nc, per * chunk * gc), BF16)
    return pl.pallas_call(
        functools.partial(kern, chunk=chunk),
        out_shape=out_shape,
        grid=(g // per,),
        in_specs=[in_spec],
        out_specs=out_spec,
        scratch_shapes=[scratch],
        compiler_params=_cp(("parallel",)),
        name="s5_to_groups" if to_groups else "s5_from_groups",
    )(x)


def s5_bidirectional(u, a_re, a_im, log_dt, b_re, b_im, c_re, c_im, chunk=S5_CHUNK):
    s, dim = u.shape
    gc, p = S5_GROUP, S5_STATE
    g = dim // gc
    chunk = min(chunk, s)
    nc = s // chunk
    gp = LANE // p
    t_mat, p_mat, q_mat, dec = _s5_matrices(a_re, a_im, log_dt, b_re, b_im, c_re, c_im, chunk)
    t_mat, p_mat, q_mat = t_mat.astype(BF16), p_mat.astype(BF16), q_mat.astype(BF16)
    u_t = _s5_relayout(u, True, s, g, chunk)

    grp = lambda a, b: pl.BlockSpec((gp, a, b), lambda i: (i, 0, 0))
    st = pl.BlockSpec((nc, LANE), lambda i: (0, i))
    st_shape = jax.ShapeDtypeStruct((nc, g * p), F32)
    xe = pl.pallas_call(
        _s5_state_kernel,
        out_shape=[st_shape] * 4,
        grid=(g // gp,),
        in_specs=[grp(nc, chunk * gc), grp(chunk * gc, 4 * p)],
        out_specs=[st] * 4,
        compiler_params=_cp(("parallel",)),
        name="s5_chunk_states",
    )(u_t, p_mat)

    lanes = min(4 * LANE, g * p)
    cols = lambda rows: pl.BlockSpec((rows, lanes), lambda i: (0, i))
    xin = pl.pallas_call(
        _s5_scan_kernel,
        out_shape=[st_shape] * 4,
        grid=(g * p // lanes,),
        in_specs=[cols(nc)] * 4 + [cols(4)],
        out_specs=[cols(nc)] * 4,
        compiler_params=_cp(("parallel",)),
        name="s5_boundary_scan",
    )(*xe, dec)

    y_t = pl.pallas_call(
        _s5_out_kernel,
        out_shape=jax.ShapeDtypeStruct((g, nc, chunk * gc), BF16),
        grid=(g // gp,),
        in_specs=[grp(nc, chunk * gc), grp(chunk * gc, chunk * gc), grp(4 * p, chunk * gc)] + [st] * 4,
        out_specs=grp(nc, chunk * gc),
        compiler_params=_cp(("parallel",)),
        name="s5_outputs",
    )(u_t, t_mat, q_mat, *xin)
    return _s5_relayout(y_t, False, s, g, chunk)


def _s5_gate_kernel(y_ref, u_ref, d_ref, w_ref, o_ref):
    y = y_ref[...].astype(F32) + d_ref[...] * u_ref[...]
    y = 0.5 * y * (1.0 + jnp.tanh(math.sqrt(2.0 / math.pi) * (y + 0.044715 * (y * y * y))))
    z = jnp.dot(y.astype(BF16), w_ref[...], preferred_element_type=F32)
    o_ref[...] = (y * _sigmoid(z)).astype(o_ref.dtype)


def s5_gate(y, u, d_skip, w_glu, tm=1024):
    s, dim = y.shape
    tm = min(tm, s)
    row = pl.BlockSpec((tm, dim), lambda i: (i, 0))
    return pl.pallas_call(
        _s5_gate_kernel,
        out_shape=jax.ShapeDtypeStruct((s, dim), BF16),
        grid=(s // tm,),
        in_specs=[row, row, pl.BlockSpec((1, dim), lambda i: (0, 0)), pl.BlockSpec((dim, dim), lambda i: (0, 0))],
        out_specs=row,
        compiler_params=_cp(("parallel",)),
        name="s5_gate",
    )(y, u, d_skip.reshape(1, dim), w_glu.astype(BF16))


def _router_kernel(x_ref, g_ref, sh_ref, sc_ref, rw_ref, h_ref, info_ref):
    h = _modnorm(x_ref[...], g_ref[...], sh_ref[...], sc_ref[...])
    half = h.shape[1] // 2
    bits = lambda v: lax.bitcast_convert_type(v.astype(BF16).astype(F32), jnp.uint32)
    h_ref[...] = (bits(h[:, :half]) >> 16) | (bits(h[:, half:]) & jnp.uint32(0xFFFF0000))
    logits = jnp.dot(h, rw_ref[...], preferred_element_type=F32, precision=lax.Precision.HIGHEST)
    lane = lax.broadcasted_iota(jnp.int32, logits.shape, 1)
    logits = jnp.where(lane < N_EXPERTS, logits, -jnp.inf)
    m1 = jnp.max(logits, axis=-1, keepdims=True)
    i1 = jnp.min(jnp.where(logits == m1, lane, LANE), axis=-1, keepdims=True)
    rest = jnp.where(lane == i1, -jnp.inf, logits)
    m2 = jnp.max(rest, axis=-1, keepdims=True)
    i2 = jnp.min(jnp.where(rest == m2, lane, LANE), axis=-1, keepdims=True)
    e = jnp.exp(m2 - m1)
    g1 = 1.0 / (1.0 + e)
    g2 = e / (1.0 + e)
    info = jnp.where(lane == 0, i1.astype(F32),
                     jnp.where(lane == 1, i2.astype(F32),
                               jnp.where(lane == 2, g1, jnp.where(lane == 3, g2, 0.0))))
    info_ref[...] = info


def moe_router(x, g, sh, sc, router_w, tm=512):
    s, d = x.shape
    tm = min(tm, s)
    rw = jnp.pad(router_w, ((0, 0), (0, LANE - router_w.shape[1])))
    vec = pl.BlockSpec((1, d), lambda i: (0, 0))
    return pl.pallas_call(
        _router_kernel,
        out_shape=[jax.ShapeDtypeStruct((s, d // 2), jnp.uint32), jax.ShapeDtypeStruct((s, LANE), F32)],
        grid=(s // tm,),
        in_specs=[pl.BlockSpec((tm, d), lambda i: (i, 0)), vec, vec, vec,
                  pl.BlockSpec((d, LANE), lambda i: (0, 0))],
        out_specs=[pl.BlockSpec((tm, d // 2), lambda i: (i, 0)), pl.BlockSpec((tm, LANE), lambda i: (i, 0))],
        compiler_params=_cp(("parallel",)),
        name="moe_router",
    )(x, g, sh, sc, rw)


def _expert_kernel(te_ref, tv_ref, src_ref, hp_hbm, w1_ref, w3_ref, w2_ref, o_ref, acc_ref, gbuf, h_ref, sem):
    t = pl.program_id(0)
    f = pl.program_id(1)
    nt = pl.num_programs(0)
    nf = pl.num_programs(1)
    tm = acc_ref.shape[0]
    per_step = tm // nf
    valid = tv_ref[t] > 0
    slot = lax.rem(t, 2)

    def row_copy(tile, buf, r):
        return pltpu.make_async_copy(hp_hbm.at[pl.ds(src_ref[tile * tm + r], 1)],
                                     gbuf.at[buf, pl.ds(r, 1)], sem.at[buf])

    def wait_half(buf):
        pltpu.make_async_copy(hp_hbm.at[pl.ds(0, tm)], gbuf.at[buf], sem.at[buf]).wait()

    @pl.when((t == 0) & (f == 0))
    def _():
        def issue(r, carry):
            row_copy(0, 0, r).start()
            return carry
        lax.fori_loop(0, tm, issue, 0)

    @pl.when((f == 0) & ((t == 0) | (tv_ref[jnp.maximum(t - 1, 0)] > 0)))
    def _():
        wait_half(slot)

    @pl.when((f == 0) & valid)
    def _():
        w = gbuf[slot]
        half = w.shape[1]
        h_ref[:, :half] = lax.bitcast_convert_type(w << 16, F32).astype(BF16)
        h_ref[:, half:] = lax.bitcast_convert_type(w & jnp.uint32(0xFFFF0000), F32).astype(BF16)

    @pl.when(f == 0)
    def _():
        acc_ref[...] = jnp.zeros_like(acc_ref)

    def compute(rows):
        for u in range(per_step):
            row_copy(t + 1, 1 - slot, f * per_step + u).start()
        h = h_ref[:rows, :]
        u = jnp.dot(h, w1_ref[...].astype(BF16), preferred_element_type=F32)
        v = jnp.dot(h, w3_ref[...].astype(BF16), preferred_element_type=F32)
        acc_ref[:rows, :] += jnp.dot((_silu(u) * v).astype(BF16), w2_ref[...].astype(BF16),
                                     preferred_element_type=F32)

    @pl.when(tv_ref[t] > tm // 2)
    def _():
        compute(tm)

    @pl.when(valid & (tv_ref[t] <= tm // 2))
    def _():
        compute(tm // 2)

    @pl.when(f == nf - 1)
    def _():
        o_ref[...] = acc_ref[...].astype(o_ref.dtype)

    @pl.when((t == nt - 1) & (f == nf - 1) & valid)
    def _():
        wait_half(1 - slot)


def expert_ffn(h_packed, src, tile_expert, tile_valid, w1, w3, w2, tm, tf=512):
    dh = h_packed.shape[1]
    d = 2 * dh
    n = src.shape[0] - tm
    f = w1.shape[2]
    tf = min(tf, f)
    nf = f // tf
    assert tm % nf == 0 and n % tm == 0

    def fblk(j, t, tv):
        return jnp.where(tv[t] > 0, j, nf - 1)

    return pl.pallas_call(
        _expert_kernel,
        out_shape=jax.ShapeDtypeStruct((n, d), BF16),
        grid_spec=pltpu.PrefetchScalarGridSpec(
            num_scalar_prefetch=3,
            grid=(n // tm, nf),
            in_specs=[pl.BlockSpec(memory_space=pl.ANY),
                      pl.BlockSpec((None, d, tf), lambda t, j, te, tv, sr: (te[t], 0, fblk(j, t, tv))),
                      pl.BlockSpec((None, d, tf), lambda t, j, te, tv, sr: (te[t], 0, fblk(j, t, tv))),
                      pl.BlockSpec((None, tf, d), lambda t, j, te, tv, sr: (te[t], fblk(j, t, tv), 0))],
            out_specs=pl.BlockSpec((tm, d), lambda t, j, te, tv, sr: (t, 0)),
            scratch_shapes=[pltpu.VMEM((tm, d), F32), pltpu.VMEM((2, tm, dh), jnp.uint32),
                            pltpu.VMEM((tm, d), BF16), pltpu.SemaphoreType.DMA((2,))]),
        compiler_params=_cp(("arbitrary", "arbitrary")),
        name="expert_ffn",
    )(tile_expert, tile_valid, src, h_packed, w1, w3, w2)


def _dispatch(idx, tm):
    s = idx.shape[0]
    e_flat = idx.reshape(-1)
    onehot = (e_flat[None, :] == jnp.arange(N_EXPERTS)[:, None]).astype(jnp.int32)
    csum = jnp.cumsum(onehot, axis=1)
    rank = jnp.sum((csum - onehot) * onehot, axis=0)
    counts = csum[:, -1]
    padded = ((counts + tm - 1) // tm) * tm
    gend = jnp.cumsum(padded)
    slot = (gend - padded)[e_flat] + rank
    nt = -(-(2 * s) // tm) + N_EXPERTS
    src = jnp.zeros(((nt + 1) * tm,), jnp.int32).at[slot].set(
        jnp.arange(2 * s, dtype=jnp.int32) // 2, mode="promise_in_bounds", unique_indices=True)
    tstart = jnp.arange(nt, dtype=jnp.int32) * tm
    valid = tstart < gend[-1]
    te = jnp.minimum(jnp.sum((tstart[:, None] >= gend[None, :]).astype(jnp.int32), axis=1), N_EXPERTS - 1)
    nvalid = gend[-1] // tm
    fill = jnp.clip((gend - padded + counts)[te] - tstart, 0, tm)
    fill = jnp.where(valid, fill, 0)
    te = jnp.where(valid, te, te[jnp.maximum(nvalid - 1, 0)])
    return slot.reshape(s, 2), src, te.astype(jnp.int32), fill.astype(jnp.int32)


def _combine_kernel(ya_ref, yb_ref, info_ref, x_ref, g_ref, gt_ref, o_ref):
    info = info_ref[...]
    y = info[:, 2:3] * ya_ref[...].astype(F32) + info[:, 3:4] * yb_ref[...].astype(F32)
    o_ref[...] = x_ref[...] + gt_ref[...] * (_rms(y) * g_ref[...])


def moe_combine(ya, yb, info, x, g, gt, tm=512):
    s, d = x.shape
    tm = min(tm, s)
    row = lambda w: pl.BlockSpec((tm, w), lambda i: (i, 0))
    vec = pl.BlockSpec((1, d), lambda i: (0, 0))
    return pl.pallas_call(
        _combine_kernel,
        out_shape=jax.ShapeDtypeStruct((s, d), F32),
        grid=(s // tm,),
        in_specs=[row(d), row(d), row(LANE), row(d), vec, vec],
        out_specs=row(d),
        compiler_params=_cp(("parallel",)),
        name="moe_combine",
    )(ya, yb, info, x, g, gt)


def moe_sublayer(x, g, sh, sc, g2, gt, router_w, w1, w3, w2, tm_e=EXPERT_TILE):
    s = x.shape[0]
    tm_e = min(tm_e, s)
    h_packed, info = moe_router(x, g, sh, sc, router_w)
    idx = info[:, :2].astype(jnp.int32)
    slot, src, te, tv = _dispatch(idx, tm_e)
    y_sorted = expert_ffn(h_packed, src, te, tv, w1, w3, w2, tm_e)
    ya = y_sorted.at[slot[:, 0]].get(mode="promise_in_bounds")
    yb = y_sorted.at[slot[:, 1]].get(mode="promise_in_bounds")
    return moe_combine(ya, yb, info, x, g2, gt)


def kernel(x, c, t5_table, norm_g, ada_w, ada_b, e_w_in, e_conv_w, e_conv_b, e_cln_g, e_cln_b, e_w_out, e_ffn_w1, e_ffn_w3, e_ffn_w2, o_w_in, o_q_norm_g, o_w_uq, o_kv_norm_g, o_w_ukv, s5_a_re, s5_a_im, s5_log_dt, s5_b_re, s5_b_im, s5_c_re, s5_c_im, s5_d, s5_w_glu, o_w_out, router_w, moe_w1, moe_w3, moe_w2):
    bsz, seq, d = x.shape
    assert bsz == 1
    depth = norm_g.shape[0]
    conv_dim = e_conv_w.shape[2]
    xs = x.reshape(seq, d)
    mod = adaln(c, ada_w, ada_b)
    for layer in range(depth):
        i = layer // 2
        sh_m, sc_m, gt_m, sh_f, sc_f, gt_f = (mod[layer, :, k * d:(k + 1) * d] for k in range(6))
        g = [norm_g[layer, k].reshape(1, d) for k in range(4)]
        if layer % 2 == 0:
            z = norm_mod_matmul(xs, g[0], sh_m, sc_m, e_w_in[i].astype(BF16))
            a = conformer_conv(z, e_conv_w[i], e_conv_b[i], e_cln_g[i], e_cln_b[i])
            o = dilated_attention(z, t5_table, 2 * conv_dim, (z.shape[1] - 2 * conv_dim) // 3)
            xs = out_proj_residual(a, o, e_w_out[i].astype(BF16), xs, g[1], gt_m)
            xs = ffn_sublayer(xs, g[2], sh_f, sc_f, g[3], gt_f, e_ffn_w1[i].astype(BF16),
                              e_ffn_w3[i].astype(BF16), e_ffn_w2[i].astype(BF16))
        else:
            q, k, v, u = odd_in_proj(xs, g[0], sh_m, sc_m, o_w_in[i], o_q_norm_g[i], o_w_uq[i],
                                     o_kv_norm_g[i], o_w_ukv[i])
            o_mla = mla_attention(q, k, v)
            y = s5_bidirectional(u, s5_a_re[i], s5_a_im[i], s5_log_dt[i], s5_b_re[i], s5_b_im[i],
                                 s5_c_re[i], s5_c_im[i])
            y = s5_gate(y, u, s5_d[i], s5_w_glu[i])
            xs = out_proj_residual(o_mla, y, o_w_out[i].astype(BF16), xs, g[1], gt_m)
            xs = moe_sublayer(xs, g[2], sh_f, sc_f, g[3], gt_f, router_w[i], moe_w1[i], moe_w3[i], moe_w2[i])
    return xs.reshape(bsz, seq, d)
```

```python
import functools
import math

import jax
import jax.numpy as jnp
from jax import lax
from jax.experimental import pallas as pl
from jax.experimental.pallas import tpu as pltpu

F32 = jnp.float32
BF16 = jnp.bfloat16

RMS_EPS = 1e-6
LN_EPS = 1e-5
NEG_INF = -1e30
LOG2E = math.log2(math.e)

CONV_WIDTH = 31
CONV_HALO = 16
DIL_HEAD_DIM = 64
DIL_CONFIGS = ((128, 1), (512, 4), (2048, 16))
DIL_HALF = 64
DIL_TILE = 1024
DIL_QB = 256
DIL_MERGE = 2
N_BUCKETS = 32
T5_MAX_DIST = DIL_CONFIGS[-1][0] // 2
MLA_HEADS = 12
NOPE_DIM = 128
ROPE_DIM = 64
V_DIM = 128
ROPE_BASE = 10000.0
MLA_QK_PAD = 256
S5_GROUP = 16
S5_STATE = 64
S5_CHUNK = 16
N_EXPERTS = 8
EXPERT_TILE = 896
LANE = 128
SUBLANES = 8
VMEM_LIMIT_MB = 56


def _cp(sem, vmem_mb=VMEM_LIMIT_MB):
    return pltpu.CompilerParams(dimension_semantics=sem, vmem_limit_bytes=vmem_mb * 1024 * 1024)


def _rms(x):
    return x * lax.rsqrt(jnp.mean(x * x, axis=-1, keepdims=True) + RMS_EPS)


def _modnorm(x, g, sh, sc):
    return (_rms(x) * g) * (1.0 + sc) + sh


def _sigmoid(x):
    return 1.0 / (1.0 + jnp.exp(-x))


def _silu(x):
    return x * _sigmoid(x)


def _adaln_kernel(c_ref, w_ref, b_ref, o_ref, cb_ref):
    @pl.when((pl.program_id(0) == 0) & (pl.program_id(1) == 0))
    def _():
        cc = c_ref[...]
        cb_ref[...] = jnp.broadcast_to(_silu(cc), cb_ref.shape)

    cb = cb_ref[...]
    for j in range(o_ref.shape[-1] // LANE):
        sl = slice(j * LANE, (j + 1) * LANE)
        o_ref[:, sl] = jnp.sum(w_ref[:, sl] * cb, axis=0, keepdims=True) + b_ref[:, sl]


def adaln(c, ada_w, ada_b, tn=1024):
    nl, d, n = ada_w.shape
    out = pl.pallas_call(
        _adaln_kernel,
        out_shape=jax.ShapeDtypeStruct((nl, 1, n), F32),
        grid=(nl, n // tn),
        in_specs=[pl.BlockSpec((d, 1), lambda l, j: (0, 0)),
                  pl.BlockSpec((None, d, tn), lambda l, j: (l, 0, j)),
                  pl.BlockSpec((None, 1, tn), lambda l, j: (l, 0, j))],
        out_specs=pl.BlockSpec((None, 1, tn), lambda l, j: (l, 0, j)),
        scratch_shapes=[pltpu.VMEM((d, LANE), F32)],
        compiler_params=_cp(("arbitrary", "arbitrary")),
        name="adaln",
    )(c.reshape(d, 1), ada_w, ada_b.reshape(nl, 1, n))
    return out


def _nmm_kernel(x_ref, g_ref, sh_ref, sc_ref, w_ref, o_ref, h_ref):
    @pl.when(pl.program_id(1) == 0)
    def _():
        h_ref[...] = _modnorm(x_ref[...], g_ref[...], sh_ref[...], sc_ref[...]).astype(h_ref.dtype)

    o_ref[...] = jnp.dot(h_ref[...], w_ref[...], preferred_element_type=F32).astype(o_ref.dtype)


def norm_mod_matmul(x, g, sh, sc, w, tm=1024, tn=1024):
    s, d = x.shape
    n = w.shape[1]
    tm, tn = min(tm, s), min(tn, n)
    vec = pl.BlockSpec((1, d), lambda i, j: (0, 0))
    return pl.pallas_call(
        _nmm_kernel,
        out_shape=jax.ShapeDtypeStruct((s, n), BF16),
        grid=(s // tm, n // tn),
        in_specs=[pl.BlockSpec((tm, d), lambda i, j: (i, 0)), vec, vec, vec,
                  pl.BlockSpec((d, tn), lambda i, j: (0, j))],
        out_specs=pl.BlockSpec((tm, tn), lambda i, j: (i, j)),
        scratch_shapes=[pltpu.VMEM((tm, d), BF16)],
        compiler_params=_cp(("parallel", "arbitrary")),
        name="norm_mod_matmul",
    )(x, g, sh, sc, w)


def _conv_kernel(av_ref, ag_ref, avp_ref, agp_ref, avn_ref, agn_ref, w_ref, b_ref, lg_ref, lb_ref,
                 o_ref, buf_ref, sh_ref, *, rows):
    i = pl.program_id(0)
    n = pl.num_programs(0)
    ts = av_ref.shape[0]

    def glu(v_ref, g_ref):
        return v_ref[...].astype(F32) * _sigmoid(g_ref[...].astype(F32))

    buf_ref[0:CONV_HALO, :] = jnp.where(i > 0, glu(avp_ref, agp_ref), 0.0)
    buf_ref[CONV_HALO:CONV_HALO + ts, :] = glu(av_ref, ag_ref)
    buf_ref[CONV_HALO + ts:2 * CONV_HALO + ts, :] = jnp.where(i < n - 1, glu(avn_ref, agn_ref), 0.0)
    buf_ref[2 * CONV_HALO + ts:, :] = jnp.zeros((SUBLANES, buf_ref.shape[1]), F32)

    span = ts + 2 * CONV_HALO
    for o in range(SUBLANES):
        sh_ref[o] = buf_ref[o:o + span, :]

    off = CONV_HALO - CONV_WIDTH // 2
    for r in range(ts // rows):
        acc = jnp.broadcast_to(b_ref[...], (rows, b_ref.shape[-1]))
        for k in range(CONV_WIDTH):
            lo = r * rows + k + off
            base = lo - lo % SUBLANES
            acc = acc + w_ref[k:k + 1, :] * sh_ref[lo % SUBLANES, base:base + rows, :]
        mu = jnp.mean(acc, axis=-1, keepdims=True)
        dlt = acc - mu
        var = jnp.mean(dlt * dlt, axis=-1, keepdims=True)
        y = dlt * lax.rsqrt(var + LN_EPS) * lg_ref[...] + lb_ref[...]
        o_ref[r * rows:(r + 1) * rows, :] = _silu(y).astype(o_ref.dtype)


def conformer_conv(z, conv_w, conv_b, cln_g, cln_b, ts=256, rows=32):
    s = z.shape[0]
    c = conv_w.shape[1]
    ts = min(ts, s)
    hb = ts // CONV_HALO
    nh = s // CONV_HALO
    main = lambda col: pl.BlockSpec((ts, c), lambda i: (i, col))
    prev = lambda col: pl.BlockSpec((CONV_HALO, c), lambda i: (jnp.maximum(i * hb - 1, 0), col))
    nxt = lambda col: pl.BlockSpec((CONV_HALO, c), lambda i: (jnp.minimum((i + 1) * hb, nh - 1), col))
    vec = lambda r: pl.BlockSpec((r, c), lambda i: (0, 0))
    return pl.pallas_call(
        functools.partial(_conv_kernel, rows=min(rows, ts)),
        out_shape=jax.ShapeDtypeStruct((s, c), BF16),
        grid=(s // ts,),
        in_specs=[main(0), main(1), prev(0), prev(1), nxt(0), nxt(1),
                  vec(CONV_WIDTH), vec(1), vec(1), vec(1)],
        out_specs=pl.BlockSpec((ts, c), lambda i: (i, 0)),
        scratch_shapes=[pltpu.VMEM((ts + 2 * CONV_HALO + SUBLANES, c), F32),
                        pltpu.VMEM((SUBLANES, ts + 2 * CONV_HALO, c), F32)],
        compiler_params=_cp(("parallel",)),
        name="conformer_conv",
    )(z, z, z, z, z, z, conv_w, conv_b.reshape(1, c), cln_g.reshape(1, c), cln_b.reshape(1, c))


def _t5_bucket(rel):
    half = N_BUCKETS // 2
    exact = half // 2
    n = jnp.abs(rel)
    large = exact + (jnp.log(jnp.maximum(n, 1).astype(F32) / exact)
                     / math.log(T5_MAX_DIST / exact) * (half - exact)).astype(jnp.int32)
    large = jnp.minimum(large, half - 1)
    return jnp.where(rel > 0, half, 0) + jnp.where(n < exact, n, large)


def _band_diag(t5_table, dil, qb):
    kl = qb + 2 * DIL_HALF
    n = -(-(qb + kl - 1) // LANE) * LANE
    dist = jnp.arange(n) - (qb - 1) - DIL_HALF
    return jnp.where((jnp.abs(dist) <= DIL_HALF)[None], t5_table[_t5_bucket(dist * dil)].T.astype(F32), NEG_INF)


def _band_bias_rows(diag_row, qb):
    n = diag_row.shape[1]
    rolled = pltpu.roll(jnp.broadcast_to(diag_row, (qb, n)), n - (qb - 1), 1, stride=1, stride_axis=0)
    return rolled[:, :qb + 2 * DIL_HALF]


def _band_group(q, kw, vw, bias_ref, pen):
    r = q.shape[0]
    is_a = lax.broadcasted_iota(jnp.int32, (1, LANE), 1) < DIL_HEAD_DIM
    q2 = jnp.concatenate([jnp.where(is_a, q, 0.0), jnp.where(is_a, 0.0, q)], axis=0).astype(BF16)
    s = lax.dot_general(q2, kw.astype(BF16), (((1,), (1,)), ((), ())), preferred_element_type=F32)
    s = s + bias_ref[...]
    if pen is not None:
        s = s + pen
    m = jnp.max(s, axis=-1, keepdims=True)
    p = jnp.exp2((s - m).astype(BF16))
    vb = vw.astype(BF16)
    pv = jnp.dot(p, jnp.concatenate([vb, jnp.ones_like(vb)], axis=1), preferred_element_type=F32)
    pick = lambda x: jnp.where(is_a, x[:r], x[r:])
    return pick(pv[:, :LANE]), pick(m), pick(pv[:, LANE:])


def _dil_kernel(q_ref, kp_ref, kc_ref, kn_ref, vp_ref, vc_ref, vn_ref, d1_ref, d4_ref, d16_ref, o_ref,
                qf, kf, vf, a1, m1, l1, a4, m4, l4, a16, m16, l16, b1_ref, b4_ref, b16_ref):
    i = pl.program_id(1)
    first = i == 0
    last = i == pl.num_programs(1) - 1
    t = DIL_TILE
    half = DIL_HALF

    @pl.when(first)
    def _():
        q16 = t // 16
        b16_ref[...] = jnp.full(b16_ref.shape, NEG_INF, F32)
        for h in range(2):
            b1_ref[h * DIL_QB:(h + 1) * DIL_QB, :] = _band_bias_rows(d1_ref[h:h + 1, :], DIL_QB)
            b4_ref[h * DIL_QB:(h + 1) * DIL_QB, :] = _band_bias_rows(d4_ref[h:h + 1, :], DIL_QB)
            blk = _band_bias_rows(d16_ref[h:h + 1, :], q16)
            for u in range(DIL_MERGE):
                r0 = (h * DIL_MERGE + u) * q16
                b16_ref[r0:r0 + q16, u * blk.shape[1]:(u + 1) * blk.shape[1]] = blk

    qf[...] = q_ref[...].astype(F32) * (DIL_HEAD_DIM ** -0.5 * LOG2E)
    for dst, (p_ref, c_ref, n_ref) in ((kf, (kp_ref, kc_ref, kn_ref)), (vf, (vp_ref, vc_ref, vn_ref))):
        dst[0:t, :] = p_ref[...].astype(F32)
        dst[t:2 * t, :] = c_ref[...].astype(F32)
        dst[2 * t:3 * t, :] = n_ref[...].astype(F32)

    def store(refs, rows, vals):
        for ref, val in zip(refs, vals):
            ref[rows, :] = val

    qb = DIL_QB
    kl = qb + 2 * half
    col = lax.broadcasted_iota(jnp.int32, (1, kl), 1)
    lo = jnp.where((col < half) & first, NEG_INF, 0.0)
    hi = jnp.where((col >= qb + half) & last, NEG_INF, 0.0)

    nb = t // qb
    for b in range(nb):
        pen = lo + hi if nb == 1 else (lo if b == 0 else (hi if b == nb - 1 else None))
        k0 = t + b * qb - half
        rows = slice(b * qb, (b + 1) * qb)
        store((a1, m1, l1), rows, _band_group(qf[rows, :], kf[k0:k0 + kl, :], vf[k0:k0 + kl, :], b1_ref, pen))

    dil = 4
    for r in range(dil):
        qrows = pl.ds(r, qb, stride=dil)
        krows = pl.ds(t - half * dil + r, kl, stride=dil)
        store((a4, m4, l4), qrows, _band_group(qf[qrows, :], kf[krows, :], vf[krows, :], b4_ref, lo + hi))

    dil = 16
    q16 = t // dil
    k16 = q16 + 2 * half
    col = lax.rem(lax.broadcasted_iota(jnp.int32, (1, DIL_MERGE * k16), 1), k16)
    pen16 = (jnp.where((col < half) & first, NEG_INF, 0.0)
             + jnp.where((col >= q16 + half) & last, NEG_INF, 0.0))
    for g in range(dil // DIL_MERGE):
        res = range(g * DIL_MERGE, (g + 1) * DIL_MERGE)
        qrows = [pl.ds(r, q16, stride=dil) for r in res]
        krows = [pl.ds(t - half * dil + r, k16, stride=dil) for r in res]
        out = _band_group(jnp.concatenate([qf[rr, :] for rr in qrows], axis=0),
                          jnp.concatenate([kf[rr, :] for rr in krows], axis=0),
                          jnp.concatenate([vf[rr, :] for rr in krows], axis=0), b16_ref, pen16)
        for u, rr in enumerate(qrows):
            store((a16, m16, l16), rr, [x[u * q16:(u + 1) * q16] for x in out])

    mm = jnp.maximum(jnp.maximum(m1[...], m4[...]), m16[...])
    w1, w4, w16 = jnp.exp2(m1[...] - mm), jnp.exp2(m4[...] - mm), jnp.exp2(m16[...] - mm)
    num = w1 * a1[...] + w4 * a4[...] + w16 * a16[...]
    den = w1 * l1[...] + w4 * l4[...] + w16 * l16[...]
    o_ref[...] = (num / den).astype(o_ref.dtype)


def dilated_attention(z, t5_table, col0, width):
    s, zw = z.shape
    t = DIL_TILE
    assert s % t == 0 and col0 % LANE == 0 and width % LANE == 0 and t // 4 == DIL_QB
    nt = s // t
    nh = t5_table.shape[1]
    cq, ck, cv = ((col0 + k * width) // LANE for k in range(3))
    q16 = t // 16
    diags = [(_band_diag(t5_table, dil, qb) * LOG2E).reshape(nh // 2, 2, -1)
             for dil, qb in ((1, DIL_QB), (4, DIL_QB), (16, q16))]
    kl, k16 = DIL_QB + 2 * DIL_HALF, q16 + 2 * DIL_HALF

    def blk(col, shift):
        return pl.BlockSpec((t, LANE), lambda hg, i: (jnp.clip(i + shift, 0, nt - 1), col + hg))

    diag_spec = lambda d: pl.BlockSpec((None,) + d.shape[1:], lambda hg, i: (hg, 0, 0))
    return pl.pallas_call(
        _dil_kernel,
        out_shape=jax.ShapeDtypeStruct((s, width), BF16),
        grid=(width // LANE, nt),
        in_specs=[blk(cq, 0), blk(ck, -1), blk(ck, 0), blk(ck, 1), blk(cv, -1), blk(cv, 0), blk(cv, 1)]
                 + [diag_spec(d) for d in diags],
        out_specs=pl.BlockSpec((t, LANE), lambda hg, i: (i, hg)),
        scratch_shapes=[pltpu.VMEM((t, LANE), F32), pltpu.VMEM((3 * t, LANE), F32),
                        pltpu.VMEM((3 * t, LANE), F32)] + [pltpu.VMEM((t, LANE), F32)] * 9
                       + [pltpu.VMEM((2 * DIL_QB, kl), F32), pltpu.VMEM((2 * DIL_QB, kl), F32),
                          pltpu.VMEM((2 * DIL_MERGE * q16, DIL_MERGE * k16), F32)],
        compiler_params=_cp(("parallel", "arbitrary")),
        name="dilated_attention",
    )(*([z] * 7 + diags))


def _oproj_kernel(a1_ref, a2_ref, w_ref, x_ref, g_ref, gt_ref, o_ref):
    k1 = a1_ref.shape[1]
    y = jnp.dot(a1_ref[...], w_ref[0:k1, :], preferred_element_type=F32)
    y = y + jnp.dot(a2_ref[...], w_ref[k1:, :], preferred_element_type=F32)
    o_ref[...] = x_ref[...] + gt_ref[...] * (_rms(y) * g_ref[...])


def out_proj_residual(a1, a2, w, x, g, gt, tm=512):
    s, d = x.shape
    k1, k2 = a1.shape[1], a2.shape[1]
    tm = min(tm, s)
    vec = pl.BlockSpec((1, d), lambda i: (0, 0))
    return pl.pallas_call(
        _oproj_kernel,
        out_shape=jax.ShapeDtypeStruct((s, d), F32),
        grid=(s // tm,),
        in_specs=[pl.BlockSpec((tm, k1), lambda i: (i, 0)), pl.BlockSpec((tm, k2), lambda i: (i, 0)),
                  pl.BlockSpec((k1 + k2, d), lambda i: (0, 0)),
                  pl.BlockSpec((tm, d), lambda i: (i, 0)), vec, vec],
        out_specs=pl.BlockSpec((tm, d), lambda i: (i, 0)),
        compiler_params=_cp(("parallel",)),
        name="out_proj_residual",
    )(a1, a2, w, x, g, gt)


def _ffn_kernel(x_ref, g_ref, sh_ref, sc_ref, g2_ref, gt_ref, w1_ref, w3_ref, w2_ref, o_ref, h_ref):
    f = pl.program_id(1)

    @pl.when(f == 0)
    def _():
        h_ref[...] = _modnorm(x_ref[...], g_ref[...], sh_ref[...], sc_ref[...]).astype(h_ref.dtype)
        o_ref[...] = jnp.zeros_like(o_ref)

    h = h_ref[...]
    u = jnp.dot(h, w1_ref[...], preferred_element_type=F32)
    v = jnp.dot(h, w3_ref[...], preferred_element_type=F32)
    o_ref[...] += jnp.dot((_silu(u) * v).astype(BF16), w2_ref[...], preferred_element_type=F32)

    @pl.when(f == pl.num_programs(1) - 1)
    def _():
        o_ref[...] = x_ref[...] + gt_ref[...] * (_rms(o_ref[...]) * g2_ref[...])


def ffn_sublayer(x, g, sh, sc, g2, gt, w1, w3, w2, tm=1024, tf=256):
    s, d = x.shape
    f = w1.shape[1]
    tm, tf = min(tm, s), min(tf, f)
    vec = pl.BlockSpec((1, d), lambda i, j: (0, 0))
    return pl.pallas_call(
        _ffn_kernel,
        out_shape=jax.ShapeDtypeStruct((s, d), F32),
        grid=(s // tm, f // tf),
        in_specs=[pl.BlockSpec((tm, d), lambda i, j: (i, 0)), vec, vec, vec, vec, vec,
                  pl.BlockSpec((d, tf), lambda i, j: (0, j)), pl.BlockSpec((d, tf), lambda i, j: (0, j)),
                  pl.BlockSpec((tf, d), lambda i, j: (j, 0))],
        out_specs=pl.BlockSpec((tm, d), lambda i, j: (i, 0)),
        scratch_shapes=[pltpu.VMEM((tm, d), BF16)],
        compiler_params=_cp(("parallel", "arbitrary")),
        name="ffn_sublayer",
    )(x, g, sh, sc, g2, gt, w1, w3, w2)


def _rope_coeffs(tab):
    half = ROPE_DIM // 2
    lane = lax.broadcasted_iota(jnp.int32, (1, LANE), 1)
    first, second = lane < half, (lane >= half) & (lane < 2 * half)
    c = jnp.where(first, tab, 0.0) + jnp.where(second, pltpu.roll(tab, half, 1), 0.0)
    s1 = jnp.where(first, -pltpu.roll(tab, LANE - half, 1), 0.0)
    s2 = jnp.where(second, tab, 0.0)
    return c, s1, s2


def _rope_slab(t, coeffs):
    c, s1, s2 = coeffs
    return t * c + pltpu.roll(t, LANE - ROPE_DIM // 2, 1) * s1 + pltpu.roll(t, ROPE_DIM // 2, 1) * s2


def _oddin_kernel(x_ref, g_ref, sh_ref, sc_ref, win_ref, qg_ref, wuq_ref, kvg_ref, wuk_ref, wuv_ref,
                  rope_ref, q_ref, k_ref, v_ref, u_ref, *, q_lora, kv_lora, scale):
    h = _modnorm(x_ref[...], g_ref[...], sh_ref[...], sc_ref[...]).astype(BF16)
    z = jnp.dot(h, win_ref[...], preferred_element_type=F32)
    o_kv, o_pe, o_u = q_lora, q_lora + kv_lora, q_lora + kv_lora + LANE
    u_ref[...] = z[:, o_u:]

    qn = (_rms(z[:, :q_lora]) * qg_ref[...]).astype(BF16)
    q = jnp.dot(qn, wuq_ref[...], preferred_element_type=F32)
    kvn = (_rms(z[:, o_kv:o_pe]) * kvg_ref[...]).astype(BF16)
    kn = jnp.dot(kvn, wuk_ref[...], preferred_element_type=F32)
    v_ref[...] = jnp.dot(kvn, wuv_ref[...], preferred_element_type=F32).astype(v_ref.dtype)
    coeffs = _rope_coeffs(rope_ref[...])
    kpe = _rope_slab(z[:, o_pe:o_u], coeffs).astype(k_ref.dtype)

    for hh in range(MLA_HEADS):
        b = hh * MLA_QK_PAD
        q_ref[:, b:b + NOPE_DIM] = (q[:, b:b + NOPE_DIM] * scale).astype(q_ref.dtype)
        qpe = _rope_slab(q[:, b + NOPE_DIM:b + MLA_QK_PAD], coeffs)
        q_ref[:, b + NOPE_DIM:b + MLA_QK_PAD] = (qpe * scale).astype(q_ref.dtype)
        k_ref[:, b:b + NOPE_DIM] = kn[:, hh * NOPE_DIM:(hh + 1) * NOPE_DIM].astype(k_ref.dtype)
        k_ref[:, b + NOPE_DIM:b + MLA_QK_PAD] = kpe


def _rope_tables(seq):
    pos = jnp.arange(seq, dtype=F32)
    inv_freq = ROPE_BASE ** (-jnp.arange(0, ROPE_DIM, 2, dtype=F32) / ROPE_DIM)
    ang = pos[:, None] * inv_freq[None, :]
    pad = jnp.zeros((seq, LANE - ROPE_DIM), F32)
    return jnp.concatenate([jnp.cos(ang), jnp.sin(ang), pad], axis=-1)


def odd_in_proj(x, g, sh, sc, w_in, q_norm_g, w_uq, kv_norm_g, w_ukv, tm=256):
    s, d = x.shape
    q_lora, kv_lora = q_norm_g.shape[0], kv_norm_g.shape[0]
    s5_dim = w_in.shape[1] - q_lora - kv_lora - ROPE_DIM
    tm = min(tm, s)
    w_in_p = jnp.concatenate(
        [w_in[:, :q_lora + kv_lora + ROPE_DIM], jnp.zeros((d, LANE - ROPE_DIM), w_in.dtype),
         w_in[:, q_lora + kv_lora + ROPE_DIM:]], axis=1).astype(BF16)
    wq = w_uq.reshape(q_lora, MLA_HEADS, NOPE_DIM + ROPE_DIM)
    wq = jnp.pad(wq, ((0, 0), (0, 0), (0, MLA_QK_PAD - NOPE_DIM - ROPE_DIM)))
    wq = wq.reshape(q_lora, MLA_HEADS * MLA_QK_PAD).astype(BF16)
    wkv = w_ukv.reshape(kv_lora, MLA_HEADS, NOPE_DIM + V_DIM)
    wuk = wkv[:, :, :NOPE_DIM].reshape(kv_lora, MLA_HEADS * NOPE_DIM).astype(BF16)
    wuv = wkv[:, :, NOPE_DIM:].reshape(kv_lora, MLA_HEADS * V_DIM).astype(BF16)
    rope = _rope_tables(s)
    scale = (NOPE_DIM + ROPE_DIM) ** -0.5 * LOG2E

    full = lambda a: pl.BlockSpec(a.shape, lambda i: (0, 0))
    row = lambda w: pl.BlockSpec((tm, w), lambda i: (i, 0))
    vec = pl.BlockSpec((1, d), lambda i: (0, 0))
    qg = q_norm_g.reshape(1, q_lora)
    kvg = kv_norm_g.reshape(1, kv_lora)
    return pl.pallas_call(
        functools.partial(_oddin_kernel, q_lora=q_lora, kv_lora=kv_lora, scale=scale),
        out_shape=[jax.ShapeDtypeStruct((s, MLA_HEADS * MLA_QK_PAD), BF16),
                   jax.ShapeDtypeStruct((s, MLA_HEADS * MLA_QK_PAD), BF16),
                   jax.ShapeDtypeStruct((s, MLA_HEADS * V_DIM), BF16),
                   jax.ShapeDtypeStruct((s, s5_dim), F32)],
        grid=(s // tm,),
        in_specs=[row(d), vec, vec, vec, full(w_in_p), full(qg), full(wq), full(kvg), full(wuk), full(wuv),
                  row(LANE)],
        out_specs=[row(MLA_HEADS * MLA_QK_PAD), row(MLA_HEADS * MLA_QK_PAD), row(MLA_HEADS * V_DIM),
                   row(s5_dim)],
        compiler_params=_cp(("parallel",)),
        name="odd_in_proj",
    )(x, g, sh, sc, w_in_p, qg, wq, kvg, wuk, wuv, rope)


def _flash_kernel(q_ref, k_ref, v_ref, o_ref, m_ref, acc_ref, s0, s1, *, rows, nk):
    t = pl.program_id(0)
    j1 = lax.rem(jnp.maximum(t - 1, 0), nk)
    tq = q_ref.shape[0]
    vd = v_ref.shape[1]

    @pl.when(t == 0)
    def _():
        m_ref[...] = jnp.full_like(m_ref, -jnp.inf)
        acc_ref[...] = jnp.zeros_like(acc_ref)
        s1[...] = jnp.full_like(s1, -jnp.inf)

    def stages(s_rd, s_wr):
        v = v_ref[...]
        v1 = jnp.concatenate([v, jnp.ones_like(v)], axis=1)
        k = k_ref[...]
        live = t >= 1
        fresh = j1 == 0
        for c in range(tq // rows):
            rs = slice(c * rows, (c + 1) * rows)
            s = s_rd[rs, :]
            m_prev = jnp.where(fresh, -jnp.inf, m_ref[rs, :])
            m_cand = jnp.maximum(m_prev, jnp.max(s, axis=-1, keepdims=True))
            m_new = jnp.where(live, m_cand, m_prev)
            m_sub = jnp.where(live, m_cand, 0.0)
            p = jnp.exp2((s - m_sub[:, :1]).astype(BF16))
            alpha = jnp.where(live, jnp.exp2(m_prev - m_new), 1.0)
            m_ref[rs, :] = m_new
            pv = jnp.dot(p, v1, preferred_element_type=F32)
            acc_ref[rs, :vd] = alpha * acc_ref[rs, :vd] + pv[:, :vd]
            acc_ref[rs, vd:] = alpha * acc_ref[rs, vd:] + pv[:, vd:]
            s_wr[rs, :] = lax.dot_general(q_ref[rs, :], k, (((1,), (1,)), ((), ())),
                                          preferred_element_type=F32)

    parity = lax.rem(t, 2)

    @pl.when(parity == 0)
    def _():
        stages(s1, s0)

    @pl.when(parity == 1)
    def _():
        stages(s0, s1)

    @pl.when((t >= 1) & (j1 == nk - 1))
    def _():
        o_ref[...] = (acc_ref[:, :vd] / acc_ref[:, vd:]).astype(o_ref.dtype)


def mla_attention(q, k, v, tq=2048, tk=2048, rows=512):
    s = q.shape[0]
    tq, tk = min(tq, s), min(tk, s)
    rows = min(rows, tq)
    ni, nk = s // tq, s // tk
    n = MLA_HEADS * ni * nk

    def item(t, lag):
        w = jnp.clip(t - lag, 0, n - 1)
        return w // (ni * nk), lax.rem(w, ni * nk) // nk, lax.rem(w, nk)

    def q_map(t):
        h, i, _ = item(t, 0)
        return i, h

    def k_map(t):
        h, _, j = item(t, 0)
        return j, h

    def v_map(t):
        h, _, j = item(t, 1)
        return j, h

    def o_map(t):
        h, i, _ = item(t, 1)
        return i, h

    return pl.pallas_call(
        functools.partial(_flash_kernel, rows=rows, nk=nk),
        out_shape=jax.ShapeDtypeStruct((s, MLA_HEADS * V_DIM), BF16),
        grid=(n + 1,),
        in_specs=[pl.BlockSpec((tq, MLA_QK_PAD), q_map),
                  pl.BlockSpec((tk, MLA_QK_PAD), k_map),
                  pl.BlockSpec((tk, V_DIM), v_map)],
        out_specs=pl.BlockSpec((tq, V_DIM), o_map),
        scratch_shapes=[pltpu.VMEM((tq, V_DIM), F32), pltpu.VMEM((tq, 2 * V_DIM), F32),
                        pltpu.VMEM((tq, tk), F32), pltpu.VMEM((tq, tk), F32)],
        compiler_params=_cp(("arbitrary",)),
        name="mla_flash_attention",
    )(q, k, v)


def _s5_matrices(a_re, a_im, log_dt, b_re, b_im, c_re, c_im, chunk):
    L = chunk
    A = lax.complex(a_re.astype(F32), a_im.astype(F32))
    dt = jnp.exp(log_dt.astype(F32))[..., None]
    adt = A * dt
    a_bar = jnp.exp(adt)
    b_bar = ((a_bar - 1.0) / A)[..., None] * lax.complex(b_re.astype(F32), b_im.astype(F32))
    c_c = lax.complex(c_re.astype(F32), c_im.astype(F32))
    kk = jnp.arange(L + 1, dtype=F32)
    apow = jnp.exp(adt[:, :, None, :] * kk[None, None, :, None].astype(jnp.complex64))
    g, p, gc = b_bar.shape[1], b_bar.shape[2], b_bar.shape[3]

    ker = jnp.real(jnp.einsum('dgcp,dgkp,dgpi->dgkci', c_c, apow[:, :, :L], b_bar))
    kb = ker[1][:, ::-1]
    diag = jnp.concatenate([kb[:, :L - 1], ker[0][:, :1] + kb[:, L - 1:], ker[0][:, 1:]], axis=1)
    lag = jnp.arange(L)[None, :] - jnp.arange(L)[:, None] + L - 1
    onehot = (jnp.arange(2 * L - 1)[:, None, None] == lag[None]).astype(F32)
    t_mat = jnp.einsum('gmci,mst->gsitc', diag, onehot, precision=lax.Precision.HIGHEST)
    t_mat = t_mat.reshape(g, L * gc, L * gc)

    pf = apow[0][:, ::-1][:, 1:][:, :, :, None] * b_bar[0][:, None]
    pb = apow[1][:, :L][:, :, :, None] * b_bar[1][:, None]
    def p_lay(m):
        return jnp.transpose(m, (0, 1, 3, 2)).reshape(g, L * gc, p)
    p_mat = jnp.concatenate([p_lay(jnp.real(pf)), p_lay(jnp.imag(pf)),
                             p_lay(jnp.real(pb)), p_lay(jnp.imag(pb))], axis=-1)

    wf = c_c[0][:, None] * apow[0][:, 1:][:, :, None, :]
    wb = c_c[1][:, None] * apow[1][:, ::-1][:, :L][:, :, None, :]
    def q_lay(m):
        return jnp.transpose(m, (0, 3, 1, 2)).reshape(g, p, L * gc)
    q_mat = jnp.concatenate([q_lay(jnp.real(wf)), q_lay(-jnp.imag(wf)),
                             q_lay(jnp.real(wb)), q_lay(-jnp.imag(wb))], axis=1)

    al = apow[:, :, L]
    dec = jnp.stack([jnp.real(al[0]), jnp.imag(al[0]), jnp.real(al[1]), jnp.imag(al[1])]).reshape(4, g * p)
    return t_mat, p_mat, q_mat, dec


def _s5_state_kernel(u_ref, p_ref, fre_ref, fim_ref, bre_ref, bim_ref):
    outs = [[], [], [], []]
    for gi in range(u_ref.shape[0]):
        xe = jnp.dot(u_ref[gi].astype(BF16), p_ref[gi], preferred_element_type=F32)
        for part in range(4):
            outs[part].append(xe[:, part * S5_STATE:(part + 1) * S5_STATE])
    for part, ref in enumerate((fre_ref, fim_ref, bre_ref, bim_ref)):
        ref[...] = jnp.concatenate(outs[part], axis=-1)


def _s5_scan_kernel(fre_ref, fim_ref, bre_ref, bim_ref, dec_ref, ofre_ref, ofim_ref, obre_ref, obim_ref):
    nc = fre_ref.shape[0]
    w = fre_ref.shape[1]
    far, fai, bar, bai = (dec_ref[i:i + 1, :] for i in range(4))

    def fwd(c, st):
        re, im = st
        ofre_ref[pl.ds(c, 1), :] = re
        ofim_ref[pl.ds(c, 1), :] = im
        return (far * re - fai * im + fre_ref[pl.ds(c, 1), :],
                far * im + fai * re + fim_ref[pl.ds(c, 1), :])

    def bwd(i, st):
        c = nc - 1 - i
        re, im = st
        obre_ref[pl.ds(c, 1), :] = re
        obim_ref[pl.ds(c, 1), :] = im
        return (bar * re - bai * im + bre_ref[pl.ds(c, 1), :],
                bar * im + bai * re + bim_ref[pl.ds(c, 1), :])

    zero = (jnp.zeros((1, w), F32), jnp.zeros((1, w), F32))
    lax.fori_loop(0, nc, fwd, zero)
    lax.fori_loop(0, nc, bwd, zero)


def _s5_out_kernel(u_ref, t_ref, q_ref, fre_ref, fim_ref, bre_ref, bim_ref, y_ref):
    for gi in range(u_ref.shape[0]):
        lanes = slice(gi * S5_STATE, (gi + 1) * S5_STATE)
        y = jnp.dot(u_ref[gi].astype(BF16), t_ref[gi], preferred_element_type=F32)
        xin = jnp.concatenate([r[:, lanes] for r in (fre_ref, fim_ref, bre_ref, bim_ref)], axis=-1)
        y_ref[gi] = (y + jnp.dot(xin.astype(BF16), q_ref[gi], preferred_element_type=F32)).astype(y_ref.dtype)


def _s5_to_groups_kernel(u_ref, o_ref, cat_ref, *, chunk):
    nc = cat_ref.shape[0]
    gc = S5_GROUP
    for s in range(chunk):
        cat_ref[:, s * LANE:(s + 1) * LANE] = u_ref[pl.ds(s, nc, stride=chunk), :].astype(BF16)
    cat = cat_ref[...]
    r = lax.broadcasted_iota(jnp.int32, (chunk * LANE, chunk * gc), 0)
    c = lax.broadcasted_iota(jnp.int32, (chunk * LANE, chunk * gc), 1)
    lane_of_col = jnp.where(r // LANE == c // gc, lax.rem(c, gc), -LANE)
    for j in range(o_ref.shape[0]):
        sel = jnp.where(lax.rem(r, LANE) - j * gc == lane_of_col, 1.0, 0.0).astype(BF16)
        o_ref[j] = jnp.dot(cat, sel, preferred_element_type=F32).astype(o_ref.dtype)


def _s5_from_groups_kernel(y_ref, o_ref, cat_ref, *, chunk):
    nc = cat_ref.shape[0]
    gc = S5_GROUP
    w = chunk * gc
    for j in range(y_ref.shape[0]):
        cat_ref[:, j * w:(j + 1) * w] = y_ref[j]
    cat = cat_ref[...]
    r = lax.broadcasted_iota(jnp.int32, (cat.shape[1], LANE), 0)
    lane = lax.broadcasted_iota(jnp.int32, (cat.shape[1], LANE), 1)
    src_lane = (r // w) * gc + lax.rem(r, gc)
    t_of_row = lax.rem(r, w) // gc
    for t in range(chunk):
        sel = jnp.where((t_of_row == t) & (src_lane == lane), 1.0, 0.0).astype(BF16)
        o_ref[pl.ds(t, nc, stride=chunk), :] = jnp.dot(cat, sel, preferred_element_type=F32)


def _s5_relayout(x, to_groups, s, g, chunk):
    gc = S5_GROUP
    nc = s // chunk
    per = LANE // gc
    nat = pl.BlockSpec((s, LANE), lambda i: (0, i))
    grp = pl.BlockSpec((per, nc, chunk * gc), lambda i: (i, 0, 0))
    if to_groups:
        kern, in_spec, out_spec = _s5_to_groups_kernel, nat, grp
        out_shape = jax.ShapeDtypeStruct((g, nc, chunk * gc), BF16)
        scratch = pltpu.VMEM((nc, chunk * LANE), BF16)
    else:
        kern, in_spec, out_spec = _s5_from_groups_kernel, grp, nat
        out_shape = jax.ShapeDtypeStruct((s, g * gc), F32)
        scratch = pltpu.VMEM((nc, per * chunk * gc), BF16)
    return pl.pallas_call(
        functools.partial(kern, chunk=chunk),
        out_shape=out_shape,
        grid=(g // per,),
        in_specs=[in_spec],
        out_specs=out_spec,
        scratch_shapes=[scratch],
        compiler_params=_cp(("parallel",)),
        name="s5_to_groups" if to_groups else "s5_from_groups",
    )(x)


def s5_bidirectional(u, a_re, a_im, log_dt, b_re, b_im, c_re, c_im, chunk=S5_CHUNK):
    s, dim = u.shape
    gc, p = S5_GROUP, S5_STATE
    g = dim // gc
    chunk = min(chunk, s)
    nc = s // chunk
    gp = LANE // p
    t_mat, p_mat, q_mat, dec = _s5_matrices(a_re, a_im, log_dt, b_re, b_im, c_re, c_im, chunk)
    t_mat, p_mat, q_mat = t_mat.astype(BF16), p_mat.astype(BF16), q_mat.astype(BF16)
    u_t = _s5_relayout(u, True, s, g, chunk)

    grp = lambda a, b: pl.BlockSpec((gp, a, b), lambda i: (i, 0, 0))
    st = pl.BlockSpec((nc, LANE), lambda i: (0, i))
    st_shape = jax.ShapeDtypeStruct((nc, g * p), F32)
    xe = pl.pallas_call(
        _s5_state_kernel,
        out_shape=[st_shape] * 4,
        grid=(g // gp,),
        in_specs=[grp(nc, chunk * gc), grp(chunk * gc, 4 * p)],
        out_specs=[st] * 4,
        compiler_params=_cp(("parallel",)),
        name="s5_chunk_states",
    )(u_t, p_mat)

    lanes = min(4 * LANE, g * p)
    cols = lambda rows: pl.BlockSpec((rows, lanes), lambda i: (0, i))
    xin = pl.pallas_call(
        _s5_scan_kernel,
        out_shape=[st_shape] * 4,
        grid=(g * p // lanes,),
        in_specs=[cols(nc)] * 4 + [cols(4)],
        out_specs=[cols(nc)] * 4,
        compiler_params=_cp(("parallel",)),
        name="s5_boundary_scan",
    )(*xe, dec)

    y_t = pl.pallas_call(
        _s5_out_kernel,
        out_shape=jax.ShapeDtypeStruct((g, nc, chunk * gc), BF16),
        grid=(g // gp,),
        in_specs=[grp(nc, chunk * gc), grp(chunk * gc, chunk * gc), grp(4 * p, chunk * gc)] + [st] * 4,
        out_specs=grp(nc, chunk * gc),
        compiler_params=_cp(("parallel",)),
        name="s5_outputs",
    )(u_t, t_mat, q_mat, *xin)
    return _s5_relayout(y_t, False, s, g, chunk)


def _s5_gate_kernel(y_ref, u_ref, d_ref, w_ref, o_ref):
    y = y_ref[...].astype(F32) + d_ref[...] * u_ref[...]
    y = 0.5 * y * (1.0 + jnp.tanh(math.sqrt(2.0 / math.pi) * (y + 0.044715 * (y * y * y))))
    z = jnp.dot(y.astype(BF16), w_ref[...], preferred_element_type=F32)
    o_ref[...] = (y * _sigmoid(z)).astype(o_ref.dtype)


def s5_gate(y, u, d_skip, w_glu, tm=1024):
    s, dim = y.shape
    tm = min(tm, s)
    row = pl.BlockSpec((tm, dim), lambda i: (i, 0))
    return pl.pallas_call(
        _s5_gate_kernel,
        out_shape=jax.ShapeDtypeStruct((s, dim), BF16),
        grid=(s // tm,),
        in_specs=[row, row, pl.BlockSpec((1, dim), lambda i: (0, 0)), pl.BlockSpec((dim, dim), lambda i: (0, 0))],
        out_specs=row,
        compiler_params=_cp(("parallel",)),
        name="s5_gate",
    )(y, u, d_skip.reshape(1, dim), w_glu.astype(BF16))


def _router_kernel(x_ref, g_ref, sh_ref, sc_ref, rw_ref, h_ref, info_ref):
    h = _modnorm(x_ref[...], g_ref[...], sh_ref[...], sc_ref[...])
    half = h.shape[1] // 2
    bits = lambda v: lax.bitcast_convert_type(v.astype(BF16).astype(F32), jnp.uint32)
    h_ref[...] = (bits(h[:, :half]) >> 16) | (bits(h[:, half:]) & jnp.uint32(0xFFFF0000))
    h_hi = h.astype(BF16)
    h_lo = (h - h_hi.astype(F32)).astype(BF16)
    p1 = jnp.dot(h_hi, rw_ref[...], preferred_element_type=F32)
    lane = lax.broadcasted_iota(jnp.int32, p1.shape, 1)
    w_hi = jnp.where(lane[:1] < N_EXPERTS, 1.0, 0.0).astype(BF16) * rw_ref[...]
    logits = p1 + pltpu.roll(p1, LANE - N_EXPERTS, 1) + jnp.dot(h_lo, w_hi, preferred_element_type=F32)
    logits = jnp.where(lane < N_EXPERTS, logits, -jnp.inf)
    m1 = jnp.max(logits, axis=-1, keepdims=True)
    i1 = jnp.min(jnp.where(logits == m1, lane, LANE), axis=-1, keepdims=True)
    rest = jnp.where(lane == i1, -jnp.inf, logits)
    m2 = jnp.max(rest, axis=-1, keepdims=True)
    i2 = jnp.min(jnp.where(rest == m2, lane, LANE), axis=-1, keepdims=True)
    e = jnp.exp(m2 - m1)
    g1 = 1.0 / (1.0 + e)
    g2 = e / (1.0 + e)
    info = jnp.where(lane == 0, i1.astype(F32),
                     jnp.where(lane == 1, i2.astype(F32),
                               jnp.where(lane == 2, g1, jnp.where(lane == 3, g2, 0.0))))
    info_ref[...] = info


def moe_router(x, g, sh, sc, router_w, tm=512):
    s, d = x.shape
    tm = min(tm, s)
    rw_hi = router_w.astype(BF16)
    rw_lo = (router_w - rw_hi.astype(F32)).astype(BF16)
    rw = jnp.pad(jnp.concatenate([rw_hi, rw_lo], axis=1), ((0, 0), (0, LANE - 2 * router_w.shape[1])))
    vec = pl.BlockSpec((1, d), lambda i: (0, 0))
    return pl.pallas_call(
        _router_kernel,
        out_shape=[jax.ShapeDtypeStruct((s, d // 2), jnp.uint32), jax.ShapeDtypeStruct((s, LANE), F32)],
        grid=(s // tm,),
        in_specs=[pl.BlockSpec((tm, d), lambda i: (i, 0)), vec, vec, vec,
                  pl.BlockSpec((d, LANE), lambda i: (0, 0))],
        out_specs=[pl.BlockSpec((tm, d // 2), lambda i: (i, 0)), pl.BlockSpec((tm, LANE), lambda i: (i, 0))],
        compiler_params=_cp(("parallel",)),
        name="moe_router",
    )(x, g, sh, sc, rw)


def _expert_kernel(te_ref, tv_ref, src_ref, hp_hbm, w1_ref, w3_ref, w2_ref, o_ref, acc_ref, gbuf, h_ref, sem):
    t = pl.program_id(0)
    f = pl.program_id(1)
    nt = pl.num_programs(0)
    nf = pl.num_programs(1)
    tm = acc_ref.shape[0]
    per_step = tm // nf
    valid = tv_ref[t] > 0
    slot = lax.rem(t, 2)

    def row_copy(tile, buf, r):
        return pltpu.make_async_copy(hp_hbm.at[pl.ds(src_ref[tile * tm + r], 1)],
                                     gbuf.at[buf, pl.ds(r, 1)], sem.at[buf])

    def wait_half(buf):
        pltpu.make_async_copy(hp_hbm.at[pl.ds(0, tm)], gbuf.at[buf], sem.at[buf]).wait()

    @pl.when((t == 0) & (f == 0))
    def _():
        def issue(r, carry):
            row_copy(0, 0, r).start()
            return carry
        lax.fori_loop(0, tm, issue, 0)

    @pl.when((f == 0) & ((t == 0) | (tv_ref[jnp.maximum(t - 1, 0)] > 0)))
    def _():
        wait_half(slot)

    @pl.when((f == 0) & valid)
    def _():
        w = gbuf[slot]
        half = w.shape[1]
        h_ref[:, :half] = lax.bitcast_convert_type(w << 16, F32).astype(BF16)
        h_ref[:, half:] = lax.bitcast_convert_type(w & jnp.uint32(0xFFFF0000), F32).astype(BF16)

    @pl.when(f == 0)
    def _():
        acc_ref[...] = jnp.zeros_like(acc_ref)

    def compute(rows):
        for u in range(per_step):
            row_copy(t + 1, 1 - slot, f * per_step + u).start()
        h = h_ref[:rows, :]
        u = jnp.dot(h, w1_ref[...].astype(BF16), preferred_element_type=F32)
        v = jnp.dot(h, w3_ref[...].astype(BF16), preferred_element_type=F32)
        acc_ref[:rows, :] += jnp.dot((_silu(u) * v).astype(BF16), w2_ref[...].astype(BF16),
                                     preferred_element_type=F32)

    quarter = tm // 4
    for k in range(1, 5):
        @pl.when((tv_ref[t] > (k - 1) * quarter) & (tv_ref[t] <= k * quarter))
        def _():
            compute(k * quarter)

    @pl.when(f == nf - 1)
    def _():
        o_ref[...] = acc_ref[...].astype(o_ref.dtype)

    @pl.when((t == nt - 1) & (f == nf - 1) & valid)
    def _():
        wait_half(1 - slot)


def expert_ffn(h_packed, src, tile_expert, tile_valid, w1, w3, w2, tm, tf=512):
    dh = h_packed.shape[1]
    d = 2 * dh
    n = src.shape[0] - tm
    f = w1.shape[2]
    tf = min(tf, f)
    nf = f // tf
    assert tm % nf == 0 and n % tm == 0

    def fblk(j, t, tv):
        return jnp.where(tv[t] > 0, j, nf - 1)

    return pl.pallas_call(
        _expert_kernel,
        out_shape=jax.ShapeDtypeStruct((n, d), BF16),
        grid_spec=pltpu.PrefetchScalarGridSpec(
            num_scalar_prefetch=3,
            grid=(n // tm, nf),
            in_specs=[pl.BlockSpec(memory_space=pl.ANY),
                      pl.BlockSpec((None, d, tf), lambda t, j, te, tv, sr: (te[t], 0, fblk(j, t, tv))),
                      pl.BlockSpec((None, d, tf), lambda t, j, te, tv, sr: (te[t], 0, fblk(j, t, tv))),
                      pl.BlockSpec((None, tf, d), lambda t, j, te, tv, sr: (te[t], fblk(j, t, tv), 0))],
            out_specs=pl.BlockSpec((tm, d), lambda t, j, te, tv, sr: (t, 0)),
            scratch_shapes=[pltpu.VMEM((tm, d), F32), pltpu.VMEM((2, tm, dh), jnp.uint32),
                            pltpu.VMEM((tm, d), BF16), pltpu.SemaphoreType.DMA((2,))]),
        compiler_params=_cp(("arbitrary", "arbitrary")),
        name="expert_ffn",
    )(tile_expert, tile_valid, src, h_packed, w1, w3, w2)


def _dispatch(idx, tm):
    s = idx.shape[0]
    e_flat = idx.reshape(-1)
    onehot = (e_flat[None, :] == jnp.arange(N_EXPERTS)[:, None]).astype(jnp.int32)
    csum = jnp.cumsum(onehot, axis=1)
    rank = jnp.sum((csum - onehot) * onehot, axis=0)
    counts = csum[:, -1]
    padded = ((counts + tm - 1) // tm) * tm
    gend = jnp.cumsum(padded)
    slot = (gend - padded)[e_flat] + rank
    nt = -(-(2 * s) // tm) + N_EXPERTS
    src = jnp.zeros(((nt + 1) * tm,), jnp.int32).at[slot].set(
        jnp.arange(2 * s, dtype=jnp.int32) // 2, mode="promise_in_bounds", unique_indices=True)
    tstart = jnp.arange(nt, dtype=jnp.int32) * tm
    valid = tstart < gend[-1]
    te = jnp.minimum(jnp.sum((tstart[:, None] >= gend[None, :]).astype(jnp.int32), axis=1), N_EXPERTS - 1)
    nvalid = gend[-1] // tm
    fill = jnp.clip((gend - padded + counts)[te] - tstart, 0, tm)
    fill = jnp.where(valid, fill, 0)
    te = jnp.where(valid, te, te[jnp.maximum(nvalid - 1, 0)])
    return slot.reshape(s, 2), src, te.astype(jnp.int32), fill.astype(jnp.int32)


def _combine_kernel(ya_ref, yb_ref, info_ref, x_ref, g_ref, gt_ref, o_ref):
    info = info_ref[...]
    y = info[:, 2:3] * ya_ref[...].astype(F32) + info[:, 3:4] * yb_ref[...].astype(F32)
    o_ref[...] = x_ref[...] + gt_ref[...] * (_rms(y) * g_ref[...])


def moe_combine(ya, yb, info, x, g, gt, tm=512):
    s, d = x.shape
    tm = min(tm, s)
    row = lambda w: pl.BlockSpec((tm, w), lambda i: (i, 0))
    vec = pl.BlockSpec((1, d), lambda i: (0, 0))
    return pl.pallas_call(
        _combine_kernel,
        out_shape=jax.ShapeDtypeStruct((s, d), F32),
        grid=(s // tm,),
        in_specs=[row(d), row(d), row(LANE), row(d), vec, vec],
        out_specs=row(d),
        compiler_params=_cp(("parallel",)),
        name="moe_combine",
    )(ya, yb, info, x, g, gt)


def moe_sublayer(x, g, sh, sc, g2, gt, router_w, w1, w3, w2, tm_e=EXPERT_TILE):
    s = x.shape[0]
    tm_e = min(tm_e, s)
    h_packed, info = moe_router(x, g, sh, sc, router_w)
    idx = info[:, :2].astype(jnp.int32)
    slot, src, te, tv = _dispatch(idx, tm_e)
    y_sorted = expert_ffn(h_packed, src, te, tv, w1, w3, w2, tm_e)
    ya = y_sorted.at[slot[:, 0]].get(mode="promise_in_bounds")
    yb = y_sorted.at[slot[:, 1]].get(mode="promise_in_bounds")
    return moe_combine(ya, yb, info, x, g2, gt)


def kernel(x, c, t5_table, norm_g, ada_w, ada_b, e_w_in, e_conv_w, e_conv_b, e_cln_g, e_cln_b, e_w_out, e_ffn_w1, e_ffn_w3, e_ffn_w2, o_w_in, o_q_norm_g, o_w_uq, o_kv_norm_g, o_w_ukv, s5_a_re, s5_a_im, s5_log_dt, s5_b_re, s5_b_im, s5_c_re, s5_c_im, s5_d, s5_w_glu, o_w_out, router_w, moe_w1, moe_w3, moe_w2):
    bsz, seq, d = x.shape
    assert bsz == 1
    depth = norm_g.shape[0]
    conv_dim = e_conv_w.shape[2]
    xs = x.reshape(seq, d)
    mod = adaln(c, ada_w, ada_b)
    for layer in range(depth):
        i = layer // 2
        sh_m, sc_m, gt_m, sh_f, sc_f, gt_f = (mod[layer, :, k * d:(k + 1) * d] for k in range(6))
        g = [norm_g[layer, k].reshape(1, d) for k in range(4)]
        if layer % 2 == 0:
            z = norm_mod_matmul(xs, g[0], sh_m, sc_m, e_w_in[i].astype(BF16))
            a = conformer_conv(z, e_conv_w[i], e_conv_b[i], e_cln_g[i], e_cln_b[i])
            o = dilated_attention(z, t5_table, 2 * conv_dim, (z.shape[1] - 2 * conv_dim) // 3)
            xs = out_proj_residual(a, o, e_w_out[i].astype(BF16), xs, g[1], gt_m)
            xs = ffn_sublayer(xs, g[2], sh_f, sc_f, g[3], gt_f, e_ffn_w1[i].astype(BF16),
                              e_ffn_w3[i].astype(BF16), e_ffn_w2[i].astype(BF16))
        else:
            q, k, v, u = odd_in_proj(xs, g[0], sh_m, sc_m, o_w_in[i], o_q_norm_g[i], o_w_uq[i],
                                     o_kv_norm_g[i], o_w_ukv[i])
            o_mla = mla_attention(q, k, v)
            y = s5_bidirectional(u, s5_a_re[i], s5_a_im[i], s5_log_dt[i], s5_b_re[i], s5_b_im[i],
                                 s5_c_re[i], s5_c_im[i])
            y = s5_gate(y, u, s5_d[i], s5_w_glu[i])
            xs = out_proj_residual(o_mla, y, o_w_out[i].astype(BF16), xs, g[1], gt_m)
            xs = moe_sublayer(xs, g[2], sh_f, sc_f, g[3], gt_f, router_w[i], moe_w1[i], moe_w3[i], moe_w2[i])
    return xs.reshape(bsz, seq, d)
```
